```python
import math
import jax, jax.numpy as jnp
from jax import lax
import numpy as np

D_MODEL = 1024
BATCH = 8
SEQ = 2048
DEPTH = 1
DEC_BATCH = 128
DEC_SEQ = 4
PAST_LEN = 2048
PAGE_SIZE = 128

HEAD_DIM = 64
HEADS_PER_GROUP = 8
ATTN_GROUPS = ((128, 1), (512, 4), (2048, 16))
N_ATTN_GROUPS = len(ATTN_GROUPS)
N_ATTN_HEADS = N_ATTN_GROUPS * HEADS_PER_GROUP
ATTN_WIDTH = N_ATTN_HEADS * HEAD_DIM
ATTN_OUT_WIDTH = HEADS_PER_GROUP * HEAD_DIM
ROT_DIM = HEAD_DIM // 4
ROPE_THETA = 500000.0
Q_BLOCK = 128

SSM_GROUP_CH = 16
SSM_WIDTH = D_MODEL // 2
N_SSM_GROUPS = SSM_WIDTH // SSM_GROUP_CH
SSM_STATE = 64
DT_MIN = 1e-3
DT_MAX = 1e-1

IN_WIDTH = 3 * ATTN_WIDTH + SSM_WIDTH + 2 * D_MODEL

N_EXPERT_GROUPS = 4
EXPERTS_PER_GROUP = 8
N_EXPERTS = N_EXPERT_GROUPS * EXPERTS_PER_GROUP
TOP_K_INNER = 2
D_EXPERT = D_MODEL // 4
MOE_BLOCK = 128

NORM_EPS = 1e-6

kernel_name = 'dilated_s5_hmoe_hybrid_step'


def rms_norm(x, g):
    xf = x.astype(jnp.float32)
    var = jnp.mean(xf * xf, axis=-1, keepdims=True)
    return (xf * lax.rsqrt(var + NORM_EPS) * g.astype(jnp.float32)).astype(x.dtype)


def partial_rope(x, pos):
    half = ROT_DIM // 2
    inv_freq = ROPE_THETA ** (-(jnp.arange(half, dtype=jnp.float32) / half))
    ang = pos.astype(jnp.float32)[:, None] * inv_freq[None, :]
    cos = jnp.cos(ang)[None, :, None, :]
    sin = jnp.sin(ang)[None, :, None, :]
    xr = x[..., :ROT_DIM].astype(jnp.float32)
    x1, x2 = xr[..., :half], xr[..., half:]
    rot = jnp.concatenate([x1 * cos - x2 * sin, x2 * cos + x1 * sin], axis=-1).astype(x.dtype)
    return jnp.concatenate([rot, x[..., ROT_DIM:]], axis=-1)


def dilated_block(q_blk, q_idx, k_all, v_all, dilation, n_keys):
    key_idx = q_idx[:, None] - dilation * jnp.arange(n_keys, dtype=jnp.int32)[None, :]
    valid = key_idx >= 0
    safe = jnp.maximum(key_idx, 0)
    kg = jnp.take(k_all, safe, axis=1)
    vg = jnp.take(v_all, safe, axis=1)
    s = jnp.einsum('bqhd,bqkhd->bhqk', q_blk, kg, preferred_element_type=jnp.float32) * (HEAD_DIM ** -0.5)
    s = jnp.where(valid[None, None], s, -jnp.inf)
    lse = jax.nn.logsumexp(s, axis=-1, keepdims=True)
    p = jnp.exp(s - lse)
    o = jnp.einsum('bhqk,bqkhd->bqhd', p, vg.astype(jnp.float32))
    return o, jnp.transpose(lse[..., 0], (0, 2, 1))


def dilated_attention(q, k_all, v_all, q_offset, dilation, n_keys):
    bsz, t_len, n_h, hd = q.shape
    if t_len <= Q_BLOCK or t_len % Q_BLOCK != 0:
        return dilated_block(q, q_offset + jnp.arange(t_len, dtype=jnp.int32), k_all, v_all, dilation, n_keys)
    n_blk = t_len // Q_BLOCK
    qb = jnp.transpose(q.reshape(bsz, n_blk, Q_BLOCK, n_h, hd), (1, 0, 2, 3, 4))

    def body(args):
        i, q_i = args
        idx = q_offset + i * Q_BLOCK + jnp.arange(Q_BLOCK, dtype=jnp.int32)
        return dilated_block(q_i, idx, k_all, v_all, dilation, n_keys)

    o, lse = lax.map(body, (jnp.arange(n_blk, dtype=jnp.int32), qb))
    o = jnp.transpose(o, (1, 0, 2, 3, 4)).reshape(bsz, t_len, n_h, hd)
    lse = jnp.transpose(lse, (1, 0, 2, 3)).reshape(bsz, t_len, n_h)
    return o, lse


def s5_scan(u, h0, ssm_log_dt, ssm_a_re, ssm_a_im, ssm_b_re, ssm_b_im, ssm_c_re, ssm_c_im, ssm_d):
    f32 = jnp.float32
    bsz, t_len, _ = u.shape
    uf = u.astype(f32).reshape(bsz, t_len, N_SSM_GROUPS, SSM_GROUP_CH)
    dt = jnp.exp(ssm_log_dt.astype(f32))[:, None]
    lam = lax.complex(ssm_a_re.astype(f32), ssm_a_im.astype(f32))
    lam_bar = jnp.exp(lam * dt)
    b_c = lax.complex(ssm_b_re.astype(f32), ssm_b_im.astype(f32))
    b_bar = ((lam_bar - 1.0) / lam)[:, :, None] * b_c
    c_c = lax.complex(ssm_c_re.astype(f32), ssm_c_im.astype(f32))
    bu = jnp.einsum('btgc,gpc->btgp', uf.astype(jnp.complex64), b_bar)
    a = jnp.broadcast_to(lam_bar, bu.shape)

    def combine(left, right):
        a_l, b_l = left
        a_r, b_r = right
        return a_l * a_r, a_r * b_l + b_r

    _, h = lax.associative_scan(combine, (a, bu), axis=1)
    if h0 is not None:
        h0c = lax.complex(h0[..., 0].astype(f32), h0[..., 1].astype(f32))
        steps = jnp.arange(1, t_len + 1, dtype=f32)[:, None, None]
        powers = jnp.exp((lam * dt)[None] * steps)
        h = h + powers[None] * h0c[:, None]
    y = jnp.real(jnp.einsum('btgp,gcp->btgc', h, c_c)) + ssm_d.astype(f32)[None, None] * uf
    h_last = h[:, -1]
    new_state = jnp.stack([jnp.real(h_last), jnp.imag(h_last)], axis=-1)
    return y.reshape(bsz, t_len, SSM_WIDTH).astype(u.dtype), new_state.astype(u.dtype)


def hier_moe(x, w_router_group, b_router_group, w_router_expert, b_router_expert, w_exp_gate, w_exp_up, w_exp_down):
    f32 = jnp.float32
    n_tok = x.shape[0]
    lg = jnp.dot(x, w_router_group, preferred_element_type=f32) + b_router_group.astype(f32)
    grp = jnp.argmax(lg, axis=-1).astype(jnp.int32)
    p_grp = jnp.take_along_axis(jax.nn.softmax(lg, axis=-1), grp[:, None], axis=-1)
    le = (jnp.dot(x, w_router_expert, preferred_element_type=f32) + b_router_expert.astype(f32)).reshape(n_tok, N_EXPERT_GROUPS, EXPERTS_PER_GROUP)
    le_grp = jnp.take_along_axis(le, grp[:, None, None], axis=1)[:, 0]
    top_v, top_i = lax.top_k(le_grp, TOP_K_INNER)
    comb_w = jax.nn.softmax(top_v, axis=-1) * p_grp
    expert_id = grp[:, None] * EXPERTS_PER_GROUP + top_i.astype(jnp.int32)
    n_assign = n_tok * TOP_K_INNER
    flat_e = expert_id.reshape(-1)
    flat_tok = jnp.repeat(jnp.arange(n_tok, dtype=jnp.int32), TOP_K_INNER)
    flat_w = comb_w.reshape(-1)
    order = jnp.argsort(flat_e)
    e_sorted = flat_e[order]
    counts = jnp.bincount(flat_e, length=N_EXPERTS).astype(jnp.int32)
    padded = ((counts + MOE_BLOCK - 1) // MOE_BLOCK) * MOE_BLOCK
    pad_end = jnp.cumsum(padded)
    pad_start = pad_end - padded
    raw_start = jnp.cumsum(counts) - counts
    rank = jnp.arange(n_assign, dtype=jnp.int32) - raw_start[e_sorted]
    slot = pad_start[e_sorted] + rank
    n_blocks = -(-n_assign // MOE_BLOCK) + N_EXPERTS
    n_slots = n_blocks * MOE_BLOCK
    slot_tok = jnp.zeros((n_slots,), jnp.int32).at[slot].set(flat_tok[order])
    slot_w = jnp.zeros((n_slots,), f32).at[slot].set(flat_w[order])
    block_start = jnp.arange(n_blocks, dtype=jnp.int32) * MOE_BLOCK
    block_e = jnp.clip(jnp.searchsorted(pad_end, block_start, side='right'), 0, N_EXPERTS - 1).astype(jnp.int32)
    xs = x[slot_tok].reshape(n_blocks, MOE_BLOCK, D_MODEL)

    def expert_block(args):
        xb, e = args
        hmid = jax.nn.silu(xb @ w_exp_gate[e]) * (xb @ w_exp_up[e])
        return hmid @ w_exp_down[e]

    yb = lax.map(expert_block, (xs, block_e))
    y = jax.ops.segment_sum(yb.reshape(n_slots, D_MODEL).astype(f32) * slot_w[:, None], slot_tok, num_segments=n_tok)
    return y.astype(x.dtype)


def trunk_layer(x, pos0, kv_bufs, ssm_h0, g_attn_norm, w_in, ssm_log_dt, ssm_a_re, ssm_a_im, ssm_b_re, ssm_b_im, ssm_c_re, ssm_c_im, ssm_d, w_glu, w_attn_branch, w_out, g_ffn_norm, w_router_group, b_router_group, w_router_expert, b_router_expert, w_exp_gate, w_exp_up, w_exp_down):
    bsz, t_len, _ = x.shape
    xn = rms_norm(x, g_attn_norm)
    proj = xn @ w_in
    q, k, v, u, gates = jnp.split(proj, [ATTN_WIDTH, 2 * ATTN_WIDTH, 3 * ATTN_WIDTH, 3 * ATTN_WIDTH + SSM_WIDTH], axis=-1)
    pos = pos0 + jnp.arange(t_len, dtype=jnp.int32)
    q = partial_rope(q.reshape(bsz, t_len, N_ATTN_HEADS, HEAD_DIM), pos)
    k = partial_rope(k.reshape(bsz, t_len, N_ATTN_HEADS, HEAD_DIM), pos)
    v = v.reshape(bsz, t_len, N_ATTN_HEADS, HEAD_DIM)
    outs, lses, new_rows = [], [], []
    for gi, (win, dil) in enumerate(ATTN_GROUPS):
        hs = slice(gi * HEADS_PER_GROUP, (gi + 1) * HEADS_PER_GROUP)
        q_g, k_g, v_g = q[:, :, hs], k[:, :, hs], v[:, :, hs]
        if kv_bufs is None:
            k_all, v_all, off = k_g, v_g, 0
            keep = min(win, t_len)
            new_rows.append(jnp.stack([k_g[:, t_len - keep:], v_g[:, t_len - keep:]], axis=2))
        else:
            buf = kv_bufs[gi]
            k_all = jnp.concatenate([buf[:, :, 0].astype(k_g.dtype), k_g], axis=1)
            v_all = jnp.concatenate([buf[:, :, 1].astype(v_g.dtype), v_g], axis=1)
            off = buf.shape[1]
            new_rows.append(jnp.stack([k_g, v_g], axis=2))
        o, lse = dilated_attention(q_g, k_all, v_all, off, dil, win // dil + 1)
        outs.append(o)
        lses.append(lse)
    wts = jax.nn.softmax(jnp.stack(lses, axis=0), axis=0)
    attn = jnp.einsum('gbth,gbthd->bthd', wts, jnp.stack(outs, axis=0)).astype(x.dtype)
    attn_out = attn.reshape(bsz, t_len, ATTN_OUT_WIDTH) @ w_attn_branch
    y_ssm, ssm_new = s5_scan(u, ssm_h0, ssm_log_dt, ssm_a_re, ssm_a_im, ssm_b_re, ssm_b_im, ssm_c_re, ssm_c_im, ssm_d)
    glu = jax.nn.gelu(y_ssm) @ w_glu
    ssm_out = glu[..., :D_MODEL] * jax.nn.sigmoid(glu[..., D_MODEL:])
    gate_attn, gate_ssm = jnp.split(gates, 2, axis=-1)
    merged = jax.nn.sigmoid(gate_attn) * attn_out + jax.nn.sigmoid(gate_ssm) * ssm_out
    h = x + merged @ w_out
    hn = rms_norm(h, g_ffn_norm).reshape(bsz * t_len, D_MODEL)
    h = h + hier_moe(hn, w_router_group, b_router_group, w_router_expert, b_router_expert, w_exp_gate, w_exp_up, w_exp_down).reshape(bsz, t_len, D_MODEL)
    return h, new_rows, ssm_new


def setup_inputs(seed: int = 0) -> dict:
    key = jax.random.key(seed)
    ks = jax.random.split(key, 32)
    f32 = jnp.float32

    def nrm(k, shape, scale):
        return jax.random.normal(k, shape, f32) * scale

    wb = [min(w, PAST_LEN) for (w, _) in ATTN_GROUPS]
    kv_tail = (2, HEADS_PER_GROUP, HEAD_DIM)
    n_idx = jnp.arange(SSM_STATE, dtype=f32)
    ssm_shape = (DEPTH, N_SSM_GROUPS, SSM_STATE)
    return {
        'x_prompt': nrm(ks[0], (BATCH, SEQ, D_MODEL), 1.0),
        'x_sample': nrm(ks[1], (DEC_BATCH, DEC_SEQ, D_MODEL), 1.0),
        'cache_kv_w128': nrm(ks[2], (DEPTH, DEC_BATCH, wb[0]) + kv_tail, 1.0),
        'cache_kv_w512': nrm(ks[3], (DEPTH, DEC_BATCH, wb[1]) + kv_tail, 1.0),
        'cache_kv_w2048': nrm(ks[4], (DEPTH, DEC_BATCH, wb[2]) + kv_tail, 1.0),
        'state_ssm': nrm(ks[5], (DEPTH, DEC_BATCH, N_SSM_GROUPS, SSM_STATE, 2), 0.5),
        'g_attn_norm': 1.0 + nrm(ks[6], (DEPTH, D_MODEL), 0.02),
        'w_in': nrm(ks[7], (DEPTH, D_MODEL, IN_WIDTH), D_MODEL ** -0.5),
        'ssm_log_dt': jax.random.uniform(ks[8], (DEPTH, N_SSM_GROUPS), f32, math.log(DT_MIN), math.log(DT_MAX)),
        'ssm_a_re': -0.5 + nrm(ks[9], ssm_shape, 0.01),
        'ssm_a_im': math.pi * n_idx + nrm(ks[10], ssm_shape, 0.01),
        'ssm_b_re': nrm(ks[11], (DEPTH, N_SSM_GROUPS, SSM_STATE, SSM_GROUP_CH), (2 * SSM_GROUP_CH) ** -0.5),
        'ssm_b_im': nrm(ks[12], (DEPTH, N_SSM_GROUPS, SSM_STATE, SSM_GROUP_CH), (2 * SSM_GROUP_CH) ** -0.5),
        'ssm_c_re': nrm(ks[13], (DEPTH, N_SSM_GROUPS, SSM_GROUP_CH, SSM_STATE), 0.5),
        'ssm_c_im': nrm(ks[14], (DEPTH, N_SSM_GROUPS, SSM_GROUP_CH, SSM_STATE), 0.5),
        'ssm_d': nrm(ks[15], (DEPTH, N_SSM_GROUPS, SSM_GROUP_CH), 1.0),
        'w_glu': nrm(ks[16], (DEPTH, SSM_WIDTH, 2 * D_MODEL), SSM_WIDTH ** -0.5),
        'w_attn_branch': nrm(ks[17], (DEPTH, ATTN_OUT_WIDTH, D_MODEL), ATTN_OUT_WIDTH ** -0.5),
        'w_out': nrm(ks[18], (DEPTH, D_MODEL, D_MODEL), D_MODEL ** -0.5),
        'g_ffn_norm': 1.0 + nrm(ks[19], (DEPTH, D_MODEL), 0.02),
        'w_router_group': nrm(ks[20], (DEPTH, D_MODEL, N_EXPERT_GROUPS), D_MODEL ** -0.5),
        'b_router_group': nrm(ks[21], (DEPTH, N_EXPERT_GROUPS), 0.01),
        'w_router_expert': nrm(ks[22], (DEPTH, D_MODEL, N_EXPERTS), D_MODEL ** -0.5),
        'b_router_expert': nrm(ks[23], (DEPTH, N_EXPERTS), 0.01),
        'w_exp_gate': nrm(ks[24], (DEPTH, N_EXPERTS, D_MODEL, D_EXPERT), D_MODEL ** -0.5),
        'w_exp_up': nrm(ks[25], (DEPTH, N_EXPERTS, D_MODEL, D_EXPERT), D_MODEL ** -0.5),
        'w_exp_down': nrm(ks[26], (DEPTH, N_EXPERTS, D_EXPERT, D_MODEL), D_EXPERT ** -0.5),
        'g_final': 1.0 + nrm(ks[27], (D_MODEL,), 0.02),
    }


def reference(x_prompt, x_sample, cache_kv_w128, cache_kv_w512, cache_kv_w2048, state_ssm, g_attn_norm, w_in, ssm_log_dt, ssm_a_re, ssm_a_im, ssm_b_re, ssm_b_im, ssm_c_re, ssm_c_im, ssm_d, w_glu, w_attn_branch, w_out, g_ffn_norm, w_router_group, b_router_group, w_router_expert, b_router_expert, w_exp_gate, w_exp_up, w_exp_down, g_final):
    h_p, h_s = x_prompt, x_sample
    rows_p = [[] for _ in ATTN_GROUPS]
    rows_s = [[] for _ in ATTN_GROUPS]
    st_p, st_s = [], []
    for layer in range(DEPTH):
        params = dict(g_attn_norm=g_attn_norm[layer], w_in=w_in[layer], ssm_log_dt=ssm_log_dt[layer], ssm_a_re=ssm_a_re[layer], ssm_a_im=ssm_a_im[layer], ssm_b_re=ssm_b_re[layer], ssm_b_im=ssm_b_im[layer], ssm_c_re=ssm_c_re[layer], ssm_c_im=ssm_c_im[layer], ssm_d=ssm_d[layer], w_glu=w_glu[layer], w_attn_branch=w_attn_branch[layer], w_out=w_out[layer], g_ffn_norm=g_ffn_norm[layer], w_router_group=w_router_group[layer], b_router_group=b_router_group[layer], w_router_expert=w_router_expert[layer], b_router_expert=b_router_expert[layer], w_exp_gate=w_exp_gate[layer], w_exp_up=w_exp_up[layer], w_exp_down=w_exp_down[layer])
        h_p, kv_p, s_p = trunk_layer(h_p, 0, None, None, **params)
        bufs = (cache_kv_w128[layer], cache_kv_w512[layer], cache_kv_w2048[layer])
        h_s, kv_s, s_s = trunk_layer(h_s, PAST_LEN, bufs, state_ssm[layer], **params)
        for gi in range(N_ATTN_GROUPS):
            rows_p[gi].append(kv_p[gi])
            rows_s[gi].append(kv_s[gi])
        st_p.append(s_p)
        st_s.append(s_s)
    y_prompt = rms_norm(h_p, g_final)
    y_sample = rms_norm(h_s, g_final)
    kv_w128_prompt = jnp.stack(rows_p[0], axis=0)
    kv_w128_sample = jnp.stack(rows_s[0], axis=0)
    kv_w512_prompt = jnp.stack(rows_p[1], axis=0)
    kv_w512_sample = jnp.stack(rows_s[1], axis=0)
    kv_w2048_prompt = jnp.stack(rows_p[2], axis=0)
    kv_w2048_sample = jnp.stack(rows_s[2], axis=0)
    ssm_state_prompt = jnp.stack(st_p, axis=0)
    ssm_state_sample = jnp.stack(st_s, axis=0)
    return (y_prompt, y_sample, kv_w128_prompt, kv_w128_sample, kv_w512_prompt, kv_w512_sample, kv_w2048_prompt, kv_w2048_sample, ssm_state_prompt, ssm_state_sample)
```

```python
import functools
import math

import jax
import jax.numpy as jnp
from jax import lax
from jax.experimental import pallas as pl
from jax.experimental.pallas import tpu as pltpu

F32 = jnp.float32
BF16 = jnp.bfloat16

D_MODEL = 1024
HEAD_DIM = 64
HEADS_PER_GROUP = 8
GROUP_WIDTH = HEADS_PER_GROUP * HEAD_DIM
ATTN_GROUPS = ((128, 1), (512, 4), (2048, 16))
N_ATTN_GROUPS = len(ATTN_GROUPS)
ATTN_WIDTH = N_ATTN_GROUPS * GROUP_WIDTH
ROT_DIM = HEAD_DIM // 4
ROPE_THETA = 500000.0
WINDOW_KEYS = 128
SSM_GROUP_CH = 16
SSM_WIDTH = D_MODEL // 2
N_SSM_GROUPS = SSM_WIDTH // SSM_GROUP_CH
SSM_STATE = 64
SSM_COLS = N_SSM_GROUPS * SSM_STATE
IN_WIDTH = 3 * ATTN_WIDTH + SSM_WIDTH + 2 * D_MODEL
N_EXPERT_GROUPS = 4
EXPERTS_PER_GROUP = 8
N_EXPERTS = N_EXPERT_GROUPS * EXPERTS_PER_GROUP
D_EXPERT = D_MODEL // 4
NORM_EPS = 1e-6

LANES = 128
SUBLANES = 8
VMEM_LIMIT = 56 * 1024 * 1024

PROJ_TILE = 256
ATTN_TILE = 128
SSM_ROWS = 512
MOE_BLOCK = 256
ROW_TILE = 512
DEC_BATCH_TILE = 4


def _params(*sem):
    return pltpu.CompilerParams(dimension_semantics=sem, vmem_limit_bytes=VMEM_LIMIT)


def _inproj_kernel(x_ref, g_ref, w_ref, cos_ref, sin_ref, q_ref, kv0_ref, kv1_ref, kv2_ref, u_ref, gate_ref):
    x = x_ref[...]
    var = jnp.mean(x * x, axis=-1, keepdims=True)
    xn = (x * lax.rsqrt(var + NORM_EPS) * g_ref[...]).astype(BF16)
    cos = cos_ref[...]
    sin = sin_ref[...]
    lane = lax.broadcasted_iota(jnp.int32, cos.shape, 1) % HEAD_DIM
    first_half = lane < ROT_DIM // 2
    rotated = lane < ROT_DIM

    def rope_chunk(c):
        partner = jnp.where(first_half, pltpu.roll(c, LANES - ROT_DIM // 2, 1), pltpu.roll(c, ROT_DIM // 2, 1))
        return jnp.where(rotated, c * cos + partner * sin, c)

    kv_refs = (kv0_ref, kv1_ref, kv2_ref)
    n_tiles = IN_WIDTH // GROUP_WIDTH
    for c in range(n_tiles):
        acc = jnp.dot(xn, w_ref[:, c * GROUP_WIDTH:(c + 1) * GROUP_WIDTH], preferred_element_type=F32)
        if c < 3:
            for j in range(GROUP_WIDTH // LANES):
                r = rope_chunk(acc[:, j * LANES:(j + 1) * LANES]) * (HEAD_DIM ** -0.5)
                q_ref[:, c * GROUP_WIDTH + j * LANES:c * GROUP_WIDTH + (j + 1) * LANES] = r.astype(BF16)
        elif c < 6:
            for j in range(GROUP_WIDTH // LANES):
                kv_refs[c - 3][:, j * LANES:(j + 1) * LANES] = rope_chunk(acc[:, j * LANES:(j + 1) * LANES])
        elif c < 9:
            kv_refs[c - 6][:, GROUP_WIDTH:] = acc
        elif c == 9:
            u_ref[...] = acc
        else:
            gate_ref[:, (c - 10) * GROUP_WIDTH:(c - 9) * GROUP_WIDTH] = acc


def _rope_tables(pos):
    half = ROT_DIM // 2
    inv_freq = ROPE_THETA ** (-(jnp.arange(half, dtype=F32) / half))
    ang = pos.astype(F32)[:, None] * inv_freq[None, :]
    cos, sin = jnp.cos(ang), jnp.sin(ang)
    n = pos.shape[0]
    rest = HEAD_DIM - ROT_DIM
    cos_h = jnp.concatenate([cos, cos, jnp.ones((n, rest), F32)], axis=1)
    sin_h = jnp.concatenate([-sin, sin, jnp.zeros((n, rest), F32)], axis=1)
    return jnp.tile(cos_h, (1, LANES // HEAD_DIM)), jnp.tile(sin_h, (1, LANES // HEAD_DIM))


def _in_projection(x2d, g, w_bf16, pos):
    n = x2d.shape[0]
    tm = PROJ_TILE
    cos_t, sin_t = _rope_tables(pos)
    n_pos_tiles = pos.shape[0] // tm
    row = lambda i: (i, 0)
    const = lambda i: (0, 0)
    tab = lambda i: (i % n_pos_tiles, 0)
    out_shape = (
        jax.ShapeDtypeStruct((n, ATTN_WIDTH), BF16),
        jax.ShapeDtypeStruct((n, 2 * GROUP_WIDTH), F32),
        jax.ShapeDtypeStruct((n, 2 * GROUP_WIDTH), F32),
        jax.ShapeDtypeStruct((n, 2 * GROUP_WIDTH), F32),
        jax.ShapeDtypeStruct((n, SSM_WIDTH), F32),
        jax.ShapeDtypeStruct((n, 2 * D_MODEL), F32),
    )
    return pl.pallas_call(
        _inproj_kernel,
        grid=(n // tm,),
        in_specs=[
            pl.BlockSpec((tm, D_MODEL), row),
            pl.BlockSpec((1, D_MODEL), const),
            pl.BlockSpec((D_MODEL, IN_WIDTH), const, pipeline_mode=pl.Buffered(1)),
            pl.BlockSpec((tm, LANES), tab),
            pl.BlockSpec((tm, LANES), tab),
        ],
        out_specs=(
            pl.BlockSpec((tm, ATTN_WIDTH), row),
            pl.BlockSpec((tm, 2 * GROUP_WIDTH), row),
            pl.BlockSpec((tm, 2 * GROUP_WIDTH), row),
            pl.BlockSpec((tm, 2 * GROUP_WIDTH), row),
            pl.BlockSpec((tm, SSM_WIDTH), row),
            pl.BlockSpec((tm, 2 * D_MODEL), row),
        ),
        out_shape=out_shape,
        compiler_params=_params("parallel"),
        name="in_projection",
    )(x2d, g.reshape(1, D_MODEL), w_bf16, cos_t, sin_t)


def _window_attn_kernel(q_ref, kvc_ref, *rest, has_prev):
    if has_prev:
        kvp_ref, o_ref, lse_ref = rest
    else:
        o_ref, lse_ref = rest
    tile = pl.program_id(2)
    q = q_ref[...]
    k = kvc_ref[:, :GROUP_WIDTH].astype(BF16)
    v = kvc_ref[:, GROUP_WIDTH:].astype(BF16)
    n_keys = ATTN_TILE
    if has_prev:
        k = jnp.concatenate([kvp_ref[:, :GROUP_WIDTH].astype(BF16), k], axis=0)
        v = jnp.concatenate([kvp_ref[:, GROUP_WIDTH:].astype(BF16), v], axis=0)
        n_keys = 2 * ATTN_TILE
    rows = lax.broadcasted_iota(jnp.int32, (ATTN_TILE, n_keys), 0)
    cols = lax.broadcasted_iota(jnp.int32, (ATTN_TILE, n_keys), 1)
    if has_prev:
        valid = (cols >= rows) & (cols <= rows + WINDOW_KEYS) & ((cols >= ATTN_TILE) | (tile > 0))
    else:
        valid = cols <= rows
    low_head = lax.broadcasted_iota(jnp.int32, (ATTN_TILE, LANES), 1) < HEAD_DIM
    for j in range(GROUP_WIDTH // LANES):
        sl = slice(j * LANES, (j + 1) * LANES)
        qj, kj, vj = q[:, sl], k[:, sl], v[:, sl]
        outs, lses = [], []
        for sel in (low_head, jnp.logical_not(low_head)):
            qm = jnp.where(sel, qj, jnp.zeros_like(qj))
            s = lax.dot_general(qm, kj, (((1,), (1,)), ((), ())), preferred_element_type=F32)
            s = jnp.where(valid, s, -jnp.inf)
            m = jnp.max(s, axis=1, keepdims=True)
            p = jnp.exp(s - m)
            l = jnp.sum(p, axis=1, keepdims=True)
            outs.append(jnp.dot(p.astype(BF16), vj, preferred_element_type=F32) / l)
            lses.append(m + jnp.log(l))
        o_ref[:, sl] = jnp.where(low_head, outs[0], outs[1])
        lse_ref[:, sl] = jnp.where(low_head, lses[0], lses[1])


def _prompt_attention(q, kv, group, dilation, batch, seq):
    tg = seq // dilation
    n_tiles = tg // ATTN_TILE
    has_prev = n_tiles > 1
    q3 = q.reshape(batch, tg, dilation * ATTN_WIDTH)
    kv3 = kv.reshape(batch, tg, dilation * 2 * GROUP_WIDTH)
    in_specs = [
        pl.BlockSpec((None, ATTN_TILE, GROUP_WIDTH), lambda b, r, t: (b, t, r * N_ATTN_GROUPS + group)),
        pl.BlockSpec((None, ATTN_TILE, 2 * GROUP_WIDTH), lambda b, r, t: (b, t, r)),
    ]
    args = [q3, kv3]
    if has_prev:
        in_specs.append(pl.BlockSpec((None, ATTN_TILE, 2 * GROUP_WIDTH), lambda b, r, t: (b, jnp.maximum(t - 1, 0), r)))
        args.append(kv3)
    out_spec = pl.BlockSpec((None, ATTN_TILE, GROUP_WIDTH), lambda b, r, t: (b, t, r))
    out_sds = jax.ShapeDtypeStruct((batch, tg, dilation * GROUP_WIDTH), F32)
    o, lse = pl.pallas_call(
        functools.partial(_window_attn_kernel, has_prev=has_prev),
        grid=(batch, dilation, n_tiles),
        in_specs=in_specs,
        out_specs=(out_spec, out_spec),
        out_shape=(out_sds, out_sds),
        compiler_params=_params("parallel", "parallel", "arbitrary"),
        name=f"prompt_attention_g{group}",
    )(*args)
    return o.reshape(batch * seq, GROUP_WIDTH), lse.reshape(batch * seq, GROUP_WIDTH)


def _decode_attn_kernel(q_ref, c0_ref, c1_ref, c2_ref, n0_ref, n1_ref, n2_ref, o_ref, *, dec_seq):
    bt = DEC_BATCH_TILE
    rows_q = dec_seq * HEADS_PER_GROUP
    qf = q_ref[...].astype(F32)
    r_iota = lax.broadcasted_iota(jnp.int32, (rows_q, GROUP_WIDTH), 0)
    l_iota = lax.broadcasted_iota(jnp.int32, (rows_q, GROUP_WIDTH), 1)
    head_mask = (l_iota // HEAD_DIM) == (r_iota % HEADS_PER_GROUP)
    step_of_row = lax.broadcasted_iota(jnp.int32, (rows_q, WINDOW_KEYS), 0) // HEADS_PER_GROUP
    cache_col = lax.broadcasted_iota(jnp.int32, (rows_q, WINDOW_KEYS), 1)
    new_row_step = lax.broadcasted_iota(jnp.int32, (rows_q, SUBLANES), 0) // HEADS_PER_GROUP
    new_col = lax.broadcasted_iota(jnp.int32, (rows_q, SUBLANES), 1)
    nt = (((1,), (1,)), ((), ()))
    cache_refs = (c0_ref, c1_ref, c2_ref)
    new_refs = (n0_ref, n1_ref, n2_ref)
    per_group = SUBLANES // dec_seq
    for bl in range(bt):
        scores, news, values, new_values = [], [], [], []
        base8 = (bl // per_group) * SUBLANES
        own = (new_col // dec_seq) == (bl % per_group)
        for g in range(N_ATTN_GROUPS):
            q_rows = []
            for t in range(dec_seq):
                qrow = qf[bl * dec_seq + t:bl * dec_seq + t + 1, g * GROUP_WIDTH:(g + 1) * GROUP_WIDTH]
                q_rows.append(jnp.broadcast_to(qrow, (HEADS_PER_GROUP, GROUP_WIDTH)))
            qm_f = jnp.where(head_mask, jnp.concatenate(q_rows, axis=0), 0.0)
            qm = qm_f.astype(BF16)
            kv_new = new_refs[g][base8:base8 + SUBLANES, :]
            k_new = kv_new[:, :GROUP_WIDTH].astype(BF16)
            new_values.append(kv_new[:, GROUP_WIDTH:].astype(BF16))
            s_new = lax.dot_general(qm, k_new, nt, preferred_element_type=F32)
            if g == 0:
                ok_new = own & ((new_col % dec_seq) <= new_row_step)
                kc = c0_ref[bl, :, :GROUP_WIDTH].astype(BF16)
                values.append([c0_ref[bl, :, GROUP_WIDTH:].astype(BF16)])
                s = lax.dot_general(qm, kc, nt, preferred_element_type=F32)
                s = jnp.where(cache_col >= step_of_row, s, -jnp.inf)
            else:
                ok_new = own & ((new_col % dec_seq) == new_row_step)
                parts, vals = [], []
                for t in range(dec_seq):
                    off = t * 2 * GROUP_WIDTH
                    kc = cache_refs[g][bl, :, off:off + GROUP_WIDTH].astype(BF16)
                    vals.append(cache_refs[g][bl, :, off + GROUP_WIDTH:off + 2 * GROUP_WIDTH].astype(BF16))
                    qt = qm_f[t * HEADS_PER_GROUP:(t + 1) * HEADS_PER_GROUP, :].astype(BF16)
                    parts.append(lax.dot_general(qt, kc, nt, preferred_element_type=F32))
                values.append(vals)
                s = jnp.concatenate(parts, axis=0)
            scores.append(s)
            news.append(jnp.where(ok_new, s_new, -jnp.inf))
        m = None
        for s in scores + news:
            ms = jnp.max(s, axis=1, keepdims=True)
            m = ms if m is None else jnp.maximum(m, ms)
        denom = jnp.zeros((rows_q, 1), F32)
        acc = jnp.zeros((rows_q, GROUP_WIDTH), F32)
        for g in range(N_ATTN_GROUPS):
            p = jnp.exp(scores[g] - m)
            pn = jnp.exp(news[g] - m)
            denom = denom + jnp.sum(p, axis=1, keepdims=True) + jnp.sum(pn, axis=1, keepdims=True)
            if g == 0:
                acc = acc + jnp.dot(p.astype(BF16), values[0][0], preferred_element_type=F32)
            else:
                parts = []
                for t in range(dec_seq):
                    pt = p[t * HEADS_PER_GROUP:(t + 1) * HEADS_PER_GROUP, :].astype(BF16)
                    parts.append(jnp.dot(pt, values[g][t], preferred_element_type=F32))
                acc = acc + jnp.concatenate(parts, axis=0)
            acc = acc + jnp.dot(pn.astype(BF16), new_values[g], preferred_element_type=F32)
        acc = jnp.where(head_mask, acc / denom, 0.0)
        for t in range(dec_seq):
            o_ref[bl * dec_seq + t:bl * dec_seq + t + 1, :] = jnp.sum(
                acc[t * HEADS_PER_GROUP:(t + 1) * HEADS_PER_GROUP, :], axis=0, keepdims=True)


def _decode_attention(q, caches, new_kv, dec_batch, dec_seq):
    bt = DEC_BATCH_TILE
    row_w = 2 * GROUP_WIDTH
    in_specs = [pl.BlockSpec((bt * dec_seq, ATTN_WIDTH), lambda i: (i, 0))]
    args = [q]
    for (win, dil), cache in zip(ATTN_GROUPS, caches):
        width = row_w if dil == 1 else dec_seq * row_w
        args.append(cache.reshape(dec_batch, WINDOW_KEYS, dil * row_w))
        in_specs.append(pl.BlockSpec((bt, WINDOW_KEYS, width), lambda i: (i, 0, 0)))
    for g in range(N_ATTN_GROUPS):
        args.append(new_kv[g])
        in_specs.append(pl.BlockSpec((bt * dec_seq, row_w), lambda i: (i, 0)))
    return pl.pallas_call(
        functools.partial(_decode_attn_kernel, dec_seq=dec_seq),
        grid=(dec_batch // bt,),
        in_specs=in_specs,
        out_specs=pl.BlockSpec((bt * dec_seq, GROUP_WIDTH), lambda i: (i, 0)),
        out_shape=jax.ShapeDtypeStruct((dec_batch * dec_seq, GROUP_WIDTH), F32),
        compiler_params=_params("parallel"),
        name="decode_attention",
    )(*args)


def _ssm_kernel(u_ref, h0r_ref, h0i_ref, ar_ref, ai_ref, bw_ref, cr_ref, ci_ref, d_ref,
                y_ref, hr_out, hi_out, bur, bui, hr_s, hi_s, *, nb, steps):
    chunk = pl.program_id(0)
    half_in = SSM_WIDTH // 2
    half_st = SSM_COLS // 2

    @pl.when(chunk == 0)
    def _():
        hr_s[...] = h0r_ref[...]
        hi_s[...] = h0i_ref[...]

    u = u_ref[...]
    ub = u.astype(BF16)
    for hf in range(2):
        r = jnp.dot(ub[:, hf * half_in:(hf + 1) * half_in], bw_ref[hf], preferred_element_type=F32)
        bur[:, hf * half_st:(hf + 1) * half_st] = r[:, :half_st]
        bui[:, hf * half_st:(hf + 1) * half_st] = r[:, half_st:]

    lane_chunk = 4 * LANES

    def sub_batch(s, carry):
        for lc in range(SSM_COLS // lane_chunk):
            ls = slice(lc * lane_chunk, (lc + 1) * lane_chunk)
            ar = ar_ref[:, ls]
            ai = ai_ref[:, ls]
            row0 = pl.multiple_of(s * SUBLANES, SUBLANES)

            def step(t, h):
                hr, hi = h
                row = pl.multiple_of(t * nb + s * SUBLANES, SUBLANES)
                nhr = ar * hr - ai * hi + bur[pl.ds(row, SUBLANES), ls]
                nhi = ar * hi + ai * hr + bui[pl.ds(row, SUBLANES), ls]
                bur[pl.ds(row, SUBLANES), ls] = nhr
                bui[pl.ds(row, SUBLANES), ls] = nhi
                return nhr, nhi

            hr, hi = lax.fori_loop(0, steps, step, (hr_s[pl.ds(row0, SUBLANES), ls], hi_s[pl.ds(row0, SUBLANES), ls]))
            hr_s[pl.ds(row0, SUBLANES), ls] = hr
            hi_s[pl.ds(row0, SUBLANES), ls] = hi
        return carry

    lax.fori_loop(0, nb // SUBLANES, sub_batch, 0)

    for hf in range(2):
        ss = slice(hf * half_st, (hf + 1) * half_st)
        y = jnp.dot(bur[:, ss].astype(BF16), cr_ref[hf], preferred_element_type=F32)
        y = y + jnp.dot(bui[:, ss].astype(BF16), ci_ref[hf], preferred_element_type=F32)
        cs = slice(hf * half_in, (hf + 1) * half_in)
        y_ref[:, cs] = y + d_ref[:, cs] * u[:, cs]

    @pl.when(chunk == pl.num_programs(0) - 1)
    def _():
        hr_out[...] = hr_s[...]
        hi_out[...] = hi_s[...]


def _ssm_weights(ssm_log_dt, a_re, a_im, b_re, b_im, c_re, c_im):
    dt = jnp.exp(ssm_log_dt.astype(F32))[:, None]
    lam = lax.complex(a_re.astype(F32), a_im.astype(F32))
    lam_bar = jnp.exp(lam * dt)
    b_bar = ((lam_bar - 1.0) / lam)[:, :, None] * lax.complex(b_re.astype(F32), b_im.astype(F32))
    gh = N_SSM_GROUPS // 2
    eye = jnp.eye(gh, dtype=F32)

    def block_diag(m):
        return jnp.einsum('gab,gh->gahb', m, eye).reshape(gh * m.shape[1], gh * m.shape[2])

    bw, cr, ci = [], [], []
    for hf in range(2):
        gs = slice(hf * gh, (hf + 1) * gh)
        b_t = jnp.transpose(b_bar[gs], (0, 2, 1))
        bw.append(jnp.concatenate([block_diag(jnp.real(b_t)), block_diag(jnp.imag(b_t))], axis=1))
        cr.append(block_diag(jnp.transpose(c_re[gs].astype(F32), (0, 2, 1))))
        ci.append(block_diag(jnp.transpose(-c_im[gs].astype(F32), (0, 2, 1))))
    ar = jnp.broadcast_to(jnp.real(lam_bar).reshape(1, SSM_COLS), (SUBLANES, SSM_COLS))
    ai = jnp.broadcast_to(jnp.imag(lam_bar).reshape(1, SSM_COLS), (SUBLANES, SSM_COLS))
    return ar, ai, jnp.stack(bw).astype(BF16), jnp.stack(cr).astype(BF16), jnp.stack(ci).astype(BF16)


def _ssm(u_tm, h0r, h0i, weights, d_flat, nb):
    ar, ai, bw, cr, ci = weights
    rows = u_tm.shape[0]
    steps = SSM_ROWS // nb
    const2 = lambda c: (0, 0)
    const3 = lambda c: (0, 0, 0)
    state_sds = jax.ShapeDtypeStruct((nb, SSM_COLS), F32)
    return pl.pallas_call(
        functools.partial(_ssm_kernel, nb=nb, steps=steps),
        grid=(rows // SSM_ROWS,),
        in_specs=[
            pl.BlockSpec((SSM_ROWS, SSM_WIDTH), lambda c: (c, 0)),
            pl.BlockSpec((nb, SSM_COLS), const2),
            pl.BlockSpec((nb, SSM_COLS), const2),
            pl.BlockSpec((SUBLANES, SSM_COLS), const2),
            pl.BlockSpec((SUBLANES, SSM_COLS), const2),
            pl.BlockSpec(bw.shape, const3),
            pl.BlockSpec(cr.shape, const3),
            pl.BlockSpec(ci.shape, const3),
            pl.BlockSpec((1, SSM_WIDTH), const2),
        ],
        out_specs=(
            pl.BlockSpec((SSM_ROWS, SSM_WIDTH), lambda c: (c, 0)),
            pl.BlockSpec((nb, SSM_COLS), const2),
            pl.BlockSpec((nb, SSM_COLS), const2),
        ),
        out_shape=(jax.ShapeDtypeStruct((rows, SSM_WIDTH), F32), state_sds, state_sds),
        scratch_shapes=[
            pltpu.VMEM((SSM_ROWS, SSM_COLS), F32),
            pltpu.VMEM((SSM_ROWS, SSM_COLS), F32),
            pltpu.VMEM((nb, SSM_COLS), F32),
            pltpu.VMEM((nb, SSM_COLS), F32),
        ],
        compiler_params=_params("arbitrary"),
        name="s5_scan",
    )(u_tm, h0r, h0i, ar, ai, bw, cr, ci, d_flat)


def _mix_kernel(*refs, merged_attn):
    if merged_attn:
        attn_ref = refs[0]
        rest = refs[1:]
    else:
        o_refs, l_refs = refs[0:3], refs[3:6]
        rest = refs[6:]
    (y_ref, gate_ref, x_ref, wa_ref, wglu_ref, wout_ref, gffn_ref, wr_ref, br_ref,
     h_ref, hn_ref, route_ref) = rest
    if merged_attn:
        attn = attn_ref[...]
    else:
        ls = [r[...] for r in l_refs]
        top = jnp.maximum(jnp.maximum(ls[0], ls[1]), ls[2])
        es = [jnp.exp(l - top) for l in ls]
        attn = (es[0] * o_refs[0][...] + es[1] * o_refs[1][...] + es[2] * o_refs[2][...]) / (es[0] + es[1] + es[2])
    attn_out = jnp.dot(attn.astype(BF16), wa_ref[...], preferred_element_type=F32)
    glu = jnp.dot(jax.nn.gelu(y_ref[...]).astype(BF16), wglu_ref[...], preferred_element_type=F32)
    ssm_out = glu[:, :D_MODEL] * jax.nn.sigmoid(glu[:, D_MODEL:])
    merged = jax.nn.sigmoid(gate_ref[:, :D_MODEL]) * attn_out + jax.nn.sigmoid(gate_ref[:, D_MODEL:]) * ssm_out
    h = x_ref[...] + jnp.dot(merged.astype(BF16), wout_ref[...], preferred_element_type=F32)
    h_ref[...] = h
    var = jnp.mean(h * h, axis=-1, keepdims=True)
    hn = h * lax.rsqrt(var + NORM_EPS) * gffn_ref[...]
    hn_ref[...] = hn
    logits = jnp.dot(hn, wr_ref[...], preferred_element_type=F32, precision=lax.Precision.HIGHEST) + br_ref[...]
    lane = lax.broadcasted_iota(jnp.int32, logits.shape, 1).astype(F32)
    far = float(LANES)

    def first_argmax(vals):
        top_v = jnp.max(vals, axis=1, keepdims=True)
        return top_v, jnp.min(jnp.where(vals == top_v, lane, far), axis=1, keepdims=True)

    group_logits = jnp.where(lane < N_EXPERT_GROUPS, logits, -jnp.inf)
    g_top, g_idx = first_argmax(group_logits)
    p_group = 1.0 / jnp.sum(jnp.exp(group_logits - g_top), axis=1, keepdims=True)
    first_lane = N_EXPERT_GROUPS + g_idx * EXPERTS_PER_GROUP
    in_group = (lane >= first_lane) & (lane < first_lane + EXPERTS_PER_GROUP)
    expert_logits = jnp.where(in_group, logits, -jnp.inf)
    v1, i1 = first_argmax(expert_logits)
    v2, i2 = first_argmax(jnp.where(lane == i1, -jnp.inf, expert_logits))
    e2 = jnp.exp(v2 - v1)
    w1 = p_group / (1.0 + e2)
    w2 = p_group * e2 / (1.0 + e2)
    route = jnp.where(lane == 0, i1 - N_EXPERT_GROUPS,
                      jnp.where(lane == 1, i2 - N_EXPERT_GROUPS,
                                jnp.where(lane == 2, w1, jnp.where(lane == 3, w2, 0.0))))
    route_ref[...] = route


def _mix(attn_inputs, y, gates, x2d, wa, wglu, wout, gffn, wr, br):
    n = x2d.shape[0]
    tm = PROJ_TILE
    row = lambda i: (i, 0)
    const = lambda i: (0, 0)
    merged_attn = len(attn_inputs) == 1
    in_specs = [pl.BlockSpec((tm, GROUP_WIDTH), row) for _ in attn_inputs]
    in_specs += [
        pl.BlockSpec((tm, SSM_WIDTH), row),
        pl.BlockSpec((tm, 2 * D_MODEL), row),
        pl.BlockSpec((tm, D_MODEL), row),
        pl.BlockSpec(wa.shape, const),
        pl.BlockSpec(wglu.shape, const),
        pl.BlockSpec(wout.shape, const),
        pl.BlockSpec((1, D_MODEL), const),
        pl.BlockSpec(wr.shape, const),
        pl.BlockSpec((1, LANES), const),
    ]
    return pl.pallas_call(
        functools.partial(_mix_kernel, merged_attn=merged_attn),
        grid=(n // tm,),
        in_specs=in_specs,
        out_specs=(pl.BlockSpec((tm, D_MODEL), row), pl.BlockSpec((tm, D_MODEL), row), pl.BlockSpec((tm, LANES), row)),
        out_shape=(jax.ShapeDtypeStruct((n, D_MODEL), F32), jax.ShapeDtypeStruct((n, D_MODEL), F32),
                   jax.ShapeDtypeStruct((n, LANES), F32)),
        compiler_params=_params("parallel"),
        name="branch_mix",
    )(*attn_inputs, y, gates, x2d, wa, wglu, wout, gffn, wr, br)


def _row_copy(src, src_row, dst, dst_row, sem):
    return pltpu.make_async_copy(src.at[pl.ds(src_row, 1)], dst.at[pl.ds(dst_row, 1)], sem)


def _dispatch_kernel(slot_ref, hn_hbm, xs_in_hbm, xs_hbm, sem):
    del xs_in_hbm
    base = pl.program_id(0) * ROW_TILE

    def issue(j, carry):
        _row_copy(hn_hbm, base + j, xs_hbm, slot_ref[2 * j], sem).start()
        _row_copy(hn_hbm, base + j, xs_hbm, slot_ref[2 * j + 1], sem).start()
        return carry

    def drain(j, carry):
        _row_copy(hn_hbm, 0, xs_hbm, 0, sem).wait()
        _row_copy(hn_hbm, 0, xs_hbm, 0, sem).wait()
        return carry

    lax.fori_loop(0, ROW_TILE, issue, 0)
    lax.fori_loop(0, ROW_TILE, drain, 0)


def _dispatch(slots, hn, n_slots):
    n = hn.shape[0]
    xs0 = jnp.zeros((n_slots, D_MODEL), F32)
    return pl.pallas_call(
        _dispatch_kernel,
        grid=(n // ROW_TILE,),
        in_specs=[
            pl.BlockSpec((2 * ROW_TILE,), lambda i: (i,), memory_space=pltpu.SMEM),
            pl.BlockSpec(memory_space=pl.ANY),
            pl.BlockSpec(memory_space=pl.ANY),
        ],
        out_specs=pl.BlockSpec(memory_space=pl.ANY),
        out_shape=jax.ShapeDtypeStruct((n_slots, D_MODEL), F32),
        scratch_shapes=[pltpu.SemaphoreType.DMA(())],
        input_output_aliases={2: 0},
        compiler_params=_params("arbitrary"),
        name="moe_dispatch",
    )(slots, hn, xs0)


def _expert_kernel(block_e_ref, n_used_ref, xs_ref, wg_ref, wu_ref, wd_ref, yb_ref):
    del block_e_ref
    i = pl.program_id(0)

    @pl.when(i < n_used_ref[0])
    def _():
        xb = xs_ref[...].astype(BF16)
        gate = jnp.dot(xb, wg_ref[...], preferred_element_type=F32)
        up = jnp.dot(xb, wu_ref[...], preferred_element_type=F32)
        hmid = (jax.nn.silu(gate) * up).astype(BF16)
        yb_ref[...] = jnp.dot(hmid, wd_ref[...], preferred_element_type=F32)

    @pl.when(i >= n_used_ref[0])
    def _():
        yb_ref[...] = jnp.zeros_like(yb_ref)


def _experts(block_e, n_used, xs, wg, wu, wd):
    n_blocks = xs.shape[0] // MOE_BLOCK
    grid_spec = pltpu.PrefetchScalarGridSpec(
        num_scalar_prefetch=2,
        grid=(n_blocks,),
        in_specs=[
            pl.BlockSpec((MOE_BLOCK, D_MODEL), lambda i, be, nu: (i, 0)),
            pl.BlockSpec((None, D_MODEL, D_EXPERT), lambda i, be, nu: (be[i], 0, 0)),
            pl.BlockSpec((None, D_MODEL, D_EXPERT), lambda i, be, nu: (be[i], 0, 0)),
            pl.BlockSpec((None, D_EXPERT, D_MODEL), lambda i, be, nu: (be[i], 0, 0)),
        ],
        out_specs=pl.BlockSpec((MOE_BLOCK, D_MODEL), lambda i, be, nu: (i, 0)),
    )
    return pl.pallas_call(
        _expert_kernel,
        grid_spec=grid_spec,
        out_shape=jax.ShapeDtypeStruct(xs.shape, F32),
        compiler_params=_params("arbitrary"),
        name="moe_experts",
    )(block_e, n_used, xs, wg, wu, wd)


def _combine_kernel(slot_ref, h_ref, route_ref, g_ref, yb_hbm, out_ref, buf_a, buf_b, sem):
    def issue(j, carry):
        _row_copy(yb_hbm, slot_ref[2 * j], buf_a, j, sem).start()
        _row_copy(yb_hbm, slot_ref[2 * j + 1], buf_b, j, sem).start()
        return carry

    def drain(j, carry):
        _row_copy(yb_hbm, 0, buf_a, 0, sem).wait()
        _row_copy(yb_hbm, 0, buf_b, 0, sem).wait()
        return carry

    lax.fori_loop(0, ROW_TILE, issue, 0)
    lax.fori_loop(0, ROW_TILE, drain, 0)
    route = route_ref[...]
    h = h_ref[...] + (route[:, 2:3] * buf_a[...] + route[:, 3:4] * buf_b[...])
    var = jnp.mean(h * h, axis=-1, keepdims=True)
    out_ref[...] = h * lax.rsqrt(var + NORM_EPS) * g_ref[...]


def _combine(slots, h, route, g_final, yb):
    n = h.shape[0]
    row = lambda i: (i, 0)
    return pl.pallas_call(
        _combine_kernel,
        grid=(n // ROW_TILE,),
        in_specs=[
            pl.BlockSpec((2 * ROW_TILE,), lambda i: (i,), memory_space=pltpu.SMEM),
            pl.BlockSpec((ROW_TILE, D_MODEL), row),
            pl.BlockSpec((ROW_TILE, LANES), row),
            pl.BlockSpec((1, D_MODEL), lambda i: (0, 0)),
            pl.BlockSpec(memory_space=pl.ANY),
        ],
        out_specs=pl.BlockSpec((ROW_TILE, D_MODEL), row),
        out_shape=jax.ShapeDtypeStruct((n, D_MODEL), F32),
        scratch_shapes=[
            pltpu.VMEM((ROW_TILE, D_MODEL), F32),
            pltpu.VMEM((ROW_TILE, D_MODEL), F32),
            pltpu.SemaphoreType.DMA(()),
        ],
        compiler_params=_params("arbitrary"),
        name="moe_combine",
    )(slots, h, route, g_final.reshape(1, D_MODEL), yb)


def _slot_assignment(route, n_blocks):
    flat_e = route[:, 0:2].astype(jnp.int32).reshape(-1)
    onehot = (flat_e[:, None] == jnp.arange(N_EXPERTS, dtype=jnp.int32)[None, :]).astype(jnp.int32)
    running = jnp.cumsum(onehot, axis=0)
    rank = jnp.sum(onehot * running, axis=1) - 1
    counts = running[-1]
    padded = ((counts + MOE_BLOCK - 1) // MOE_BLOCK) * MOE_BLOCK
    pad_end = jnp.cumsum(padded)
    pad_start = pad_end - padded
    slots = jnp.sum(onehot * pad_start[None, :], axis=1) + rank
    block_start = jnp.arange(n_blocks, dtype=jnp.int32) * MOE_BLOCK
    block_e = jnp.clip(jnp.searchsorted(pad_end, block_start, side='right'), 0, N_EXPERTS - 1).astype(jnp.int32)
    n_used = (pad_end[-1:] // MOE_BLOCK).astype(jnp.int32)
    return slots.astype(jnp.int32), block_e, n_used


def _moe_and_final_norm(h, hn, route, wg, wu, wd, g_final):
    n = h.shape[0]
    n_blocks = (2 * n) // MOE_BLOCK + N_EXPERTS
    slots, block_e, n_used = _slot_assignment(route, n_blocks)
    xs = _dispatch(slots, hn, n_blocks * MOE_BLOCK)
    yb = _experts(block_e, n_used, xs, wg, wu, wd)
    return _combine(slots, h, route, g_final, yb)


def kernel(x_prompt, x_sample, cache_kv_w128, cache_kv_w512, cache_kv_w2048, state_ssm, g_attn_norm, w_in, ssm_log_dt, ssm_a_re, ssm_a_im, ssm_b_re, ssm_b_im, ssm_c_re, ssm_c_im, ssm_d, w_glu, w_attn_branch, w_out, g_ffn_norm, w_router_group, b_router_group, w_router_expert, b_router_expert, w_exp_gate, w_exp_up, w_exp_down, g_final):
    batch, seq, _ = x_prompt.shape
    dec_batch, dec_seq, _ = x_sample.shape
    past_len = cache_kv_w2048.shape[2]
    layer = 0

    w_in_b = w_in[layer].astype(BF16)
    wa = w_attn_branch[layer].astype(BF16)
    wglu = w_glu[layer].astype(BF16)
    wout = w_out[layer].astype(BF16)
    wg = w_exp_gate[layer].astype(BF16)
    wu = w_exp_up[layer].astype(BF16)
    wd = w_exp_down[layer].astype(BF16)
    gffn = g_ffn_norm[layer].reshape(1, D_MODEL)
    pad = LANES - N_EXPERT_GROUPS - N_EXPERTS
    wr = jnp.concatenate([w_router_group[layer], w_router_expert[layer], jnp.zeros((D_MODEL, pad), F32)], axis=1)
    br = jnp.concatenate([b_router_group[layer], b_router_expert[layer], jnp.zeros((pad,), F32)]).reshape(1, LANES)
    ssm_w = _ssm_weights(ssm_log_dt[layer], ssm_a_re[layer], ssm_a_im[layer], ssm_b_re[layer], ssm_b_im[layer],
                         ssm_c_re[layer], ssm_c_im[layer])
    d_flat = ssm_d[layer].astype(F32).reshape(1, SSM_WIDTH)

    def time_major(a, nb, steps):
        return jnp.transpose(a.reshape(nb, steps, -1), (1, 0, 2)).reshape(nb * steps, -1)

    def batch_major(a, nb, steps):
        return jnp.transpose(a.reshape(steps, nb, -1), (1, 0, 2)).reshape(nb * steps, -1)

    xp = x_prompt.reshape(batch * seq, D_MODEL)
    q_p, kv0_p, kv1_p, kv2_p, u_p, gates_p = _in_projection(xp, g_attn_norm[layer], w_in_b, jnp.arange(seq, dtype=jnp.int32))
    kv_p = (kv0_p, kv1_p, kv2_p)
    attn_in = []
    lse_in = []
    for g, (win, dil) in enumerate(ATTN_GROUPS):
        o, lse = _prompt_attention(q_p, kv_p[g], g, dil, batch, seq)
        attn_in.append(o)
        lse_in.append(lse)
    zeros_state = jnp.zeros((batch, SSM_COLS), F32)
    y_tm, hr_p, hi_p = _ssm(time_major(u_p, batch, seq), zeros_state, zeros_state, ssm_w, d_flat, batch)
    y_p = batch_major(y_tm, batch, seq)
    h_p, hn_p, route_p = _mix(attn_in + lse_in, y_p, gates_p, xp, wa, wglu, wout, gffn, wr, br)
    out_p = _moe_and_final_norm(h_p, hn_p, route_p, wg, wu, wd, g_final)

    xs = x_sample.reshape(dec_batch * dec_seq, D_MODEL)
    pos_s = past_len + (jnp.arange(dec_batch * dec_seq, dtype=jnp.int32) % dec_seq)
    q_s, kv0_s, kv1_s, kv2_s, u_s, gates_s = _in_projection(xs, g_attn_norm[layer], w_in_b, pos_s)
    kv_s = (kv0_s, kv1_s, kv2_s)
    caches = (cache_kv_w128[layer], cache_kv_w512[layer], cache_kv_w2048[layer])
    attn_s = _decode_attention(q_s, caches, kv_s, dec_batch, dec_seq)
    st = state_ssm[layer].astype(F32).reshape(dec_batch, SSM_COLS, 2)
    ys_tm, hr_s, hi_s = _ssm(time_major(u_s, dec_batch, dec_seq), st[:, :, 0], st[:, :, 1], ssm_w, d_flat, dec_batch)
    y_s = batch_major(ys_tm, dec_batch, dec_seq)
    h_s, hn_s, route_s = _mix([attn_s], y_s, gates_s, xs, wa, wglu, wout, gffn, wr, br)
    out_s = _moe_and_final_norm(h_s, hn_s, route_s, wg, wu, wd, g_final)

    kv_tail = (2, HEADS_PER_GROUP, HEAD_DIM)
    outs = [out_p.reshape(batch, seq, D_MODEL), out_s.reshape(dec_batch, dec_seq, D_MODEL)]
    for g, (win, dil) in enumerate(ATTN_GROUPS):
        keep = min(win, seq)
        rows_p = kv_p[g].reshape((batch, seq) + kv_tail)[:, seq - keep:]
        outs.append(rows_p[None])
        outs.append(kv_s[g].reshape((1, dec_batch, dec_seq) + kv_tail))
    outs.append(jnp.stack([hr_p, hi_p], axis=-1).reshape(1, batch, N_SSM_GROUPS, SSM_STATE, 2))
    outs.append(jnp.stack([hr_s, hi_s], axis=-1).reshape(1, dec_batch, N_SSM_GROUPS, SSM_STATE, 2))
    return tuple(outs)
```

```python
import functools
import math

import jax
import jax.numpy as jnp
from jax import lax
from jax.experimental import pallas as pl
from jax.experimental.pallas import tpu as pltpu

F32 = jnp.float32
BF16 = jnp.bfloat16

D_MODEL = 1024
HEAD_DIM = 64
HEADS_PER_GROUP = 8
GROUP_WIDTH = HEADS_PER_GROUP * HEAD_DIM
ATTN_GROUPS = ((128, 1), (512, 4), (2048, 16))
N_ATTN_GROUPS = len(ATTN_GROUPS)
ATTN_WIDTH = N_ATTN_GROUPS * GROUP_WIDTH
ROT_DIM = HEAD_DIM // 4
ROPE_THETA = 500000.0
WINDOW_KEYS = 128
SSM_GROUP_CH = 16
SSM_WIDTH = D_MODEL // 2
N_SSM_GROUPS = SSM_WIDTH // SSM_GROUP_CH
SSM_STATE = 64
SSM_COLS = N_SSM_GROUPS * SSM_STATE
IN_WIDTH = 3 * ATTN_WIDTH + SSM_WIDTH + 2 * D_MODEL
N_EXPERT_GROUPS = 4
EXPERTS_PER_GROUP = 8
N_EXPERTS = N_EXPERT_GROUPS * EXPERTS_PER_GROUP
D_EXPERT = D_MODEL // 4
NORM_EPS = 1e-6

LANES = 128
SUBLANES = 8
VMEM_LIMIT = 56 * 1024 * 1024

PROJ_TILE = 256
ATTN_TILE = 128
SSM_ROWS = 512
MOE_BLOCK = 256
ROW_TILE = 512
DMA_UNROLL = 8


def _params(*sem):
    return pltpu.CompilerParams(dimension_semantics=sem, vmem_limit_bytes=VMEM_LIMIT)


def _inproj_kernel(x_ref, g_ref, w_ref, cos_ref, sin_ref, q_ref, kv0_ref, kv1_ref, kv2_ref, u_ref, gate_ref):
    x = x_ref[...]
    var = jnp.mean(x * x, axis=-1, keepdims=True)
    xn = (x * lax.rsqrt(var + NORM_EPS) * g_ref[...]).astype(BF16)
    cos = cos_ref[...]
    sin = sin_ref[...]
    lane = lax.broadcasted_iota(jnp.int32, cos.shape, 1) % HEAD_DIM
    first_half = lane < ROT_DIM // 2
    rotated = lane < ROT_DIM

    def rope_chunk(c):
        partner = jnp.where(first_half, pltpu.roll(c, LANES - ROT_DIM // 2, 1), pltpu.roll(c, ROT_DIM // 2, 1))
        return jnp.where(rotated, c * cos + partner * sin, c)

    kv_refs = (kv0_ref, kv1_ref, kv2_ref)
    n_tiles = IN_WIDTH // GROUP_WIDTH
    for c in range(n_tiles):
        acc = jnp.dot(xn, w_ref[:, c * GROUP_WIDTH:(c + 1) * GROUP_WIDTH], preferred_element_type=F32)
        if c < 3:
            for j in range(GROUP_WIDTH // LANES):
                r = rope_chunk(acc[:, j * LANES:(j + 1) * LANES]) * (HEAD_DIM ** -0.5)
                q_ref[:, c * GROUP_WIDTH + j * LANES:c * GROUP_WIDTH + (j + 1) * LANES] = r.astype(BF16)
        elif c < 6:
            for j in range(GROUP_WIDTH // LANES):
                kv_refs[c - 3][:, j * LANES:(j + 1) * LANES] = rope_chunk(acc[:, j * LANES:(j + 1) * LANES])
        elif c < 9:
            kv_refs[c - 6][:, GROUP_WIDTH:] = acc
        elif c == 9:
            u_ref[...] = acc
        else:
            gate_ref[:, (c - 10) * GROUP_WIDTH:(c - 9) * GROUP_WIDTH] = acc


def _rope_tables(pos):
    half = ROT_DIM // 2
    inv_freq = ROPE_THETA ** (-(jnp.arange(half, dtype=F32) / half))
    ang = pos.astype(F32)[:, None] * inv_freq[None, :]
    cos, sin = jnp.cos(ang), jnp.sin(ang)
    n = pos.shape[0]
    rest = HEAD_DIM - ROT_DIM
    cos_h = jnp.concatenate([cos, cos, jnp.ones((n, rest), F32)], axis=1)
    sin_h = jnp.concatenate([-sin, sin, jnp.zeros((n, rest), F32)], axis=1)
    return jnp.tile(cos_h, (1, LANES // HEAD_DIM)), jnp.tile(sin_h, (1, LANES // HEAD_DIM))


def _in_projection(x2d, g, w_bf16, pos):
    n = x2d.shape[0]
    tm = PROJ_TILE
    cos_t, sin_t = _rope_tables(pos)
    n_pos_tiles = pos.shape[0] // tm
    row = lambda i: (i, 0)
    const = lambda i: (0, 0)
    tab = lambda i: (i % n_pos_tiles, 0)
    out_shape = (
        jax.ShapeDtypeStruct((n, ATTN_WIDTH), BF16),
        jax.ShapeDtypeStruct((n, 2 * GROUP_WIDTH), F32),
        jax.ShapeDtypeStruct((n, 2 * GROUP_WIDTH), F32),
        jax.ShapeDtypeStruct((n, 2 * GROUP_WIDTH), F32),
        jax.ShapeDtypeStruct((n, SSM_WIDTH), F32),
        jax.ShapeDtypeStruct((n, 2 * D_MODEL), F32),
    )
    return pl.pallas_call(
        _inproj_kernel,
        grid=(n // tm,),
        in_specs=[
            pl.BlockSpec((tm, D_MODEL), row),
            pl.BlockSpec((1, D_MODEL), const),
            pl.BlockSpec((D_MODEL, IN_WIDTH), const, pipeline_mode=pl.Buffered(1)),
            pl.BlockSpec((tm, LANES), tab),
            pl.BlockSpec((tm, LANES), tab),
        ],
        out_specs=(
            pl.BlockSpec((tm, ATTN_WIDTH), row),
            pl.BlockSpec((tm, 2 * GROUP_WIDTH), row),
            pl.BlockSpec((tm, 2 * GROUP_WIDTH), row),
            pl.BlockSpec((tm, 2 * GROUP_WIDTH), row),
            pl.BlockSpec((tm, SSM_WIDTH), row),
            pl.BlockSpec((tm, 2 * D_MODEL), row),
        ),
        out_shape=out_shape,
        compiler_params=_params("parallel"),
        name="in_projection",
    )(x2d, g.reshape(1, D_MODEL), w_bf16, cos_t, sin_t)


def _window_attn_kernel(q_ref, kvc_ref, *rest, has_prev):
    if has_prev:
        kvp_ref, o_ref, lse_ref = rest
    else:
        o_ref, lse_ref = rest
    tile = pl.program_id(2)
    q = q_ref[...]
    k = kvc_ref[:, :GROUP_WIDTH].astype(BF16)
    v = kvc_ref[:, GROUP_WIDTH:].astype(BF16)
    n_keys = ATTN_TILE
    if has_prev:
        k = jnp.concatenate([kvp_ref[:, :GROUP_WIDTH].astype(BF16), k], axis=0)
        v = jnp.concatenate([kvp_ref[:, GROUP_WIDTH:].astype(BF16), v], axis=0)
        n_keys = 2 * ATTN_TILE
    rows = lax.broadcasted_iota(jnp.int32, (ATTN_TILE, n_keys), 0)
    cols = lax.broadcasted_iota(jnp.int32, (ATTN_TILE, n_keys), 1)
    if has_prev:
        valid = (cols >= rows) & (cols <= rows + WINDOW_KEYS) & ((cols >= ATTN_TILE) | (tile > 0))
    else:
        valid = cols <= rows
    low_head = lax.broadcasted_iota(jnp.int32, (ATTN_TILE, LANES), 1) < HEAD_DIM
    for j in range(GROUP_WIDTH // LANES):
        sl = slice(j * LANES, (j + 1) * LANES)
        qj, kj, vj = q[:, sl], k[:, sl], v[:, sl]
        outs, lses = [], []
        for sel in (low_head, jnp.logical_not(low_head)):
            qm = jnp.where(sel, qj, jnp.zeros_like(qj))
            s = lax.dot_general(qm, kj, (((1,), (1,)), ((), ())), preferred_element_type=F32)
            s = jnp.where(valid, s, -jnp.inf)
            m = jnp.max(s, axis=1, keepdims=True)
            p = jnp.exp(s - m)
            l = jnp.sum(p, axis=1, keepdims=True)
            outs.append(jnp.dot(p.astype(BF16), vj, preferred_element_type=F32) / l)
            lses.append(m + jnp.log(l))
        o_ref[:, sl] = jnp.where(low_head, outs[0], outs[1])
        lse_ref[:, sl] = jnp.where(low_head, lses[0], lses[1])


def _prompt_attention(q, kv, group, dilation, batch, seq):
    tg = seq // dilation
    n_tiles = tg // ATTN_TILE
    has_prev = n_tiles > 1
    q3 = q.reshape(batch, tg, dilation * ATTN_WIDTH)
    kv3 = kv.reshape(batch, tg, dilation * 2 * GROUP_WIDTH)
    in_specs = [
        pl.BlockSpec((None, ATTN_TILE, GROUP_WIDTH), lambda b, r, t: (b, t, r * N_ATTN_GROUPS + group)),
        pl.BlockSpec((None, ATTN_TILE, 2 * GROUP_WIDTH), lambda b, r, t: (b, t, r)),
    ]
    args = [q3, kv3]
    if has_prev:
        in_specs.append(pl.BlockSpec((None, ATTN_TILE, 2 * GROUP_WIDTH), lambda b, r, t: (b, jnp.maximum(t - 1, 0), r)))
        args.append(kv3)
    out_spec = pl.BlockSpec((None, ATTN_TILE, GROUP_WIDTH), lambda b, r, t: (b, t, r))
    out_sds = jax.ShapeDtypeStruct((batch, tg, dilation * GROUP_WIDTH), F32)
    o, lse = pl.pallas_call(
        functools.partial(_window_attn_kernel, has_prev=has_prev),
        grid=(batch, dilation, n_tiles),
        in_specs=in_specs,
        out_specs=(out_spec, out_spec),
        out_shape=(out_sds, out_sds),
        compiler_params=_params("parallel", "parallel", "arbitrary"),
        name=f"prompt_attention_g{group}",
    )(*args)
    return o.reshape(batch * seq, GROUP_WIDTH), lse.reshape(batch * seq, GROUP_WIDTH)


def _sublane_total(x):
    x = x + pltpu.roll(x, 4, 0)
    x = x + pltpu.roll(x, 2, 0)
    return x + pltpu.roll(x, 1, 0)


def _head_sum(prod):
    width = prod.shape[1]
    row = lax.broadcasted_iota(jnp.int32, (HEADS_PER_GROUP, width), 0)
    out = jnp.zeros((HEADS_PER_GROUP, width), F32)
    for h in range(HEADS_PER_GROUP):
        part = prod[h * HEAD_DIM:h * HEAD_DIM + SUBLANES]
        for j in range(1, HEAD_DIM // SUBLANES):
            part = part + prod[h * HEAD_DIM + j * SUBLANES:h * HEAD_DIM + (j + 1) * SUBLANES]
        out = jnp.where(row == h, _sublane_total(part), out)
    return out


def _head_expand(x):
    width = x.shape[1]
    pieces = []
    for h in range(HEADS_PER_GROUP):
        pieces.extend([jnp.broadcast_to(x[h:h + 1, :], (SUBLANES, width))] * (HEAD_DIM // SUBLANES))
    return jnp.concatenate(pieces, axis=0)


def _split_dot(acc, sel):
    hi = acc.astype(BF16)
    lo = (acc - hi.astype(F32)).astype(BF16)
    return jnp.dot(hi, sel, preferred_element_type=F32) + jnp.dot(lo, sel, preferred_element_type=F32)


def _decode_attn_kernel(q_ref, n0_ref, n1_ref, n2_ref, c0_ref, c1_ref, c2_ref, o_ref, *, dec_seq):
    nq = SUBLANES
    n_sub = HEAD_DIM // SUBLANES
    neg = -jnp.inf
    q_t = q_ref[...].T
    step = lax.broadcasted_iota(jnp.int32, (HEADS_PER_GROUP, nq), 1)
    step_wide = lax.broadcasted_iota(jnp.int32, (GROUP_WIDTH, nq), 1)
    real_step = step < dec_seq
    new_refs = (n0_ref, n1_ref, n2_ref)
    cache_refs = (c0_ref, c1_ref, c2_ref)

    def column(x, t, width):
        return jnp.broadcast_to(x[:, t:t + 1], (x.shape[0], width))

    def place(cols):
        rows = cols[0].shape[0]
        lane = lax.broadcasted_iota(jnp.int32, (rows, nq), 1)
        out = jnp.zeros((rows, nq), F32)
        for t, c in enumerate(cols):
            out = jnp.where(lane == t, jnp.broadcast_to(c, (rows, nq)), out)
        return out

    outs, lses = [], []
    for g, (win, dil) in enumerate(ATTN_GROUPS):
        c_ref = cache_refs[g]
        kv_new = new_refs[g][...]
        kn_t = kv_new[:, :GROUP_WIDTH].T
        vn_t = kv_new[:, GROUP_WIDTH:].T
        q_g = q_t[g * GROUP_WIDTH:(g + 1) * GROUP_WIDTH, :]
        pos = lax.broadcasted_iota(jnp.int32, (HEADS_PER_GROUP, win), 1)
        row8 = lax.broadcasted_iota(jnp.int32, (HEADS_PER_GROUP, win), 0)
        n_tiles = win // LANES

        def cache_scores(q_pat):
            s = jnp.zeros((HEADS_PER_GROUP, win), F32)
            for h in range(HEADS_PER_GROUP):
                part = None
                for j in range(n_sub):
                    r0 = h * HEAD_DIM + j * SUBLANES
                    qp = q_pat[r0:r0 + SUBLANES, :]
                    term = c_ref[0, r0:r0 + SUBLANES, :] * jnp.concatenate([qp] * n_tiles, axis=1)
                    part = term if part is None else part + term
                s = jnp.where(row8 == h, _sublane_total(part), s)
            return s

        def weighted_values(p):
            rows = []
            for h in range(HEADS_PER_GROUP):
                ph = jnp.broadcast_to(p[h:h + 1, :], (SUBLANES, win))
                for j in range(n_sub):
                    r0 = h * HEAD_DIM + j * SUBLANES
                    prod = ph * c_ref[1, r0:r0 + SUBLANES, :]
                    a = prod[:, :LANES]
                    for tile in range(1, n_tiles):
                        a = a + prod[:, tile * LANES:(tile + 1) * LANES]
                    rows.append(a)
            return jnp.concatenate(rows, axis=0)

        if dil == 1:
            s_new = [_head_sum(q_g * column(kn_t, u, nq)) for u in range(dec_seq)]
            ok_new = [(step >= u) & real_step for u in range(dec_seq)]
            s_cache = [cache_scores(column(q_g, t, LANES)) for t in range(dec_seq)]
            m = place([jnp.max(jnp.where(pos >= t, s_cache[t], neg), axis=1, keepdims=True) for t in range(dec_seq)])
            for u in range(dec_seq):
                m = jnp.maximum(m, jnp.where(ok_new[u], s_new[u], neg))
            e_cache = [jnp.where(pos >= t, jnp.exp(s_cache[t] - m[:, t:t + 1]), 0.0) for t in range(dec_seq)]
            e_new = [jnp.where(ok_new[u], jnp.exp(s_new[u] - m), 0.0) for u in range(dec_seq)]
            denom = place([jnp.sum(e, axis=1, keepdims=True) for e in e_cache])
            for u in range(dec_seq):
                denom = denom + e_new[u]
            denom = jnp.where(real_step, denom, 1.0)
            inv = 1.0 / denom
            cols = []
            for t in range(dec_seq):
                acc = weighted_values(e_cache[t] * inv[:, t:t + 1])
                cols.append(jnp.sum(acc, axis=1, keepdims=True))
            o_g = place(cols)
            for u in range(dec_seq):
                o_g = o_g + _head_expand(e_new[u] * inv) * column(vn_t, u, nq)
        else:
            res = pos % dil
            q_pat = jnp.zeros((GROUP_WIDTH, LANES), F32)
            lane_res = lax.broadcasted_iota(jnp.int32, (GROUP_WIDTH, LANES), 1) % dil
            for t in range(dec_seq):
                q_pat = jnp.where(lane_res == t, column(q_g, t, LANES), q_pat)
            s_cache = cache_scores(q_pat)
            s_new = jnp.where(real_step, _head_sum(q_g * kn_t), 0.0)
            m_cache = place([jnp.max(jnp.where(res == t, s_cache, neg), axis=1, keepdims=True) for t in range(dec_seq)])
            m = jnp.maximum(m_cache, s_new)

            def by_position(stat):
                out = jnp.zeros((HEADS_PER_GROUP, win), F32)
                for t in range(dec_seq):
                    out = jnp.where(res == t, column(stat, t, win), out)
                return out

            e_cache = jnp.where(res < dec_seq, jnp.exp(s_cache - by_position(m)), 0.0)
            e_new = jnp.exp(s_new - m)
            denom = place([jnp.sum(jnp.where(res == t, e_cache, 0.0), axis=1, keepdims=True)
                           for t in range(dec_seq)]) + e_new
            inv = 1.0 / denom
            acc = weighted_values(e_cache * by_position(inv))
            l_i = lax.broadcasted_iota(jnp.int32, (LANES, LANES), 0)
            t_i = lax.broadcasted_iota(jnp.int32, (LANES, LANES), 1)
            sel = ((l_i % dil == t_i) & (t_i < dec_seq)).astype(BF16)
            o_g = _split_dot(acc, sel)[:, :nq] + _head_expand(e_new * inv) * vn_t
        outs.append(o_g)
        lses.append(m + jnp.log(denom))

    top = jnp.maximum(jnp.maximum(lses[0], lses[1]), lses[2])
    ws = [jnp.exp(l - top) for l in lses]
    total = ws[0] + ws[1] + ws[2]
    merged = jnp.zeros((GROUP_WIDTH, nq), F32)
    for g in range(N_ATTN_GROUPS):
        merged = merged + _head_expand(ws[g] / total) * outs[g]
    o_ref[...] = jnp.where(step_wide < dec_seq, merged, 0.0)


def _decode_attention(q, caches, new_kv, dec_batch, dec_seq):
    nq = SUBLANES

    def pad_steps(a):
        a = a.astype(F32).reshape(dec_batch, dec_seq, a.shape[-1])
        return jnp.pad(a, ((0, 0), (0, nq - dec_seq), (0, 0)))

    args = [pad_steps(q)] + [pad_steps(n) for n in new_kv]
    in_specs = [pl.BlockSpec((None, nq, ATTN_WIDTH), lambda b: (b, 0, 0))]
    in_specs += [pl.BlockSpec((None, nq, 2 * GROUP_WIDTH), lambda b: (b, 0, 0)) for _ in new_kv]
    for (win, dil), cache in zip(ATTN_GROUPS, caches):
        args.append(jnp.transpose(cache, (0, 2, 3, 4, 1)).reshape(dec_batch, 2, GROUP_WIDTH, win))
        in_specs.append(pl.BlockSpec((None, 2, GROUP_WIDTH, win), lambda b: (b, 0, 0, 0)))
    o_t = pl.pallas_call(
        functools.partial(_decode_attn_kernel, dec_seq=dec_seq),
        grid=(dec_batch,),
        in_specs=in_specs,
        out_specs=pl.BlockSpec((None, GROUP_WIDTH, nq), lambda b: (b, 0, 0)),
        out_shape=jax.ShapeDtypeStruct((dec_batch, GROUP_WIDTH, nq), F32),
        compiler_params=_params("parallel"),
        name="decode_attention",
    )(*args)
    return jnp.transpose(o_t[:, :, :dec_seq], (0, 2, 1)).reshape(dec_batch * dec_seq, GROUP_WIDTH)


def _ssm_kernel(u_ref, h0r_ref, h0i_ref, ar_ref, ai_ref, bw_ref, cr_ref, ci_ref, d_ref,
                y_ref, hr_out, hi_out, bur, bui, hr_s, hi_s, *, nb, steps):
    chunk = pl.program_id(0)
    half_in = SSM_WIDTH // 2
    half_st = SSM_COLS // 2

    @pl.when(chunk == 0)
    def _():
        hr_s[...] = h0r_ref[...]
        hi_s[...] = h0i_ref[...]

    u = u_ref[...]
    ub = u.astype(BF16)
    for hf in range(2):
        r = jnp.dot(ub[:, hf * half_in:(hf + 1) * half_in], bw_ref[hf], preferred_element_type=F32)
        bur[:, hf * half_st:(hf + 1) * half_st] = r[:, :half_st]
        bui[:, hf * half_st:(hf + 1) * half_st] = r[:, half_st:]

    lane_chunk = 4 * LANES

    def sub_batch(s, carry):
        for lc in range(SSM_COLS // lane_chunk):
            ls = slice(lc * lane_chunk, (lc + 1) * lane_chunk)
            ar = ar_ref[:, ls]
            ai = ai_ref[:, ls]
            row0 = pl.multiple_of(s * SUBLANES, SUBLANES)

            def step(t, h):
                hr, hi = h
                row = pl.multiple_of(t * nb + s * SUBLANES, SUBLANES)
                nhr = ar * hr - ai * hi + bur[pl.ds(row, SUBLANES), ls]
                nhi = ar * hi + ai * hr + bui[pl.ds(row, SUBLANES), ls]
                bur[pl.ds(row, SUBLANES), ls] = nhr
                bui[pl.ds(row, SUBLANES), ls] = nhi
                return nhr, nhi

            hr, hi = lax.fori_loop(0, steps, step, (hr_s[pl.ds(row0, SUBLANES), ls], hi_s[pl.ds(row0, SUBLANES), ls]))
            hr_s[pl.ds(row0, SUBLANES), ls] = hr
            hi_s[pl.ds(row0, SUBLANES), ls] = hi
        return carry

    lax.fori_loop(0, nb // SUBLANES, sub_batch, 0)

    for hf in range(2):
        ss = slice(hf * half_st, (hf + 1) * half_st)
        y = jnp.dot(bur[:, ss].astype(BF16), cr_ref[hf], preferred_element_type=F32)
        y = y + jnp.dot(bui[:, ss].astype(BF16), ci_ref[hf], preferred_element_type=F32)
        cs = slice(hf * half_in, (hf + 1) * half_in)
        y_ref[:, cs] = y + d_ref[:, cs] * u[:, cs]

    @pl.when(chunk == pl.num_programs(0) - 1)
    def _():
        hr_out[...] = hr_s[...]
        hi_out[...] = hi_s[...]


def _ssm_weights(ssm_log_dt, a_re, a_im, b_re, b_im, c_re, c_im):
    dt = jnp.exp(ssm_log_dt.astype(F32))[:, None]
    lam = lax.complex(a_re.astype(F32), a_im.astype(F32))
    lam_bar = jnp.exp(lam * dt)
    b_bar = ((lam_bar - 1.0) / lam)[:, :, None] * lax.complex(b_re.astype(F32), b_im.astype(F32))
    gh = N_SSM_GROUPS // 2
    eye = jnp.eye(gh, dtype=F32)

    def block_diag(m):
        return jnp.einsum('gab,gh->gahb', m, eye).reshape(gh * m.shape[1], gh * m.shape[2])

    bw, cr, ci = [], [], []
    for hf in range(2):
        gs = slice(hf * gh, (hf + 1) * gh)
        b_t = jnp.transpose(b_bar[gs], (0, 2, 1))
        bw.append(jnp.concatenate([block_diag(jnp.real(b_t)), block_diag(jnp.imag(b_t))], axis=1))
        cr.append(block_diag(jnp.transpose(c_re[gs].astype(F32), (0, 2, 1))))
        ci.append(block_diag(jnp.transpose(-c_im[gs].astype(F32), (0, 2, 1))))
    ar = jnp.broadcast_to(jnp.real(lam_bar).reshape(1, SSM_COLS), (SUBLANES, SSM_COLS))
    ai = jnp.broadcast_to(jnp.imag(lam_bar).reshape(1, SSM_COLS), (SUBLANES, SSM_COLS))
    return ar, ai, jnp.stack(bw).astype(BF16), jnp.stack(cr).astype(BF16), jnp.stack(ci).astype(BF16)


def _ssm(u_tm, h0r, h0i, weights, d_flat, nb):
    ar, ai, bw, cr, ci = weights
    rows = u_tm.shape[0]
    steps = SSM_ROWS // nb
    const2 = lambda c: (0, 0)
    const3 = lambda c: (0, 0, 0)
    state_sds = jax.ShapeDtypeStruct((nb, SSM_COLS), F32)
    return pl.pallas_call(
        functools.partial(_ssm_kernel, nb=nb, steps=steps),
        grid=(rows // SSM_ROWS,),
        in_specs=[
            pl.BlockSpec((SSM_ROWS, SSM_WIDTH), lambda c: (c, 0)),
            pl.BlockSpec((nb, SSM_COLS), const2),
            pl.BlockSpec((nb, SSM_COLS), const2),
            pl.BlockSpec((SUBLANES, SSM_COLS), const2),
            pl.BlockSpec((SUBLANES, SSM_COLS), const2),
            pl.BlockSpec(bw.shape, const3),
            pl.BlockSpec(cr.shape, const3),
            pl.BlockSpec(ci.shape, const3),
            pl.BlockSpec((1, SSM_WIDTH), const2),
        ],
        out_specs=(
            pl.BlockSpec((SSM_ROWS, SSM_WIDTH), lambda c: (c, 0)),
            pl.BlockSpec((nb, SSM_COLS), const2),
            pl.BlockSpec((nb, SSM_COLS), const2),
        ),
        out_shape=(jax.ShapeDtypeStruct((rows, SSM_WIDTH), F32), state_sds, state_sds),
        scratch_shapes=[
            pltpu.VMEM((SSM_ROWS, SSM_COLS), F32),
            pltpu.VMEM((SSM_ROWS, SSM_COLS), F32),
            pltpu.VMEM((nb, SSM_COLS), F32),
            pltpu.VMEM((nb, SSM_COLS), F32),
        ],
        compiler_params=_params("arbitrary"),
        name="s5_scan",
    )(u_tm, h0r, h0i, ar, ai, bw, cr, ci, d_flat)


def _mix_kernel(*refs, merged_attn):
    if merged_attn:
        attn_ref = refs[0]
        rest = refs[1:]
    else:
        o_refs, l_refs = refs[0:3], refs[3:6]
        rest = refs[6:]
    (y_ref, gate_ref, x_ref, wa_ref, wglu_ref, wout_ref, gffn_ref, wr_ref, br_ref,
     h_ref, hn_ref, route_ref) = rest
    if merged_attn:
        attn = attn_ref[...]
    else:
        ls = [r[...] for r in l_refs]
        top = jnp.maximum(jnp.maximum(ls[0], ls[1]), ls[2])
        es = [jnp.exp(l - top) for l in ls]
        attn = (es[0] * o_refs[0][...] + es[1] * o_refs[1][...] + es[2] * o_refs[2][...]) / (es[0] + es[1] + es[2])
    attn_out = jnp.dot(attn.astype(BF16), wa_ref[...], preferred_element_type=F32)
    glu = jnp.dot(jax.nn.gelu(y_ref[...]).astype(BF16), wglu_ref[...], preferred_element_type=F32)
    ssm_out = glu[:, :D_MODEL] * jax.nn.sigmoid(glu[:, D_MODEL:])
    merged = jax.nn.sigmoid(gate_ref[:, :D_MODEL]) * attn_out + jax.nn.sigmoid(gate_ref[:, D_MODEL:]) * ssm_out
    h = x_ref[...] + jnp.dot(merged.astype(BF16), wout_ref[...], preferred_element_type=F32)
    h_ref[...] = h
    var = jnp.mean(h * h, axis=-1, keepdims=True)
    hn = h * lax.rsqrt(var + NORM_EPS) * gffn_ref[...]
    hn_ref[...] = hn
    logits = jnp.dot(hn, wr_ref[...], preferred_element_type=F32, precision=lax.Precision.HIGHEST) + br_ref[...]
    lane = lax.broadcasted_iota(jnp.int32, logits.shape, 1).astype(F32)
    far = float(LANES)

    def first_argmax(vals):
        top_v = jnp.max(vals, axis=1, keepdims=True)
        return top_v, jnp.min(jnp.where(vals == top_v, lane, far), axis=1, keepdims=True)

    group_logits = jnp.where(lane < N_EXPERT_GROUPS, logits, -jnp.inf)
    g_top, g_idx = first_argmax(group_logits)
    p_group = 1.0 / jnp.sum(jnp.exp(group_logits - g_top), axis=1, keepdims=True)
    first_lane = N_EXPERT_GROUPS + g_idx * EXPERTS_PER_GROUP
    in_group = (lane >= first_lane) & (lane < first_lane + EXPERTS_PER_GROUP)
    expert_logits = jnp.where(in_group, logits, -jnp.inf)
    v1, i1 = first_argmax(expert_logits)
    v2, i2 = first_argmax(jnp.where(lane == i1, -jnp.inf, expert_logits))
    e2 = jnp.exp(v2 - v1)
    w1 = p_group / (1.0 + e2)
    w2 = p_group * e2 / (1.0 + e2)
    route = jnp.where(lane == 0, i1 - N_EXPERT_GROUPS,
                      jnp.where(lane == 1, i2 - N_EXPERT_GROUPS,
                                jnp.where(lane == 2, w1, jnp.where(lane == 3, w2, 0.0))))
    route_ref[...] = route


def _mix(attn_inputs, y, gates, x2d, wa, wglu, wout, gffn, wr, br):
    n = x2d.shape[0]
    tm = PROJ_TILE
    row = lambda i: (i, 0)
    const = lambda i: (0, 0)
    merged_attn = len(attn_inputs) == 1
    in_specs = [pl.BlockSpec((tm, GROUP_WIDTH), row) for _ in attn_inputs]
    in_specs += [
        pl.BlockSpec((tm, SSM_WIDTH), row),
        pl.BlockSpec((tm, 2 * D_MODEL), row),
        pl.BlockSpec((tm, D_MODEL), row),
        pl.BlockSpec(wa.shape, const),
        pl.BlockSpec(wglu.shape, const),
        pl.BlockSpec(wout.shape, const),
        pl.BlockSpec((1, D_MODEL), const),
        pl.BlockSpec(wr.shape, const),
        pl.BlockSpec((1, LANES), const),
    ]
    return pl.pallas_call(
        functools.partial(_mix_kernel, merged_attn=merged_attn),
        grid=(n // tm,),
        in_specs=in_specs,
        out_specs=(pl.BlockSpec((tm, D_MODEL), row), pl.BlockSpec((tm, D_MODEL), row), pl.BlockSpec((tm, LANES), row)),
        out_shape=(jax.ShapeDtypeStruct((n, D_MODEL), F32), jax.ShapeDtypeStruct((n, D_MODEL), F32),
                   jax.ShapeDtypeStruct((n, LANES), F32)),
        compiler_params=_params("parallel"),
        name="branch_mix",
    )(*attn_inputs, y, gates, x2d, wa, wglu, wout, gffn, wr, br)


def _row_copy(src, src_row, dst, dst_row, sem):
    return pltpu.make_async_copy(src.at[pl.ds(src_row, 1)], dst.at[pl.ds(dst_row, 1)], sem)


def _dispatch_kernel(slot_ref, hn_ref, xs_in_hbm, xs_hbm, sem):
    del xs_in_hbm

    def issue(j, carry):
        _row_copy(hn_ref, j, xs_hbm, slot_ref[2 * j], sem).start()
        _row_copy(hn_ref, j, xs_hbm, slot_ref[2 * j + 1], sem).start()
        return carry

    def drain(j, carry):
        _row_copy(hn_ref, 0, xs_hbm, 0, sem).wait()
        _row_copy(hn_ref, 0, xs_hbm, 0, sem).wait()
        return carry

    lax.fori_loop(0, ROW_TILE, issue, 0, unroll=DMA_UNROLL)
    lax.fori_loop(0, ROW_TILE, drain, 0, unroll=DMA_UNROLL)


def _dispatch(slots, hn, n_slots):
    n = hn.shape[0]
    xs0 = jnp.zeros((n_slots, D_MODEL), F32)
    return pl.pallas_call(
        _dispatch_kernel,
        grid=(n // ROW_TILE,),
        in_specs=[
            pl.BlockSpec((2 * ROW_TILE,), lambda i: (i,), memory_space=pltpu.SMEM),
            pl.BlockSpec((ROW_TILE, D_MODEL), lambda i: (i, 0)),
            pl.BlockSpec(memory_space=pl.ANY),
        ],
        out_specs=pl.BlockSpec(memory_space=pl.ANY),
        out_shape=jax.ShapeDtypeStruct((n_slots, D_MODEL), F32),
        scratch_shapes=[pltpu.SemaphoreType.DMA(())],
        input_output_aliases={2: 0},
        compiler_params=_params("arbitrary"),
        name="moe_dispatch",
    )(slots, hn, xs0)


def _expert_kernel(block_e_ref, n_used_ref, xs_ref, wg_ref, wu_ref, wd_ref, yb_ref):
    del block_e_ref
    i = pl.program_id(0)

    @pl.when(i < n_used_ref[0])
    def _():
        xb = xs_ref[...].astype(BF16)
        gate = jnp.dot(xb, wg_ref[...], preferred_element_type=F32)
        up = jnp.dot(xb, wu_ref[...], preferred_element_type=F32)
        hmid = (jax.nn.silu(gate) * up).astype(BF16)
        yb_ref[...] = jnp.dot(hmid, wd_ref[...], preferred_element_type=F32)

    @pl.when(i >= n_used_ref[0])
    def _():
        yb_ref[...] = jnp.zeros_like(yb_ref)


def _experts(block_e, n_used, xs, wg, wu, wd):
    n_blocks = xs.shape[0] // MOE_BLOCK
    grid_spec = pltpu.PrefetchScalarGridSpec(
        num_scalar_prefetch=2,
        grid=(n_blocks,),
        in_specs=[
            pl.BlockSpec((MOE_BLOCK, D_MODEL), lambda i, be, nu: (i, 0)),
            pl.BlockSpec((None, D_MODEL, D_EXPERT), lambda i, be, nu: (be[i], 0, 0)),
            pl.BlockSpec((None, D_MODEL, D_EXPERT), lambda i, be, nu: (be[i], 0, 0)),
            pl.BlockSpec((None, D_EXPERT, D_MODEL), lambda i, be, nu: (be[i], 0, 0)),
        ],
        out_specs=pl.BlockSpec((MOE_BLOCK, D_MODEL), lambda i, be, nu: (i, 0)),
    )
    return pl.pallas_call(
        _expert_kernel,
        grid_spec=grid_spec,
        out_shape=jax.ShapeDtypeStruct(xs.shape, F32),
        compiler_params=_params("arbitrary"),
        name="moe_experts",
    )(block_e, n_used, xs, wg, wu, wd)


def _combine_kernel(slot_ref, h_ref, route_ref, g_ref, yb_hbm, out_ref, buf_a, buf_b, sem):
    def issue(j, carry):
        _row_copy(yb_hbm, slot_ref[2 * j], buf_a, j, sem).start()
        _row_copy(yb_hbm, slot_ref[2 * j + 1], buf_b, j, sem).start()
        return carry

    def drain(j, carry):
        _row_copy(yb_hbm, 0, buf_a, 0, sem).wait()
        _row_copy(yb_hbm, 0, buf_b, 0, sem).wait()
        return carry

    lax.fori_loop(0, ROW_TILE, issue, 0, unroll=DMA_UNROLL)
    lax.fori_loop(0, ROW_TILE, drain, 0, unroll=DMA_UNROLL)
    route = route_ref[...]
    h = h_ref[...] + (route[:, 2:3] * buf_a[...] + route[:, 3:4] * buf_b[...])
    var = jnp.mean(h * h, axis=-1, keepdims=True)
    out_ref[...] = h * lax.rsqrt(var + NORM_EPS) * g_ref[...]


def _combine(slots, h, route, g_final, yb):
    n = h.shape[0]
    row = lambda i: (i, 0)
    return pl.pallas_call(
        _combine_kernel,
        grid=(n // ROW_TILE,),
        in_specs=[
            pl.BlockSpec((2 * ROW_TILE,), lambda i: (i,), memory_space=pltpu.SMEM),
            pl.BlockSpec((ROW_TILE, D_MODEL), row),
            pl.BlockSpec((ROW_TILE, LANES), row),
            pl.BlockSpec((1, D_MODEL), lambda i: (0, 0)),
            pl.BlockSpec(memory_space=pl.ANY),
        ],
        out_specs=pl.BlockSpec((ROW_TILE, D_MODEL), row),
        out_shape=jax.ShapeDtypeStruct((n, D_MODEL), F32),
        scratch_shapes=[
            pltpu.VMEM((ROW_TILE, D_MODEL), F32),
            pltpu.VMEM((ROW_TILE, D_MODEL), F32),
            pltpu.SemaphoreType.DMA(()),
        ],
        compiler_params=_params("arbitrary"),
        name="moe_combine",
    )(slots, h, route, g_final.reshape(1, D_MODEL), yb)


def _slot_assignment(route, n_blocks):
    flat_e = route[:, 0:2].astype(jnp.int32).reshape(-1)
    onehot = (flat_e[:, None] == jnp.arange(N_EXPERTS, dtype=jnp.int32)[None, :]).astype(jnp.int32)
    running = jnp.cumsum(onehot, axis=0)
    rank = jnp.sum(onehot * running, axis=1) - 1
    counts = running[-1]
    padded = ((counts + MOE_BLOCK - 1) // MOE_BLOCK) * MOE_BLOCK
    pad_end = jnp.cumsum(padded)
    pad_start = pad_end - padded
    slots = jnp.sum(onehot * pad_start[None, :], axis=1) + rank
    block_start = jnp.arange(n_blocks, dtype=jnp.int32) * MOE_BLOCK
    block_e = jnp.minimum(jnp.sum((pad_end[None, :] <= block_start[:, None]).astype(jnp.int32), axis=1), N_EXPERTS - 1)
    n_used = (pad_end[-1:] // MOE_BLOCK).astype(jnp.int32)
    return slots.astype(jnp.int32), block_e, n_used


def _moe_and_final_norm(h, hn, route, wg, wu, wd, g_final):
    n = h.shape[0]
    n_blocks = (2 * n) // MOE_BLOCK + N_EXPERTS
    slots, block_e, n_used = _slot_assignment(route, n_blocks)
    xs = _dispatch(slots, hn, n_blocks * MOE_BLOCK)
    yb = _experts(block_e, n_used, xs, wg, wu, wd)
    return _combine(slots, h, route, g_final, yb)


def kernel(x_prompt, x_sample, cache_kv_w128, cache_kv_w512, cache_kv_w2048, state_ssm, g_attn_norm, w_in, ssm_log_dt, ssm_a_re, ssm_a_im, ssm_b_re, ssm_b_im, ssm_c_re, ssm_c_im, ssm_d, w_glu, w_attn_branch, w_out, g_ffn_norm, w_router_group, b_router_group, w_router_expert, b_router_expert, w_exp_gate, w_exp_up, w_exp_down, g_final):
    batch, seq, _ = x_prompt.shape
    dec_batch, dec_seq, _ = x_sample.shape
    past_len = cache_kv_w2048.shape[2]
    layer = 0

    w_in_b = w_in[layer].astype(BF16)
    wa = w_attn_branch[layer].astype(BF16)
    wglu = w_glu[layer].astype(BF16)
    wout = w_out[layer].astype(BF16)
    wg = w_exp_gate[layer].astype(BF16)
    wu = w_exp_up[layer].astype(BF16)
    wd = w_exp_down[layer].astype(BF16)
    gffn = g_ffn_norm[layer].reshape(1, D_MODEL)
    pad = LANES - N_EXPERT_GROUPS - N_EXPERTS
    wr = jnp.concatenate([w_router_group[layer], w_router_expert[layer], jnp.zeros((D_MODEL, pad), F32)], axis=1)
    br = jnp.concatenate([b_router_group[layer], b_router_expert[layer], jnp.zeros((pad,), F32)]).reshape(1, LANES)
    ssm_w = _ssm_weights(ssm_log_dt[layer], ssm_a_re[layer], ssm_a_im[layer], ssm_b_re[layer], ssm_b_im[layer],
                         ssm_c_re[layer], ssm_c_im[layer])
    d_flat = ssm_d[layer].astype(F32).reshape(1, SSM_WIDTH)

    def time_major(a, nb, steps):
        return jnp.transpose(a.reshape(nb, steps, -1), (1, 0, 2)).reshape(nb * steps, -1)

    def batch_major(a, nb, steps):
        return jnp.transpose(a.reshape(steps, nb, -1), (1, 0, 2)).reshape(nb * steps, -1)

    xp = x_prompt.reshape(batch * seq, D_MODEL)
    q_p, kv0_p, kv1_p, kv2_p, u_p, gates_p = _in_projection(xp, g_attn_norm[layer], w_in_b, jnp.arange(seq, dtype=jnp.int32))
    kv_p = (kv0_p, kv1_p, kv2_p)
    attn_in = []
    lse_in = []
    for g, (win, dil) in enumerate(ATTN_GROUPS):
        o, lse = _prompt_attention(q_p, kv_p[g], g, dil, batch, seq)
        attn_in.append(o)
        lse_in.append(lse)
    zeros_state = jnp.zeros((batch, SSM_COLS), F32)
    y_tm, hr_p, hi_p = _ssm(time_major(u_p, batch, seq), zeros_state, zeros_state, ssm_w, d_flat, batch)
    y_p = batch_major(y_tm, batch, seq)
    h_p, hn_p, route_p = _mix(attn_in + lse_in, y_p, gates_p, xp, wa, wglu, wout, gffn, wr, br)
    out_p = _moe_and_final_norm(h_p, hn_p, route_p, wg, wu, wd, g_final)

    xs = x_sample.reshape(dec_batch * dec_seq, D_MODEL)
    pos_s = past_len + (jnp.arange(dec_batch * dec_seq, dtype=jnp.int32) % dec_seq)
    q_s, kv0_s, kv1_s, kv2_s, u_s, gates_s = _in_projection(xs, g_attn_norm[layer], w_in_b, pos_s)
    kv_s = (kv0_s, kv1_s, kv2_s)
    caches = (cache_kv_w128[layer], cache_kv_w512[layer], cache_kv_w2048[layer])
    attn_s = _decode_attention(q_s, caches, kv_s, dec_batch, dec_seq)
    st = state_ssm[layer].astype(F32).reshape(dec_batch, SSM_COLS, 2)
    ys_tm, hr_s, hi_s = _ssm(time_major(u_s, dec_batch, dec_seq), st[:, :, 0], st[:, :, 1], ssm_w, d_flat, dec_batch)
    y_s = batch_major(ys_tm, dec_batch, dec_seq)
    h_s, hn_s, route_s = _mix([attn_s], y_s, gates_s, xs, wa, wglu, wout, gffn, wr, br)
    out_s = _moe_and_final_norm(h_s, hn_s, route_s, wg, wu, wd, g_final)

    kv_tail = (2, HEADS_PER_GROUP, HEAD_DIM)
    outs = [out_p.reshape(batch, seq, D_MODEL), out_s.reshape(dec_batch, dec_seq, D_MODEL)]
    for g, (win, dil) in enumerate(ATTN_GROUPS):
        keep = min(win, seq)
        rows_p = kv_p[g].reshape((batch, seq) + kv_tail)[:, seq - keep:]
        outs.append(rows_p[None])
        outs.append(kv_s[g].reshape((1, dec_batch, dec_seq) + kv_tail))
    outs.append(jnp.stack([hr_p, hi_p], axis=-1).reshape(1, batch, N_SSM_GROUPS, SSM_STATE, 2))
    outs.append(jnp.stack([hr_s, hi_s], axis=-1).reshape(1, dec_batch, N_SSM_GROUPS, SSM_STATE, 2))
    return tuple(outs)
```

```python
import functools
import math

import jax
import jax.numpy as jnp
from jax import lax
from jax.experimental import pallas as pl
from jax.experimental.pallas import tpu as pltpu

F32 = jnp.float32
BF16 = jnp.bfloat16

D_MODEL = 1024
HEAD_DIM = 64
HEADS_PER_GROUP = 8
GROUP_WIDTH = HEADS_PER_GROUP * HEAD_DIM
ATTN_GROUPS = ((128, 1), (512, 4), (2048, 16))
N_ATTN_GROUPS = len(ATTN_GROUPS)
ATTN_WIDTH = N_ATTN_GROUPS * GROUP_WIDTH
ROT_DIM = HEAD_DIM // 4
ROPE_THETA = 500000.0
WINDOW_KEYS = 128
SSM_GROUP_CH = 16
SSM_WIDTH = D_MODEL // 2
N_SSM_GROUPS = SSM_WIDTH // SSM_GROUP_CH
SSM_STATE = 64
SSM_COLS = N_SSM_GROUPS * SSM_STATE
IN_WIDTH = 3 * ATTN_WIDTH + SSM_WIDTH + 2 * D_MODEL
N_EXPERT_GROUPS = 4
EXPERTS_PER_GROUP = 8
N_EXPERTS = N_EXPERT_GROUPS * EXPERTS_PER_GROUP
D_EXPERT = D_MODEL // 4
NORM_EPS = 1e-6

LANES = 128
SUBLANES = 8
VMEM_LIMIT = 56 * 1024 * 1024

PROJ_TILE = 256
ATTN_TILE = 128
SSM_ROWS = 512
MOE_BLOCK = 256
ROW_TILE = 512
DMA_UNROLL = 8


def _params(*sem):
    return pltpu.CompilerParams(dimension_semantics=sem, vmem_limit_bytes=VMEM_LIMIT)


def _normed_input(x_ref, g_ref):
    x = x_ref[...]
    var = jnp.mean(x * x, axis=-1, keepdims=True)
    return (x * lax.rsqrt(var + NORM_EPS) * g_ref[...]).astype(BF16)


def _rope_fn(cos_ref, sin_ref):
    cos = cos_ref[...]
    sin = sin_ref[...]
    lane = lax.broadcasted_iota(jnp.int32, cos.shape, 1) % HEAD_DIM
    first_half = lane < ROT_DIM // 2
    rotated = lane < ROT_DIM

    def rope_chunk(c):
        partner = jnp.where(first_half, pltpu.roll(c, LANES - ROT_DIM // 2, 1), pltpu.roll(c, ROT_DIM // 2, 1))
        return jnp.where(rotated, c * cos + partner * sin, c)

    def rope(t):
        return jnp.concatenate([rope_chunk(t[:, j * LANES:(j + 1) * LANES]) for j in range(GROUP_WIDTH // LANES)], axis=1)

    return rope


def _projection_tiles(xn, w_ref):
    for c in range(IN_WIDTH // GROUP_WIDTH):
        yield c, jnp.dot(xn, w_ref[:, c * GROUP_WIDTH:(c + 1) * GROUP_WIDTH], preferred_element_type=F32)


def _inproj_decode_kernel(x_ref, g_ref, w_ref, cos_ref, sin_ref, q_ref, kv0_ref, kv1_ref, kv2_ref, u_ref, gate_ref):
    rope = _rope_fn(cos_ref, sin_ref)
    kv_refs = (kv0_ref, kv1_ref, kv2_ref)
    for c, acc in _projection_tiles(_normed_input(x_ref, g_ref), w_ref):
        if c < 3:
            q_ref[:, c * GROUP_WIDTH:(c + 1) * GROUP_WIDTH] = (rope(acc) * (HEAD_DIM ** -0.5)).astype(BF16)
        elif c < 6:
            kv_refs[c - 3][:, :GROUP_WIDTH] = rope(acc)
        elif c < 9:
            kv_refs[c - 6][:, GROUP_WIDTH:] = acc
        elif c == 9:
            u_ref[...] = acc
        else:
            gate_ref[:, (c - 10) * GROUP_WIDTH:(c - 9) * GROUP_WIDTH] = acc


def _inproj_prompt_kernel(x_ref, g_ref, w_ref, cos_ref, sin_ref,
                          q0_ref, q1_ref, q2_ref, kv0_ref, kv1_ref, kv2_ref, kt0_ref, kt1_ref, kt2_ref,
                          u_ref, gate_ref, scr, *, tiles_per_seq, keeps):
    rope = _rope_fn(cos_ref, sin_ref)
    tm = x_ref.shape[0]
    tile_in_seq = pl.program_id(0) % tiles_per_seq
    q_refs = (q0_ref, q1_ref, q2_ref)
    kv_refs = (kv0_ref, kv1_ref, kv2_ref)
    kt_refs = (kt0_ref, kt1_ref, kt2_ref)

    def store_rows(dst_ref, val, dil, col0, col_stride):
        if dil == 1:
            dst_ref[:, col0:col0 + GROUP_WIDTH] = val.astype(BF16)
            return
        n_chunks = GROUP_WIDTH // LANES
        for j in range(n_chunks):
            scr[j] = val[:, j * LANES:(j + 1) * LANES]
        rows = tm // dil
        for r in range(dil):
            piece = jnp.concatenate([scr[j, pl.ds(r, rows, stride=dil), :] for j in range(n_chunks)], axis=1)
            dst_ref[:, col0 + r * col_stride:col0 + r * col_stride + GROUP_WIDTH] = piece.astype(BF16)

    def store_transposed(dst_ref, half, val, keep):
        width = min(keep, tm)
        first_tile = tiles_per_seq - max(keep // tm, 1)

        @pl.when(tile_in_seq >= first_tile)
        def _():
            dst_ref[half] = val[tm - width:, :].T

    for c, acc in _projection_tiles(_normed_input(x_ref, g_ref), w_ref):
        g = c % N_ATTN_GROUPS
        dil = ATTN_GROUPS[g][1]
        if c < 3:
            store_rows(q_refs[g], rope(acc) * (HEAD_DIM ** -0.5), dil, 0, GROUP_WIDTH)
        elif c < 6:
            k = rope(acc)
            store_rows(kv_refs[g], k, dil, 0, 2 * GROUP_WIDTH)
            store_transposed(kt_refs[g], 0, k, keeps[g])
        elif c < 9:
            store_rows(kv_refs[g], acc, dil, GROUP_WIDTH, 2 * GROUP_WIDTH)
            store_transposed(kt_refs[g], 1, acc, keeps[g])
        elif c == 9:
            u_ref[...] = acc
        else:
            gate_ref[:, (c - 10) * GROUP_WIDTH:(c - 9) * GROUP_WIDTH] = acc


def _rope_tables(pos):
    half = ROT_DIM // 2
    inv_freq = ROPE_THETA ** (-(jnp.arange(half, dtype=F32) / half))
    ang = pos.astype(F32)[:, None] * inv_freq[None, :]
    cos, sin = jnp.cos(ang), jnp.sin(ang)
    n = pos.shape[0]
    rest = HEAD_DIM - ROT_DIM
    cos_h = jnp.concatenate([cos, cos, jnp.ones((n, rest), F32)], axis=1)
    sin_h = jnp.concatenate([-sin, sin, jnp.zeros((n, rest), F32)], axis=1)
    return jnp.tile(cos_h, (1, LANES // HEAD_DIM)), jnp.tile(sin_h, (1, LANES // HEAD_DIM))


def _inproj_in_specs(tm, n_pos_tiles):
    const = lambda i: (0, 0)
    tab = lambda i: (i % n_pos_tiles, 0)
    return [
        pl.BlockSpec((tm, D_MODEL), lambda i: (i, 0)),
        pl.BlockSpec((1, D_MODEL), const),
        pl.BlockSpec((D_MODEL, IN_WIDTH), const, pipeline_mode=pl.Buffered(1)),
        pl.BlockSpec((tm, LANES), tab),
        pl.BlockSpec((tm, LANES), tab),
    ]


def _in_projection_decode(x2d, g, w_bf16, pos):
    n = x2d.shape[0]
    tm = PROJ_TILE
    cos_t, sin_t = _rope_tables(pos)
    row = lambda i: (i, 0)
    widths = (ATTN_WIDTH, 2 * GROUP_WIDTH, 2 * GROUP_WIDTH, 2 * GROUP_WIDTH, SSM_WIDTH, 2 * D_MODEL)
    dtypes = (BF16, F32, F32, F32, F32, F32)
    return pl.pallas_call(
        _inproj_decode_kernel,
        grid=(n // tm,),
        in_specs=_inproj_in_specs(tm, pos.shape[0] // tm),
        out_specs=tuple(pl.BlockSpec((tm, w), row) for w in widths),
        out_shape=tuple(jax.ShapeDtypeStruct((n, w), d) for w, d in zip(widths, dtypes)),
        compiler_params=_params("parallel"),
        name="in_projection_decode",
    )(x2d, g.reshape(1, D_MODEL), w_bf16, cos_t, sin_t)


def _in_projection_prompt(x2d, g, w_bf16, batch, seq):
    n = batch * seq
    tm = PROJ_TILE
    tiles_per_seq = seq // tm
    cos_t, sin_t = _rope_tables(jnp.arange(seq, dtype=jnp.int32))
    keeps = tuple(min(win, seq) for win, _ in ATTN_GROUPS)
    row = lambda i: (i, 0)
    out_specs, out_shape = [], []
    for width in (GROUP_WIDTH, 2 * GROUP_WIDTH):
        for _, dil in ATTN_GROUPS:
            out_specs.append(pl.BlockSpec((tm // dil, dil * width), row))
            out_shape.append(jax.ShapeDtypeStruct((n // dil, dil * width), BF16))
    for keep in keeps:
        blk = min(keep, tm)
        first_tile = tiles_per_seq - max(keep // tm, 1)
        out_specs.append(pl.BlockSpec(
            (None, 2, GROUP_WIDTH, blk),
            lambda i, first_tile=first_tile: (i // tiles_per_seq, 0, 0, jnp.maximum(i % tiles_per_seq - first_tile, 0))))
        out_shape.append(jax.ShapeDtypeStruct((batch, 2, GROUP_WIDTH, keep), F32))
    out_specs += [pl.BlockSpec((tm, SSM_WIDTH), row), pl.BlockSpec((tm, 2 * D_MODEL), row)]
    out_shape += [jax.ShapeDtypeStruct((n, SSM_WIDTH), F32), jax.ShapeDtypeStruct((n, 2 * D_MODEL), F32)]
    outs = pl.pallas_call(
        functools.partial(_inproj_prompt_kernel, tiles_per_seq=tiles_per_seq, keeps=keeps),
        grid=(n // tm,),
        in_specs=_inproj_in_specs(tm, tiles_per_seq),
        out_specs=tuple(out_specs),
        out_shape=tuple(out_shape),
        scratch_shapes=[pltpu.VMEM((GROUP_WIDTH // LANES, tm, LANES), F32)],
        compiler_params=_params("arbitrary"),
        name="in_projection_prompt",
    )(x2d, g.reshape(1, D_MODEL), w_bf16, cos_t, sin_t)
    return outs[0:3], outs[3:6], outs[6:9], outs[9], outs[10]


def _window_attn_kernel(q_ref, kvc_ref, *rest, has_prev):
    if has_prev:
        kvp_ref, o_ref, lse_ref = rest
    else:
        o_ref, lse_ref = rest
    tile = pl.program_id(2)
    q = q_ref[...]
    k = kvc_ref[:, :GROUP_WIDTH]
    v = kvc_ref[:, GROUP_WIDTH:]
    n_keys = ATTN_TILE
    if has_prev:
        k = jnp.concatenate([kvp_ref[:, :GROUP_WIDTH], k], axis=0)
        v = jnp.concatenate([kvp_ref[:, GROUP_WIDTH:], v], axis=0)
        n_keys = 2 * ATTN_TILE
    rows = lax.broadcasted_iota(jnp.int32, (ATTN_TILE, n_keys), 0)
    cols = lax.broadcasted_iota(jnp.int32, (ATTN_TILE, n_keys), 1)
    if has_prev:
        valid = (cols >= rows) & (cols <= rows + WINDOW_KEYS) & ((cols >= ATTN_TILE) | (tile > 0))
    else:
        valid = cols <= rows
    low_head = lax.broadcasted_iota(jnp.int32, (ATTN_TILE, LANES), 1) < HEAD_DIM
    for j in range(GROUP_WIDTH // LANES):
        sl = slice(j * LANES, (j + 1) * LANES)
        qj, kj, vj = q[:, sl], k[:, sl], v[:, sl]
        outs, lses = [], []
        for sel in (low_head, jnp.logical_not(low_head)):
            qm = jnp.where(sel, qj, jnp.zeros_like(qj))
            s = lax.dot_general(qm, kj, (((1,), (1,)), ((), ())), preferred_element_type=F32)
            s = jnp.where(valid, s, -jnp.inf)
            m = jnp.max(s, axis=1, keepdims=True)
            p = jnp.exp(s - m)
            l = jnp.sum(p, axis=1, keepdims=True)
            outs.append(jnp.dot(p.astype(BF16), vj, preferred_element_type=F32) / l)
            lses.append(m + jnp.log(l))
        o_ref[:, sl] = jnp.where(low_head, outs[0], outs[1])
        lse_ref[:, sl] = jnp.where(low_head, lses[0], lses[1])


def _prompt_attention(q, kv, group, dilation, batch, seq):
    tg = seq // dilation
    n_tiles = tg // ATTN_TILE
    has_prev = n_tiles > 1
    q3 = q.reshape(batch, tg, dilation * GROUP_WIDTH)
    kv3 = kv.reshape(batch, tg, dilation * 2 * GROUP_WIDTH)
    in_specs = [
        pl.BlockSpec((None, ATTN_TILE, GROUP_WIDTH), lambda b, r, t: (b, t, r)),
        pl.BlockSpec((None, ATTN_TILE, 2 * GROUP_WIDTH), lambda b, r, t: (b, t, r)),
    ]
    args = [q3, kv3]
    if has_prev:
        in_specs.append(pl.BlockSpec((None, ATTN_TILE, 2 * GROUP_WIDTH), lambda b, r, t: (b, jnp.maximum(t - 1, 0), r)))
        args.append(kv3)
    out_spec = pl.BlockSpec((None, ATTN_TILE, GROUP_WIDTH), lambda b, r, t: (b, t, r))
    out_sds = jax.ShapeDtypeStruct((batch, tg, dilation * GROUP_WIDTH), F32)
    o, lse = pl.pallas_call(
        functools.partial(_window_attn_kernel, has_prev=has_prev),
        grid=(batch, dilation, n_tiles),
        in_specs=in_specs,
        out_specs=(out_spec, out_spec),
        out_shape=(out_sds, out_sds),
        compiler_params=_params("parallel", "parallel", "arbitrary"),
        name=f"prompt_attention_g{group}",
    )(*args)
    flat = (batch * tg, dilation * GROUP_WIDTH)
    return o.reshape(flat), lse.reshape(flat)


def _sublane_total(x):
    x = x + pltpu.roll(x, 4, 0)
    x = x + pltpu.roll(x, 2, 0)
    return x + pltpu.roll(x, 1, 0)


def _head_sum(prod):
    width = prod.shape[1]
    row = lax.broadcasted_iota(jnp.int32, (HEADS_PER_GROUP, width), 0)
    out = jnp.zeros((HEADS_PER_GROUP, width), F32)
    for h in range(HEADS_PER_GROUP):
        part = prod[h * HEAD_DIM:h * HEAD_DIM + SUBLANES]
        for j in range(1, HEAD_DIM // SUBLANES):
            part = part + prod[h * HEAD_DIM + j * SUBLANES:h * HEAD_DIM + (j + 1) * SUBLANES]
        out = jnp.where(row == h, _sublane_total(part), out)
    return out


def _head_expand(x):
    width = x.shape[1]
    pieces = []
    for h in range(HEADS_PER_GROUP):
        pieces.extend([jnp.broadcast_to(x[h:h + 1, :], (SUBLANES, width))] * (HEAD_DIM // SUBLANES))
    return jnp.concatenate(pieces, axis=0)


def _split_dot(acc, sel):
    hi = acc.astype(BF16)
    lo = (acc - hi.astype(F32)).astype(BF16)
    return jnp.dot(hi, sel, preferred_element_type=F32) + jnp.dot(lo, sel, preferred_element_type=F32)


def _decode_attn_kernel(q_ref, n0_ref, n1_ref, n2_ref, c0_ref, c1_ref, c2_ref, o_ref, *, dec_seq):
    nq = SUBLANES
    n_sub = HEAD_DIM // SUBLANES
    neg = -jnp.inf
    q_t = q_ref[...].T
    step = lax.broadcasted_iota(jnp.int32, (HEADS_PER_GROUP, nq), 1)
    step_wide = lax.broadcasted_iota(jnp.int32, (GROUP_WIDTH, nq), 1)
    real_step = step < dec_seq
    new_refs = (n0_ref, n1_ref, n2_ref)
    cache_refs = (c0_ref, c1_ref, c2_ref)

    def column(x, t, width):
        return jnp.broadcast_to(x[:, t:t + 1], (x.shape[0], width))

    def place(cols):
        rows = cols[0].shape[0]
        lane = lax.broadcasted_iota(jnp.int32, (rows, nq), 1)
        out = jnp.zeros((rows, nq), F32)
        for t, c in enumerate(cols):
            out = jnp.where(lane == t, jnp.broadcast_to(c, (rows, nq)), out)
        return out

    outs, lses = [], []
    for g, (win, dil) in enumerate(ATTN_GROUPS):
        c_ref = cache_refs[g]
        kv_new = new_refs[g][...]
        kn_t = kv_new[:, :GROUP_WIDTH].T
        vn_t = kv_new[:, GROUP_WIDTH:].T
        q_g = q_t[g * GROUP_WIDTH:(g + 1) * GROUP_WIDTH, :]
        pos = lax.broadcasted_iota(jnp.int32, (HEADS_PER_GROUP, win), 1)
        row8 = lax.broadcasted_iota(jnp.int32, (HEADS_PER_GROUP, win), 0)
        n_tiles = win // LANES

        def cache_scores(q_pat):
            s = jnp.zeros((HEADS_PER_GROUP, win), F32)
            for h in range(HEADS_PER_GROUP):
                part = None
                for j in range(n_sub):
                    r0 = h * HEAD_DIM + j * SUBLANES
                    qp = q_pat[r0:r0 + SUBLANES, :]
                    term = c_ref[0, r0:r0 + SUBLANES, :] * jnp.concatenate([qp] * n_tiles, axis=1)
                    part = term if part is None else part + term
                s = jnp.where(row8 == h, _sublane_total(part), s)
            return s

        def weighted_values(p):
            rows = []
            for h in range(HEADS_PER_GROUP):
                ph = jnp.broadcast_to(p[h:h + 1, :], (SUBLANES, win))
                for j in range(n_sub):
                    r0 = h * HEAD_DIM + j * SUBLANES
                    prod = ph * c_ref[1, r0:r0 + SUBLANES, :]
                    a = prod[:, :LANES]
                    for tile in range(1, n_tiles):
                        a = a + prod[:, tile * LANES:(tile + 1) * LANES]
                    rows.append(a)
            return jnp.concatenate(rows, axis=0)

        if dil == 1:
            s_new = [_head_sum(q_g * column(kn_t, u, nq)) for u in range(dec_seq)]
            ok_new = [(step >= u) & real_step for u in range(dec_seq)]
            s_cache = [cache_scores(column(q_g, t, LANES)) for t in range(dec_seq)]
            m = place([jnp.max(jnp.where(pos >= t, s_cache[t], neg), axis=1, keepdims=True) for t in range(dec_seq)])
            for u in range(dec_seq):
                m = jnp.maximum(m, jnp.where(ok_new[u], s_new[u], neg))
            e_cache = [jnp.where(pos >= t, jnp.exp(s_cache[t] - m[:, t:t + 1]), 0.0) for t in range(dec_seq)]
            e_new = [jnp.where(ok_new[u], jnp.exp(s_new[u] - m), 0.0) for u in range(dec_seq)]
            denom = place([jnp.sum(e, axis=1, keepdims=True) for e in e_cache])
            for u in range(dec_seq):
                denom = denom + e_new[u]
            denom = jnp.where(real_step, denom, 1.0)
            inv = 1.0 / denom
            cols = []
            for t in range(dec_seq):
                acc = weighted_values(e_cache[t] * inv[:, t:t + 1])
                cols.append(jnp.sum(acc, axis=1, keepdims=True))
            o_g = place(cols)
            for u in range(dec_seq):
                o_g = o_g + _head_expand(e_new[u] * inv) * column(vn_t, u, nq)
        else:
            res = pos % dil
            q_pat = jnp.zeros((GROUP_WIDTH, LANES), F32)
            lane_res = lax.broadcasted_iota(jnp.int32, (GROUP_WIDTH, LANES), 1) % dil
            for t in range(dec_seq):
                q_pat = jnp.where(lane_res == t, column(q_g, t, LANES), q_pat)
            s_cache = cache_scores(q_pat)
            s_new = jnp.where(real_step, _head_sum(q_g * kn_t), 0.0)
            m_cache = place([jnp.max(jnp.where(res == t, s_cache, neg), axis=1, keepdims=True) for t in range(dec_seq)])
            m = jnp.maximum(m_cache, s_new)

            def by_position(stat):
                out = jnp.zeros((HEADS_PER_GROUP, win), F32)
                for t in range(dec_seq):
                    out = jnp.where(res == t, column(stat, t, win), out)
                return out

            e_cache = jnp.where(res < dec_seq, jnp.exp(s_cache - by_position(m)), 0.0)
            e_new = jnp.exp(s_new - m)
            denom = place([jnp.sum(jnp.where(res == t, e_cache, 0.0), axis=1, keepdims=True)
                           for t in range(dec_seq)]) + e_new
            inv = 1.0 / denom
            acc = weighted_values(e_cache * by_position(inv))
            l_i = lax.broadcasted_iota(jnp.int32, (LANES, LANES), 0)
            t_i = lax.broadcasted_iota(jnp.int32, (LANES, LANES), 1)
            sel = ((l_i % dil == t_i) & (t_i < dec_seq)).astype(BF16)
            o_g = _split_dot(acc, sel)[:, :nq] + _head_expand(e_new * inv) * vn_t
        outs.append(o_g)
        lses.append(m + jnp.log(denom))

    top = jnp.maximum(jnp.maximum(lses[0], lses[1]), lses[2])
    ws = [jnp.exp(l - top) for l in lses]
    total = ws[0] + ws[1] + ws[2]
    merged = jnp.zeros((GROUP_WIDTH, nq), F32)
    for g in range(N_ATTN_GROUPS):
        merged = merged + _head_expand(ws[g] / total) * outs[g]
    o_ref[...] = jnp.where(step_wide < dec_seq, merged, 0.0)


def _decode_attention(q, caches, new_kv, dec_batch, dec_seq):
    nq = SUBLANES

    def pad_steps(a):
        a = a.astype(F32).reshape(dec_batch, dec_seq, a.shape[-1])
        return jnp.pad(a, ((0, 0), (0, nq - dec_seq), (0, 0)))

    args = [pad_steps(q)] + [pad_steps(n) for n in new_kv]
    in_specs = [pl.BlockSpec((None, nq, ATTN_WIDTH), lambda b: (b, 0, 0))]
    in_specs += [pl.BlockSpec((None, nq, 2 * GROUP_WIDTH), lambda b: (b, 0, 0)) for _ in new_kv]
    for (win, dil), cache in zip(ATTN_GROUPS, caches):
        args.append(jnp.transpose(cache, (0, 2, 3, 4, 1)).reshape(dec_batch, 2, GROUP_WIDTH, win))
        in_specs.append(pl.BlockSpec((None, 2, GROUP_WIDTH, win), lambda b: (b, 0, 0, 0)))
    o_t = pl.pallas_call(
        functools.partial(_decode_attn_kernel, dec_seq=dec_seq),
        grid=(dec_batch,),
        in_specs=in_specs,
        out_specs=pl.BlockSpec((None, GROUP_WIDTH, nq), lambda b: (b, 0, 0)),
        out_shape=jax.ShapeDtypeStruct((dec_batch, GROUP_WIDTH, nq), F32),
        compiler_params=_params("parallel"),
        name="decode_attention",
    )(*args)
    return jnp.transpose(o_t[:, :, :dec_seq], (0, 2, 1)).reshape(dec_batch * dec_seq, GROUP_WIDTH)


def _ssm_kernel(u_ref, h0r_ref, h0i_ref, ar_ref, ai_ref, bw_ref, cr_ref, ci_ref, d_ref,
                y_ref, hr_out, hi_out, bur, bui, hr_s, hi_s, *, nb, steps):
    chunk = pl.program_id(0)
    half_in = SSM_WIDTH // 2
    half_st = SSM_COLS // 2

    @pl.when(chunk == 0)
    def _():
        hr_s[...] = h0r_ref[...]
        hi_s[...] = h0i_ref[...]

    u = u_ref[...]
    ub = u.astype(BF16)
    for hf in range(2):
        r = jnp.dot(ub[:, hf * half_in:(hf + 1) * half_in], bw_ref[hf], preferred_element_type=F32)
        bur[:, hf * half_st:(hf + 1) * half_st] = r[:, :half_st]
        bui[:, hf * half_st:(hf + 1) * half_st] = r[:, half_st:]

    lane_chunk = 4 * LANES

    def sub_batch(s, carry):
        for lc in range(SSM_COLS // lane_chunk):
            ls = slice(lc * lane_chunk, (lc + 1) * lane_chunk)
            ar = ar_ref[:, ls]
            ai = ai_ref[:, ls]
            row0 = pl.multiple_of(s * SUBLANES, SUBLANES)

            def step(t, h):
                hr, hi = h
                row = pl.multiple_of(t * nb + s * SUBLANES, SUBLANES)
                nhr = ar * hr - ai * hi + bur[pl.ds(row, SUBLANES), ls]
                nhi = ar * hi + ai * hr + bui[pl.ds(row, SUBLANES), ls]
                bur[pl.ds(row, SUBLANES), ls] = nhr
                bui[pl.ds(row, SUBLANES), ls] = nhi
                return nhr, nhi

            hr, hi = lax.fori_loop(0, steps, step, (hr_s[pl.ds(row0, SUBLANES), ls], hi_s[pl.ds(row0, SUBLANES), ls]))
            hr_s[pl.ds(row0, SUBLANES), ls] = hr
            hi_s[pl.ds(row0, SUBLANES), ls] = hi
        return carry

    lax.fori_loop(0, nb // SUBLANES, sub_batch, 0)

    for hf in range(2):
        ss = slice(hf * half_st, (hf + 1) * half_st)
        y = jnp.dot(bur[:, ss].astype(BF16), cr_ref[hf], preferred_element_type=F32)
        y = y + jnp.dot(bui[:, ss].astype(BF16), ci_ref[hf], preferred_element_type=F32)
        cs = slice(hf * half_in, (hf + 1) * half_in)
        y_ref[:, cs] = y + d_ref[:, cs] * u[:, cs]

    @pl.when(chunk == pl.num_programs(0) - 1)
    def _():
        hr_out[...] = hr_s[...]
        hi_out[...] = hi_s[...]


def _ssm_weights(ssm_log_dt, a_re, a_im, b_re, b_im, c_re, c_im):
    dt = jnp.exp(ssm_log_dt.astype(F32))[:, None]
    lam = lax.complex(a_re.astype(F32), a_im.astype(F32))
    lam_bar = jnp.exp(lam * dt)
    b_bar = ((lam_bar - 1.0) / lam)[:, :, None] * lax.complex(b_re.astype(F32), b_im.astype(F32))
    gh = N_SSM_GROUPS // 2
    eye = jnp.eye(gh, dtype=F32)

    def block_diag(m):
        return jnp.einsum('gab,gh->gahb', m, eye).reshape(gh * m.shape[1], gh * m.shape[2])

    bw, cr, ci = [], [], []
    for hf in range(2):
        gs = slice(hf * gh, (hf + 1) * gh)
        b_t = jnp.transpose(b_bar[gs], (0, 2, 1))
        bw.append(jnp.concatenate([block_diag(jnp.real(b_t)), block_diag(jnp.imag(b_t))], axis=1))
        cr.append(block_diag(jnp.transpose(c_re[gs].astype(F32), (0, 2, 1))))
        ci.append(block_diag(jnp.transpose(-c_im[gs].astype(F32), (0, 2, 1))))
    ar = jnp.broadcast_to(jnp.real(lam_bar).reshape(1, SSM_COLS), (SUBLANES, SSM_COLS))
    ai = jnp.broadcast_to(jnp.imag(lam_bar).reshape(1, SSM_COLS), (SUBLANES, SSM_COLS))
    return ar, ai, jnp.stack(bw).astype(BF16), jnp.stack(cr).astype(BF16), jnp.stack(ci).astype(BF16)


def _ssm(u_tm, h0r, h0i, weights, d_flat, nb):
    ar, ai, bw, cr, ci = weights
    rows = u_tm.shape[0]
    steps = SSM_ROWS // nb
    const2 = lambda c: (0, 0)
    const3 = lambda c: (0, 0, 0)
    state_sds = jax.ShapeDtypeStruct((nb, SSM_COLS), F32)
    return pl.pallas_call(
        functools.partial(_ssm_kernel, nb=nb, steps=steps),
        grid=(rows // SSM_ROWS,),
        in_specs=[
            pl.BlockSpec((SSM_ROWS, SSM_WIDTH), lambda c: (c, 0)),
            pl.BlockSpec((nb, SSM_COLS), const2),
            pl.BlockSpec((nb, SSM_COLS), const2),
            pl.BlockSpec((SUBLANES, SSM_COLS), const2),
            pl.BlockSpec((SUBLANES, SSM_COLS), const2),
            pl.BlockSpec(bw.shape, const3),
            pl.BlockSpec(cr.shape, const3),
            pl.BlockSpec(ci.shape, const3),
            pl.BlockSpec((1, SSM_WIDTH), const2),
        ],
        out_specs=(
            pl.BlockSpec((SSM_ROWS, SSM_WIDTH), lambda c: (c, 0)),
            pl.BlockSpec((nb, SSM_COLS), const2),
            pl.BlockSpec((nb, SSM_COLS), const2),
        ),
        out_shape=(jax.ShapeDtypeStruct((rows, SSM_WIDTH), F32), state_sds, state_sds),
        scratch_shapes=[
            pltpu.VMEM((SSM_ROWS, SSM_COLS), F32),
            pltpu.VMEM((SSM_ROWS, SSM_COLS), F32),
            pltpu.VMEM((nb, SSM_COLS), F32),
            pltpu.VMEM((nb, SSM_COLS), F32),
        ],
        compiler_params=_params("arbitrary"),
        name="s5_scan",
    )(u_tm, h0r, h0i, ar, ai, bw, cr, ci, d_flat)


def _mix_kernel(*refs, merged_attn):
    if merged_attn:
        attn_ref = refs[0]
        rest = refs[1:13]
    else:
        group_refs = refs[0:2 * N_ATTN_GROUPS]
        rest = refs[2 * N_ATTN_GROUPS:2 * N_ATTN_GROUPS + 12]
        scratch = refs[2 * N_ATTN_GROUPS + 12:]
    (y_ref, gate_ref, x_ref, wa_ref, wglu_ref, wout_ref, gffn_ref, wr_ref, br_ref,
     h_ref, hn_ref, route_ref) = rest
    tm = x_ref.shape[0]
    if merged_attn:
        attn = attn_ref[...]
    else:
        natural = []
        for idx, ref in enumerate(group_refs):
            dil = ATTN_GROUPS[idx % N_ATTN_GROUPS][1]
            if dil == 1:
                natural.append(ref[...])
                continue
            scr = scratch[idx]
            n_chunks = GROUP_WIDTH // LANES
            for r in range(dil):
                for j in range(n_chunks):
                    col = r * GROUP_WIDTH + j * LANES
                    scr[j, pl.ds(r, tm // dil, stride=dil), :] = ref[:, col:col + LANES]
            natural.append(jnp.concatenate([scr[j] for j in range(n_chunks)], axis=1))
        os, ls = natural[:N_ATTN_GROUPS], natural[N_ATTN_GROUPS:]
        top = jnp.maximum(jnp.maximum(ls[0], ls[1]), ls[2])
        es = [jnp.exp(l - top) for l in ls]
        attn = (es[0] * os[0] + es[1] * os[1] + es[2] * os[2]) / (es[0] + es[1] + es[2])
    attn_out = jnp.dot(attn.astype(BF16), wa_ref[...], preferred_element_type=F32)
    glu = jnp.dot(jax.nn.gelu(y_ref[...]).astype(BF16), wglu_ref[...], preferred_element_type=F32)
    ssm_out = glu[:, :D_MODEL] * jax.nn.sigmoid(glu[:, D_MODEL:])
    merged = jax.nn.sigmoid(gate_ref[:, :D_MODEL]) * attn_out + jax.nn.sigmoid(gate_ref[:, D_MODEL:]) * ssm_out
    h = x_ref[...] + jnp.dot(merged.astype(BF16), wout_ref[...], preferred_element_type=F32)
    h_ref[...] = h
    var = jnp.mean(h * h, axis=-1, keepdims=True)
    hn = h * lax.rsqrt(var + NORM_EPS) * gffn_ref[...]
    hn_ref[...] = hn
    logits = jnp.dot(hn, wr_ref[...], preferred_element_type=F32, precision=lax.Precision.HIGHEST) + br_ref[...]
    lane = lax.broadcasted_iota(jnp.int32, logits.shape, 1).astype(F32)
    far = float(LANES)

    def first_argmax(vals):
        top_v = jnp.max(vals, axis=1, keepdims=True)
        return top_v, jnp.min(jnp.where(vals == top_v, lane, far), axis=1, keepdims=True)

    group_logits = jnp.where(lane < N_EXPERT_GROUPS, logits, -jnp.inf)
    g_top, g_idx = first_argmax(group_logits)
    p_group = 1.0 / jnp.sum(jnp.exp(group_logits - g_top), axis=1, keepdims=True)
    first_lane = N_EXPERT_GROUPS + g_idx * EXPERTS_PER_GROUP
    in_group = (lane >= first_lane) & (lane < first_lane + EXPERTS_PER_GROUP)
    expert_logits = jnp.where(in_group, logits, -jnp.inf)
    v1, i1 = first_argmax(expert_logits)
    v2, i2 = first_argmax(jnp.where(lane == i1, -jnp.inf, expert_logits))
    e2 = jnp.exp(v2 - v1)
    w1 = p_group / (1.0 + e2)
    w2 = p_group * e2 / (1.0 + e2)
    route = jnp.where(lane == 0, i1 - N_EXPERT_GROUPS,
                      jnp.where(lane == 1, i2 - N_EXPERT_GROUPS,
                                jnp.where(lane == 2, w1, jnp.where(lane == 3, w2, 0.0))))
    route_ref[...] = route


def _mix(attn_inputs, y, gates, x2d, wa, wglu, wout, gffn, wr, br):
    n = x2d.shape[0]
    tm = PROJ_TILE
    row = lambda i: (i, 0)
    const = lambda i: (0, 0)
    merged_attn = len(attn_inputs) == 1
    in_specs = [pl.BlockSpec((tm * GROUP_WIDTH // a.shape[1], a.shape[1]), row) for a in attn_inputs]
    scratch = [] if merged_attn else [pltpu.VMEM((GROUP_WIDTH // LANES, tm, LANES), F32) for _ in attn_inputs]
    in_specs += [
        pl.BlockSpec((tm, SSM_WIDTH), row),
        pl.BlockSpec((tm, 2 * D_MODEL), row),
        pl.BlockSpec((tm, D_MODEL), row),
        pl.BlockSpec(wa.shape, const),
        pl.BlockSpec(wglu.shape, const),
        pl.BlockSpec(wout.shape, const),
        pl.BlockSpec((1, D_MODEL), const),
        pl.BlockSpec(wr.shape, const),
        pl.BlockSpec((1, LANES), const),
    ]
    return pl.pallas_call(
        functools.partial(_mix_kernel, merged_attn=merged_attn),
        grid=(n // tm,),
        in_specs=in_specs,
        out_specs=(pl.BlockSpec((tm, D_MODEL), row), pl.BlockSpec((tm, D_MODEL), row), pl.BlockSpec((tm, LANES), row)),
        out_shape=(jax.ShapeDtypeStruct((n, D_MODEL), F32), jax.ShapeDtypeStruct((n, D_MODEL), F32),
                   jax.ShapeDtypeStruct((n, LANES), F32)),
        scratch_shapes=scratch,
        compiler_params=_params("parallel"),
        name="branch_mix",
    )(*attn_inputs, y, gates, x2d, wa, wglu, wout, gffn, wr, br)


def _row_copy(src, src_row, dst, dst_row, sem):
    return pltpu.make_async_copy(src.at[pl.ds(src_row, 1)], dst.at[pl.ds(dst_row, 1)], sem)


def _dispatch_kernel(slot_ref, hn_ref, xs_in_hbm, xs_hbm, sem):
    del xs_in_hbm

    def issue(j, carry):
        _row_copy(hn_ref, j, xs_hbm, slot_ref[2 * j], sem).start()
        _row_copy(hn_ref, j, xs_hbm, slot_ref[2 * j + 1], sem).start()
        return carry

    def drain(j, carry):
        _row_copy(hn_ref, 0, xs_hbm, 0, sem).wait()
        _row_copy(hn_ref, 0, xs_hbm, 0, sem).wait()
        return carry

    lax.fori_loop(0, ROW_TILE, issue, 0, unroll=DMA_UNROLL)
    lax.fori_loop(0, ROW_TILE, drain, 0, unroll=DMA_UNROLL)


def _dispatch(slots, hn, n_slots):
    n = hn.shape[0]
    xs0 = jnp.zeros((n_slots, D_MODEL), F32)
    return pl.pallas_call(
        _dispatch_kernel,
        grid=(n // ROW_TILE,),
        in_specs=[
            pl.BlockSpec((2 * ROW_TILE,), lambda i: (i,), memory_space=pltpu.SMEM),
            pl.BlockSpec((ROW_TILE, D_MODEL), lambda i: (i, 0)),
            pl.BlockSpec(memory_space=pl.ANY),
        ],
        out_specs=pl.BlockSpec(memory_space=pl.ANY),
        out_shape=jax.ShapeDtypeStruct((n_slots, D_MODEL), F32),
        scratch_shapes=[pltpu.SemaphoreType.DMA(())],
        input_output_aliases={2: 0},
        compiler_params=_params("arbitrary"),
        name="moe_dispatch",
    )(slots, hn, xs0)


def _expert_kernel(block_e_ref, n_used_ref, xs_ref, wg_ref, wu_ref, wd_ref, yb_ref):
    del block_e_ref
    i = pl.program_id(0)

    @pl.when(i < n_used_ref[0])
    def _():
        xb = xs_ref[...].astype(BF16)
        gate = jnp.dot(xb, wg_ref[...], preferred_element_type=F32)
        up = jnp.dot(xb, wu_ref[...], preferred_element_type=F32)
        hmid = (jax.nn.silu(gate) * up).astype(BF16)
        yb_ref[...] = jnp.dot(hmid, wd_ref[...], preferred_element_type=F32)

    @pl.when(i >= n_used_ref[0])
    def _():
        yb_ref[...] = jnp.zeros_like(yb_ref)


def _experts(block_e, n_used, xs, wg, wu, wd):
    n_blocks = xs.shape[0] // MOE_BLOCK
    grid_spec = pltpu.PrefetchScalarGridSpec(
        num_scalar_prefetch=2,
        grid=(n_blocks,),
        in_specs=[
            pl.BlockSpec((MOE_BLOCK, D_MODEL), lambda i, be, nu: (i, 0)),
            pl.BlockSpec((None, D_MODEL, D_EXPERT), lambda i, be, nu: (be[i], 0, 0)),
            pl.BlockSpec((None, D_MODEL, D_EXPERT), lambda i, be, nu: (be[i], 0, 0)),
            pl.BlockSpec((None, D_EXPERT, D_MODEL), lambda i, be, nu: (be[i], 0, 0)),
        ],
        out_specs=pl.BlockSpec((MOE_BLOCK, D_MODEL), lambda i, be, nu: (i, 0)),
    )
    return pl.pallas_call(
        _expert_kernel,
        grid_spec=grid_spec,
        out_shape=jax.ShapeDtypeStruct(xs.shape, F32),
        compiler_params=_params("arbitrary"),
        name="moe_experts",
    )(block_e, n_used, xs, wg, wu, wd)


def _combine_kernel(slot_ref, h_ref, route_ref, g_ref, yb_hbm, out_ref, buf_a, buf_b, sem):
    def issue(j, carry):
        _row_copy(yb_hbm, slot_ref[2 * j], buf_a, j, sem).start()
        _row_copy(yb_hbm, slot_ref[2 * j + 1], buf_b, j, sem).start()
        return carry

    def drain(j, carry):
        _row_copy(yb_hbm, 0, buf_a, 0, sem).wait()
        _row_copy(yb_hbm, 0, buf_b, 0, sem).wait()
        return carry

    lax.fori_loop(0, ROW_TILE, issue, 0, unroll=DMA_UNROLL)
    lax.fori_loop(0, ROW_TILE, drain, 0, unroll=DMA_UNROLL)
    route = route_ref[...]
    h = h_ref[...] + (route[:, 2:3] * buf_a[...] + route[:, 3:4] * buf_b[...])
    var = jnp.mean(h * h, axis=-1, keepdims=True)
    out_ref[...] = h * lax.rsqrt(var + NORM_EPS) * g_ref[...]


def _combine(slots, h, route, g_final, yb):
    n = h.shape[0]
    row = lambda i: (i, 0)
    return pl.pallas_call(
        _combine_kernel,
        grid=(n // ROW_TILE,),
        in_specs=[
            pl.BlockSpec((2 * ROW_TILE,), lambda i: (i,), memory_space=pltpu.SMEM),
            pl.BlockSpec((ROW_TILE, D_MODEL), row),
            pl.BlockSpec((ROW_TILE, LANES), row),
            pl.BlockSpec((1, D_MODEL), lambda i: (0, 0)),
            pl.BlockSpec(memory_space=pl.ANY),
        ],
        out_specs=pl.BlockSpec((ROW_TILE, D_MODEL), row),
        out_shape=jax.ShapeDtypeStruct((n, D_MODEL), F32),
        scratch_shapes=[
            pltpu.VMEM((ROW_TILE, D_MODEL), F32),
            pltpu.VMEM((ROW_TILE, D_MODEL), F32),
            pltpu.SemaphoreType.DMA(()),
        ],
        compiler_params=_params("arbitrary"),
        name="moe_combine",
    )(slots, h, route, g_final.reshape(1, D_MODEL), yb)


def _slot_assignment(route, n_blocks):
    flat_e = route[:, 0:2].astype(jnp.int32).reshape(-1)
    onehot = (flat_e[:, None] == jnp.arange(N_EXPERTS, dtype=jnp.int32)[None, :]).astype(jnp.int32)
    running = jnp.cumsum(onehot, axis=0)
    rank = jnp.sum(onehot * running, axis=1) - 1
    counts = running[-1]
    padded = ((counts + MOE_BLOCK - 1) // MOE_BLOCK) * MOE_BLOCK
    pad_end = jnp.cumsum(padded)
    pad_start = pad_end - padded
    slots = jnp.sum(onehot * pad_start[None, :], axis=1) + rank
    block_start = jnp.arange(n_blocks, dtype=jnp.int32) * MOE_BLOCK
    block_e = jnp.minimum(jnp.sum((pad_end[None, :] <= block_start[:, None]).astype(jnp.int32), axis=1), N_EXPERTS - 1)
    n_used = (pad_end[-1:] // MOE_BLOCK).astype(jnp.int32)
    return slots.astype(jnp.int32), block_e, n_used


def _moe_and_final_norm(h, hn, route, wg, wu, wd, g_final):
    n = h.shape[0]
    n_blocks = (2 * n) // MOE_BLOCK + N_EXPERTS
    slots, block_e, n_used = _slot_assignment(route, n_blocks)
    xs = _dispatch(slots, hn, n_blocks * MOE_BLOCK)
    yb = _experts(block_e, n_used, xs, wg, wu, wd)
    return _combine(slots, h, route, g_final, yb)


def kernel(x_prompt, x_sample, cache_kv_w128, cache_kv_w512, cache_kv_w2048, state_ssm, g_attn_norm, w_in, ssm_log_dt, ssm_a_re, ssm_a_im, ssm_b_re, ssm_b_im, ssm_c_re, ssm_c_im, ssm_d, w_glu, w_attn_branch, w_out, g_ffn_norm, w_router_group, b_router_group, w_router_expert, b_router_expert, w_exp_gate, w_exp_up, w_exp_down, g_final):
    batch, seq, _ = x_prompt.shape
    dec_batch, dec_seq, _ = x_sample.shape
    past_len = cache_kv_w2048.shape[2]
    layer = 0

    w_in_b = w_in[layer].astype(BF16)
    wa = w_attn_branch[layer].astype(BF16)
    wglu = w_glu[layer].astype(BF16)
    wout = w_out[layer].astype(BF16)
    wg = w_exp_gate[layer].astype(BF16)
    wu = w_exp_up[layer].astype(BF16)
    wd = w_exp_down[layer].astype(BF16)
    gffn = g_ffn_norm[layer].reshape(1, D_MODEL)
    pad = LANES - N_EXPERT_GROUPS - N_EXPERTS
    wr = jnp.concatenate([w_router_group[layer], w_router_expert[layer], jnp.zeros((D_MODEL, pad), F32)], axis=1)
    br = jnp.concatenate([b_router_group[layer], b_router_expert[layer], jnp.zeros((pad,), F32)]).reshape(1, LANES)
    ssm_w = _ssm_weights(ssm_log_dt[layer], ssm_a_re[layer], ssm_a_im[layer], ssm_b_re[layer], ssm_b_im[layer],
                         ssm_c_re[layer], ssm_c_im[layer])
    d_flat = ssm_d[layer].astype(F32).reshape(1, SSM_WIDTH)

    def time_major(a, nb, steps):
        return jnp.transpose(a.reshape(nb, steps, -1), (1, 0, 2)).reshape(nb * steps, -1)

    def batch_major(a, nb, steps):
        return jnp.transpose(a.reshape(steps, nb, -1), (1, 0, 2)).reshape(nb * steps, -1)

    xp = x_prompt.reshape(batch * seq, D_MODEL)
    q_p, kv_p, kv_t_p, u_p, gates_p = _in_projection_prompt(xp, g_attn_norm[layer], w_in_b, batch, seq)
    attn_in = []
    lse_in = []
    for g, (win, dil) in enumerate(ATTN_GROUPS):
        o, lse = _prompt_attention(q_p[g], kv_p[g], g, dil, batch, seq)
        attn_in.append(o)
        lse_in.append(lse)
    zeros_state = jnp.zeros((batch, SSM_COLS), F32)
    y_tm, hr_p, hi_p = _ssm(time_major(u_p, batch, seq), zeros_state, zeros_state, ssm_w, d_flat, batch)
    y_p = batch_major(y_tm, batch, seq)
    h_p, hn_p, route_p = _mix(attn_in + lse_in, y_p, gates_p, xp, wa, wglu, wout, gffn, wr, br)
    out_p = _moe_and_final_norm(h_p, hn_p, route_p, wg, wu, wd, g_final)

    xs = x_sample.reshape(dec_batch * dec_seq, D_MODEL)
    pos_s = past_len + (jnp.arange(dec_batch * dec_seq, dtype=jnp.int32) % dec_seq)
    q_s, kv0_s, kv1_s, kv2_s, u_s, gates_s = _in_projection_decode(xs, g_attn_norm[layer], w_in_b, pos_s)
    kv_s = (kv0_s, kv1_s, kv2_s)
    caches = (cache_kv_w128[layer], cache_kv_w512[layer], cache_kv_w2048[layer])
    attn_s = _decode_attention(q_s, caches, kv_s, dec_batch, dec_seq)
    st = state_ssm[layer].astype(F32).reshape(dec_batch, SSM_COLS, 2)
    ys_tm, hr_s, hi_s = _ssm(time_major(u_s, dec_batch, dec_seq), st[:, :, 0], st[:, :, 1], ssm_w, d_flat, dec_batch)
    y_s = batch_major(ys_tm, dec_batch, dec_seq)
    h_s, hn_s, route_s = _mix([attn_s], y_s, gates_s, xs, wa, wglu, wout, gffn, wr, br)
    out_s = _moe_and_final_norm(h_s, hn_s, route_s, wg, wu, wd, g_final)

    kv_tail = (2, HEADS_PER_GROUP, HEAD_DIM)
    outs = [out_p.reshape(batch, seq, D_MODEL), out_s.reshape(dec_batch, dec_seq, D_MODEL)]
    for g, (win, dil) in enumerate(ATTN_GROUPS):
        keep = min(win, seq)
        rows_p = jnp.transpose(kv_t_p[g].reshape(batch, 2, HEADS_PER_GROUP, HEAD_DIM, keep), (0, 4, 1, 2, 3))
        outs.append(rows_p[None])
        outs.append(kv_s[g].reshape((1, dec_batch, dec_seq) + kv_tail))
    outs.append(jnp.stack([hr_p, hi_p], axis=-1).reshape(1, batch, N_SSM_GROUPS, SSM_STATE, 2))
    outs.append(jnp.stack([hr_s, hi_s], axis=-1).reshape(1, dec_batch, N_SSM_GROUPS, SSM_STATE, 2))
    return tuple(outs)
```

```python
import functools
import math

import jax
import jax.numpy as jnp
from jax import lax
from jax.experimental import pallas as pl
from jax.experimental.pallas import tpu as pltpu

F32 = jnp.float32
BF16 = jnp.bfloat16

D_MODEL = 1024
HEAD_DIM = 64
HEADS_PER_GROUP = 8
GROUP_WIDTH = HEADS_PER_GROUP * HEAD_DIM
ATTN_GROUPS = ((128, 1), (512, 4), (2048, 16))
N_ATTN_GROUPS = len(ATTN_GROUPS)
ATTN_WIDTH = N_ATTN_GROUPS * GROUP_WIDTH
ROT_DIM = HEAD_DIM // 4
ROPE_THETA = 500000.0
WINDOW_KEYS = 128
SSM_GROUP_CH = 16
SSM_WIDTH = D_MODEL // 2
N_SSM_GROUPS = SSM_WIDTH // SSM_GROUP_CH
SSM_STATE = 64
SSM_COLS = N_SSM_GROUPS * SSM_STATE
IN_WIDTH = 3 * ATTN_WIDTH + SSM_WIDTH + 2 * D_MODEL
N_EXPERT_GROUPS = 4
EXPERTS_PER_GROUP = 8
N_EXPERTS = N_EXPERT_GROUPS * EXPERTS_PER_GROUP
D_EXPERT = D_MODEL // 4
NORM_EPS = 1e-6

LANES = 128
SUBLANES = 8
VMEM_LIMIT = 56 * 1024 * 1024

PROJ_TILE = 256
ATTN_TILE = 128
ATTN_RESIDUES_PER_STEP = 4
ATTN_TILES_PER_STEP = 8
SSM_ROWS = 512
MOE_BLOCK = 256
ROW_TILE = 512
DMA_UNROLL = 8


def _params(*sem):
    return pltpu.CompilerParams(dimension_semantics=sem, vmem_limit_bytes=VMEM_LIMIT)


def _normed_input(x_ref, g_ref):
    x = x_ref[...]
    var = jnp.mean(x * x, axis=-1, keepdims=True)
    return (x * lax.rsqrt(var + NORM_EPS) * g_ref[...]).astype(BF16)


def _rope_fn(cos_ref, sin_ref):
    cos = cos_ref[...]
    sin = sin_ref[...]
    lane = lax.broadcasted_iota(jnp.int32, cos.shape, 1) % HEAD_DIM
    first_half = lane < ROT_DIM // 2
    rotated = lane < ROT_DIM

    def rope_chunk(c):
        partner = jnp.where(first_half, pltpu.roll(c, LANES - ROT_DIM // 2, 1), pltpu.roll(c, ROT_DIM // 2, 1))
        return jnp.where(rotated, c * cos + partner * sin, c)

    def rope(t):
        return jnp.concatenate([rope_chunk(t[:, j * LANES:(j + 1) * LANES]) for j in range(GROUP_WIDTH // LANES)], axis=1)

    return rope


def _projection_tiles(xn, w_ref):
    for c in range(IN_WIDTH // GROUP_WIDTH):
        yield c, jnp.dot(xn, w_ref[:, c * GROUP_WIDTH:(c + 1) * GROUP_WIDTH], preferred_element_type=F32)


def _inproj_decode_kernel(x_ref, g_ref, w_ref, cos_ref, sin_ref, q_ref, kv0_ref, kv1_ref, kv2_ref, u_ref, gate_ref):
    rope = _rope_fn(cos_ref, sin_ref)
    kv_refs = (kv0_ref, kv1_ref, kv2_ref)
    for c, acc in _projection_tiles(_normed_input(x_ref, g_ref), w_ref):
        if c < 3:
            q_ref[:, c * GROUP_WIDTH:(c + 1) * GROUP_WIDTH] = (rope(acc) * (HEAD_DIM ** -0.5)).astype(BF16)
        elif c < 6:
            kv_refs[c - 3][:, :GROUP_WIDTH] = rope(acc)
        elif c < 9:
            kv_refs[c - 6][:, GROUP_WIDTH:] = acc
        elif c == 9:
            u_ref[...] = acc
        else:
            gate_ref[:, (c - 10) * GROUP_WIDTH:(c - 9) * GROUP_WIDTH] = acc


def _inproj_prompt_kernel(x_ref, g_ref, w_ref, cos_ref, sin_ref,
                          q0_ref, q1_ref, q2_ref, kv0_ref, kv1_ref, kv2_ref, kt0_ref, kt1_ref, kt2_ref,
                          u_ref, gate_ref, *scratch, tiles_per_seq, keeps):
    rope = _rope_fn(cos_ref, sin_ref)
    tm = x_ref.shape[0]
    tile_in_seq = pl.program_id(0) % tiles_per_seq
    q_refs = (q0_ref, q1_ref, q2_ref)
    kv_refs = (kv0_ref, kv1_ref, kv2_ref)
    kt_refs = (kt0_ref, kt1_ref, kt2_ref)

    dilated = [g for g, (_, d) in enumerate(ATTN_GROUPS) if d > 1]

    def store_rows(dst_ref, val, kind, g, col0, col_stride):
        dil = ATTN_GROUPS[g][1]
        if dil == 1:
            dst_ref[:, col0:col0 + GROUP_WIDTH] = val.astype(BF16)
            return
        scr = scratch[kind * len(dilated) + dilated.index(g)]
        n_chunks = GROUP_WIDTH // LANES
        for j in range(n_chunks):
            scr[j] = val[:, j * LANES:(j + 1) * LANES]
        rows = tm // dil
        for r in range(dil):
            piece = jnp.concatenate([scr[j, pl.ds(r, rows, stride=dil), :] for j in range(n_chunks)], axis=1)
            dst_ref[:, col0 + r * col_stride:col0 + r * col_stride + GROUP_WIDTH] = piece.astype(BF16)

    def store_transposed(dst_ref, half, val, keep):
        width = min(keep, tm)
        first_tile = tiles_per_seq - max(keep // tm, 1)

        @pl.when(tile_in_seq >= first_tile)
        def _():
            dst_ref[half] = val[tm - width:, :].T

    for c, acc in _projection_tiles(_normed_input(x_ref, g_ref), w_ref):
        g = c % N_ATTN_GROUPS
        if c < 3:
            store_rows(q_refs[g], rope(acc) * (HEAD_DIM ** -0.5), 0, g, 0, GROUP_WIDTH)
        elif c < 6:
            k = rope(acc)
            store_rows(kv_refs[g], k, 1, g, 0, 2 * GROUP_WIDTH)
            store_transposed(kt_refs[g], 0, k, keeps[g])
        elif c < 9:
            store_rows(kv_refs[g], acc, 2, g, GROUP_WIDTH, 2 * GROUP_WIDTH)
            store_transposed(kt_refs[g], 1, acc, keeps[g])
        elif c == 9:
            u_ref[...] = acc
        else:
            gate_ref[:, (c - 10) * GROUP_WIDTH:(c - 9) * GROUP_WIDTH] = acc


def _rope_tables(pos):
    half = ROT_DIM // 2
    inv_freq = ROPE_THETA ** (-(jnp.arange(half, dtype=F32) / half))
    ang = pos.astype(F32)[:, None] * inv_freq[None, :]
    cos, sin = jnp.cos(ang), jnp.sin(ang)
    n = pos.shape[0]
    rest = HEAD_DIM - ROT_DIM
    cos_h = jnp.concatenate([cos, cos, jnp.ones((n, rest), F32)], axis=1)
    sin_h = jnp.concatenate([-sin, sin, jnp.zeros((n, rest), F32)], axis=1)
    return jnp.tile(cos_h, (1, LANES // HEAD_DIM)), jnp.tile(sin_h, (1, LANES // HEAD_DIM))


def _inproj_in_specs(tm, n_pos_tiles):
    const = lambda i: (0, 0)
    tab = lambda i: (i % n_pos_tiles, 0)
    return [
        pl.BlockSpec((tm, D_MODEL), lambda i: (i, 0)),
        pl.BlockSpec((1, D_MODEL), const),
        pl.BlockSpec((D_MODEL, IN_WIDTH), const, pipeline_mode=pl.Buffered(1)),
        pl.BlockSpec((tm, LANES), tab),
        pl.BlockSpec((tm, LANES), tab),
    ]


def _in_projection_decode(x2d, g, w_bf16, pos):
    n = x2d.shape[0]
    tm = PROJ_TILE
    cos_t, sin_t = _rope_tables(pos)
    row = lambda i: (i, 0)
    widths = (ATTN_WIDTH, 2 * GROUP_WIDTH, 2 * GROUP_WIDTH, 2 * GROUP_WIDTH, SSM_WIDTH, 2 * D_MODEL)
    dtypes = (BF16, F32, F32, F32, F32, F32)
    return pl.pallas_call(
        _inproj_decode_kernel,
        grid=(n // tm,),
        in_specs=_inproj_in_specs(tm, pos.shape[0] // tm),
        out_specs=tuple(pl.BlockSpec((tm, w), row) for w in widths),
        out_shape=tuple(jax.ShapeDtypeStruct((n, w), d) for w, d in zip(widths, dtypes)),
        compiler_params=_params("parallel"),
        name="in_projection_decode",
    )(x2d, g.reshape(1, D_MODEL), w_bf16, cos_t, sin_t)


def _in_projection_prompt(x2d, g, w_bf16, batch, seq):
    n = batch * seq
    tm = PROJ_TILE
    tiles_per_seq = seq // tm
    cos_t, sin_t = _rope_tables(jnp.arange(seq, dtype=jnp.int32))
    keeps = tuple(min(win, seq) for win, _ in ATTN_GROUPS)
    row = lambda i: (i, 0)
    out_specs, out_shape = [], []
    for width in (GROUP_WIDTH, 2 * GROUP_WIDTH):
        for _, dil in ATTN_GROUPS:
            out_specs.append(pl.BlockSpec((tm // dil, dil * width), row))
            out_shape.append(jax.ShapeDtypeStruct((n // dil, dil * width), BF16))
    for keep in keeps:
        blk = min(keep, tm)
        first_tile = tiles_per_seq - max(keep // tm, 1)
        out_specs.append(pl.BlockSpec(
            (None, 2, GROUP_WIDTH, blk),
            lambda i, first_tile=first_tile: (i // tiles_per_seq, 0, 0, jnp.maximum(i % tiles_per_seq - first_tile, 0))))
        out_shape.append(jax.ShapeDtypeStruct((batch, 2, GROUP_WIDTH, keep), F32))
    out_specs += [pl.BlockSpec((tm, SSM_WIDTH), row), pl.BlockSpec((tm, 2 * D_MODEL), row)]
    out_shape += [jax.ShapeDtypeStruct((n, SSM_WIDTH), F32), jax.ShapeDtypeStruct((n, 2 * D_MODEL), F32)]
    outs = pl.pallas_call(
        functools.partial(_inproj_prompt_kernel, tiles_per_seq=tiles_per_seq, keeps=keeps),
        grid=(n // tm,),
        in_specs=_inproj_in_specs(tm, tiles_per_seq),
        out_specs=tuple(out_specs),
        out_shape=tuple(out_shape),
        scratch_shapes=[pltpu.VMEM((GROUP_WIDTH // LANES, tm, LANES), F32)
                        for _ in range(3 * sum(1 for _, d in ATTN_GROUPS if d > 1))],
        compiler_params=_params("arbitrary"),
        name="in_projection_prompt",
    )(x2d, g.reshape(1, D_MODEL), w_bf16, cos_t, sin_t)
    return outs[0:3], outs[3:6], outs[6:9], outs[9], outs[10]


def _window_attn_kernel(q_ref, kvc_ref, *rest, windowed, residues, sub_tiles):
    if windowed:
        kvp_ref, o_ref, lse_ref = rest
    else:
        o_ref, lse_ref = rest
    n_keys = 2 * ATTN_TILE if windowed else ATTN_TILE
    rows = lax.broadcasted_iota(jnp.int32, (ATTN_TILE, n_keys), 0)
    cols = lax.broadcasted_iota(jnp.int32, (ATTN_TILE, n_keys), 1)
    if windowed:
        band = (cols >= rows) & (cols <= rows + WINDOW_KEYS)
        first_tile_band = band & ((cols >= ATTN_TILE) | (pl.program_id(2) > 0))
    else:
        band = cols <= rows
    low_head = lax.broadcasted_iota(jnp.int32, (ATTN_TILE, LANES), 1) < HEAD_DIM
    for r in range(residues):
        for s in range(sub_tiles):
            rs = slice(s * ATTN_TILE, (s + 1) * ATTN_TILE)
            k0 = r * 2 * GROUP_WIDTH
            q = q_ref[rs, r * GROUP_WIDTH:(r + 1) * GROUP_WIDTH]
            k = kvc_ref[rs, k0:k0 + GROUP_WIDTH]
            v = kvc_ref[rs, k0 + GROUP_WIDTH:k0 + 2 * GROUP_WIDTH]
            valid = band
            if windowed:
                if s == 0:
                    prev_ref, ps, valid = kvp_ref, slice(0, ATTN_TILE), first_tile_band
                else:
                    prev_ref, ps = kvc_ref, slice((s - 1) * ATTN_TILE, s * ATTN_TILE)
                k = jnp.concatenate([prev_ref[ps, k0:k0 + GROUP_WIDTH], k], axis=0)
                v = jnp.concatenate([prev_ref[ps, k0 + GROUP_WIDTH:k0 + 2 * GROUP_WIDTH], v], axis=0)
            valid2 = jnp.concatenate([valid, valid], axis=0)
            for j in range(GROUP_WIDTH // LANES):
                sl = slice(j * LANES, (j + 1) * LANES)
                osl = slice(r * GROUP_WIDTH + j * LANES, r * GROUP_WIDTH + (j + 1) * LANES)
                qj, kj, vj = q[:, sl], k[:, sl], v[:, sl]
                zero = jnp.zeros_like(qj)
                qm = jnp.concatenate([jnp.where(low_head, qj, zero), jnp.where(low_head, zero, qj)], axis=0)
                sc = lax.dot_general(qm, kj, (((1,), (1,)), ((), ())), preferred_element_type=F32)
                sc = jnp.where(valid2, sc, -jnp.inf)
                m = jnp.max(sc, axis=1, keepdims=True)
                p = jnp.exp(sc - m)
                l = jnp.sum(p, axis=1, keepdims=True)
                o2 = jnp.dot(p.astype(BF16), vj, preferred_element_type=F32) / l
                lse2 = jnp.broadcast_to(m + jnp.log(l), o2.shape)
                o_ref[rs, osl] = jnp.where(low_head, o2[:ATTN_TILE], o2[ATTN_TILE:])
                lse_ref[rs, osl] = jnp.where(low_head, lse2[:ATTN_TILE], lse2[ATTN_TILE:])


def _prompt_attention(q, kv, group, dilation, batch, seq):
    tg = seq // dilation
    n_tiles = tg // ATTN_TILE
    windowed = n_tiles > 1
    residues = min(dilation, ATTN_RESIDUES_PER_STEP)
    sub_tiles = min(n_tiles, ATTN_TILES_PER_STEP // residues)
    rows = sub_tiles * ATTN_TILE
    q3 = q.reshape(batch, tg, dilation * GROUP_WIDTH)
    kv3 = kv.reshape(batch, tg, dilation * 2 * GROUP_WIDTH)
    in_specs = [
        pl.BlockSpec((None, rows, residues * GROUP_WIDTH), lambda b, r, t: (b, t, r)),
        pl.BlockSpec((None, rows, residues * 2 * GROUP_WIDTH), lambda b, r, t: (b, t, r)),
    ]
    args = [q3, kv3]
    if windowed:
        in_specs.append(pl.BlockSpec((None, ATTN_TILE, residues * 2 * GROUP_WIDTH),
                                     lambda b, r, t: (b, jnp.maximum(t * sub_tiles - 1, 0), r)))
        args.append(kv3)
    out_spec = pl.BlockSpec((None, rows, residues * GROUP_WIDTH), lambda b, r, t: (b, t, r))
    out_sds = jax.ShapeDtypeStruct((batch, tg, dilation * GROUP_WIDTH), F32)
    o, lse = pl.pallas_call(
        functools.partial(_window_attn_kernel, windowed=windowed, residues=residues, sub_tiles=sub_tiles),
        grid=(batch, dilation // residues, n_tiles // sub_tiles),
        in_specs=in_specs,
        out_specs=(out_spec, out_spec),
        out_shape=(out_sds, out_sds),
        compiler_params=_params("parallel", "parallel", "arbitrary"),
        name=f"prompt_attention_g{group}",
    )(*args)
    flat = (batch * tg, dilation * GROUP_WIDTH)
    return o.reshape(flat), lse.reshape(flat)


def _sublane_total(x):
    x = x + pltpu.roll(x, 4, 0)
    x = x + pltpu.roll(x, 2, 0)
    return x + pltpu.roll(x, 1, 0)


def _head_sum(prod):
    width = prod.shape[1]
    row = lax.broadcasted_iota(jnp.int32, (HEADS_PER_GROUP, width), 0)
    out = jnp.zeros((HEADS_PER_GROUP, width), F32)
    for h in range(HEADS_PER_GROUP):
        part = prod[h * HEAD_DIM:h * HEAD_DIM + SUBLANES]
        for j in range(1, HEAD_DIM // SUBLANES):
            part = part + prod[h * HEAD_DIM + j * SUBLANES:h * HEAD_DIM + (j + 1) * SUBLANES]
        out = jnp.where(row == h, _sublane_total(part), out)
    return out


def _head_expand(x):
    width = x.shape[1]
    pieces = []
    for h in range(HEADS_PER_GROUP):
        pieces.extend([jnp.broadcast_to(x[h:h + 1, :], (SUBLANES, width))] * (HEAD_DIM // SUBLANES))
    return jnp.concatenate(pieces, axis=0)


def _split_dot(acc, sel, terms=2):
    out, rem = None, acc
    for _ in range(terms):
        hi = rem.astype(BF16)
        part = jnp.dot(hi, sel, preferred_element_type=F32)
        out = part if out is None else out + part
        rem = rem - hi.astype(F32)
    return out


def _decode_attn_kernel(q_ref, n0_ref, n1_ref, n2_ref, c0_ref, c1_ref, c2_ref, o_ref, *, dec_seq):
    nq = SUBLANES
    n_sub = HEAD_DIM // SUBLANES
    neg = -jnp.inf
    q_t = q_ref[...].T
    step = lax.broadcasted_iota(jnp.int32, (HEADS_PER_GROUP, nq), 1)
    step_wide = lax.broadcasted_iota(jnp.int32, (GROUP_WIDTH, nq), 1)
    real_step = step < dec_seq
    new_refs = (n0_ref, n1_ref, n2_ref)
    cache_refs = (c0_ref, c1_ref, c2_ref)

    def column(x, t, width):
        return jnp.broadcast_to(x[:, t:t + 1], (x.shape[0], width))

    def place(cols):
        rows = cols[0].shape[0]
        lane = lax.broadcasted_iota(jnp.int32, (rows, nq), 1)
        out = jnp.zeros((rows, nq), F32)
        for t, c in enumerate(cols):
            out = jnp.where(lane == t, jnp.broadcast_to(c, (rows, nq)), out)
        return out

    outs, lses = [], []
    for g, (win, dil) in enumerate(ATTN_GROUPS):
        c_ref = cache_refs[g]
        kv_new = new_refs[g][...]
        kn_t = kv_new[:, :GROUP_WIDTH].T
        vn_t = kv_new[:, GROUP_WIDTH:].T
        q_g = q_t[g * GROUP_WIDTH:(g + 1) * GROUP_WIDTH, :]
        pos = lax.broadcasted_iota(jnp.int32, (HEADS_PER_GROUP, win), 1)
        row8 = lax.broadcasted_iota(jnp.int32, (HEADS_PER_GROUP, win), 0)
        n_tiles = win // LANES

        def cache_scores(q_pat):
            s = jnp.zeros((HEADS_PER_GROUP, win), F32)
            for h in range(HEADS_PER_GROUP):
                part = None
                for j in range(n_sub):
                    r0 = h * HEAD_DIM + j * SUBLANES
                    qp = q_pat[r0:r0 + SUBLANES, :]
                    term = c_ref[0, r0:r0 + SUBLANES, :] * jnp.concatenate([qp] * n_tiles, axis=1)
                    part = term if part is None else part + term
                s = jnp.where(row8 == h, _sublane_total(part), s)
            return s

        def weighted_values(p):
            rows = []
            for h in range(HEADS_PER_GROUP):
                ph = jnp.broadcast_to(p[h:h + 1, :], (SUBLANES, win))
                for j in range(n_sub):
                    r0 = h * HEAD_DIM + j * SUBLANES
                    prod = ph * c_ref[1, r0:r0 + SUBLANES, :]
                    a = prod[:, :LANES]
                    for tile in range(1, n_tiles):
                        a = a + prod[:, tile * LANES:(tile + 1) * LANES]
                    rows.append(a)
            return jnp.concatenate(rows, axis=0)

        if dil == 1:
            spread = (lax.broadcasted_iota(jnp.int32, (nq, dec_seq * LANES), 0)
                      == lax.broadcasted_iota(jnp.int32, (nq, dec_seq * LANES), 1) // LANES).astype(BF16)
            q_cols = _split_dot(q_g, spread, terms=1)
            kn_cols = _split_dot(kn_t, spread, terms=3)
            vn_cols = _split_dot(vn_t, spread, terms=3)
            s_new = [_head_sum(q_g * kn_cols[:, u * LANES:u * LANES + nq]) for u in range(dec_seq)]
            ok_new = [(step >= u) & real_step for u in range(dec_seq)]
            s_cache = [cache_scores(q_cols[:, t * LANES:(t + 1) * LANES]) for t in range(dec_seq)]
            m = place([jnp.max(jnp.where(pos >= t, s_cache[t], neg), axis=1, keepdims=True) for t in range(dec_seq)])
            for u in range(dec_seq):
                m = jnp.maximum(m, jnp.where(ok_new[u], s_new[u], neg))
            e_cache = [jnp.where(pos >= t, jnp.exp(s_cache[t] - m[:, t:t + 1]), 0.0) for t in range(dec_seq)]
            e_new = [jnp.where(ok_new[u], jnp.exp(s_new[u] - m), 0.0) for u in range(dec_seq)]
            denom = place([jnp.sum(e, axis=1, keepdims=True) for e in e_cache])
            for u in range(dec_seq):
                denom = denom + e_new[u]
            denom = jnp.where(real_step, denom, 1.0)
            inv = 1.0 / denom
            cols = []
            for t in range(dec_seq):
                acc = weighted_values(e_cache[t] * inv[:, t:t + 1])
                cols.append(jnp.sum(acc, axis=1, keepdims=True))
            o_g = place(cols)
            for u in range(dec_seq):
                o_g = o_g + _head_expand(e_new[u] * inv) * vn_cols[:, u * LANES:u * LANES + nq]
        else:
            res = pos % dil
            t_i = lax.broadcasted_iota(jnp.int32, (nq, LANES), 0)
            l_i = lax.broadcasted_iota(jnp.int32, (nq, LANES), 1)
            residue_pat = ((l_i % dil == t_i) & (t_i < dec_seq)).astype(BF16)
            s_cache = cache_scores(_split_dot(q_g, residue_pat, terms=1))
            s_new = jnp.where(real_step, _head_sum(q_g * kn_t), 0.0)
            m_cache = place([jnp.max(jnp.where(res == t, s_cache, neg), axis=1, keepdims=True) for t in range(dec_seq)])
            m = jnp.maximum(m_cache, s_new)

            def by_position(stat):
                out = jnp.zeros((HEADS_PER_GROUP, win), F32)
                for t in range(dec_seq):
                    out = jnp.where(res == t, column(stat, t, win), out)
                return out

            e_cache = jnp.where(res < dec_seq, jnp.exp(s_cache - by_position(m)), 0.0)
            e_new = jnp.exp(s_new - m)
            denom = place([jnp.sum(jnp.where(res == t, e_cache, 0.0), axis=1, keepdims=True)
                           for t in range(dec_seq)]) + e_new
            inv = 1.0 / denom
            acc = weighted_values(e_cache * by_position(inv))
            l_i = lax.broadcasted_iota(jnp.int32, (LANES, LANES), 0)
            t_i = lax.broadcasted_iota(jnp.int32, (LANES, LANES), 1)
            sel = ((l_i % dil == t_i) & (t_i < dec_seq)).astype(BF16)
            o_g = _split_dot(acc, sel)[:, :nq] + _head_expand(e_new * inv) * vn_t
        outs.append(o_g)
        lses.append(m + jnp.log(denom))

    top = jnp.maximum(jnp.maximum(lses[0], lses[1]), lses[2])
    ws = [jnp.exp(l - top) for l in lses]
    total = ws[0] + ws[1] + ws[2]
    merged = jnp.zeros((GROUP_WIDTH, nq), F32)
    for g in range(N_ATTN_GROUPS):
        merged = merged + _head_expand(ws[g] / total) * outs[g]
    o_ref[...] = jnp.where(step_wide < dec_seq, merged, 0.0)


def _decode_attention(q, caches, new_kv, dec_batch, dec_seq):
    nq = SUBLANES

    def pad_steps(a):
        a = a.astype(F32).reshape(dec_batch, dec_seq, a.shape[-1])
        return jnp.pad(a, ((0, 0), (0, nq - dec_seq), (0, 0)))

    args = [pad_steps(q)] + [pad_steps(n) for n in new_kv]
    in_specs = [pl.BlockSpec((None, nq, ATTN_WIDTH), lambda b: (b, 0, 0))]
    in_specs += [pl.BlockSpec((None, nq, 2 * GROUP_WIDTH), lambda b: (b, 0, 0)) for _ in new_kv]
    for (win, dil), cache in zip(ATTN_GROUPS, caches):
        args.append(jnp.transpose(cache, (0, 2, 3, 4, 1)).reshape(dec_batch, 2, GROUP_WIDTH, win))
        in_specs.append(pl.BlockSpec((None, 2, GROUP_WIDTH, win), lambda b: (b, 0, 0, 0)))
    o_t = pl.pallas_call(
        functools.partial(_decode_attn_kernel, dec_seq=dec_seq),
        grid=(dec_batch,),
        in_specs=in_specs,
        out_specs=pl.BlockSpec((None, GROUP_WIDTH, nq), lambda b: (b, 0, 0)),
        out_shape=jax.ShapeDtypeStruct((dec_batch, GROUP_WIDTH, nq), F32),
        compiler_params=_params("parallel"),
        name="decode_attention",
    )(*args)
    return jnp.transpose(o_t[:, :, :dec_seq], (0, 2, 1)).reshape(dec_batch * dec_seq, GROUP_WIDTH)


def _ssm_kernel(u_ref, h0r_ref, h0i_ref, ar_ref, ai_ref, bw_ref, cr_ref, ci_ref, d_ref,
                y_ref, hr_out, hi_out, bur, bui, hr_s, hi_s, *, nb, steps):
    chunk = pl.program_id(0)
    half_in = SSM_WIDTH // 2
    half_st = SSM_COLS // 2

    @pl.when(chunk == 0)
    def _():
        hr_s[...] = h0r_ref[...]
        hi_s[...] = h0i_ref[...]

    u = u_ref[...]
    ub = u.astype(BF16)
    for hf in range(2):
        r = jnp.dot(ub[:, hf * half_in:(hf + 1) * half_in], bw_ref[hf], preferred_element_type=F32)
        bur[:, hf * half_st:(hf + 1) * half_st] = r[:, :half_st]
        bui[:, hf * half_st:(hf + 1) * half_st] = r[:, half_st:]

    lane_chunk = 4 * LANES

    def sub_batch(s, carry):
        for lc in range(SSM_COLS // lane_chunk):
            ls = slice(lc * lane_chunk, (lc + 1) * lane_chunk)
            ar = ar_ref[:, ls]
            ai = ai_ref[:, ls]
            row0 = pl.multiple_of(s * SUBLANES, SUBLANES)

            def step(t, h):
                hr, hi = h
                row = pl.multiple_of(t * nb + s * SUBLANES, SUBLANES)
                nhr = ar * hr - ai * hi + bur[pl.ds(row, SUBLANES), ls]
                nhi = ar * hi + ai * hr + bui[pl.ds(row, SUBLANES), ls]
                bur[pl.ds(row, SUBLANES), ls] = nhr
                bui[pl.ds(row, SUBLANES), ls] = nhi
                return nhr, nhi

            hr, hi = lax.fori_loop(0, steps, step, (hr_s[pl.ds(row0, SUBLANES), ls], hi_s[pl.ds(row0, SUBLANES), ls]))
            hr_s[pl.ds(row0, SUBLANES), ls] = hr
            hi_s[pl.ds(row0, SUBLANES), ls] = hi
        return carry

    lax.fori_loop(0, nb // SUBLANES, sub_batch, 0)

    for hf in range(2):
        ss = slice(hf * half_st, (hf + 1) * half_st)
        y = jnp.dot(bur[:, ss].astype(BF16), cr_ref[hf], preferred_element_type=F32)
        y = y + jnp.dot(bui[:, ss].astype(BF16), ci_ref[hf], preferred_element_type=F32)
        cs = slice(hf * half_in, (hf + 1) * half_in)
        y_ref[:, cs] = y + d_ref[:, cs] * u[:, cs]

    @pl.when(chunk == pl.num_programs(0) - 1)
    def _():
        hr_out[...] = hr_s[...]
        hi_out[...] = hi_s[...]


def _ssm_weights(ssm_log_dt, a_re, a_im, b_re, b_im, c_re, c_im):
    dt = jnp.exp(ssm_log_dt.astype(F32))[:, None]
    lam = lax.complex(a_re.astype(F32), a_im.astype(F32))
    lam_bar = jnp.exp(lam * dt)
    b_bar = ((lam_bar - 1.0) / lam)[:, :, None] * lax.complex(b_re.astype(F32), b_im.astype(F32))
    gh = N_SSM_GROUPS // 2
    eye = jnp.eye(gh, dtype=F32)

    def block_diag(m):
        return jnp.einsum('gab,gh->gahb', m, eye).reshape(gh * m.shape[1], gh * m.shape[2])

    bw, cr, ci = [], [], []
    for hf in range(2):
        gs = slice(hf * gh, (hf + 1) * gh)
        b_t = jnp.transpose(b_bar[gs], (0, 2, 1))
        bw.append(jnp.concatenate([block_diag(jnp.real(b_t)), block_diag(jnp.imag(b_t))], axis=1))
        cr.append(block_diag(jnp.transpose(c_re[gs].astype(F32), (0, 2, 1))))
        ci.append(block_diag(jnp.transpose(-c_im[gs].astype(F32), (0, 2, 1))))
    ar = jnp.broadcast_to(jnp.real(lam_bar).reshape(1, SSM_COLS), (SUBLANES, SSM_COLS))
    ai = jnp.broadcast_to(jnp.imag(lam_bar).reshape(1, SSM_COLS), (SUBLANES, SSM_COLS))
    return ar, ai, jnp.stack(bw).astype(BF16), jnp.stack(cr).astype(BF16), jnp.stack(ci).astype(BF16)


def _ssm(u_tm, h0r, h0i, weights, d_flat, nb):
    ar, ai, bw, cr, ci = weights
    rows = u_tm.shape[0]
    steps = SSM_ROWS // nb
    const2 = lambda c: (0, 0)
    const3 = lambda c: (0, 0, 0)
    state_sds = jax.ShapeDtypeStruct((nb, SSM_COLS), F32)
    return pl.pallas_call(
        functools.partial(_ssm_kernel, nb=nb, steps=steps),
        grid=(rows // SSM_ROWS,),
        in_specs=[
            pl.BlockSpec((SSM_ROWS, SSM_WIDTH), lambda c: (c, 0)),
            pl.BlockSpec((nb, SSM_COLS), const2),
            pl.BlockSpec((nb, SSM_COLS), const2),
            pl.BlockSpec((SUBLANES, SSM_COLS), const2),
            pl.BlockSpec((SUBLANES, SSM_COLS), const2),
            pl.BlockSpec(bw.shape, const3),
            pl.BlockSpec(cr.shape, const3),
            pl.BlockSpec(ci.shape, const3),
            pl.BlockSpec((1, SSM_WIDTH), const2),
        ],
        out_specs=(
            pl.BlockSpec((SSM_ROWS, SSM_WIDTH), lambda c: (c, 0)),
            pl.BlockSpec((nb, SSM_COLS), const2),
            pl.BlockSpec((nb, SSM_COLS), const2),
        ),
        out_shape=(jax.ShapeDtypeStruct((rows, SSM_WIDTH), F32), state_sds, state_sds),
        scratch_shapes=[
            pltpu.VMEM((SSM_ROWS, SSM_COLS), F32),
            pltpu.VMEM((SSM_ROWS, SSM_COLS), F32),
            pltpu.VMEM((nb, SSM_COLS), F32),
            pltpu.VMEM((nb, SSM_COLS), F32),
        ],
        compiler_params=_params("arbitrary"),
        name="s5_scan",
    )(u_tm, h0r, h0i, ar, ai, bw, cr, ci, d_flat)


def _mix_kernel(*refs, merged_attn):
    if merged_attn:
        attn_ref = refs[0]
        rest = refs[1:13]
    else:
        group_refs = refs[0:2 * N_ATTN_GROUPS]
        rest = refs[2 * N_ATTN_GROUPS:2 * N_ATTN_GROUPS + 12]
        scratch = refs[2 * N_ATTN_GROUPS + 12:]
    (y_ref, gate_ref, x_ref, wa_ref, wglu_ref, wout_ref, gffn_ref, wr_ref, br_ref,
     h_ref, hn_ref, route_ref) = rest
    tm = x_ref.shape[0]
    if merged_attn:
        attn = attn_ref[...]
    else:
        natural = []
        for idx, ref in enumerate(group_refs):
            dil = ATTN_GROUPS[idx % N_ATTN_GROUPS][1]
            if dil == 1:
                natural.append(ref[...])
                continue
            scr = scratch[idx]
            n_chunks = GROUP_WIDTH // LANES
            for r in range(dil):
                for j in range(n_chunks):
                    col = r * GROUP_WIDTH + j * LANES
                    scr[j, pl.ds(r, tm // dil, stride=dil), :] = ref[:, col:col + LANES]
            natural.append(jnp.concatenate([scr[j] for j in range(n_chunks)], axis=1))
        os, ls = natural[:N_ATTN_GROUPS], natural[N_ATTN_GROUPS:]
        top = jnp.maximum(jnp.maximum(ls[0], ls[1]), ls[2])
        es = [jnp.exp(l - top) for l in ls]
        attn = (es[0] * os[0] + es[1] * os[1] + es[2] * os[2]) / (es[0] + es[1] + es[2])
    attn_out = jnp.dot(attn.astype(BF16), wa_ref[...], preferred_element_type=F32)
    glu = jnp.dot(jax.nn.gelu(y_ref[...]).astype(BF16), wglu_ref[...], preferred_element_type=F32)
    ssm_out = glu[:, :D_MODEL] * jax.nn.sigmoid(glu[:, D_MODEL:])
    merged = jax.nn.sigmoid(gate_ref[:, :D_MODEL]) * attn_out + jax.nn.sigmoid(gate_ref[:, D_MODEL:]) * ssm_out
    h = x_ref[...] + jnp.dot(merged.astype(BF16), wout_ref[...], preferred_element_type=F32)
    h_ref[...] = h
    var = jnp.mean(h * h, axis=-1, keepdims=True)
    hn = h * lax.rsqrt(var + NORM_EPS) * gffn_ref[...]
    hn_ref[...] = hn
    logits = jnp.dot(hn, wr_ref[...], preferred_element_type=F32, precision=lax.Precision.HIGHEST) + br_ref[...]
    lane = lax.broadcasted_iota(jnp.int32, logits.shape, 1).astype(F32)
    far = float(LANES)

    def first_argmax(vals):
        top_v = jnp.max(vals, axis=1, keepdims=True)
        return top_v, jnp.min(jnp.where(vals == top_v, lane, far), axis=1, keepdims=True)

    group_logits = jnp.where(lane < N_EXPERT_GROUPS, logits, -jnp.inf)
    g_top, g_idx = first_argmax(group_logits)
    p_group = 1.0 / jnp.sum(jnp.exp(group_logits - g_top), axis=1, keepdims=True)
    first_lane = N_EXPERT_GROUPS + g_idx * EXPERTS_PER_GROUP
    in_group = (lane >= first_lane) & (lane < first_lane + EXPERTS_PER_GROUP)
    expert_logits = jnp.where(in_group, logits, -jnp.inf)
    v1, i1 = first_argmax(expert_logits)
    v2, i2 = first_argmax(jnp.where(lane == i1, -jnp.inf, expert_logits))
    e2 = jnp.exp(v2 - v1)
    w1 = p_group / (1.0 + e2)
    w2 = p_group * e2 / (1.0 + e2)
    route = jnp.where(lane == 0, i1 - N_EXPERT_GROUPS,
                      jnp.where(lane == 1, i2 - N_EXPERT_GROUPS,
                                jnp.where(lane == 2, w1, jnp.where(lane == 3, w2, 0.0))))
    route_ref[...] = route


def _mix(attn_inputs, y, gates, x2d, wa, wglu, wout, gffn, wr, br):
    n = x2d.shape[0]
    tm = PROJ_TILE
    row = lambda i: (i, 0)
    const = lambda i: (0, 0)
    merged_attn = len(attn_inputs) == 1
    in_specs = [pl.BlockSpec((tm * GROUP_WIDTH // a.shape[1], a.shape[1]), row) for a in attn_inputs]
    scratch = [] if merged_attn else [pltpu.VMEM((GROUP_WIDTH // LANES, tm, LANES), F32) for _ in attn_inputs]
    in_specs += [
        pl.BlockSpec((tm, SSM_WIDTH), row),
        pl.BlockSpec((tm, 2 * D_MODEL), row),
        pl.BlockSpec((tm, D_MODEL), row),
        pl.BlockSpec(wa.shape, const),
        pl.BlockSpec(wglu.shape, const),
        pl.BlockSpec(wout.shape, const),
        pl.BlockSpec((1, D_MODEL), const),
        pl.BlockSpec(wr.shape, const),
        pl.BlockSpec((1, LANES), const),
    ]
    return pl.pallas_call(
        functools.partial(_mix_kernel, merged_attn=merged_attn),
        grid=(n // tm,),
        in_specs=in_specs,
        out_specs=(pl.BlockSpec((tm, D_MODEL), row), pl.BlockSpec((tm, D_MODEL), row), pl.BlockSpec((tm, LANES), row)),
        out_shape=(jax.ShapeDtypeStruct((n, D_MODEL), F32), jax.ShapeDtypeStruct((n, D_MODEL), F32),
                   jax.ShapeDtypeStruct((n, LANES), F32)),
        scratch_shapes=scratch,
        compiler_params=_params("parallel"),
        name="branch_mix",
    )(*attn_inputs, y, gates, x2d, wa, wglu, wout, gffn, wr, br)


def _row_copy(src, src_row, dst, dst_row, sem):
    return pltpu.make_async_copy(src.at[pl.ds(src_row, 1)], dst.at[pl.ds(dst_row, 1)], sem)


def _dispatch_kernel(slot_ref, hn_ref, xs_in_hbm, xs_hbm, sem):
    del xs_in_hbm

    def issue(j, carry):
        _row_copy(hn_ref, j, xs_hbm, slot_ref[2 * j], sem).start()
        _row_copy(hn_ref, j, xs_hbm, slot_ref[2 * j + 1], sem).start()
        return carry

    def drain(j, carry):
        _row_copy(hn_ref, 0, xs_hbm, 0, sem).wait()
        _row_copy(hn_ref, 0, xs_hbm, 0, sem).wait()
        return carry

    lax.fori_loop(0, ROW_TILE, issue, 0, unroll=DMA_UNROLL)
    lax.fori_loop(0, ROW_TILE, drain, 0, unroll=DMA_UNROLL)


def _dispatch(slots, hn, n_slots):
    n = hn.shape[0]
    xs0 = jnp.zeros((n_slots, D_MODEL), F32)
    return pl.pallas_call(
        _dispatch_kernel,
        grid=(n // ROW_TILE,),
        in_specs=[
            pl.BlockSpec((2 * ROW_TILE,), lambda i: (i,), memory_space=pltpu.SMEM),
            pl.BlockSpec((ROW_TILE, D_MODEL), lambda i: (i, 0)),
            pl.BlockSpec(memory_space=pl.ANY),
        ],
        out_specs=pl.BlockSpec(memory_space=pl.ANY),
        out_shape=jax.ShapeDtypeStruct((n_slots, D_MODEL), F32),
        scratch_shapes=[pltpu.SemaphoreType.DMA(())],
        input_output_aliases={2: 0},
        compiler_params=_params("arbitrary"),
        name="moe_dispatch",
    )(slots, hn, xs0)


def _expert_kernel(block_e_ref, n_used_ref, xs_ref, wg_ref, wu_ref, wd_ref, yb_ref):
    del block_e_ref
    i = pl.program_id(0)

    @pl.when(i < n_used_ref[0])
    def _():
        xb = xs_ref[...].astype(BF16)
        gate = jnp.dot(xb, wg_ref[...], preferred_element_type=F32)
        up = jnp.dot(xb, wu_ref[...], preferred_element_type=F32)
        hmid = (jax.nn.silu(gate) * up).astype(BF16)
        yb_ref[...] = jnp.dot(hmid, wd_ref[...], preferred_element_type=F32)

    @pl.when(i >= n_used_ref[0])
    def _():
        yb_ref[...] = jnp.zeros_like(yb_ref)


def _experts(block_e, n_used, xs, wg, wu, wd):
    n_blocks = xs.shape[0] // MOE_BLOCK
    grid_spec = pltpu.PrefetchScalarGridSpec(
        num_scalar_prefetch=2,
        grid=(n_blocks,),
        in_specs=[
            pl.BlockSpec((MOE_BLOCK, D_MODEL), lambda i, be, nu: (i, 0)),
            pl.BlockSpec((None, D_MODEL, D_EXPERT), lambda i, be, nu: (be[i], 0, 0)),
            pl.BlockSpec((None, D_MODEL, D_EXPERT), lambda i, be, nu: (be[i], 0, 0)),
            pl.BlockSpec((None, D_EXPERT, D_MODEL), lambda i, be, nu: (be[i], 0, 0)),
        ],
        out_specs=pl.BlockSpec((MOE_BLOCK, D_MODEL), lambda i, be, nu: (i, 0)),
    )
    return pl.pallas_call(
        _expert_kernel,
        grid_spec=grid_spec,
        out_shape=jax.ShapeDtypeStruct(xs.shape, F32),
        compiler_params=_params("arbitrary"),
        name="moe_experts",
    )(block_e, n_used, xs, wg, wu, wd)


def _combine_kernel(slot_ref, h_ref, route_ref, g_ref, yb_hbm, out_ref, buf_a, buf_b, sem):
    def issue(j, carry):
        _row_copy(yb_hbm, slot_ref[2 * j], buf_a, j, sem).start()
        _row_copy(yb_hbm, slot_ref[2 * j + 1], buf_b, j, sem).start()
        return carry

    def drain(j, carry):
        _row_copy(yb_hbm, 0, buf_a, 0, sem).wait()
        _row_copy(yb_hbm, 0, buf_b, 0, sem).wait()
        return carry

    lax.fori_loop(0, ROW_TILE, issue, 0, unroll=DMA_UNROLL)
    lax.fori_loop(0, ROW_TILE, drain, 0, unroll=DMA_UNROLL)
    route = route_ref[...]
    h = h_ref[...] + (route[:, 2:3] * buf_a[...] + route[:, 3:4] * buf_b[...])
    var = jnp.mean(h * h, axis=-1, keepdims=True)
    out_ref[...] = h * lax.rsqrt(var + NORM_EPS) * g_ref[...]


def _combine(slots, h, route, g_final, yb):
    n = h.shape[0]
    row = lambda i: (i, 0)
    return pl.pallas_call(
        _combine_kernel,
        grid=(n // ROW_TILE,),
        in_specs=[
            pl.BlockSpec((2 * ROW_TILE,), lambda i: (i,), memory_space=pltpu.SMEM),
            pl.BlockSpec((ROW_TILE, D_MODEL), row),
            pl.BlockSpec((ROW_TILE, LANES), row),
            pl.BlockSpec((1, D_MODEL), lambda i: (0, 0)),
            pl.BlockSpec(memory_space=pl.ANY),
        ],
        out_specs=pl.BlockSpec((ROW_TILE, D_MODEL), row),
        out_shape=jax.ShapeDtypeStruct((n, D_MODEL), F32),
        scratch_shapes=[
            pltpu.VMEM((ROW_TILE, D_MODEL), F32),
            pltpu.VMEM((ROW_TILE, D_MODEL), F32),
            pltpu.SemaphoreType.DMA(()),
        ],
        compiler_params=_params("arbitrary"),
        name="moe_combine",
    )(slots, h, route, g_final.reshape(1, D_MODEL), yb)


def _slot_assignment(route, n_blocks):
    flat_e = route[:, 0:2].astype(jnp.int32).reshape(-1)
    onehot = (flat_e[:, None] == jnp.arange(N_EXPERTS, dtype=jnp.int32)[None, :]).astype(jnp.int32)
    running = jnp.cumsum(onehot, axis=0)
    rank = jnp.sum(onehot * running, axis=1) - 1
    counts = running[-1]
    padded = ((counts + MOE_BLOCK - 1) // MOE_BLOCK) * MOE_BLOCK
    pad_end = jnp.cumsum(padded)
    pad_start = pad_end - padded
    slots = jnp.sum(onehot * pad_start[None, :], axis=1) + rank
    block_start = jnp.arange(n_blocks, dtype=jnp.int32) * MOE_BLOCK
    block_e = jnp.minimum(jnp.sum((pad_end[None, :] <= block_start[:, None]).astype(jnp.int32), axis=1), N_EXPERTS - 1)
    n_used = (pad_end[-1:] // MOE_BLOCK).astype(jnp.int32)
    return slots.astype(jnp.int32), block_e, n_used


def _moe_and_final_norm(h, hn, route, wg, wu, wd, g_final):
    n = h.shape[0]
    n_blocks = (2 * n) // MOE_BLOCK + N_EXPERTS
    slots, block_e, n_used = _slot_assignment(route, n_blocks)
    xs = _dispatch(slots, hn, n_blocks * MOE_BLOCK)
    yb = _experts(block_e, n_used, xs, wg, wu, wd)
    return _combine(slots, h, route, g_final, yb)


def kernel(x_prompt, x_sample, cache_kv_w128, cache_kv_w512, cache_kv_w2048, state_ssm, g_attn_norm, w_in, ssm_log_dt, ssm_a_re, ssm_a_im, ssm_b_re, ssm_b_im, ssm_c_re, ssm_c_im, ssm_d, w_glu, w_attn_branch, w_out, g_ffn_norm, w_router_group, b_router_group, w_router_expert, b_router_expert, w_exp_gate, w_exp_up, w_exp_down, g_final):
    batch, seq, _ = x_prompt.shape
    dec_batch, dec_seq, _ = x_sample.shape
    past_len = cache_kv_w2048.shape[2]
    layer = 0

    w_in_b = w_in[layer].astype(BF16)
    wa = w_attn_branch[layer].astype(BF16)
    wglu = w_glu[layer].astype(BF16)
    wout = w_out[layer].astype(BF16)
    wg = w_exp_gate[layer].astype(BF16)
    wu = w_exp_up[layer].astype(BF16)
    wd = w_exp_down[layer].astype(BF16)
    gffn = g_ffn_norm[layer].reshape(1, D_MODEL)
    pad = LANES - N_EXPERT_GROUPS - N_EXPERTS
    wr = jnp.concatenate([w_router_group[layer], w_router_expert[layer], jnp.zeros((D_MODEL, pad), F32)], axis=1)
    br = jnp.concatenate([b_router_group[layer], b_router_expert[layer], jnp.zeros((pad,), F32)]).reshape(1, LANES)
    ssm_w = _ssm_weights(ssm_log_dt[layer], ssm_a_re[layer], ssm_a_im[layer], ssm_b_re[layer], ssm_b_im[layer],
                         ssm_c_re[layer], ssm_c_im[layer])
    d_flat = ssm_d[layer].astype(F32).reshape(1, SSM_WIDTH)

    def time_major(a, nb, steps):
        return jnp.transpose(a.reshape(nb, steps, -1), (1, 0, 2)).reshape(nb * steps, -1)

    def batch_major(a, nb, steps):
        return jnp.transpose(a.reshape(steps, nb, -1), (1, 0, 2)).reshape(nb * steps, -1)

    xp = x_prompt.reshape(batch * seq, D_MODEL)
    q_p, kv_p, kv_t_p, u_p, gates_p = _in_projection_prompt(xp, g_attn_norm[layer], w_in_b, batch, seq)
    attn_in = []
    lse_in = []
    for g, (win, dil) in enumerate(ATTN_GROUPS):
        o, lse = _prompt_attention(q_p[g], kv_p[g], g, dil, batch, seq)
        attn_in.append(o)
        lse_in.append(lse)
    zeros_state = jnp.zeros((batch, SSM_COLS), F32)
    y_tm, hr_p, hi_p = _ssm(time_major(u_p, batch, seq), zeros_state, zeros_state, ssm_w, d_flat, batch)
    y_p = batch_major(y_tm, batch, seq)
    h_p, hn_p, route_p = _mix(attn_in + lse_in, y_p, gates_p, xp, wa, wglu, wout, gffn, wr, br)
    out_p = _moe_and_final_norm(h_p, hn_p, route_p, wg, wu, wd, g_final)

    xs = x_sample.reshape(dec_batch * dec_seq, D_MODEL)
    pos_s = past_len + (jnp.arange(dec_batch * dec_seq, dtype=jnp.int32) % dec_seq)
    q_s, kv0_s, kv1_s, kv2_s, u_s, gates_s = _in_projection_decode(xs, g_attn_norm[layer], w_in_b, pos_s)
    kv_s = (kv0_s, kv1_s, kv2_s)
    caches = (cache_kv_w128[layer], cache_kv_w512[layer], cache_kv_w2048[layer])
    attn_s = _decode_attention(q_s, caches, kv_s, dec_batch, dec_seq)
    st = state_ssm[layer].astype(F32).reshape(dec_batch, SSM_COLS, 2)
    ys_tm, hr_s, hi_s = _ssm(time_major(u_s, dec_batch, dec_seq), st[:, :, 0], st[:, :, 1], ssm_w, d_flat, dec_batch)
    y_s = batch_major(ys_tm, dec_batch, dec_seq)
    h_s, hn_s, route_s = _mix([attn_s], y_s, gates_s, xs, wa, wglu, wout, gffn, wr, br)
    out_s = _moe_and_final_norm(h_s, hn_s, route_s, wg, wu, wd, g_final)

    kv_tail = (2, HEADS_PER_GROUP, HEAD_DIM)
    outs = [out_p.reshape(batch, seq, D_MODEL), out_s.reshape(dec_batch, dec_seq, D_MODEL)]
    for g, (win, dil) in enumerate(ATTN_GROUPS):
        keep = min(win, seq)
        rows_p = jnp.transpose(kv_t_p[g].reshape(batch, 2, HEADS_PER_GROUP, HEAD_DIM, keep), (0, 4, 1, 2, 3))
        outs.append(rows_p[None])
        outs.append(kv_s[g].reshape((1, dec_batch, dec_seq) + kv_tail))
    outs.append(jnp.stack([hr_p, hi_p], axis=-1).reshape(1, batch, N_SSM_GROUPS, SSM_STATE, 2))
    outs.append(jnp.stack([hr_s, hi_s], axis=-1).reshape(1, dec_batch, N_SSM_GROUPS, SSM_STATE, 2))
    return tuple(outs)
```

```python
import functools
import math

import jax
import jax.numpy as jnp
from jax import lax
from jax.experimental import pallas as pl
from jax.experimental.pallas import tpu as pltpu

F32 = jnp.float32
BF16 = jnp.bfloat16

D_MODEL = 1024
HEAD_DIM = 64
HEADS_PER_GROUP = 8
GROUP_WIDTH = HEADS_PER_GROUP * HEAD_DIM
ATTN_GROUPS = ((128, 1), (512, 4), (2048, 16))
N_ATTN_GROUPS = len(ATTN_GROUPS)
ATTN_WIDTH = N_ATTN_GROUPS * GROUP_WIDTH
ROT_DIM = HEAD_DIM // 4
ROPE_THETA = 500000.0
WINDOW_KEYS = 128
SSM_GROUP_CH = 16
SSM_WIDTH = D_MODEL // 2
N_SSM_GROUPS = SSM_WIDTH // SSM_GROUP_CH
SSM_STATE = 64
SSM_COLS = N_SSM_GROUPS * SSM_STATE
IN_WIDTH = 3 * ATTN_WIDTH + SSM_WIDTH + 2 * D_MODEL
N_EXPERT_GROUPS = 4
EXPERTS_PER_GROUP = 8
N_EXPERTS = N_EXPERT_GROUPS * EXPERTS_PER_GROUP
D_EXPERT = D_MODEL // 4
NORM_EPS = 1e-6

LANES = 128
SUBLANES = 8
VMEM_LIMIT = 56 * 1024 * 1024

PROJ_TILE = 256
MATMUL_LOOKAHEAD = 1
ATTN_TILE = 128
ATTN_RESIDUES_PER_STEP = 4
ATTN_TILES_PER_STEP = 8
SSM_ROWS = 512
MOE_BLOCK = 256
ROW_TILE = 512
DMA_UNROLL = 8


def _params(*sem):
    return pltpu.CompilerParams(dimension_semantics=sem, vmem_limit_bytes=VMEM_LIMIT)


def _normed_input(x_ref, g_ref):
    x = x_ref[...]
    var = jnp.mean(x * x, axis=-1, keepdims=True)
    return (x * lax.rsqrt(var + NORM_EPS) * g_ref[...]).astype(BF16)


def _rope_fn(cos_ref, sin_ref):
    cos = cos_ref[...]
    sin = sin_ref[...]
    lane = lax.broadcasted_iota(jnp.int32, cos.shape, 1) % HEAD_DIM
    first_half = lane < ROT_DIM // 2
    rotated = lane < ROT_DIM

    def rope_chunk(c):
        partner = jnp.where(first_half, pltpu.roll(c, LANES - ROT_DIM // 2, 1), pltpu.roll(c, ROT_DIM // 2, 1))
        return jnp.where(rotated, c * cos + partner * sin, c)

    def rope(t):
        return jnp.concatenate([rope_chunk(t[:, j * LANES:(j + 1) * LANES]) for j in range(GROUP_WIDTH // LANES)], axis=1)

    return rope


def _projection_tiles(xn, w_ref):
    for c in range(IN_WIDTH // GROUP_WIDTH):
        yield c, jnp.dot(xn, w_ref[:, c * GROUP_WIDTH:(c + 1) * GROUP_WIDTH], preferred_element_type=F32)


def _inproj_decode_kernel(x_ref, g_ref, w_ref, cos_ref, sin_ref, q_ref, kv0_ref, kv1_ref, kv2_ref, u_ref, gate_ref):
    rope = _rope_fn(cos_ref, sin_ref)
    kv_refs = (kv0_ref, kv1_ref, kv2_ref)
    for c, acc in _projection_tiles(_normed_input(x_ref, g_ref), w_ref):
        if c < 3:
            q_ref[:, c * GROUP_WIDTH:(c + 1) * GROUP_WIDTH] = (rope(acc) * (HEAD_DIM ** -0.5)).astype(BF16)
        elif c < 6:
            kv_refs[c - 3][:, :GROUP_WIDTH] = rope(acc)
        elif c < 9:
            kv_refs[c - 6][:, GROUP_WIDTH:] = acc
        elif c == 9:
            u_ref[...] = acc
        else:
            gate_ref[:, (c - 10) * GROUP_WIDTH:(c - 9) * GROUP_WIDTH] = acc


def _inproj_prompt_kernel(x_ref, g_ref, w_ref, cos_ref, sin_ref,
                          q0_ref, q1_ref, q2_ref, kv0_ref, kv1_ref, kv2_ref, kt0_ref, kt1_ref, kt2_ref,
                          u_ref, gate_ref, *scratch, tiles_per_seq, keeps):
    rope = _rope_fn(cos_ref, sin_ref)
    tm = x_ref.shape[0]
    tile_in_seq = pl.program_id(0) % tiles_per_seq
    q_refs = (q0_ref, q1_ref, q2_ref)
    kv_refs = (kv0_ref, kv1_ref, kv2_ref)
    kt_refs = (kt0_ref, kt1_ref, kt2_ref)

    dilated = [g for g, (_, d) in enumerate(ATTN_GROUPS) if d > 1]

    def store_rows(dst_ref, val, kind, g, col0, col_stride):
        dil = ATTN_GROUPS[g][1]
        if dil == 1:
            dst_ref[:, col0:col0 + GROUP_WIDTH] = val.astype(BF16)
            return
        scr = scratch[kind * len(dilated) + dilated.index(g)]
        n_chunks = GROUP_WIDTH // LANES
        for j in range(n_chunks):
            scr[j] = val[:, j * LANES:(j + 1) * LANES]
        rows = tm // dil
        for r in range(dil):
            piece = jnp.concatenate([scr[j, pl.ds(r, rows, stride=dil), :] for j in range(n_chunks)], axis=1)
            dst_ref[:, col0 + r * col_stride:col0 + r * col_stride + GROUP_WIDTH] = piece.astype(BF16)

    def store_transposed(dst_ref, half, val, keep):
        width = min(keep, tm)
        first_tile = tiles_per_seq - max(keep // tm, 1)

        @pl.when(tile_in_seq >= first_tile)
        def _():
            dst_ref[half] = val[tm - width:, :].T

    def emit(c, acc):
        g = c % N_ATTN_GROUPS
        if c < 3:
            store_rows(q_refs[g], rope(acc) * (HEAD_DIM ** -0.5), 0, g, 0, GROUP_WIDTH)
        elif c < 6:
            k = rope(acc)
            store_rows(kv_refs[g], k, 1, g, 0, 2 * GROUP_WIDTH)
            store_transposed(kt_refs[g], 0, k, keeps[g])
        elif c < 9:
            store_rows(kv_refs[g], acc, 2, g, GROUP_WIDTH, 2 * GROUP_WIDTH)
            store_transposed(kt_refs[g], 1, acc, keeps[g])
        elif c == 9:
            u_ref[...] = acc
        else:
            gate_ref[:, (c - 10) * GROUP_WIDTH:(c - 9) * GROUP_WIDTH] = acc

    pending = []
    for item in _projection_tiles(_normed_input(x_ref, g_ref), w_ref):
        pending.append(item)
        if len(pending) > MATMUL_LOOKAHEAD:
            emit(*pending.pop(0))
    for item in pending:
        emit(*item)


def _rope_tables(pos):
    half = ROT_DIM // 2
    inv_freq = ROPE_THETA ** (-(jnp.arange(half, dtype=F32) / half))
    ang = pos.astype(F32)[:, None] * inv_freq[None, :]
    cos, sin = jnp.cos(ang), jnp.sin(ang)
    n = pos.shape[0]
    rest = HEAD_DIM - ROT_DIM
    cos_h = jnp.concatenate([cos, cos, jnp.ones((n, rest), F32)], axis=1)
    sin_h = jnp.concatenate([-sin, sin, jnp.zeros((n, rest), F32)], axis=1)
    return jnp.tile(cos_h, (1, LANES // HEAD_DIM)), jnp.tile(sin_h, (1, LANES // HEAD_DIM))


def _inproj_in_specs(tm, n_pos_tiles):
    const = lambda i: (0, 0)
    tab = lambda i: (i % n_pos_tiles, 0)
    return [
        pl.BlockSpec((tm, D_MODEL), lambda i: (i, 0)),
        pl.BlockSpec((1, D_MODEL), const),
        pl.BlockSpec((D_MODEL, IN_WIDTH), const, pipeline_mode=pl.Buffered(1)),
        pl.BlockSpec((tm, LANES), tab),
        pl.BlockSpec((tm, LANES), tab),
    ]


def _in_projection_decode(x2d, g, w_bf16, pos):
    n = x2d.shape[0]
    tm = PROJ_TILE
    cos_t, sin_t = _rope_tables(pos)
    row = lambda i: (i, 0)
    widths = (ATTN_WIDTH, 2 * GROUP_WIDTH, 2 * GROUP_WIDTH, 2 * GROUP_WIDTH, SSM_WIDTH, 2 * D_MODEL)
    dtypes = (BF16, F32, F32, F32, F32, F32)
    return pl.pallas_call(
        _inproj_decode_kernel,
        grid=(n // tm,),
        in_specs=_inproj_in_specs(tm, pos.shape[0] // tm),
        out_specs=tuple(pl.BlockSpec((tm, w), row) for w in widths),
        out_shape=tuple(jax.ShapeDtypeStruct((n, w), d) for w, d in zip(widths, dtypes)),
        compiler_params=_params("parallel"),
        name="in_projection_decode",
    )(x2d, g.reshape(1, D_MODEL), w_bf16, cos_t, sin_t)


def _in_projection_prompt(x2d, g, w_bf16, batch, seq):
    n = batch * seq
    tm = PROJ_TILE
    tiles_per_seq = seq // tm
    cos_t, sin_t = _rope_tables(jnp.arange(seq, dtype=jnp.int32))
    keeps = tuple(min(win, seq) for win, _ in ATTN_GROUPS)
    row = lambda i: (i, 0)
    out_specs, out_shape = [], []
    for width in (GROUP_WIDTH, 2 * GROUP_WIDTH):
        for _, dil in ATTN_GROUPS:
            out_specs.append(pl.BlockSpec((tm // dil, dil * width), row))
            out_shape.append(jax.ShapeDtypeStruct((n // dil, dil * width), BF16))
    for keep in keeps:
        blk = min(keep, tm)
        first_tile = tiles_per_seq - max(keep // tm, 1)
        out_specs.append(pl.BlockSpec(
            (None, 2, GROUP_WIDTH, blk),
            lambda i, first_tile=first_tile: (i // tiles_per_seq, 0, 0, jnp.maximum(i % tiles_per_seq - first_tile, 0))))
        out_shape.append(jax.ShapeDtypeStruct((batch, 2, GROUP_WIDTH, keep), F32))
    out_specs += [pl.BlockSpec((tm, SSM_WIDTH), row), pl.BlockSpec((tm, 2 * D_MODEL), row)]
    out_shape += [jax.ShapeDtypeStruct((n, SSM_WIDTH), F32), jax.ShapeDtypeStruct((n, 2 * D_MODEL), F32)]
    outs = pl.pallas_call(
        functools.partial(_inproj_prompt_kernel, tiles_per_seq=tiles_per_seq, keeps=keeps),
        grid=(n // tm,),
        in_specs=_inproj_in_specs(tm, tiles_per_seq),
        out_specs=tuple(out_specs),
        out_shape=tuple(out_shape),
        scratch_shapes=[pltpu.VMEM((GROUP_WIDTH // LANES, tm, LANES), F32)
                        for _ in range(3 * sum(1 for _, d in ATTN_GROUPS if d > 1))],
        compiler_params=_params("arbitrary"),
        name="in_projection_prompt",
    )(x2d, g.reshape(1, D_MODEL), w_bf16, cos_t, sin_t)
    return outs[0:3], outs[3:6], outs[6:9], outs[9], outs[10]


def _window_attn_kernel(q_ref, kvc_ref, *rest, windowed, residues, sub_tiles):
    if windowed:
        kvp_ref, o_ref, lse_ref = rest
    else:
        o_ref, lse_ref = rest
    n_keys = 2 * ATTN_TILE if windowed else ATTN_TILE
    rows = lax.broadcasted_iota(jnp.int32, (ATTN_TILE, n_keys), 0)
    cols = lax.broadcasted_iota(jnp.int32, (ATTN_TILE, n_keys), 1)
    if windowed:
        band = (cols >= rows) & (cols <= rows + WINDOW_KEYS)
        first_tile_band = band & ((cols >= ATTN_TILE) | (pl.program_id(2) > 0))
    else:
        band = cols <= rows
    low_head = lax.broadcasted_iota(jnp.int32, (ATTN_TILE, LANES), 1) < HEAD_DIM
    for r in range(residues):
        for s in range(sub_tiles):
            rs = slice(s * ATTN_TILE, (s + 1) * ATTN_TILE)
            k0 = r * 2 * GROUP_WIDTH
            q = q_ref[rs, r * GROUP_WIDTH:(r + 1) * GROUP_WIDTH]
            k = kvc_ref[rs, k0:k0 + GROUP_WIDTH]
            v = kvc_ref[rs, k0 + GROUP_WIDTH:k0 + 2 * GROUP_WIDTH]
            valid = band
            if windowed:
                if s == 0:
                    prev_ref, ps, valid = kvp_ref, slice(0, ATTN_TILE), first_tile_band
                else:
                    prev_ref, ps = kvc_ref, slice((s - 1) * ATTN_TILE, s * ATTN_TILE)
                k = jnp.concatenate([prev_ref[ps, k0:k0 + GROUP_WIDTH], k], axis=0)
                v = jnp.concatenate([prev_ref[ps, k0 + GROUP_WIDTH:k0 + 2 * GROUP_WIDTH], v], axis=0)
            valid2 = jnp.concatenate([valid, valid], axis=0)
            for j in range(GROUP_WIDTH // LANES):
                sl = slice(j * LANES, (j + 1) * LANES)
                osl = slice(r * GROUP_WIDTH + j * LANES, r * GROUP_WIDTH + (j + 1) * LANES)
                qj, kj, vj = q[:, sl], k[:, sl], v[:, sl]
                zero = jnp.zeros_like(qj)
                qm = jnp.concatenate([jnp.where(low_head, qj, zero), jnp.where(low_head, zero, qj)], axis=0)
                sc = lax.dot_general(qm, kj, (((1,), (1,)), ((), ())), preferred_element_type=F32)
                sc = jnp.where(valid2, sc, -jnp.inf)
                m = jnp.max(sc, axis=1, keepdims=True)
                p = jnp.exp(sc - m)
                l = jnp.sum(p, axis=1, keepdims=True)
                o2 = jnp.dot(p.astype(BF16), vj, preferred_element_type=F32) / l
                lse2 = jnp.broadcast_to(m + jnp.log(l), o2.shape)
                o_ref[rs, osl] = jnp.where(low_head, o2[:ATTN_TILE], o2[ATTN_TILE:])
                lse_ref[rs, osl] = jnp.where(low_head, lse2[:ATTN_TILE], lse2[ATTN_TILE:])


def _prompt_attention(q, kv, group, dilation, batch, seq):
    tg = seq // dilation
    n_tiles = tg // ATTN_TILE
    windowed = n_tiles > 1
    residues = min(dilation, ATTN_RESIDUES_PER_STEP)
    sub_tiles = min(n_tiles, ATTN_TILES_PER_STEP // residues)
    rows = sub_tiles * ATTN_TILE
    q3 = q.reshape(batch, tg, dilation * GROUP_WIDTH)
    kv3 = kv.reshape(batch, tg, dilation * 2 * GROUP_WIDTH)
    in_specs = [
        pl.BlockSpec((None, rows, residues * GROUP_WIDTH), lambda b, r, t: (b, t, r)),
        pl.BlockSpec((None, rows, residues * 2 * GROUP_WIDTH), lambda b, r, t: (b, t, r)),
    ]
    args = [q3, kv3]
    if windowed:
        in_specs.append(pl.BlockSpec((None, ATTN_TILE, residues * 2 * GROUP_WIDTH),
                                     lambda b, r, t: (b, jnp.maximum(t * sub_tiles - 1, 0), r)))
        args.append(kv3)
    out_spec = pl.BlockSpec((None, rows, residues * GROUP_WIDTH), lambda b, r, t: (b, t, r))
    out_sds = jax.ShapeDtypeStruct((batch, tg, dilation * GROUP_WIDTH), F32)
    o, lse = pl.pallas_call(
        functools.partial(_window_attn_kernel, windowed=windowed, residues=residues, sub_tiles=sub_tiles),
        grid=(batch, dilation // residues, n_tiles // sub_tiles),
        in_specs=in_specs,
        out_specs=(out_spec, out_spec),
        out_shape=(out_sds, out_sds),
        compiler_params=_params("parallel", "parallel", "arbitrary"),
        name=f"prompt_attention_g{group}",
    )(*args)
    flat = (batch * tg, dilation * GROUP_WIDTH)
    return o.reshape(flat), lse.reshape(flat)


def _sublane_total(x):
    x = x + pltpu.roll(x, 4, 0)
    x = x + pltpu.roll(x, 2, 0)
    return x + pltpu.roll(x, 1, 0)


def _head_sum(prod):
    width = prod.shape[1]
    row = lax.broadcasted_iota(jnp.int32, (HEADS_PER_GROUP, width), 0)
    out = jnp.zeros((HEADS_PER_GROUP, width), F32)
    for h in range(HEADS_PER_GROUP):
        part = prod[h * HEAD_DIM:h * HEAD_DIM + SUBLANES]
        for j in range(1, HEAD_DIM // SUBLANES):
            part = part + prod[h * HEAD_DIM + j * SUBLANES:h * HEAD_DIM + (j + 1) * SUBLANES]
        out = jnp.where(row == h, _sublane_total(part), out)
    return out


def _head_expand(x):
    width = x.shape[1]
    pieces = []
    for h in range(HEADS_PER_GROUP):
        pieces.extend([jnp.broadcast_to(x[h:h + 1, :], (SUBLANES, width))] * (HEAD_DIM // SUBLANES))
    return jnp.concatenate(pieces, axis=0)


def _split_dot(acc, sel, terms=2):
    out, rem = None, acc
    for _ in range(terms):
        hi = rem.astype(BF16)
        part = jnp.dot(hi, sel, preferred_element_type=F32)
        out = part if out is None else out + part
        rem = rem - hi.astype(F32)
    return out


def _decode_attn_kernel(q_ref, n0_ref, n1_ref, n2_ref, c0_ref, c1_ref, c2_ref, o_ref, *, dec_seq):
    nq = SUBLANES
    n_sub = HEAD_DIM // SUBLANES
    neg = -jnp.inf
    q_t = q_ref[...].T
    step = lax.broadcasted_iota(jnp.int32, (HEADS_PER_GROUP, nq), 1)
    step_wide = lax.broadcasted_iota(jnp.int32, (GROUP_WIDTH, nq), 1)
    real_step = step < dec_seq
    new_refs = (n0_ref, n1_ref, n2_ref)
    cache_refs = (c0_ref, c1_ref, c2_ref)

    def column(x, t, width):
        return jnp.broadcast_to(x[:, t:t + 1], (x.shape[0], width))

    def place(cols):
        rows = cols[0].shape[0]
        lane = lax.broadcasted_iota(jnp.int32, (rows, nq), 1)
        out = jnp.zeros((rows, nq), F32)
        for t, c in enumerate(cols):
            out = jnp.where(lane == t, jnp.broadcast_to(c, (rows, nq)), out)
        return out

    outs, lses = [], []
    for g, (win, dil) in enumerate(ATTN_GROUPS):
        c_ref = cache_refs[g]
        kv_new = new_refs[g][...]
        kn_t = kv_new[:, :GROUP_WIDTH].T
        vn_t = kv_new[:, GROUP_WIDTH:].T
        q_g = q_t[g * GROUP_WIDTH:(g + 1) * GROUP_WIDTH, :]
        pos = lax.broadcasted_iota(jnp.int32, (HEADS_PER_GROUP, win), 1)
        row8 = lax.broadcasted_iota(jnp.int32, (HEADS_PER_GROUP, win), 0)
        n_tiles = win // LANES

        def cache_scores(q_pat):
            s = jnp.zeros((HEADS_PER_GROUP, win), F32)
            for h in range(HEADS_PER_GROUP):
                part = None
                for j in range(n_sub):
                    r0 = h * HEAD_DIM + j * SUBLANES
                    qp = q_pat[r0:r0 + SUBLANES, :]
                    term = c_ref[0, r0:r0 + SUBLANES, :] * jnp.concatenate([qp] * n_tiles, axis=1)
                    part = term if part is None else part + term
                s = jnp.where(row8 == h, _sublane_total(part), s)
            return s

        def weighted_values(p):
            rows = []
            for h in range(HEADS_PER_GROUP):
                ph = jnp.broadcast_to(p[h:h + 1, :], (SUBLANES, win))
                for j in range(n_sub):
                    r0 = h * HEAD_DIM + j * SUBLANES
                    prod = ph * c_ref[1, r0:r0 + SUBLANES, :]
                    a = prod[:, :LANES]
                    for tile in range(1, n_tiles):
                        a = a + prod[:, tile * LANES:(tile + 1) * LANES]
                    rows.append(a)
            return jnp.concatenate(rows, axis=0)

        if dil == 1:
            spread = (lax.broadcasted_iota(jnp.int32, (nq, dec_seq * LANES), 0)
                      == lax.broadcasted_iota(jnp.int32, (nq, dec_seq * LANES), 1) // LANES).astype(BF16)
            q_cols = _split_dot(q_g, spread, terms=1)
            kn_cols = _split_dot(kn_t, spread, terms=3)
            vn_cols = _split_dot(vn_t, spread, terms=3)
            s_new = [_head_sum(q_g * kn_cols[:, u * LANES:u * LANES + nq]) for u in range(dec_seq)]
            ok_new = [(step >= u) & real_step for u in range(dec_seq)]
            s_cache = [cache_scores(q_cols[:, t * LANES:(t + 1) * LANES]) for t in range(dec_seq)]
            m = place([jnp.max(jnp.where(pos >= t, s_cache[t], neg), axis=1, keepdims=True) for t in range(dec_seq)])
            for u in range(dec_seq):
                m = jnp.maximum(m, jnp.where(ok_new[u], s_new[u], neg))
            e_cache = [jnp.where(pos >= t, jnp.exp(s_cache[t] - m[:, t:t + 1]), 0.0) for t in range(dec_seq)]
            e_new = [jnp.where(ok_new[u], jnp.exp(s_new[u] - m), 0.0) for u in range(dec_seq)]
            denom = place([jnp.sum(e, axis=1, keepdims=True) for e in e_cache])
            for u in range(dec_seq):
                denom = denom + e_new[u]
            denom = jnp.where(real_step, denom, 1.0)
            inv = 1.0 / denom
            cols = []
            for t in range(dec_seq):
                acc = weighted_values(e_cache[t] * inv[:, t:t + 1])
                cols.append(jnp.sum(acc, axis=1, keepdims=True))
            o_g = place(cols)
            for u in range(dec_seq):
                o_g = o_g + _head_expand(e_new[u] * inv) * vn_cols[:, u * LANES:u * LANES + nq]
        else:
            res = pos % dil
            t_i = lax.broadcasted_iota(jnp.int32, (nq, LANES), 0)
            l_i = lax.broadcasted_iota(jnp.int32, (nq, LANES), 1)
            residue_pat = ((l_i % dil == t_i) & (t_i < dec_seq)).astype(BF16)
            s_cache = cache_scores(_split_dot(q_g, residue_pat, terms=1))
            s_new = jnp.where(real_step, _head_sum(q_g * kn_t), 0.0)
            m_cache = place([jnp.max(jnp.where(res == t, s_cache, neg), axis=1, keepdims=True) for t in range(dec_seq)])
            m = jnp.maximum(m_cache, s_new)

            def by_position(stat):
                out = jnp.zeros((HEADS_PER_GROUP, win), F32)
                for t in range(dec_seq):
                    out = jnp.where(res == t, column(stat, t, win), out)
                return out

            e_cache = jnp.where(res < dec_seq, jnp.exp(s_cache - by_position(m)), 0.0)
            e_new = jnp.exp(s_new - m)
            denom = place([jnp.sum(jnp.where(res == t, e_cache, 0.0), axis=1, keepdims=True)
                           for t in range(dec_seq)]) + e_new
            inv = 1.0 / denom
            acc = weighted_values(e_cache * by_position(inv))
            l_i = lax.broadcasted_iota(jnp.int32, (LANES, LANES), 0)
            t_i = lax.broadcasted_iota(jnp.int32, (LANES, LANES), 1)
            sel = ((l_i % dil == t_i) & (t_i < dec_seq)).astype(BF16)
            o_g = _split_dot(acc, sel)[:, :nq] + _head_expand(e_new * inv) * vn_t
        outs.append(o_g)
        lses.append(m + jnp.log(denom))

    top = jnp.maximum(jnp.maximum(lses[0], lses[1]), lses[2])
    ws = [jnp.exp(l - top) for l in lses]
    total = ws[0] + ws[1] + ws[2]
    merged = jnp.zeros((GROUP_WIDTH, nq), F32)
    for g in range(N_ATTN_GROUPS):
        merged = merged + _head_expand(ws[g] / total) * outs[g]
    o_ref[...] = jnp.where(step_wide < dec_seq, merged, 0.0)


def _decode_attention(q, caches, new_kv, dec_batch, dec_seq):
    nq = SUBLANES

    def pad_steps(a):
        a = a.astype(F32).reshape(dec_batch, dec_seq, a.shape[-1])
        return jnp.pad(a, ((0, 0), (0, nq - dec_seq), (0, 0)))

    args = [pad_steps(q)] + [pad_steps(n) for n in new_kv]
    in_specs = [pl.BlockSpec((None, nq, ATTN_WIDTH), lambda b: (b, 0, 0))]
    in_specs += [pl.BlockSpec((None, nq, 2 * GROUP_WIDTH), lambda b: (b, 0, 0)) for _ in new_kv]
    for (win, dil), cache in zip(ATTN_GROUPS, caches):
        args.append(jnp.transpose(cache, (0, 2, 3, 4, 1)).reshape(dec_batch, 2, GROUP_WIDTH, win))
        in_specs.append(pl.BlockSpec((None, 2, GROUP_WIDTH, win), lambda b: (b, 0, 0, 0)))
    o_t = pl.pallas_call(
        functools.partial(_decode_attn_kernel, dec_seq=dec_seq),
        grid=(dec_batch,),
        in_specs=in_specs,
        out_specs=pl.BlockSpec((None, GROUP_WIDTH, nq), lambda b: (b, 0, 0)),
        out_shape=jax.ShapeDtypeStruct((dec_batch, GROUP_WIDTH, nq), F32),
        compiler_params=_params("parallel"),
        name="decode_attention",
    )(*args)
    return jnp.transpose(o_t[:, :, :dec_seq], (0, 2, 1)).reshape(dec_batch * dec_seq, GROUP_WIDTH)


def _ssm_kernel(u_ref, h0r_ref, h0i_ref, ar_ref, ai_ref, bw_ref, cr_ref, ci_ref, d_ref,
                y_ref, hr_out, hi_out, bur, bui, hr_s, hi_s, *, nb, steps):
    chunk = pl.program_id(0)
    half_in = SSM_WIDTH // 2
    half_st = SSM_COLS // 2

    @pl.when(chunk == 0)
    def _():
        hr_s[...] = h0r_ref[...]
        hi_s[...] = h0i_ref[...]

    u = u_ref[...]
    ub = u.astype(BF16)
    for hf in range(2):
        r = jnp.dot(ub[:, hf * half_in:(hf + 1) * half_in], bw_ref[hf], preferred_element_type=F32)
        bur[:, hf * half_st:(hf + 1) * half_st] = r[:, :half_st]
        bui[:, hf * half_st:(hf + 1) * half_st] = r[:, half_st:]

    lane_chunk = 4 * LANES

    def sub_batch(s, carry):
        for lc in range(SSM_COLS // lane_chunk):
            ls = slice(lc * lane_chunk, (lc + 1) * lane_chunk)
            ar = ar_ref[:, ls]
            ai = ai_ref[:, ls]
            row0 = pl.multiple_of(s * SUBLANES, SUBLANES)

            def step(t, h):
                hr, hi = h
                row = pl.multiple_of(t * nb + s * SUBLANES, SUBLANES)
                nhr = ar * hr - ai * hi + bur[pl.ds(row, SUBLANES), ls]
                nhi = ar * hi + ai * hr + bui[pl.ds(row, SUBLANES), ls]
                bur[pl.ds(row, SUBLANES), ls] = nhr
                bui[pl.ds(row, SUBLANES), ls] = nhi
                return nhr, nhi

            hr, hi = lax.fori_loop(0, steps, step, (hr_s[pl.ds(row0, SUBLANES), ls], hi_s[pl.ds(row0, SUBLANES), ls]))
            hr_s[pl.ds(row0, SUBLANES), ls] = hr
            hi_s[pl.ds(row0, SUBLANES), ls] = hi
        return carry

    lax.fori_loop(0, nb // SUBLANES, sub_batch, 0)

    for hf in range(2):
        ss = slice(hf * half_st, (hf + 1) * half_st)
        y = jnp.dot(bur[:, ss].astype(BF16), cr_ref[hf], preferred_element_type=F32)
        y = y + jnp.dot(bui[:, ss].astype(BF16), ci_ref[hf], preferred_element_type=F32)
        cs = slice(hf * half_in, (hf + 1) * half_in)
        y_ref[:, cs] = y + d_ref[:, cs] * u[:, cs]

    @pl.when(chunk == pl.num_programs(0) - 1)
    def _():
        hr_out[...] = hr_s[...]
        hi_out[...] = hi_s[...]


def _ssm_weights(ssm_log_dt, a_re, a_im, b_re, b_im, c_re, c_im):
    dt = jnp.exp(ssm_log_dt.astype(F32))[:, None]
    lam = lax.complex(a_re.astype(F32), a_im.astype(F32))
    lam_bar = jnp.exp(lam * dt)
    b_bar = ((lam_bar - 1.0) / lam)[:, :, None] * lax.complex(b_re.astype(F32), b_im.astype(F32))
    gh = N_SSM_GROUPS // 2
    eye = jnp.eye(gh, dtype=F32)

    def block_diag(m):
        return jnp.einsum('gab,gh->gahb', m, eye).reshape(gh * m.shape[1], gh * m.shape[2])

    bw, cr, ci = [], [], []
    for hf in range(2):
        gs = slice(hf * gh, (hf + 1) * gh)
        b_t = jnp.transpose(b_bar[gs], (0, 2, 1))
        bw.append(jnp.concatenate([block_diag(jnp.real(b_t)), block_diag(jnp.imag(b_t))], axis=1))
        cr.append(block_diag(jnp.transpose(c_re[gs].astype(F32), (0, 2, 1))))
        ci.append(block_diag(jnp.transpose(-c_im[gs].astype(F32), (0, 2, 1))))
    ar = jnp.broadcast_to(jnp.real(lam_bar).reshape(1, SSM_COLS), (SUBLANES, SSM_COLS))
    ai = jnp.broadcast_to(jnp.imag(lam_bar).reshape(1, SSM_COLS), (SUBLANES, SSM_COLS))
    return ar, ai, jnp.stack(bw).astype(BF16), jnp.stack(cr).astype(BF16), jnp.stack(ci).astype(BF16)


def _ssm(u_tm, h0r, h0i, weights, d_flat, nb):
    ar, ai, bw, cr, ci = weights
    rows = u_tm.shape[0]
    steps = SSM_ROWS // nb
    const2 = lambda c: (0, 0)
    const3 = lambda c: (0, 0, 0)
    state_sds = jax.ShapeDtypeStruct((nb, SSM_COLS), F32)
    return pl.pallas_call(
        functools.partial(_ssm_kernel, nb=nb, steps=steps),
        grid=(rows // SSM_ROWS,),
        in_specs=[
            pl.BlockSpec((SSM_ROWS, SSM_WIDTH), lambda c: (c, 0)),
            pl.BlockSpec((nb, SSM_COLS), const2),
            pl.BlockSpec((nb, SSM_COLS), const2),
            pl.BlockSpec((SUBLANES, SSM_COLS), const2),
            pl.BlockSpec((SUBLANES, SSM_COLS), const2),
            pl.BlockSpec(bw.shape, const3),
            pl.BlockSpec(cr.shape, const3),
            pl.BlockSpec(ci.shape, const3),
            pl.BlockSpec((1, SSM_WIDTH), const2),
        ],
        out_specs=(
            pl.BlockSpec((SSM_ROWS, SSM_WIDTH), lambda c: (c, 0)),
            pl.BlockSpec((nb, SSM_COLS), const2),
            pl.BlockSpec((nb, SSM_COLS), const2),
        ),
        out_shape=(jax.ShapeDtypeStruct((rows, SSM_WIDTH), F32), state_sds, state_sds),
        scratch_shapes=[
            pltpu.VMEM((SSM_ROWS, SSM_COLS), F32),
            pltpu.VMEM((SSM_ROWS, SSM_COLS), F32),
            pltpu.VMEM((nb, SSM_COLS), F32),
            pltpu.VMEM((nb, SSM_COLS), F32),
        ],
        compiler_params=_params("arbitrary"),
        name="s5_scan",
    )(u_tm, h0r, h0i, ar, ai, bw, cr, ci, d_flat)


def _mix_kernel(*refs, merged_attn):
    if merged_attn:
        attn_ref = refs[0]
        rest = refs[1:13]
    else:
        group_refs = refs[0:2 * N_ATTN_GROUPS]
        rest = refs[2 * N_ATTN_GROUPS:2 * N_ATTN_GROUPS + 12]
        scratch = refs[2 * N_ATTN_GROUPS + 12:]
    (y_ref, gate_ref, x_ref, wa_ref, wglu_ref, wout_ref, gffn_ref, wr_ref, br_ref,
     h_ref, hn_ref, route_ref) = rest
    tm = x_ref.shape[0]
    if merged_attn:
        attn = attn_ref[...]
    else:
        natural = []
        for idx, ref in enumerate(group_refs):
            dil = ATTN_GROUPS[idx % N_ATTN_GROUPS][1]
            if dil == 1:
                natural.append(ref[...])
                continue
            scr = scratch[idx]
            n_chunks = GROUP_WIDTH // LANES
            for r in range(dil):
                for j in range(n_chunks):
                    col = r * GROUP_WIDTH + j * LANES
                    scr[j, pl.ds(r, tm // dil, stride=dil), :] = ref[:, col:col + LANES]
            natural.append(jnp.concatenate([scr[j] for j in range(n_chunks)], axis=1))
        os, ls = natural[:N_ATTN_GROUPS], natural[N_ATTN_GROUPS:]
        top = jnp.maximum(jnp.maximum(ls[0], ls[1]), ls[2])
        es = [jnp.exp(l - top) for l in ls]
        attn = (es[0] * os[0] + es[1] * os[1] + es[2] * os[2]) / (es[0] + es[1] + es[2])
    attn_out = jnp.dot(attn.astype(BF16), wa_ref[...], preferred_element_type=F32)
    glu = jnp.dot(jax.nn.gelu(y_ref[...]).astype(BF16), wglu_ref[...], preferred_element_type=F32)
    ssm_out = glu[:, :D_MODEL] * jax.nn.sigmoid(glu[:, D_MODEL:])
    merged = jax.nn.sigmoid(gate_ref[:, :D_MODEL]) * attn_out + jax.nn.sigmoid(gate_ref[:, D_MODEL:]) * ssm_out
    h = x_ref[...] + jnp.dot(merged.astype(BF16), wout_ref[...], preferred_element_type=F32)
    h_ref[...] = h
    var = jnp.mean(h * h, axis=-1, keepdims=True)
    hn = h * lax.rsqrt(var + NORM_EPS) * gffn_ref[...]
    hn_ref[...] = hn
    logits = jnp.dot(hn, wr_ref[...], preferred_element_type=F32, precision=lax.Precision.HIGHEST) + br_ref[...]
    lane = lax.broadcasted_iota(jnp.int32, logits.shape, 1).astype(F32)
    far = float(LANES)

    def first_argmax(vals):
        top_v = jnp.max(vals, axis=1, keepdims=True)
        return top_v, jnp.min(jnp.where(vals == top_v, lane, far), axis=1, keepdims=True)

    group_logits = jnp.where(lane < N_EXPERT_GROUPS, logits, -jnp.inf)
    g_top, g_idx = first_argmax(group_logits)
    p_group = 1.0 / jnp.sum(jnp.exp(group_logits - g_top), axis=1, keepdims=True)
    first_lane = N_EXPERT_GROUPS + g_idx * EXPERTS_PER_GROUP
    in_group = (lane >= first_lane) & (lane < first_lane + EXPERTS_PER_GROUP)
    expert_logits = jnp.where(in_group, logits, -jnp.inf)
    v1, i1 = first_argmax(expert_logits)
    v2, i2 = first_argmax(jnp.where(lane == i1, -jnp.inf, expert_logits))
    e2 = jnp.exp(v2 - v1)
    w1 = p_group / (1.0 + e2)
    w2 = p_group * e2 / (1.0 + e2)
    route = jnp.where(lane == 0, i1 - N_EXPERT_GROUPS,
                      jnp.where(lane == 1, i2 - N_EXPERT_GROUPS,
                                jnp.where(lane == 2, w1, jnp.where(lane == 3, w2, 0.0))))
    route_ref[...] = route


def _mix(attn_inputs, y, gates, x2d, wa, wglu, wout, gffn, wr, br):
    n = x2d.shape[0]
    tm = PROJ_TILE
    row = lambda i: (i, 0)
    const = lambda i: (0, 0)
    merged_attn = len(attn_inputs) == 1
    in_specs = [pl.BlockSpec((tm * GROUP_WIDTH // a.shape[1], a.shape[1]), row) for a in attn_inputs]
    scratch = [] if merged_attn else [pltpu.VMEM((GROUP_WIDTH // LANES, tm, LANES), F32) for _ in attn_inputs]
    in_specs += [
        pl.BlockSpec((tm, SSM_WIDTH), row),
        pl.BlockSpec((tm, 2 * D_MODEL), row),
        pl.BlockSpec((tm, D_MODEL), row),
        pl.BlockSpec(wa.shape, const),
        pl.BlockSpec(wglu.shape, const),
        pl.BlockSpec(wout.shape, const),
        pl.BlockSpec((1, D_MODEL), const),
        pl.BlockSpec(wr.shape, const),
        pl.BlockSpec((1, LANES), const),
    ]
    return pl.pallas_call(
        functools.partial(_mix_kernel, merged_attn=merged_attn),
        grid=(n // tm,),
        in_specs=in_specs,
        out_specs=(pl.BlockSpec((tm, D_MODEL), row), pl.BlockSpec((tm, D_MODEL), row), pl.BlockSpec((tm, LANES), row)),
        out_shape=(jax.ShapeDtypeStruct((n, D_MODEL), F32), jax.ShapeDtypeStruct((n, D_MODEL), F32),
                   jax.ShapeDtypeStruct((n, LANES), F32)),
        scratch_shapes=scratch,
        compiler_params=_params("parallel"),
        name="branch_mix",
    )(*attn_inputs, y, gates, x2d, wa, wglu, wout, gffn, wr, br)


def _row_copy(src, src_row, dst, dst_row, sem):
    return pltpu.make_async_copy(src.at[pl.ds(src_row, 1)], dst.at[pl.ds(dst_row, 1)], sem)


def _dispatch_kernel(pad_end_ref, padded_ref, n_used_ref, slot_ref, hn_ref, xs_hbm, zero_buf, sem, zero_sem):
    n_blocks = xs_hbm.shape[0] // MOE_BLOCK

    @pl.when(pl.program_id(0) == 0)
    def _():
        zero_buf[...] = jnp.zeros_like(zero_buf)

        def block_copy(b):
            start = pl.multiple_of(b * MOE_BLOCK, MOE_BLOCK)
            return pltpu.make_async_copy(zero_buf, xs_hbm.at[pl.ds(start, MOE_BLOCK)], zero_sem)

        def segment_tails(action):
            def body(e, carry):
                @pl.when(padded_ref[e] > 0)
                def _():
                    action(block_copy(pad_end_ref[e] // MOE_BLOCK - 1))
                return carry
            lax.fori_loop(0, N_EXPERTS, body, 0)

        def unused_blocks(action):
            def body(b, carry):
                @pl.when(b >= n_used_ref[0])
                def _():
                    action(block_copy(b))
                return carry
            lax.fori_loop(0, n_blocks, body, 0)

        segment_tails(lambda cp: cp.start())
        unused_blocks(lambda cp: cp.start())
        segment_tails(lambda cp: cp.wait())
        unused_blocks(lambda cp: cp.wait())

    def issue(j, carry):
        _row_copy(hn_ref, j, xs_hbm, slot_ref[2 * j], sem).start(priority=0)
        _row_copy(hn_ref, j, xs_hbm, slot_ref[2 * j + 1], sem).start(priority=1)
        return carry

    def drain(j, carry):
        _row_copy(hn_ref, 0, xs_hbm, 0, sem).wait()
        _row_copy(hn_ref, 0, xs_hbm, 0, sem).wait()
        return carry

    lax.fori_loop(0, ROW_TILE, issue, 0, unroll=DMA_UNROLL)
    lax.fori_loop(0, ROW_TILE, drain, 0, unroll=DMA_UNROLL)


def _dispatch(pad_end, padded, n_used, slots, hn, n_slots):
    n = hn.shape[0]
    return pl.pallas_call(
        _dispatch_kernel,
        grid_spec=pltpu.PrefetchScalarGridSpec(
            num_scalar_prefetch=3,
            grid=(n // ROW_TILE,),
            in_specs=[
                pl.BlockSpec((2 * ROW_TILE,), lambda i, *_: (i,), memory_space=pltpu.SMEM),
                pl.BlockSpec((ROW_TILE, D_MODEL), lambda i, *_: (i, 0)),
            ],
            out_specs=pl.BlockSpec(memory_space=pl.ANY),
            scratch_shapes=[pltpu.VMEM((MOE_BLOCK, D_MODEL), F32), pltpu.SemaphoreType.DMA(()),
                            pltpu.SemaphoreType.DMA(())],
        ),
        out_shape=jax.ShapeDtypeStruct((n_slots, D_MODEL), F32),
        compiler_params=_params("arbitrary"),
        name="moe_dispatch",
    )(pad_end, padded, n_used, slots, hn)


def _expert_kernel(block_e_ref, n_used_ref, xs_ref, wg_ref, wu_ref, wd_ref, yb_ref):
    del block_e_ref
    i = pl.program_id(0)

    @pl.when(i < n_used_ref[0])
    def _():
        xb = xs_ref[...].astype(BF16)
        gate = jnp.dot(xb, wg_ref[...].astype(BF16), preferred_element_type=F32)
        up = jnp.dot(xb, wu_ref[...].astype(BF16), preferred_element_type=F32)
        hmid = (jax.nn.silu(gate) * up).astype(BF16)
        yb_ref[...] = jnp.dot(hmid, wd_ref[...].astype(BF16), preferred_element_type=F32)

    @pl.when(i >= n_used_ref[0])
    def _():
        yb_ref[...] = jnp.zeros_like(yb_ref)


def _experts(block_e, n_used, xs, wg, wu, wd):
    n_blocks = xs.shape[0] // MOE_BLOCK
    grid_spec = pltpu.PrefetchScalarGridSpec(
        num_scalar_prefetch=2,
        grid=(n_blocks,),
        in_specs=[
            pl.BlockSpec((MOE_BLOCK, D_MODEL), lambda i, be, nu: (jnp.minimum(i, nu[0] - 1), 0)),
            pl.BlockSpec((None, D_MODEL, D_EXPERT), lambda i, be, nu: (be[i], 0, 0)),
            pl.BlockSpec((None, D_MODEL, D_EXPERT), lambda i, be, nu: (be[i], 0, 0)),
            pl.BlockSpec((None, D_EXPERT, D_MODEL), lambda i, be, nu: (be[i], 0, 0)),
        ],
        out_specs=pl.BlockSpec((MOE_BLOCK, D_MODEL), lambda i, be, nu: (i, 0)),
    )
    return pl.pallas_call(
        _expert_kernel,
        grid_spec=grid_spec,
        out_shape=jax.ShapeDtypeStruct(xs.shape, F32),
        compiler_params=_params("arbitrary"),
        name="moe_experts",
    )(block_e, n_used, xs, wg, wu, wd)


def _combine_kernel(slot_ref, h_ref, route_ref, g_ref, yb_hbm, out_ref, buf_a, buf_b, sem):
    def issue(j, carry):
        _row_copy(yb_hbm, slot_ref[2 * j], buf_a, j, sem).start(priority=0)
        _row_copy(yb_hbm, slot_ref[2 * j + 1], buf_b, j, sem).start(priority=1)
        return carry

    def drain(j, carry):
        _row_copy(yb_hbm, 0, buf_a, 0, sem).wait()
        _row_copy(yb_hbm, 0, buf_b, 0, sem).wait()
        return carry

    lax.fori_loop(0, ROW_TILE, issue, 0, unroll=DMA_UNROLL)
    lax.fori_loop(0, ROW_TILE, drain, 0, unroll=DMA_UNROLL)
    route = route_ref[...]
    h = h_ref[...] + (route[:, 2:3] * buf_a[...] + route[:, 3:4] * buf_b[...])
    var = jnp.mean(h * h, axis=-1, keepdims=True)
    out_ref[...] = h * lax.rsqrt(var + NORM_EPS) * g_ref[...]


def _combine(slots, h, route, g_final, yb):
    n = h.shape[0]
    row = lambda i: (i, 0)
    return pl.pallas_call(
        _combine_kernel,
        grid=(n // ROW_TILE,),
        in_specs=[
            pl.BlockSpec((2 * ROW_TILE,), lambda i: (i,), memory_space=pltpu.SMEM),
            pl.BlockSpec((ROW_TILE, D_MODEL), row),
            pl.BlockSpec((ROW_TILE, LANES), row),
            pl.BlockSpec((1, D_MODEL), lambda i: (0, 0)),
            pl.BlockSpec(memory_space=pl.ANY),
        ],
        out_specs=pl.BlockSpec((ROW_TILE, D_MODEL), row),
        out_shape=jax.ShapeDtypeStruct((n, D_MODEL), F32),
        scratch_shapes=[
            pltpu.VMEM((ROW_TILE, D_MODEL), F32),
            pltpu.VMEM((ROW_TILE, D_MODEL), F32),
            pltpu.SemaphoreType.DMA(()),
        ],
        compiler_params=_params("arbitrary"),
        name="moe_combine",
    )(slots, h, route, g_final.reshape(1, D_MODEL), yb)


def _slot_assignment(route, n_blocks):
    flat_e = route[:, 0:2].astype(jnp.int32).reshape(-1)
    onehot = (flat_e[:, None] == jnp.arange(N_EXPERTS, dtype=jnp.int32)[None, :]).astype(jnp.int32)
    running = jnp.cumsum(onehot, axis=0)
    rank = jnp.sum(onehot * running, axis=1) - 1
    counts = running[-1]
    padded = ((counts + MOE_BLOCK - 1) // MOE_BLOCK) * MOE_BLOCK
    pad_end = jnp.cumsum(padded)
    pad_start = pad_end - padded
    slots = jnp.sum(onehot * pad_start[None, :], axis=1) + rank
    block_start = jnp.arange(n_blocks, dtype=jnp.int32) * MOE_BLOCK
    block_e = jnp.minimum(jnp.sum((pad_end[None, :] <= block_start[:, None]).astype(jnp.int32), axis=1), N_EXPERTS - 1)
    n_used = (pad_end[-1:] // MOE_BLOCK).astype(jnp.int32)
    return slots.astype(jnp.int32), block_e, n_used, pad_end.astype(jnp.int32), padded.astype(jnp.int32)


def _moe_and_final_norm(h, hn, route, wg, wu, wd, g_final):
    n = h.shape[0]
    n_blocks = (2 * n) // MOE_BLOCK + N_EXPERTS
    slots, block_e, n_used, pad_end, padded = _slot_assignment(route, n_blocks)
    xs = _dispatch(pad_end, padded, n_used, slots, hn, n_blocks * MOE_BLOCK)
    yb = _experts(block_e, n_used, xs, wg, wu, wd)
    return _combine(slots, h, route, g_final, yb)


def kernel(x_prompt, x_sample, cache_kv_w128, cache_kv_w512, cache_kv_w2048, state_ssm, g_attn_norm, w_in, ssm_log_dt, ssm_a_re, ssm_a_im, ssm_b_re, ssm_b_im, ssm_c_re, ssm_c_im, ssm_d, w_glu, w_attn_branch, w_out, g_ffn_norm, w_router_group, b_router_group, w_router_expert, b_router_expert, w_exp_gate, w_exp_up, w_exp_down, g_final):
    batch, seq, _ = x_prompt.shape
    dec_batch, dec_seq, _ = x_sample.shape
    past_len = cache_kv_w2048.shape[2]
    layer = 0

    w_in_b = w_in[layer].astype(BF16)
    wa = w_attn_branch[layer].astype(BF16)
    wglu = w_glu[layer].astype(BF16)
    wout = w_out[layer].astype(BF16)
    wg, wu, wd = w_exp_gate[layer], w_exp_up[layer], w_exp_down[layer]
    gffn = g_ffn_norm[layer].reshape(1, D_MODEL)
    pad = LANES - N_EXPERT_GROUPS - N_EXPERTS
    wr = jnp.concatenate([w_router_group[layer], w_router_expert[layer], jnp.zeros((D_MODEL, pad), F32)], axis=1)
    br = jnp.concatenate([b_router_group[layer], b_router_expert[layer], jnp.zeros((pad,), F32)]).reshape(1, LANES)
    ssm_w = _ssm_weights(ssm_log_dt[layer], ssm_a_re[layer], ssm_a_im[layer], ssm_b_re[layer], ssm_b_im[layer],
                         ssm_c_re[layer], ssm_c_im[layer])
    d_flat = ssm_d[layer].astype(F32).reshape(1, SSM_WIDTH)

    def time_major(a, nb, steps):
        return jnp.transpose(a.reshape(nb, steps, -1), (1, 0, 2)).reshape(nb * steps, -1)

    def batch_major(a, nb, steps):
        return jnp.transpose(a.reshape(steps, nb, -1), (1, 0, 2)).reshape(nb * steps, -1)

    xp = x_prompt.reshape(batch * seq, D_MODEL)
    q_p, kv_p, kv_t_p, u_p, gates_p = _in_projection_prompt(xp, g_attn_norm[layer], w_in_b, batch, seq)
    attn_in = []
    lse_in = []
    for g, (win, dil) in enumerate(ATTN_GROUPS):
        o, lse = _prompt_attention(q_p[g], kv_p[g], g, dil, batch, seq)
        attn_in.append(o)
        lse_in.append(lse)
    zeros_state = jnp.zeros((batch, SSM_COLS), F32)
    y_tm, hr_p, hi_p = _ssm(time_major(u_p, batch, seq), zeros_state, zeros_state, ssm_w, d_flat, batch)
    y_p = batch_major(y_tm, batch, seq)
    h_p, hn_p, route_p = _mix(attn_in + lse_in, y_p, gates_p, xp, wa, wglu, wout, gffn, wr, br)
    out_p = _moe_and_final_norm(h_p, hn_p, route_p, wg, wu, wd, g_final)

    xs = x_sample.reshape(dec_batch * dec_seq, D_MODEL)
    pos_s = past_len + (jnp.arange(dec_batch * dec_seq, dtype=jnp.int32) % dec_seq)
    q_s, kv0_s, kv1_s, kv2_s, u_s, gates_s = _in_projection_decode(xs, g_attn_norm[layer], w_in_b, pos_s)
    kv_s = (kv0_s, kv1_s, kv2_s)
    caches = (cache_kv_w128[layer], cache_kv_w512[layer], cache_kv_w2048[layer])
    attn_s = _decode_attention(q_s, caches, kv_s, dec_batch, dec_seq)
    st = state_ssm[layer].astype(F32).reshape(dec_batch, SSM_COLS, 2)
    ys_tm, hr_s, hi_s = _ssm(time_major(u_s, dec_batch, dec_seq), st[:, :, 0], st[:, :, 1], ssm_w, d_flat, dec_batch)
    y_s = batch_major(ys_tm, dec_batch, dec_seq)
    h_s, hn_s, route_s = _mix([attn_s], y_s, gates_s, xs, wa, wglu, wout, gffn, wr, br)
    out_s = _moe_and_final_norm(h_s, hn_s, route_s, wg, wu, wd, g_final)

    kv_tail = (2, HEADS_PER_GROUP, HEAD_DIM)
    outs = [out_p.reshape(batch, seq, D_MODEL), out_s.reshape(dec_batch, dec_seq, D_MODEL)]
    for g, (win, dil) in enumerate(ATTN_GROUPS):
        keep = min(win, seq)
        rows_p = jnp.transpose(kv_t_p[g].reshape(batch, 2, HEADS_PER_GROUP, HEAD_DIM, keep), (0, 4, 1, 2, 3))
        outs.append(rows_p[None])
        outs.append(kv_s[g].reshape((1, dec_batch, dec_seq) + kv_tail))
    outs.append(jnp.stack([hr_p, hi_p], axis=-1).reshape(1, batch, N_SSM_GROUPS, SSM_STATE, 2))
    outs.append(jnp.stack([hr_s, hi_s], axis=-1).reshape(1, dec_batch, N_SSM_GROUPS, SSM_STATE, 2))
    return tuple(outs)
```

```python
import functools
import math

import jax
import jax.numpy as jnp
from jax import lax
from jax.experimental import pallas as pl
from jax.experimental.pallas import tpu as pltpu

F32 = jnp.float32
BF16 = jnp.bfloat16

D_MODEL = 1024
HEAD_DIM = 64
HEADS_PER_GROUP = 8
GROUP_WIDTH = HEADS_PER_GROUP * HEAD_DIM
ATTN_GROUPS = ((128, 1), (512, 4), (2048, 16))
N_ATTN_GROUPS = len(ATTN_GROUPS)
ATTN_WIDTH = N_ATTN_GROUPS * GROUP_WIDTH
ROT_DIM = HEAD_DIM // 4
ROPE_THETA = 500000.0
WINDOW_KEYS = 128
SSM_GROUP_CH = 16
SSM_WIDTH = D_MODEL // 2
N_SSM_GROUPS = SSM_WIDTH // SSM_GROUP_CH
SSM_STATE = 64
SSM_COLS = N_SSM_GROUPS * SSM_STATE
IN_WIDTH = 3 * ATTN_WIDTH + SSM_WIDTH + 2 * D_MODEL
N_EXPERT_GROUPS = 4
EXPERTS_PER_GROUP = 8
N_EXPERTS = N_EXPERT_GROUPS * EXPERTS_PER_GROUP
D_EXPERT = D_MODEL // 4
NORM_EPS = 1e-6

LANES = 128
SUBLANES = 8
VMEM_LIMIT = 56 * 1024 * 1024

PROJ_TILE = 256
MATMUL_LOOKAHEAD = 1
ATTN_TILE = 128
ATTN_RESIDUES_PER_STEP = 4
ATTN_TILES_PER_STEP = 8
SSM_ROWS = 512
MOE_BLOCK = 256
ROW_TILE = 512
DMA_UNROLL = 8


def _params(*sem):
    return pltpu.CompilerParams(dimension_semantics=sem, vmem_limit_bytes=VMEM_LIMIT)


def _normed_input(x_ref, g_ref):
    x = x_ref[...]
    var = jnp.mean(x * x, axis=-1, keepdims=True)
    return (x * lax.rsqrt(var + NORM_EPS) * g_ref[...]).astype(BF16)


def _rope_fn(cos_ref, sin_ref):
    cos = cos_ref[...]
    sin = sin_ref[...]
    lane = lax.broadcasted_iota(jnp.int32, cos.shape, 1) % HEAD_DIM
    first_half = lane < ROT_DIM // 2
    rotated = lane < ROT_DIM

    def rope_chunk(c):
        partner = jnp.where(first_half, pltpu.roll(c, LANES - ROT_DIM // 2, 1), pltpu.roll(c, ROT_DIM // 2, 1))
        return jnp.where(rotated, c * cos + partner * sin, c)

    def rope(t):
        return jnp.concatenate([rope_chunk(t[:, j * LANES:(j + 1) * LANES]) for j in range(GROUP_WIDTH // LANES)], axis=1)

    return rope


def _projection_tiles(xn, w_ref):
    for c in range(IN_WIDTH // GROUP_WIDTH):
        yield c, jnp.dot(xn, w_ref[:, c * GROUP_WIDTH:(c + 1) * GROUP_WIDTH], preferred_element_type=F32)


def _inproj_decode_kernel(x_ref, g_ref, w_ref, cos_ref, sin_ref, q_ref, kv0_ref, kv1_ref, kv2_ref, u_ref, gate_ref):
    rope = _rope_fn(cos_ref, sin_ref)
    kv_refs = (kv0_ref, kv1_ref, kv2_ref)
    for c, acc in _projection_tiles(_normed_input(x_ref, g_ref), w_ref):
        if c < 3:
            q_ref[:, c * GROUP_WIDTH:(c + 1) * GROUP_WIDTH] = (rope(acc) * (HEAD_DIM ** -0.5)).astype(BF16)
        elif c < 6:
            kv_refs[c - 3][:, :GROUP_WIDTH] = rope(acc)
        elif c < 9:
            kv_refs[c - 6][:, GROUP_WIDTH:] = acc
        elif c == 9:
            u_ref[...] = acc
        else:
            gate_ref[:, (c - 10) * GROUP_WIDTH:(c - 9) * GROUP_WIDTH] = acc


def _transposed_projection(wt, xn, cos_t, sin_t):
    t = lax.dot_general(wt, xn, (((1,), (1,)), ((), ())), preferred_element_type=F32)
    if cos_t is None:
        return t
    half_rot = ROT_DIM // 2
    pieces = []
    for h in range(HEADS_PER_GROUP):
        x1 = t[h * HEAD_DIM:h * HEAD_DIM + half_rot]
        x2 = t[h * HEAD_DIM + half_rot:h * HEAD_DIM + ROT_DIM]
        pieces += [x1 * cos_t - x2 * sin_t, x2 * cos_t + x1 * sin_t, t[h * HEAD_DIM + ROT_DIM:(h + 1) * HEAD_DIM]]
    return jnp.concatenate(pieces, axis=0)


def _kv_tail_kernel(x_ref, g_ref, wk_ref, wv_ref, cos_t_ref, sin_t_ref, kt_ref):
    xn = _normed_input(x_ref, g_ref)
    kt_ref[0] = _transposed_projection(wk_ref[...], xn, cos_t_ref[...], sin_t_ref[...])
    kt_ref[1] = _transposed_projection(wv_ref[...], xn, None, None)


def _inproj_prompt_kernel(x_ref, g_ref, w_ref, cos_ref, sin_ref, wt_ref, cos_t_ref, sin_t_ref, *refs, full_groups):
    q_refs, kv_refs = refs[0:N_ATTN_GROUPS], refs[N_ATTN_GROUPS:2 * N_ATTN_GROUPS]
    kt_refs = refs[2 * N_ATTN_GROUPS:2 * N_ATTN_GROUPS + len(full_groups)]
    u_ref, gate_ref = refs[2 * N_ATTN_GROUPS + len(full_groups):2 * N_ATTN_GROUPS + len(full_groups) + 2]
    scratch = refs[2 * N_ATTN_GROUPS + len(full_groups) + 2:]
    rope = _rope_fn(cos_ref, sin_ref)
    tm = x_ref.shape[0]

    dilated = [g for g, (_, d) in enumerate(ATTN_GROUPS) if d > 1]

    def store_rows(dst_ref, val, kind, g, col0, col_stride):
        dil = ATTN_GROUPS[g][1]
        if dil == 1:
            dst_ref[:, col0:col0 + GROUP_WIDTH] = val.astype(BF16)
            return
        scr = scratch[kind * len(dilated) + dilated.index(g)]
        n_chunks = GROUP_WIDTH // LANES
        for j in range(n_chunks):
            scr[j] = val[:, j * LANES:(j + 1) * LANES]
        rows = tm // dil
        for r in range(dil):
            piece = jnp.concatenate([scr[j, pl.ds(r, rows, stride=dil), :] for j in range(n_chunks)], axis=1)
            dst_ref[:, col0 + r * col_stride:col0 + r * col_stride + GROUP_WIDTH] = piece.astype(BF16)

    xn = _normed_input(x_ref, g_ref)

    def emit(c, acc):
        g = c % N_ATTN_GROUPS
        if c < 3:
            store_rows(q_refs[g], rope(acc) * (HEAD_DIM ** -0.5), 0, g, 0, GROUP_WIDTH)
        elif c < 6:
            store_rows(kv_refs[g], rope(acc), 1, g, 0, 2 * GROUP_WIDTH)
            if g in full_groups:
                kt_refs[full_groups.index(g)][0] = _transposed_projection(
                    wt_ref[full_groups.index(g)], xn, cos_t_ref[...], sin_t_ref[...])
        elif c < 9:
            store_rows(kv_refs[g], acc, 2, g, GROUP_WIDTH, 2 * GROUP_WIDTH)
            if g in full_groups:
                kt_refs[full_groups.index(g)][1] = _transposed_projection(
                    wt_ref[len(full_groups) + full_groups.index(g)], xn, None, None)
        elif c == 9:
            u_ref[...] = acc
        else:
            gate_ref[:, (c - 10) * GROUP_WIDTH:(c - 9) * GROUP_WIDTH] = acc

    pending = []
    for item in _projection_tiles(xn, w_ref):
        pending.append(item)
        if len(pending) > MATMUL_LOOKAHEAD:
            emit(*pending.pop(0))
    for item in pending:
        emit(*item)


def _rope_tables(pos):
    half = ROT_DIM // 2
    inv_freq = ROPE_THETA ** (-(jnp.arange(half, dtype=F32) / half))
    ang = pos.astype(F32)[:, None] * inv_freq[None, :]
    cos, sin = jnp.cos(ang), jnp.sin(ang)
    n = pos.shape[0]
    rest = HEAD_DIM - ROT_DIM
    cos_h = jnp.concatenate([cos, cos, jnp.ones((n, rest), F32)], axis=1)
    sin_h = jnp.concatenate([-sin, sin, jnp.zeros((n, rest), F32)], axis=1)
    return jnp.tile(cos_h, (1, LANES // HEAD_DIM)), jnp.tile(sin_h, (1, LANES // HEAD_DIM))


def _inproj_in_specs(tm, n_pos_tiles):
    const = lambda i: (0, 0)
    tab = lambda i: (i % n_pos_tiles, 0)
    return [
        pl.BlockSpec((tm, D_MODEL), lambda i: (i, 0)),
        pl.BlockSpec((1, D_MODEL), const),
        pl.BlockSpec((D_MODEL, IN_WIDTH), const, pipeline_mode=pl.Buffered(1)),
        pl.BlockSpec((tm, LANES), tab),
        pl.BlockSpec((tm, LANES), tab),
    ]


def _in_projection_decode(x2d, g, w_bf16, pos):
    n = x2d.shape[0]
    tm = PROJ_TILE
    cos_t, sin_t = _rope_tables(pos)
    row = lambda i: (i, 0)
    widths = (ATTN_WIDTH, 2 * GROUP_WIDTH, 2 * GROUP_WIDTH, 2 * GROUP_WIDTH, SSM_WIDTH, 2 * D_MODEL)
    dtypes = (BF16, F32, F32, F32, F32, F32)
    return pl.pallas_call(
        _inproj_decode_kernel,
        grid=(n // tm,),
        in_specs=_inproj_in_specs(tm, pos.shape[0] // tm),
        out_specs=tuple(pl.BlockSpec((tm, w), row) for w in widths),
        out_shape=tuple(jax.ShapeDtypeStruct((n, w), d) for w, d in zip(widths, dtypes)),
        compiler_params=_params("parallel"),
        name="in_projection_decode",
    )(x2d, g.reshape(1, D_MODEL), w_bf16, cos_t, sin_t)


def _in_projection_prompt(x2d, g, w_bf16, batch, seq):
    n = batch * seq
    tm = PROJ_TILE
    tiles_per_seq = seq // tm
    pos = jnp.arange(seq, dtype=jnp.int32)
    cos_t, sin_t = _rope_tables(pos)
    half = ROT_DIM // 2
    ang_t = (ROPE_THETA ** (-(jnp.arange(half, dtype=F32) / half)))[:, None] * pos.astype(F32)[None, :]
    cos_tr, sin_tr = jnp.cos(ang_t), jnp.sin(ang_t)
    wk_t = jnp.transpose(w_bf16[:, ATTN_WIDTH:2 * ATTN_WIDTH]).reshape(N_ATTN_GROUPS, GROUP_WIDTH, D_MODEL)
    wv_t = jnp.transpose(w_bf16[:, 2 * ATTN_WIDTH:3 * ATTN_WIDTH]).reshape(N_ATTN_GROUPS, GROUP_WIDTH, D_MODEL)
    keeps = tuple(min(win, seq) for win, _ in ATTN_GROUPS)
    full_groups = tuple(i for i, keep in enumerate(keeps) if keep == seq)
    w_full_t = jnp.concatenate([wk_t[jnp.array(full_groups)], wv_t[jnp.array(full_groups)]], axis=0)
    g_row = g.reshape(1, D_MODEL)
    row = lambda i: (i, 0)
    in_specs = _inproj_in_specs(tm, tiles_per_seq) + [
        pl.BlockSpec(w_full_t.shape, lambda i: (0, 0, 0), pipeline_mode=pl.Buffered(1)),
        pl.BlockSpec((half, tm), lambda i: (0, i % tiles_per_seq)),
        pl.BlockSpec((half, tm), lambda i: (0, i % tiles_per_seq)),
    ]
    out_specs, out_shape = [], []
    for width in (GROUP_WIDTH, 2 * GROUP_WIDTH):
        for _, dil in ATTN_GROUPS:
            out_specs.append(pl.BlockSpec((tm // dil, dil * width), row))
            out_shape.append(jax.ShapeDtypeStruct((n // dil, dil * width), BF16))
    for _ in full_groups:
        out_specs.append(pl.BlockSpec((None, 2, GROUP_WIDTH, tm), lambda i: (i // tiles_per_seq, 0, 0, i % tiles_per_seq)))
        out_shape.append(jax.ShapeDtypeStruct((batch, 2, GROUP_WIDTH, seq), F32))
    out_specs += [pl.BlockSpec((tm, SSM_WIDTH), row), pl.BlockSpec((tm, 2 * D_MODEL), row)]
    out_shape += [jax.ShapeDtypeStruct((n, SSM_WIDTH), F32), jax.ShapeDtypeStruct((n, 2 * D_MODEL), F32)]
    outs = pl.pallas_call(
        functools.partial(_inproj_prompt_kernel, full_groups=full_groups),
        grid=(n // tm,),
        in_specs=in_specs,
        out_specs=tuple(out_specs),
        out_shape=tuple(out_shape),
        scratch_shapes=[pltpu.VMEM((GROUP_WIDTH // LANES, tm, LANES), F32)
                        for _ in range(3 * sum(1 for _, d in ATTN_GROUPS if d > 1))],
        compiler_params=_params("parallel"),
        name="in_projection_prompt",
    )(x2d, g_row, w_bf16, cos_t, sin_t, w_full_t, cos_tr, sin_tr)
    n_full = len(full_groups)
    kv_t = dict(zip(full_groups, outs[6:6 + n_full]))
    for grp, keep in enumerate(keeps):
        if grp in full_groups:
            continue
        blk = min(keep, tm)
        first = (seq - keep) // blk
        per_seq = seq // blk
        tail = lambda b, j, first=first: (0, first + j)
        kv_t[grp] = pl.pallas_call(
            _kv_tail_kernel,
            grid=(batch, keep // blk),
            in_specs=[
                pl.BlockSpec((blk, D_MODEL), lambda b, j, first=first, per_seq=per_seq: (b * per_seq + first + j, 0)),
                pl.BlockSpec((1, D_MODEL), lambda b, j: (0, 0)),
                pl.BlockSpec((None, GROUP_WIDTH, D_MODEL), lambda b, j, grp=grp: (grp, 0, 0)),
                pl.BlockSpec((None, GROUP_WIDTH, D_MODEL), lambda b, j, grp=grp: (grp, 0, 0)),
                pl.BlockSpec((half, blk), tail),
                pl.BlockSpec((half, blk), tail),
            ],
            out_specs=pl.BlockSpec((None, 2, GROUP_WIDTH, blk), lambda b, j: (b, 0, 0, j)),
            out_shape=jax.ShapeDtypeStruct((batch, 2, GROUP_WIDTH, keep), F32),
            compiler_params=_params("parallel", "parallel"),
            name=f"kv_tail_g{grp}",
        )(x2d, g_row, wk_t, wv_t, cos_tr, sin_tr)
    return outs[0:3], outs[3:6], tuple(kv_t[i] for i in range(N_ATTN_GROUPS)), outs[6 + n_full], outs[7 + n_full]


def _window_attn_kernel(q_ref, kvc_ref, *rest, windowed, residues, sub_tiles):
    if windowed:
        kvp_ref, o_ref, lse_ref = rest
    else:
        o_ref, lse_ref = rest
    n_keys = 2 * ATTN_TILE if windowed else ATTN_TILE
    rows = lax.broadcasted_iota(jnp.int32, (ATTN_TILE, n_keys), 0)
    cols = lax.broadcasted_iota(jnp.int32, (ATTN_TILE, n_keys), 1)
    if windowed:
        band = (cols >= rows) & (cols <= rows + WINDOW_KEYS)
        first_tile_band = band & ((cols >= ATTN_TILE) | (pl.program_id(2) > 0))
    else:
        band = cols <= rows
    low_head = lax.broadcasted_iota(jnp.int32, (ATTN_TILE, LANES), 1) < HEAD_DIM
    for r in range(residues):
        for s in range(sub_tiles):
            rs = slice(s * ATTN_TILE, (s + 1) * ATTN_TILE)
            k0 = r * 2 * GROUP_WIDTH
            q = q_ref[rs, r * GROUP_WIDTH:(r + 1) * GROUP_WIDTH]
            k = kvc_ref[rs, k0:k0 + GROUP_WIDTH]
            v = kvc_ref[rs, k0 + GROUP_WIDTH:k0 + 2 * GROUP_WIDTH]
            valid = band
            if windowed:
                if s == 0:
                    prev_ref, ps, valid = kvp_ref, slice(0, ATTN_TILE), first_tile_band
                else:
                    prev_ref, ps = kvc_ref, slice((s - 1) * ATTN_TILE, s * ATTN_TILE)
                k = jnp.concatenate([prev_ref[ps, k0:k0 + GROUP_WIDTH], k], axis=0)
                v = jnp.concatenate([prev_ref[ps, k0 + GROUP_WIDTH:k0 + 2 * GROUP_WIDTH], v], axis=0)
            valid2 = jnp.concatenate([valid, valid], axis=0)
            for j in range(GROUP_WIDTH // LANES):
                sl = slice(j * LANES, (j + 1) * LANES)
                osl = slice(r * GROUP_WIDTH + j * LANES, r * GROUP_WIDTH + (j + 1) * LANES)
                qj, kj, vj = q[:, sl], k[:, sl], v[:, sl]
                zero = jnp.zeros_like(qj)
                qm = jnp.concatenate([jnp.where(low_head, qj, zero), jnp.where(low_head, zero, qj)], axis=0)
                sc = lax.dot_general(qm, kj, (((1,), (1,)), ((), ())), preferred_element_type=F32)
                sc = jnp.where(valid2, sc, -jnp.inf)
                m = jnp.max(sc, axis=1, keepdims=True)
                p = jnp.exp(sc - m)
                l = jnp.sum(p, axis=1, keepdims=True)
                o2 = jnp.dot(p.astype(BF16), vj, preferred_element_type=F32) / l
                lse2 = jnp.broadcast_to(m + jnp.log(l), o2.shape)
                o_ref[rs, osl] = jnp.where(low_head, o2[:ATTN_TILE], o2[ATTN_TILE:])
                lse_ref[rs, osl] = jnp.where(low_head, lse2[:ATTN_TILE], lse2[ATTN_TILE:])


def _prompt_attention(q, kv, group, dilation, batch, seq):
    tg = seq // dilation
    n_tiles = tg // ATTN_TILE
    windowed = n_tiles > 1
    residues = min(dilation, ATTN_RESIDUES_PER_STEP)
    sub_tiles = min(n_tiles, ATTN_TILES_PER_STEP // residues)
    rows = sub_tiles * ATTN_TILE
    q3 = q.reshape(batch, tg, dilation * GROUP_WIDTH)
    kv3 = kv.reshape(batch, tg, dilation * 2 * GROUP_WIDTH)
    in_specs = [
        pl.BlockSpec((None, rows, residues * GROUP_WIDTH), lambda b, r, t: (b, t, r)),
        pl.BlockSpec((None, rows, residues * 2 * GROUP_WIDTH), lambda b, r, t: (b, t, r)),
    ]
    args = [q3, kv3]
    if windowed:
        in_specs.append(pl.BlockSpec((None, ATTN_TILE, residues * 2 * GROUP_WIDTH),
                                     lambda b, r, t: (b, jnp.maximum(t * sub_tiles - 1, 0), r)))
        args.append(kv3)
    out_spec = pl.BlockSpec((None, rows, residues * GROUP_WIDTH), lambda b, r, t: (b, t, r))
    out_sds = jax.ShapeDtypeStruct((batch, tg, dilation * GROUP_WIDTH), F32)
    o, lse = pl.pallas_call(
        functools.partial(_window_attn_kernel, windowed=windowed, residues=residues, sub_tiles=sub_tiles),
        grid=(batch, dilation // residues, n_tiles // sub_tiles),
        in_specs=in_specs,
        out_specs=(out_spec, out_spec),
        out_shape=(out_sds, out_sds),
        compiler_params=_params("parallel", "parallel", "arbitrary"),
        name=f"prompt_attention_g{group}",
    )(*args)
    flat = (batch * tg, dilation * GROUP_WIDTH)
    return o.reshape(flat), lse.reshape(flat)


def _sublane_total(x):
    x = x + pltpu.roll(x, 4, 0)
    x = x + pltpu.roll(x, 2, 0)
    return x + pltpu.roll(x, 1, 0)


def _head_sum(prod):
    width = prod.shape[1]
    row = lax.broadcasted_iota(jnp.int32, (HEADS_PER_GROUP, width), 0)
    out = jnp.zeros((HEADS_PER_GROUP, width), F32)
    for h in range(HEADS_PER_GROUP):
        part = prod[h * HEAD_DIM:h * HEAD_DIM + SUBLANES]
        for j in range(1, HEAD_DIM // SUBLANES):
            part = part + prod[h * HEAD_DIM + j * SUBLANES:h * HEAD_DIM + (j + 1) * SUBLANES]
        out = jnp.where(row == h, _sublane_total(part), out)
    return out


def _head_expand(x):
    width = x.shape[1]
    pieces = []
    for h in range(HEADS_PER_GROUP):
        pieces.extend([jnp.broadcast_to(x[h:h + 1, :], (SUBLANES, width))] * (HEAD_DIM // SUBLANES))
    return jnp.concatenate(pieces, axis=0)


def _split_dot(acc, sel, terms=2):
    out, rem = None, acc
    for _ in range(terms):
        hi = rem.astype(BF16)
        part = jnp.dot(hi, sel, preferred_element_type=F32)
        out = part if out is None else out + part
        rem = rem - hi.astype(F32)
    return out


def _decode_attn_kernel(q_ref, n0_ref, n1_ref, n2_ref, c0_ref, c1_ref, c2_ref, o_ref, *, dec_seq):
    nq = SUBLANES
    n_sub = HEAD_DIM // SUBLANES
    neg = -jnp.inf
    q_t = q_ref[...].T
    step = lax.broadcasted_iota(jnp.int32, (HEADS_PER_GROUP, nq), 1)
    step_wide = lax.broadcasted_iota(jnp.int32, (GROUP_WIDTH, nq), 1)
    real_step = step < dec_seq
    new_refs = (n0_ref, n1_ref, n2_ref)
    cache_refs = (c0_ref, c1_ref, c2_ref)

    def column(x, t, width):
        return jnp.broadcast_to(x[:, t:t + 1], (x.shape[0], width))

    def place(cols):
        rows = cols[0].shape[0]
        lane = lax.broadcasted_iota(jnp.int32, (rows, nq), 1)
        out = jnp.zeros((rows, nq), F32)
        for t, c in enumerate(cols):
            out = jnp.where(lane == t, jnp.broadcast_to(c, (rows, nq)), out)
        return out

    outs, lses = [], []
    for g, (win, dil) in enumerate(ATTN_GROUPS):
        c_ref = cache_refs[g]
        kv_new = new_refs[g][...]
        kn_t = kv_new[:, :GROUP_WIDTH].T
        vn_t = kv_new[:, GROUP_WIDTH:].T
        q_g = q_t[g * GROUP_WIDTH:(g + 1) * GROUP_WIDTH, :]
        pos = lax.broadcasted_iota(jnp.int32, (HEADS_PER_GROUP, win), 1)
        row8 = lax.broadcasted_iota(jnp.int32, (HEADS_PER_GROUP, win), 0)
        n_tiles = win // LANES

        def cache_scores(q_pat):
            s = jnp.zeros((HEADS_PER_GROUP, win), F32)
            for h in range(HEADS_PER_GROUP):
                part = None
                for j in range(n_sub):
                    r0 = h * HEAD_DIM + j * SUBLANES
                    qp = q_pat[r0:r0 + SUBLANES, :]
                    term = c_ref[0, r0:r0 + SUBLANES, :] * jnp.concatenate([qp] * n_tiles, axis=1)
                    part = term if part is None else part + term
                s = jnp.where(row8 == h, _sublane_total(part), s)
            return s

        def weighted_values(p):
            rows = []
            for h in range(HEADS_PER_GROUP):
                ph = jnp.broadcast_to(p[h:h + 1, :], (SUBLANES, win))
                for j in range(n_sub):
                    r0 = h * HEAD_DIM + j * SUBLANES
                    prod = ph * c_ref[1, r0:r0 + SUBLANES, :]
                    a = prod[:, :LANES]
                    for tile in range(1, n_tiles):
                        a = a + prod[:, tile * LANES:(tile + 1) * LANES]
                    rows.append(a)
            return jnp.concatenate(rows, axis=0)

        if dil == 1:
            spread = (lax.broadcasted_iota(jnp.int32, (nq, dec_seq * LANES), 0)
                      == lax.broadcasted_iota(jnp.int32, (nq, dec_seq * LANES), 1) // LANES).astype(BF16)
            q_cols = _split_dot(q_g, spread, terms=1)
            kn_cols = _split_dot(kn_t, spread, terms=3)
            vn_cols = _split_dot(vn_t, spread, terms=3)
            s_new = [_head_sum(q_g * kn_cols[:, u * LANES:u * LANES + nq]) for u in range(dec_seq)]
            ok_new = [(step >= u) & real_step for u in range(dec_seq)]
            s_cache = [cache_scores(q_cols[:, t * LANES:(t + 1) * LANES]) for t in range(dec_seq)]
            m = place([jnp.max(jnp.where(pos >= t, s_cache[t], neg), axis=1, keepdims=True) for t in range(dec_seq)])
            for u in range(dec_seq):
                m = jnp.maximum(m, jnp.where(ok_new[u], s_new[u], neg))
            e_cache = [jnp.where(pos >= t, jnp.exp(s_cache[t] - m[:, t:t + 1]), 0.0) for t in range(dec_seq)]
            e_new = [jnp.where(ok_new[u], jnp.exp(s_new[u] - m), 0.0) for u in range(dec_seq)]
            denom = place([jnp.sum(e, axis=1, keepdims=True) for e in e_cache])
            for u in range(dec_seq):
                denom = denom + e_new[u]
            denom = jnp.where(real_step, denom, 1.0)
            inv = 1.0 / denom
            cols = []
            for t in range(dec_seq):
                acc = weighted_values(e_cache[t] * inv[:, t:t + 1])
                cols.append(jnp.sum(acc, axis=1, keepdims=True))
            o_g = place(cols)
            for u in range(dec_seq):
                o_g = o_g + _head_expand(e_new[u] * inv) * vn_cols[:, u * LANES:u * LANES + nq]
        else:
            res = pos % dil
            t_i = lax.broadcasted_iota(jnp.int32, (nq, LANES), 0)
            l_i = lax.broadcasted_iota(jnp.int32, (nq, LANES), 1)
            residue_pat = ((l_i % dil == t_i) & (t_i < dec_seq)).astype(BF16)
            s_cache = cache_scores(_split_dot(q_g, residue_pat, terms=1))
            s_new = jnp.where(real_step, _head_sum(q_g * kn_t), 0.0)
            m_cache = place([jnp.max(jnp.where(res == t, s_cache, neg), axis=1, keepdims=True) for t in range(dec_seq)])
            m = jnp.maximum(m_cache, s_new)

            def by_position(stat):
                out = jnp.zeros((HEADS_PER_GROUP, win), F32)
                for t in range(dec_seq):
                    out = jnp.where(res == t, column(stat, t, win), out)
                return out

            e_cache = jnp.where(res < dec_seq, jnp.exp(s_cache - by_position(m)), 0.0)
            e_new = jnp.exp(s_new - m)
            denom = place([jnp.sum(jnp.where(res == t, e_cache, 0.0), axis=1, keepdims=True)
                           for t in range(dec_seq)]) + e_new
            inv = 1.0 / denom
            acc = weighted_values(e_cache * by_position(inv))
            l_i = lax.broadcasted_iota(jnp.int32, (LANES, LANES), 0)
            t_i = lax.broadcasted_iota(jnp.int32, (LANES, LANES), 1)
            sel = ((l_i % dil == t_i) & (t_i < dec_seq)).astype(BF16)
            o_g = _split_dot(acc, sel)[:, :nq] + _head_expand(e_new * inv) * vn_t
        outs.append(o_g)
        lses.append(m + jnp.log(denom))

    top = jnp.maximum(jnp.maximum(lses[0], lses[1]), lses[2])
    ws = [jnp.exp(l - top) for l in lses]
    total = ws[0] + ws[1] + ws[2]
    merged = jnp.zeros((GROUP_WIDTH, nq), F32)
    for g in range(N_ATTN_GROUPS):
        merged = merged + _head_expand(ws[g] / total) * outs[g]
    o_ref[...] = jnp.where(step_wide < dec_seq, merged, 0.0).T


def _decode_attention(q, caches, new_kv, dec_batch, dec_seq):
    nq = SUBLANES

    def pad_steps(a):
        a = a.astype(F32).reshape(dec_batch, dec_seq, a.shape[-1])
        return jnp.pad(a, ((0, 0), (0, nq - dec_seq), (0, 0)))

    args = [pad_steps(q)] + [pad_steps(n) for n in new_kv]
    in_specs = [pl.BlockSpec((None, nq, ATTN_WIDTH), lambda b: (b, 0, 0))]
    in_specs += [pl.BlockSpec((None, nq, 2 * GROUP_WIDTH), lambda b: (b, 0, 0)) for _ in new_kv]
    for (win, dil), cache in zip(ATTN_GROUPS, caches):
        args.append(jnp.transpose(cache, (0, 2, 3, 4, 1)).reshape(dec_batch, 2, GROUP_WIDTH, win))
        in_specs.append(pl.BlockSpec((None, 2, GROUP_WIDTH, win), lambda b: (b, 0, 0, 0)))
    o_t = pl.pallas_call(
        functools.partial(_decode_attn_kernel, dec_seq=dec_seq),
        grid=(dec_batch,),
        in_specs=in_specs,
        out_specs=pl.BlockSpec((None, nq, GROUP_WIDTH), lambda b: (b, 0, 0)),
        out_shape=jax.ShapeDtypeStruct((dec_batch, nq, GROUP_WIDTH), F32),
        compiler_params=_params("parallel"),
        name="decode_attention",
    )(*args)
    return o_t[:, :dec_seq, :].reshape(dec_batch * dec_seq, GROUP_WIDTH)


def _ssm_kernel(u_ref, h0r_ref, h0i_ref, ar_ref, ai_ref, bw_ref, cr_ref, ci_ref, d_ref,
                y_ref, hr_out, hi_out, bur, bui, hr_s, hi_s, *, nb, steps):
    chunk = pl.program_id(0)
    half_in = SSM_WIDTH // 2
    half_st = SSM_COLS // 2

    @pl.when(chunk == 0)
    def _():
        hr_s[...] = h0r_ref[...]
        hi_s[...] = h0i_ref[...]

    u = u_ref[...]
    ub = u.astype(BF16)
    for hf in range(2):
        r = jnp.dot(ub[:, hf * half_in:(hf + 1) * half_in], bw_ref[hf], preferred_element_type=F32)
        bur[:, hf * half_st:(hf + 1) * half_st] = r[:, :half_st]
        bui[:, hf * half_st:(hf + 1) * half_st] = r[:, half_st:]

    lane_chunk = 4 * LANES

    def sub_batch(s, carry):
        for lc in range(SSM_COLS // lane_chunk):
            ls = slice(lc * lane_chunk, (lc + 1) * lane_chunk)
            ar = ar_ref[:, ls]
            ai = ai_ref[:, ls]
            row0 = pl.multiple_of(s * SUBLANES, SUBLANES)

            def step(t, h):
                hr, hi = h
                row = pl.multiple_of(t * nb + s * SUBLANES, SUBLANES)
                nhr = ar * hr - ai * hi + bur[pl.ds(row, SUBLANES), ls]
                nhi = ar * hi + ai * hr + bui[pl.ds(row, SUBLANES), ls]
                bur[pl.ds(row, SUBLANES), ls] = nhr
                bui[pl.ds(row, SUBLANES), ls] = nhi
                return nhr, nhi

            hr, hi = lax.fori_loop(0, steps, step, (hr_s[pl.ds(row0, SUBLANES), ls], hi_s[pl.ds(row0, SUBLANES), ls]))
            hr_s[pl.ds(row0, SUBLANES), ls] = hr
            hi_s[pl.ds(row0, SUBLANES), ls] = hi
        return carry

    lax.fori_loop(0, nb // SUBLANES, sub_batch, 0)

    for hf in range(2):
        ss = slice(hf * half_st, (hf + 1) * half_st)
        y = jnp.dot(bur[:, ss].astype(BF16), cr_ref[hf], preferred_element_type=F32)
        y = y + jnp.dot(bui[:, ss].astype(BF16), ci_ref[hf], preferred_element_type=F32)
        cs = slice(hf * half_in, (hf + 1) * half_in)
        y_ref[:, cs] = y + d_ref[:, cs] * u[:, cs]

    @pl.when(chunk == pl.num_programs(0) - 1)
    def _():
        hr_out[...] = hr_s[...]
        hi_out[...] = hi_s[...]


def _ssm_weights(ssm_log_dt, a_re, a_im, b_re, b_im, c_re, c_im):
    dt = jnp.exp(ssm_log_dt.astype(F32))[:, None]
    lam = lax.complex(a_re.astype(F32), a_im.astype(F32))
    lam_bar = jnp.exp(lam * dt)
    b_bar = ((lam_bar - 1.0) / lam)[:, :, None] * lax.complex(b_re.astype(F32), b_im.astype(F32))
    gh = N_SSM_GROUPS // 2
    eye = jnp.eye(gh, dtype=F32)

    def block_diag(m):
        return jnp.einsum('gab,gh->gahb', m, eye).reshape(gh * m.shape[1], gh * m.shape[2])

    bw, cr, ci = [], [], []
    for hf in range(2):
        gs = slice(hf * gh, (hf + 1) * gh)
        b_t = jnp.transpose(b_bar[gs], (0, 2, 1))
        bw.append(jnp.concatenate([block_diag(jnp.real(b_t)), block_diag(jnp.imag(b_t))], axis=1))
        cr.append(block_diag(jnp.transpose(c_re[gs].astype(F32), (0, 2, 1))))
        ci.append(block_diag(jnp.transpose(-c_im[gs].astype(F32), (0, 2, 1))))
    ar = jnp.broadcast_to(jnp.real(lam_bar).reshape(1, SSM_COLS), (SUBLANES, SSM_COLS))
    ai = jnp.broadcast_to(jnp.imag(lam_bar).reshape(1, SSM_COLS), (SUBLANES, SSM_COLS))
    return ar, ai, jnp.stack(bw).astype(BF16), jnp.stack(cr).astype(BF16), jnp.stack(ci).astype(BF16)


def _ssm(u_tm, h0r, h0i, weights, d_flat, nb):
    ar, ai, bw, cr, ci = weights
    rows = u_tm.shape[0]
    steps = SSM_ROWS // nb
    const2 = lambda c: (0, 0)
    const3 = lambda c: (0, 0, 0)
    state_sds = jax.ShapeDtypeStruct((nb, SSM_COLS), F32)
    return pl.pallas_call(
        functools.partial(_ssm_kernel, nb=nb, steps=steps),
        grid=(rows // SSM_ROWS,),
        in_specs=[
            pl.BlockSpec((SSM_ROWS, SSM_WIDTH), lambda c: (c, 0)),
            pl.BlockSpec((nb, SSM_COLS), const2),
            pl.BlockSpec((nb, SSM_COLS), const2),
            pl.BlockSpec((SUBLANES, SSM_COLS), const2),
            pl.BlockSpec((SUBLANES, SSM_COLS), const2),
            pl.BlockSpec(bw.shape, const3),
            pl.BlockSpec(cr.shape, const3),
            pl.BlockSpec(ci.shape, const3),
            pl.BlockSpec((1, SSM_WIDTH), const2),
        ],
        out_specs=(
            pl.BlockSpec((SSM_ROWS, SSM_WIDTH), lambda c: (c, 0)),
            pl.BlockSpec((nb, SSM_COLS), const2),
            pl.BlockSpec((nb, SSM_COLS), const2),
        ),
        out_shape=(jax.ShapeDtypeStruct((rows, SSM_WIDTH), F32), state_sds, state_sds),
        scratch_shapes=[
            pltpu.VMEM((SSM_ROWS, SSM_COLS), F32),
            pltpu.VMEM((SSM_ROWS, SSM_COLS), F32),
            pltpu.VMEM((nb, SSM_COLS), F32),
            pltpu.VMEM((nb, SSM_COLS), F32),
        ],
        compiler_params=_params("arbitrary"),
        name="s5_scan",
    )(u_tm, h0r, h0i, ar, ai, bw, cr, ci, d_flat)


def _mix_kernel(*refs, merged_attn):
    if merged_attn:
        attn_ref = refs[0]
        rest = refs[1:13]
    else:
        group_refs = refs[0:2 * N_ATTN_GROUPS]
        rest = refs[2 * N_ATTN_GROUPS:2 * N_ATTN_GROUPS + 12]
        scratch = refs[2 * N_ATTN_GROUPS + 12:]
    (y_ref, gate_ref, x_ref, wa_ref, wglu_ref, wout_ref, gffn_ref, wr_ref, br_ref,
     h_ref, hn_ref, route_ref) = rest
    tm = x_ref.shape[0]
    if merged_attn:
        attn = attn_ref[...]
    else:
        natural = []
        for idx, ref in enumerate(group_refs):
            dil = ATTN_GROUPS[idx % N_ATTN_GROUPS][1]
            if dil == 1:
                natural.append(ref[...])
                continue
            scr = scratch[idx]
            n_chunks = GROUP_WIDTH // LANES
            for r in range(dil):
                for j in range(n_chunks):
                    col = r * GROUP_WIDTH + j * LANES
                    scr[j, pl.ds(r, tm // dil, stride=dil), :] = ref[:, col:col + LANES]
            natural.append(jnp.concatenate([scr[j] for j in range(n_chunks)], axis=1))
        os, ls = natural[:N_ATTN_GROUPS], natural[N_ATTN_GROUPS:]
        top = jnp.maximum(jnp.maximum(ls[0], ls[1]), ls[2])
        es = [jnp.exp(l - top) for l in ls]
        attn = (es[0] * os[0] + es[1] * os[1] + es[2] * os[2]) / (es[0] + es[1] + es[2])
    attn_out = jnp.dot(attn.astype(BF16), wa_ref[...], preferred_element_type=F32)
    glu = jnp.dot(jax.nn.gelu(y_ref[...]).astype(BF16), wglu_ref[...], preferred_element_type=F32)
    ssm_out = glu[:, :D_MODEL] * jax.nn.sigmoid(glu[:, D_MODEL:])
    merged = jax.nn.sigmoid(gate_ref[:, :D_MODEL]) * attn_out + jax.nn.sigmoid(gate_ref[:, D_MODEL:]) * ssm_out
    h = x_ref[...] + jnp.dot(merged.astype(BF16), wout_ref[...], preferred_element_type=F32)
    h_ref[...] = h
    var = jnp.mean(h * h, axis=-1, keepdims=True)
    hn = h * lax.rsqrt(var + NORM_EPS) * gffn_ref[...]
    hn_ref[...] = hn
    hn_hi = hn.astype(BF16)
    hn_lo = (hn - hn_hi.astype(F32)).astype(BF16)
    w_hi = wr_ref[:, :LANES]
    w_lo = wr_ref[:, LANES:]
    logits = (jnp.dot(hn_hi, w_hi, preferred_element_type=F32) + jnp.dot(hn_hi, w_lo, preferred_element_type=F32)
              + jnp.dot(hn_lo, w_hi, preferred_element_type=F32)) + br_ref[...]
    lane = lax.broadcasted_iota(jnp.int32, logits.shape, 1).astype(F32)
    far = float(LANES)

    def first_argmax(vals):
        top_v = jnp.max(vals, axis=1, keepdims=True)
        return top_v, jnp.min(jnp.where(vals == top_v, lane, far), axis=1, keepdims=True)

    group_logits = jnp.where(lane < N_EXPERT_GROUPS, logits, -jnp.inf)
    g_top, g_idx = first_argmax(group_logits)
    p_group = 1.0 / jnp.sum(jnp.exp(group_logits - g_top), axis=1, keepdims=True)
    first_lane = N_EXPERT_GROUPS + g_idx * EXPERTS_PER_GROUP
    in_group = (lane >= first_lane) & (lane < first_lane + EXPERTS_PER_GROUP)
    expert_logits = jnp.where(in_group, logits, -jnp.inf)
    v1, i1 = first_argmax(expert_logits)
    v2, i2 = first_argmax(jnp.where(lane == i1, -jnp.inf, expert_logits))
    e2 = jnp.exp(v2 - v1)
    w1 = p_group / (1.0 + e2)
    w2 = p_group * e2 / (1.0 + e2)
    route = jnp.where(lane == 0, i1 - N_EXPERT_GROUPS,
                      jnp.where(lane == 1, i2 - N_EXPERT_GROUPS,
                                jnp.where(lane == 2, w1, jnp.where(lane == 3, w2, 0.0))))
    route_ref[...] = route


def _mix(attn_inputs, y, gates, x2d, wa, wglu, wout, gffn, wr, br):
    n = x2d.shape[0]
    tm = PROJ_TILE
    row = lambda i: (i, 0)
    const = lambda i: (0, 0)
    merged_attn = len(attn_inputs) == 1
    in_specs = [pl.BlockSpec((tm * GROUP_WIDTH // a.shape[1], a.shape[1]), row) for a in attn_inputs]
    scratch = [] if merged_attn else [pltpu.VMEM((GROUP_WIDTH // LANES, tm, LANES), F32) for _ in attn_inputs]
    in_specs += [
        pl.BlockSpec((tm, SSM_WIDTH), row),
        pl.BlockSpec((tm, 2 * D_MODEL), row),
        pl.BlockSpec((tm, D_MODEL), row),
        pl.BlockSpec(wa.shape, const),
        pl.BlockSpec(wglu.shape, const),
        pl.BlockSpec(wout.shape, const),
        pl.BlockSpec((1, D_MODEL), const),
        pl.BlockSpec(wr.shape, const),
        pl.BlockSpec((1, LANES), const),
    ]
    return pl.pallas_call(
        functools.partial(_mix_kernel, merged_attn=merged_attn),
        grid=(n // tm,),
        in_specs=in_specs,
        out_specs=(pl.BlockSpec((tm, D_MODEL), row), pl.BlockSpec((tm, D_MODEL), row), pl.BlockSpec((tm, LANES), row)),
        out_shape=(jax.ShapeDtypeStruct((n, D_MODEL), F32), jax.ShapeDtypeStruct((n, D_MODEL), F32),
                   jax.ShapeDtypeStruct((n, LANES), F32)),
        scratch_shapes=scratch,
        compiler_params=_params("parallel"),
        name="branch_mix",
    )(*attn_inputs, y, gates, x2d, wa, wglu, wout, gffn, wr, br)


def _row_copy(src, src_row, dst, dst_row, sem):
    return pltpu.make_async_copy(src.at[pl.ds(src_row, 1)], dst.at[pl.ds(dst_row, 1)], sem)


def _dispatch_kernel(pad_end_ref, padded_ref, n_used_ref, slot_ref, hn_ref, xs_hbm, zero_buf, sem, zero_sem):
    n_blocks = xs_hbm.shape[0] // MOE_BLOCK

    @pl.when(pl.program_id(0) == 0)
    def _():
        zero_buf[...] = jnp.zeros_like(zero_buf)

        def block_copy(b):
            start = pl.multiple_of(b * MOE_BLOCK, MOE_BLOCK)
            return pltpu.make_async_copy(zero_buf, xs_hbm.at[pl.ds(start, MOE_BLOCK)], zero_sem)

        def segment_tails(action):
            def body(e, carry):
                @pl.when(padded_ref[e] > 0)
                def _():
                    action(block_copy(pad_end_ref[e] // MOE_BLOCK - 1))
                return carry
            lax.fori_loop(0, N_EXPERTS, body, 0)

        def unused_blocks(action):
            def body(b, carry):
                @pl.when(b >= n_used_ref[0])
                def _():
                    action(block_copy(b))
                return carry
            lax.fori_loop(0, n_blocks, body, 0)

        segment_tails(lambda cp: cp.start())
        unused_blocks(lambda cp: cp.start())
        segment_tails(lambda cp: cp.wait())
        unused_blocks(lambda cp: cp.wait())

    def issue(j, carry):
        _row_copy(hn_ref, j, xs_hbm, slot_ref[2 * j], sem).start(priority=0)
        _row_copy(hn_ref, j, xs_hbm, slot_ref[2 * j + 1], sem).start(priority=1)
        return carry

    def drain(j, carry):
        _row_copy(hn_ref, 0, xs_hbm, 0, sem).wait()
        _row_copy(hn_ref, 0, xs_hbm, 0, sem).wait()
        return carry

    lax.fori_loop(0, ROW_TILE, issue, 0, unroll=DMA_UNROLL)
    lax.fori_loop(0, ROW_TILE, drain, 0, unroll=DMA_UNROLL)


def _dispatch(pad_end, padded, n_used, slots, hn, n_slots):
    n = hn.shape[0]
    return pl.pallas_call(
        _dispatch_kernel,
        grid_spec=pltpu.PrefetchScalarGridSpec(
            num_scalar_prefetch=3,
            grid=(n // ROW_TILE,),
            in_specs=[
                pl.BlockSpec((2 * ROW_TILE,), lambda i, *_: (i,), memory_space=pltpu.SMEM),
                pl.BlockSpec((ROW_TILE, D_MODEL), lambda i, *_: (i, 0)),
            ],
            out_specs=pl.BlockSpec(memory_space=pl.ANY),
            scratch_shapes=[pltpu.VMEM((MOE_BLOCK, D_MODEL), F32), pltpu.SemaphoreType.DMA(()),
                            pltpu.SemaphoreType.DMA(())],
        ),
        out_shape=jax.ShapeDtypeStruct((n_slots, D_MODEL), F32),
        compiler_params=_params("arbitrary"),
        name="moe_dispatch",
    )(pad_end, padded, n_used, slots, hn)


def _expert_kernel(block_e_ref, n_used_ref, xs_ref, wg_ref, wu_ref, wd_ref, yb_ref):
    del block_e_ref
    i = pl.program_id(0)

    @pl.when(i < n_used_ref[0])
    def _():
        xb = xs_ref[...].astype(BF16)
        gate = jnp.dot(xb, wg_ref[...].astype(BF16), preferred_element_type=F32)
        up = jnp.dot(xb, wu_ref[...].astype(BF16), preferred_element_type=F32)
        hmid = (jax.nn.silu(gate) * up).astype(BF16)
        yb_ref[...] = jnp.dot(hmid, wd_ref[...].astype(BF16), preferred_element_type=F32)

    @pl.when(i >= n_used_ref[0])
    def _():
        yb_ref[...] = jnp.zeros_like(yb_ref)


def _experts(block_e, n_used, xs, wg, wu, wd):
    n_blocks = xs.shape[0] // MOE_BLOCK
    grid_spec = pltpu.PrefetchScalarGridSpec(
        num_scalar_prefetch=2,
        grid=(n_blocks,),
        in_specs=[
            pl.BlockSpec((MOE_BLOCK, D_MODEL), lambda i, be, nu: (jnp.minimum(i, nu[0] - 1), 0)),
            pl.BlockSpec((None, D_MODEL, D_EXPERT), lambda i, be, nu: (be[i], 0, 0)),
            pl.BlockSpec((None, D_MODEL, D_EXPERT), lambda i, be, nu: (be[i], 0, 0)),
            pl.BlockSpec((None, D_EXPERT, D_MODEL), lambda i, be, nu: (be[i], 0, 0)),
        ],
        out_specs=pl.BlockSpec((MOE_BLOCK, D_MODEL), lambda i, be, nu: (i, 0)),
    )
    return pl.pallas_call(
        _expert_kernel,
        grid_spec=grid_spec,
        out_shape=jax.ShapeDtypeStruct(xs.shape, F32),
        compiler_params=_params("arbitrary"),
        name="moe_experts",
    )(block_e, n_used, xs, wg, wu, wd)


def _combine_kernel(slot_ref, h_ref, route_ref, g_ref, yb_hbm, out_ref, buf_a, buf_b, sem):
    def issue(j, carry):
        _row_copy(yb_hbm, slot_ref[2 * j], buf_a, j, sem).start(priority=0)
        _row_copy(yb_hbm, slot_ref[2 * j + 1], buf_b, j, sem).start(priority=1)
        return carry

    def drain(j, carry):
        _row_copy(yb_hbm, 0, buf_a, 0, sem).wait()
        _row_copy(yb_hbm, 0, buf_b, 0, sem).wait()
        return carry

    lax.fori_loop(0, ROW_TILE, issue, 0, unroll=DMA_UNROLL)
    lax.fori_loop(0, ROW_TILE, drain, 0, unroll=DMA_UNROLL)
    route = route_ref[...]
    h = h_ref[...] + (route[:, 2:3] * buf_a[...] + route[:, 3:4] * buf_b[...])
    var = jnp.mean(h * h, axis=-1, keepdims=True)
    out_ref[...] = h * lax.rsqrt(var + NORM_EPS) * g_ref[...]


def _combine(slots, h, route, g_final, yb):
    n = h.shape[0]
    row = lambda i: (i, 0)
    return pl.pallas_call(
        _combine_kernel,
        grid=(n // ROW_TILE,),
        in_specs=[
            pl.BlockSpec((2 * ROW_TILE,), lambda i: (i,), memory_space=pltpu.SMEM),
            pl.BlockSpec((ROW_TILE, D_MODEL), row),
            pl.BlockSpec((ROW_TILE, LANES), row),
            pl.BlockSpec((1, D_MODEL), lambda i: (0, 0)),
            pl.BlockSpec(memory_space=pl.ANY),
        ],
        out_specs=pl.BlockSpec((ROW_TILE, D_MODEL), row),
        out_shape=jax.ShapeDtypeStruct((n, D_MODEL), F32),
        scratch_shapes=[
            pltpu.VMEM((ROW_TILE, D_MODEL), F32),
            pltpu.VMEM((ROW_TILE, D_MODEL), F32),
            pltpu.SemaphoreType.DMA(()),
        ],
        compiler_params=_params("arbitrary"),
        name="moe_combine",
    )(slots, h, route, g_final.reshape(1, D_MODEL), yb)


def _slot_assignment(route, n_blocks):
    flat_e = route[:, 0:2].astype(jnp.int32).reshape(-1)
    onehot = (flat_e[:, None] == jnp.arange(N_EXPERTS, dtype=jnp.int32)[None, :]).astype(jnp.int32)
    running = jnp.cumsum(onehot, axis=0)
    rank = jnp.sum(onehot * running, axis=1) - 1
    counts = running[-1]
    padded = ((counts + MOE_BLOCK - 1) // MOE_BLOCK) * MOE_BLOCK
    pad_end = jnp.cumsum(padded)
    pad_start = pad_end - padded
    slots = jnp.sum(onehot * pad_start[None, :], axis=1) + rank
    block_start = jnp.arange(n_blocks, dtype=jnp.int32) * MOE_BLOCK
    block_e = jnp.minimum(jnp.sum((pad_end[None, :] <= block_start[:, None]).astype(jnp.int32), axis=1), N_EXPERTS - 1)
    n_used = (pad_end[-1:] // MOE_BLOCK).astype(jnp.int32)
    return slots.astype(jnp.int32), block_e, n_used, pad_end.astype(jnp.int32), padded.astype(jnp.int32)


def _moe_and_final_norm(h, hn, route, wg, wu, wd, g_final):
    n = h.shape[0]
    n_blocks = (2 * n) // MOE_BLOCK + N_EXPERTS
    slots, block_e, n_used, pad_end, padded = _slot_assignment(route, n_blocks)
    xs = _dispatch(pad_end, padded, n_used, slots, hn, n_blocks * MOE_BLOCK)
    yb = _experts(block_e, n_used, xs, wg, wu, wd)
    return _combine(slots, h, route, g_final, yb)


def kernel(x_prompt, x_sample, cache_kv_w128, cache_kv_w512, cache_kv_w2048, state_ssm, g_attn_norm, w_in, ssm_log_dt, ssm_a_re, ssm_a_im, ssm_b_re, ssm_b_im, ssm_c_re, ssm_c_im, ssm_d, w_glu, w_attn_branch, w_out, g_ffn_norm, w_router_group, b_router_group, w_router_expert, b_router_expert, w_exp_gate, w_exp_up, w_exp_down, g_final):
    batch, seq, _ = x_prompt.shape
    dec_batch, dec_seq, _ = x_sample.shape
    past_len = cache_kv_w2048.shape[2]
    layer = 0

    w_in_b = w_in[layer].astype(BF16)
    wa = w_attn_branch[layer].astype(BF16)
    wglu = w_glu[layer].astype(BF16)
    wout = w_out[layer].astype(BF16)
    wg, wu, wd = w_exp_gate[layer], w_exp_up[layer], w_exp_down[layer]
    gffn = g_ffn_norm[layer].reshape(1, D_MODEL)
    pad = LANES - N_EXPERT_GROUPS - N_EXPERTS
    wr_f32 = jnp.concatenate([w_router_group[layer], w_router_expert[layer], jnp.zeros((D_MODEL, pad), F32)], axis=1)
    wr_hi = wr_f32.astype(BF16)
    wr = jnp.concatenate([wr_hi, (wr_f32 - wr_hi.astype(F32)).astype(BF16)], axis=1)
    br = jnp.concatenate([b_router_group[layer], b_router_expert[layer], jnp.zeros((pad,), F32)]).reshape(1, LANES)
    ssm_w = _ssm_weights(ssm_log_dt[layer], ssm_a_re[layer], ssm_a_im[layer], ssm_b_re[layer], ssm_b_im[layer],
                         ssm_c_re[layer], ssm_c_im[layer])
    d_flat = ssm_d[layer].astype(F32).reshape(1, SSM_WIDTH)

    def time_major(a, nb, steps):
        return jnp.transpose(a.reshape(nb, steps, -1), (1, 0, 2)).reshape(nb * steps, -1)

    def batch_major(a, nb, steps):
        return jnp.transpose(a.reshape(steps, nb, -1), (1, 0, 2)).reshape(nb * steps, -1)

    xp = x_prompt.reshape(batch * seq, D_MODEL)
    q_p, kv_p, kv_t_p, u_p, gates_p = _in_projection_prompt(xp, g_attn_norm[layer], w_in_b, batch, seq)
    attn_in = []
    lse_in = []
    for g, (win, dil) in enumerate(ATTN_GROUPS):
        o, lse = _prompt_attention(q_p[g], kv_p[g], g, dil, batch, seq)
        attn_in.append(o)
        lse_in.append(lse)
    zeros_state = jnp.zeros((batch, SSM_COLS), F32)
    y_tm, hr_p, hi_p = _ssm(time_major(u_p, batch, seq), zeros_state, zeros_state, ssm_w, d_flat, batch)
    y_p = batch_major(y_tm, batch, seq)
    h_p, hn_p, route_p = _mix(attn_in + lse_in, y_p, gates_p, xp, wa, wglu, wout, gffn, wr, br)
    out_p = _moe_and_final_norm(h_p, hn_p, route_p, wg, wu, wd, g_final)

    xs = x_sample.reshape(dec_batch * dec_seq, D_MODEL)
    pos_s = past_len + (jnp.arange(dec_batch * dec_seq, dtype=jnp.int32) % dec_seq)
    q_s, kv0_s, kv1_s, kv2_s, u_s, gates_s = _in_projection_decode(xs, g_attn_norm[layer], w_in_b, pos_s)
    kv_s = (kv0_s, kv1_s, kv2_s)
    caches = (cache_kv_w128[layer], cache_kv_w512[layer], cache_kv_w2048[layer])
    attn_s = _decode_attention(q_s, caches, kv_s, dec_batch, dec_seq)
    st = state_ssm[layer].astype(F32).reshape(dec_batch, SSM_COLS, 2)
    ys_tm, hr_s, hi_s = _ssm(time_major(u_s, dec_batch, dec_seq), st[:, :, 0], st[:, :, 1], ssm_w, d_flat, dec_batch)
    y_s = batch_major(ys_tm, dec_batch, dec_seq)
    h_s, hn_s, route_s = _mix([attn_s], y_s, gates_s, xs, wa, wglu, wout, gffn, wr, br)
    out_s = _moe_and_final_norm(h_s, hn_s, route_s, wg, wu, wd, g_final)

    kv_tail = (2, HEADS_PER_GROUP, HEAD_DIM)
    outs = [out_p.reshape(batch, seq, D_MODEL), out_s.reshape(dec_batch, dec_seq, D_MODEL)]
    for g, (win, dil) in enumerate(ATTN_GROUPS):
        keep = min(win, seq)
        rows_p = jnp.transpose(kv_t_p[g].reshape(batch, 2, HEADS_PER_GROUP, HEAD_DIM, keep), (0, 4, 1, 2, 3))
        outs.append(rows_p[None])
        outs.append(kv_s[g].reshape((1, dec_batch, dec_seq) + kv_tail))
    outs.append(jnp.stack([hr_p, hi_p], axis=-1).reshape(1, batch, N_SSM_GROUPS, SSM_STATE, 2))
    outs.append(jnp.stack([hr_s, hi_s], axis=-1).reshape(1, dec_batch, N_SSM_GROUPS, SSM_STATE, 2))
    return tuple(outs)
```

```python
import functools
import math

import jax
import jax.numpy as jnp
from jax import lax
from jax.experimental import pallas as pl
from jax.experimental.pallas import tpu as pltpu

F32 = jnp.float32
BF16 = jnp.bfloat16

D_MODEL = 1024
HEAD_DIM = 64
HEADS_PER_GROUP = 8
GROUP_WIDTH = HEADS_PER_GROUP * HEAD_DIM
ATTN_GROUPS = ((128, 1), (512, 4), (2048, 16))
N_ATTN_GROUPS = len(ATTN_GROUPS)
ATTN_WIDTH = N_ATTN_GROUPS * GROUP_WIDTH
ROT_DIM = HEAD_DIM // 4
ROPE_THETA = 500000.0
WINDOW_KEYS = 128
SSM_GROUP_CH = 16
SSM_WIDTH = D_MODEL // 2
N_SSM_GROUPS = SSM_WIDTH // SSM_GROUP_CH
SSM_STATE = 64
SSM_COLS = N_SSM_GROUPS * SSM_STATE
IN_WIDTH = 3 * ATTN_WIDTH + SSM_WIDTH + 2 * D_MODEL
N_EXPERT_GROUPS = 4
EXPERTS_PER_GROUP = 8
N_EXPERTS = N_EXPERT_GROUPS * EXPERTS_PER_GROUP
D_EXPERT = D_MODEL // 4
NORM_EPS = 1e-6

LANES = 128
SUBLANES = 8
TOKEN_TILE_ROWS = D_MODEL // LANES
VMEM_LIMIT = 56 * 1024 * 1024

PROJ_TILE = 256
MATMUL_LOOKAHEAD = 1
ATTN_TILE = 128
ATTN_RESIDUES_PER_STEP = 4
ATTN_TILES_PER_STEP = 8
SSM_ROWS = 512
MOE_BLOCK = 256
ROW_TILE = 512
DMA_UNROLL = 8


def _params(*sem):
    return pltpu.CompilerParams(dimension_semantics=sem, vmem_limit_bytes=VMEM_LIMIT)


def _normed_input(x_ref, g_ref):
    x = x_ref[...]
    var = jnp.mean(x * x, axis=-1, keepdims=True)
    return (x * lax.rsqrt(var + NORM_EPS) * g_ref[...]).astype(BF16)


def _rope_fn(cos_ref, sin_ref):
    cos = cos_ref[...]
    sin = sin_ref[...]
    lane = lax.broadcasted_iota(jnp.int32, cos.shape, 1) % HEAD_DIM
    first_half = lane < ROT_DIM // 2
    rotated = lane < ROT_DIM

    def rope_chunk(c):
        partner = jnp.where(first_half, pltpu.roll(c, LANES - ROT_DIM // 2, 1), pltpu.roll(c, ROT_DIM // 2, 1))
        return jnp.where(rotated, c * cos + partner * sin, c)

    def rope(t):
        return jnp.concatenate([rope_chunk(t[:, j * LANES:(j + 1) * LANES]) for j in range(GROUP_WIDTH // LANES)], axis=1)

    return rope


def _projection_tiles(xn, w_ref):
    for c in range(IN_WIDTH // GROUP_WIDTH):
        yield c, jnp.dot(xn, w_ref[:, c * GROUP_WIDTH:(c + 1) * GROUP_WIDTH], preferred_element_type=F32)


def _inproj_decode_kernel(x_ref, g_ref, w_ref, cos_ref, sin_ref, q_ref, kv0_ref, kv1_ref, kv2_ref, u_ref, gate_ref):
    rope = _rope_fn(cos_ref, sin_ref)
    kv_refs = (kv0_ref, kv1_ref, kv2_ref)
    for c, acc in _projection_tiles(_normed_input(x_ref, g_ref), w_ref):
        if c < 3:
            q_ref[:, c * GROUP_WIDTH:(c + 1) * GROUP_WIDTH] = (rope(acc) * (HEAD_DIM ** -0.5)).astype(BF16)
        elif c < 6:
            kv_refs[c - 3][:, :GROUP_WIDTH] = rope(acc)
        elif c < 9:
            kv_refs[c - 6][:, GROUP_WIDTH:] = acc
        elif c == 9:
            u_ref[...] = acc
        else:
            gate_ref[:, (c - 10) * GROUP_WIDTH:(c - 9) * GROUP_WIDTH] = acc


def _transposed_projection(wt, xn, cos_t, sin_t):
    t = lax.dot_general(wt, xn, (((1,), (1,)), ((), ())), preferred_element_type=F32)
    if cos_t is None:
        return t
    half_rot = ROT_DIM // 2
    pieces = []
    for h in range(HEADS_PER_GROUP):
        x1 = t[h * HEAD_DIM:h * HEAD_DIM + half_rot]
        x2 = t[h * HEAD_DIM + half_rot:h * HEAD_DIM + ROT_DIM]
        pieces += [x1 * cos_t - x2 * sin_t, x2 * cos_t + x1 * sin_t, t[h * HEAD_DIM + ROT_DIM:(h + 1) * HEAD_DIM]]
    return jnp.concatenate(pieces, axis=0)


def _kv_tail_kernel(x_ref, g_ref, wk_ref, wv_ref, cos_t_ref, sin_t_ref, kt_ref):
    xn = _normed_input(x_ref, g_ref)
    kt_ref[0] = _transposed_projection(wk_ref[...], xn, cos_t_ref[...], sin_t_ref[...])
    kt_ref[1] = _transposed_projection(wv_ref[...], xn, None, None)


def _inproj_prompt_kernel(x_ref, g_ref, w_ref, cos_ref, sin_ref, wt_ref, cos_t_ref, sin_t_ref, *refs, full_groups):
    q_refs, kv_refs = refs[0:N_ATTN_GROUPS], refs[N_ATTN_GROUPS:2 * N_ATTN_GROUPS]
    kt_refs = refs[2 * N_ATTN_GROUPS:2 * N_ATTN_GROUPS + len(full_groups)]
    u_ref, gate_ref = refs[2 * N_ATTN_GROUPS + len(full_groups):2 * N_ATTN_GROUPS + len(full_groups) + 2]
    scratch = refs[2 * N_ATTN_GROUPS + len(full_groups) + 2:]
    rope = _rope_fn(cos_ref, sin_ref)
    tm = x_ref.shape[0]

    dilated = [g for g, (_, d) in enumerate(ATTN_GROUPS) if d > 1]

    def store_rows(dst_ref, val, kind, g, col0, col_stride):
        dil = ATTN_GROUPS[g][1]
        if dil == 1:
            dst_ref[:, col0:col0 + GROUP_WIDTH] = val.astype(BF16)
            return
        scr = scratch[kind * len(dilated) + dilated.index(g)]
        n_chunks = GROUP_WIDTH // LANES
        for j in range(n_chunks):
            scr[j] = val[:, j * LANES:(j + 1) * LANES]
        rows = tm // dil
        for r in range(dil):
            piece = jnp.concatenate([scr[j, pl.ds(r, rows, stride=dil), :] for j in range(n_chunks)], axis=1)
            dst_ref[:, col0 + r * col_stride:col0 + r * col_stride + GROUP_WIDTH] = piece.astype(BF16)

    xn = _normed_input(x_ref, g_ref)

    def emit(c, acc):
        g = c % N_ATTN_GROUPS
        if c < 3:
            store_rows(q_refs[g], rope(acc) * (HEAD_DIM ** -0.5), 0, g, 0, GROUP_WIDTH)
        elif c < 6:
            store_rows(kv_refs[g], rope(acc), 1, g, 0, 2 * GROUP_WIDTH)
            if g in full_groups:
                kt_refs[full_groups.index(g)][0] = _transposed_projection(
                    wt_ref[full_groups.index(g)], xn, cos_t_ref[...], sin_t_ref[...])
        elif c < 9:
            store_rows(kv_refs[g], acc, 2, g, GROUP_WIDTH, 2 * GROUP_WIDTH)
            if g in full_groups:
                kt_refs[full_groups.index(g)][1] = _transposed_projection(
                    wt_ref[len(full_groups) + full_groups.index(g)], xn, None, None)
        elif c == 9:
            u_ref[...] = acc
        else:
            gate_ref[:, (c - 10) * GROUP_WIDTH:(c - 9) * GROUP_WIDTH] = acc

    pending = []
    for item in _projection_tiles(xn, w_ref):
        pending.append(item)
        if len(pending) > MATMUL_LOOKAHEAD:
            emit(*pending.pop(0))
    for item in pending:
        emit(*item)


def _rope_tables(pos):
    half = ROT_DIM // 2
    inv_freq = ROPE_THETA ** (-(jnp.arange(half, dtype=F32) / half))
    ang = pos.astype(F32)[:, None] * inv_freq[None, :]
    cos, sin = jnp.cos(ang), jnp.sin(ang)
    n = pos.shape[0]
    rest = HEAD_DIM - ROT_DIM
    cos_h = jnp.concatenate([cos, cos, jnp.ones((n, rest), F32)], axis=1)
    sin_h = jnp.concatenate([-sin, sin, jnp.zeros((n, rest), F32)], axis=1)
    return jnp.tile(cos_h, (1, LANES // HEAD_DIM)), jnp.tile(sin_h, (1, LANES // HEAD_DIM))


def _inproj_in_specs(tm, n_pos_tiles):
    const = lambda i: (0, 0)
    tab = lambda i: (i % n_pos_tiles, 0)
    return [
        pl.BlockSpec((tm, D_MODEL), lambda i: (i, 0)),
        pl.BlockSpec((1, D_MODEL), const),
        pl.BlockSpec((D_MODEL, IN_WIDTH), const, pipeline_mode=pl.Buffered(1)),
        pl.BlockSpec((tm, LANES), tab),
        pl.BlockSpec((tm, LANES), tab),
    ]


def _in_projection_decode(x2d, g, w_bf16, pos):
    n = x2d.shape[0]
    tm = PROJ_TILE
    cos_t, sin_t = _rope_tables(pos)
    row = lambda i: (i, 0)
    widths = (ATTN_WIDTH, 2 * GROUP_WIDTH, 2 * GROUP_WIDTH, 2 * GROUP_WIDTH, SSM_WIDTH, 2 * D_MODEL)
    dtypes = (BF16, F32, F32, F32, F32, F32)
    return pl.pallas_call(
        _inproj_decode_kernel,
        grid=(n // tm,),
        in_specs=_inproj_in_specs(tm, pos.shape[0] // tm),
        out_specs=tuple(pl.BlockSpec((tm, w), row) for w in widths),
        out_shape=tuple(jax.ShapeDtypeStruct((n, w), d) for w, d in zip(widths, dtypes)),
        compiler_params=_params("parallel"),
        name="in_projection_decode",
    )(x2d, g.reshape(1, D_MODEL), w_bf16, cos_t, sin_t)


def _in_projection_prompt(x2d, g, w_bf16, batch, seq):
    n = batch * seq
    tm = PROJ_TILE
    tiles_per_seq = seq // tm
    pos = jnp.arange(seq, dtype=jnp.int32)
    cos_t, sin_t = _rope_tables(pos)
    half = ROT_DIM // 2
    ang_t = (ROPE_THETA ** (-(jnp.arange(half, dtype=F32) / half)))[:, None] * pos.astype(F32)[None, :]
    cos_tr, sin_tr = jnp.cos(ang_t), jnp.sin(ang_t)
    wk_t = jnp.transpose(w_bf16[:, ATTN_WIDTH:2 * ATTN_WIDTH]).reshape(N_ATTN_GROUPS, GROUP_WIDTH, D_MODEL)
    wv_t = jnp.transpose(w_bf16[:, 2 * ATTN_WIDTH:3 * ATTN_WIDTH]).reshape(N_ATTN_GROUPS, GROUP_WIDTH, D_MODEL)
    keeps = tuple(min(win, seq) for win, _ in ATTN_GROUPS)
    full_groups = tuple(i for i, keep in enumerate(keeps) if keep == seq)
    w_full_t = jnp.concatenate([wk_t[jnp.array(full_groups)], wv_t[jnp.array(full_groups)]], axis=0)
    g_row = g.reshape(1, D_MODEL)
    row = lambda i: (i, 0)
    in_specs = _inproj_in_specs(tm, tiles_per_seq) + [
        pl.BlockSpec(w_full_t.shape, lambda i: (0, 0, 0), pipeline_mode=pl.Buffered(1)),
        pl.BlockSpec((half, tm), lambda i: (0, i % tiles_per_seq)),
        pl.BlockSpec((half, tm), lambda i: (0, i % tiles_per_seq)),
    ]
    out_specs, out_shape = [], []
    for width in (GROUP_WIDTH, 2 * GROUP_WIDTH):
        for _, dil in ATTN_GROUPS:
            out_specs.append(pl.BlockSpec((tm // dil, dil * width), row))
            out_shape.append(jax.ShapeDtypeStruct((n // dil, dil * width), BF16))
    for _ in full_groups:
        out_specs.append(pl.BlockSpec((None, 2, GROUP_WIDTH, tm), lambda i: (i // tiles_per_seq, 0, 0, i % tiles_per_seq)))
        out_shape.append(jax.ShapeDtypeStruct((batch, 2, GROUP_WIDTH, seq), F32))
    out_specs += [pl.BlockSpec((tm, SSM_WIDTH), row), pl.BlockSpec((tm, 2 * D_MODEL), row)]
    out_shape += [jax.ShapeDtypeStruct((n, SSM_WIDTH), F32), jax.ShapeDtypeStruct((n, 2 * D_MODEL), F32)]
    outs = pl.pallas_call(
        functools.partial(_inproj_prompt_kernel, full_groups=full_groups),
        grid=(n // tm,),
        in_specs=in_specs,
        out_specs=tuple(out_specs),
        out_shape=tuple(out_shape),
        scratch_shapes=[pltpu.VMEM((GROUP_WIDTH // LANES, tm, LANES), F32)
                        for _ in range(3 * sum(1 for _, d in ATTN_GROUPS if d > 1))],
        compiler_params=_params("parallel"),
        name="in_projection_prompt",
    )(x2d, g_row, w_bf16, cos_t, sin_t, w_full_t, cos_tr, sin_tr)
    n_full = len(full_groups)
    kv_t = dict(zip(full_groups, outs[6:6 + n_full]))
    for grp, keep in enumerate(keeps):
        if grp in full_groups:
            continue
        blk = min(keep, tm)
        first = (seq - keep) // blk
        per_seq = seq // blk
        tail = lambda b, j, first=first: (0, first + j)
        kv_t[grp] = pl.pallas_call(
            _kv_tail_kernel,
            grid=(batch, keep // blk),
            in_specs=[
                pl.BlockSpec((blk, D_MODEL), lambda b, j, first=first, per_seq=per_seq: (b * per_seq + first + j, 0)),
                pl.BlockSpec((1, D_MODEL), lambda b, j: (0, 0)),
                pl.BlockSpec((None, GROUP_WIDTH, D_MODEL), lambda b, j, grp=grp: (grp, 0, 0)),
                pl.BlockSpec((None, GROUP_WIDTH, D_MODEL), lambda b, j, grp=grp: (grp, 0, 0)),
                pl.BlockSpec((half, blk), tail),
                pl.BlockSpec((half, blk), tail),
            ],
            out_specs=pl.BlockSpec((None, 2, GROUP_WIDTH, blk), lambda b, j: (b, 0, 0, j)),
            out_shape=jax.ShapeDtypeStruct((batch, 2, GROUP_WIDTH, keep), F32),
            compiler_params=_params("parallel", "parallel"),
            name=f"kv_tail_g{grp}",
        )(x2d, g_row, wk_t, wv_t, cos_tr, sin_tr)
    return outs[0:3], outs[3:6], tuple(kv_t[i] for i in range(N_ATTN_GROUPS)), outs[6 + n_full], outs[7 + n_full]


def _window_attn_kernel(q_ref, kvc_ref, *rest, windowed, residues, sub_tiles):
    if windowed:
        kvp_ref, o_ref, lse_ref = rest
    else:
        o_ref, lse_ref = rest
    n_keys = 2 * ATTN_TILE if windowed else ATTN_TILE
    rows = lax.broadcasted_iota(jnp.int32, (ATTN_TILE, n_keys), 0)
    cols = lax.broadcasted_iota(jnp.int32, (ATTN_TILE, n_keys), 1)
    if windowed:
        band = (cols >= rows) & (cols <= rows + WINDOW_KEYS)
        first_tile_band = band & ((cols >= ATTN_TILE) | (pl.program_id(2) > 0))
    else:
        band = cols <= rows
    low_head = lax.broadcasted_iota(jnp.int32, (ATTN_TILE, LANES), 1) < HEAD_DIM
    for r in range(residues):
        for s in range(sub_tiles):
            rs = slice(s * ATTN_TILE, (s + 1) * ATTN_TILE)
            k0 = r * 2 * GROUP_WIDTH
            q = q_ref[rs, r * GROUP_WIDTH:(r + 1) * GROUP_WIDTH]
            k = kvc_ref[rs, k0:k0 + GROUP_WIDTH]
            v = kvc_ref[rs, k0 + GROUP_WIDTH:k0 + 2 * GROUP_WIDTH]
            valid = band
            if windowed:
                if s == 0:
                    prev_ref, ps, valid = kvp_ref, slice(0, ATTN_TILE), first_tile_band
                else:
                    prev_ref, ps = kvc_ref, slice((s - 1) * ATTN_TILE, s * ATTN_TILE)
                k = jnp.concatenate([prev_ref[ps, k0:k0 + GROUP_WIDTH], k], axis=0)
                v = jnp.concatenate([prev_ref[ps, k0 + GROUP_WIDTH:k0 + 2 * GROUP_WIDTH], v], axis=0)
            valid2 = jnp.concatenate([valid, valid], axis=0)
            for j in range(GROUP_WIDTH // LANES):
                sl = slice(j * LANES, (j + 1) * LANES)
                osl = slice(r * GROUP_WIDTH + j * LANES, r * GROUP_WIDTH + (j + 1) * LANES)
                qj, kj, vj = q[:, sl], k[:, sl], v[:, sl]
                zero = jnp.zeros_like(qj)
                qm = jnp.concatenate([jnp.where(low_head, qj, zero), jnp.where(low_head, zero, qj)], axis=0)
                sc = lax.dot_general(qm, kj, (((1,), (1,)), ((), ())), preferred_element_type=F32)
                sc = jnp.where(valid2, sc, -jnp.inf)
                m = jnp.max(sc, axis=1, keepdims=True)
                p = jnp.exp(sc - m)
                l = jnp.sum(p, axis=1, keepdims=True)
                o2 = jnp.dot(p.astype(BF16), vj, preferred_element_type=F32) / l
                lse2 = jnp.broadcast_to(m + jnp.log(l), o2.shape)
                o_ref[rs, osl] = jnp.where(low_head, o2[:ATTN_TILE], o2[ATTN_TILE:])
                lse_ref[rs, osl] = jnp.where(low_head, lse2[:ATTN_TILE], lse2[ATTN_TILE:])


def _prompt_attention(q, kv, group, dilation, batch, seq):
    tg = seq // dilation
    n_tiles = tg // ATTN_TILE
    windowed = n_tiles > 1
    residues = min(dilation, ATTN_RESIDUES_PER_STEP)
    sub_tiles = min(n_tiles, ATTN_TILES_PER_STEP // residues)
    rows = sub_tiles * ATTN_TILE
    q3 = q.reshape(batch, tg, dilation * GROUP_WIDTH)
    kv3 = kv.reshape(batch, tg, dilation * 2 * GROUP_WIDTH)
    in_specs = [
        pl.BlockSpec((None, rows, residues * GROUP_WIDTH), lambda b, r, t: (b, t, r)),
        pl.BlockSpec((None, rows, residues * 2 * GROUP_WIDTH), lambda b, r, t: (b, t, r)),
    ]
    args = [q3, kv3]
    if windowed:
        in_specs.append(pl.BlockSpec((None, ATTN_TILE, residues * 2 * GROUP_WIDTH),
                                     lambda b, r, t: (b, jnp.maximum(t * sub_tiles - 1, 0), r)))
        args.append(kv3)
    out_spec = pl.BlockSpec((None, rows, residues * GROUP_WIDTH), lambda b, r, t: (b, t, r))
    out_sds = jax.ShapeDtypeStruct((batch, tg, dilation * GROUP_WIDTH), F32)
    o, lse = pl.pallas_call(
        functools.partial(_window_attn_kernel, windowed=windowed, residues=residues, sub_tiles=sub_tiles),
        grid=(batch, dilation // residues, n_tiles // sub_tiles),
        in_specs=in_specs,
        out_specs=(out_spec, out_spec),
        out_shape=(out_sds, out_sds),
        compiler_params=_params("parallel", "parallel", "arbitrary"),
        name=f"prompt_attention_g{group}",
    )(*args)
    flat = (batch * tg, dilation * GROUP_WIDTH)
    return o.reshape(flat), lse.reshape(flat)


def _sublane_total(x):
    x = x + pltpu.roll(x, 4, 0)
    x = x + pltpu.roll(x, 2, 0)
    return x + pltpu.roll(x, 1, 0)


def _head_sum(prod):
    width = prod.shape[1]
    row = lax.broadcasted_iota(jnp.int32, (HEADS_PER_GROUP, width), 0)
    out = jnp.zeros((HEADS_PER_GROUP, width), F32)
    for h in range(HEADS_PER_GROUP):
        part = prod[h * HEAD_DIM:h * HEAD_DIM + SUBLANES]
        for j in range(1, HEAD_DIM // SUBLANES):
            part = part + prod[h * HEAD_DIM + j * SUBLANES:h * HEAD_DIM + (j + 1) * SUBLANES]
        out = jnp.where(row == h, _sublane_total(part), out)
    return out


def _head_expand(x):
    width = x.shape[1]
    pieces = []
    for h in range(HEADS_PER_GROUP):
        pieces.extend([jnp.broadcast_to(x[h:h + 1, :], (SUBLANES, width))] * (HEAD_DIM // SUBLANES))
    return jnp.concatenate(pieces, axis=0)


def _split_dot(acc, sel, terms=2):
    out, rem = None, acc
    for _ in range(terms):
        hi = rem.astype(BF16)
        part = jnp.dot(hi, sel, preferred_element_type=F32)
        out = part if out is None else out + part
        rem = rem - hi.astype(F32)
    return out


def _decode_attn_kernel(q_ref, n0_ref, n1_ref, n2_ref, c0_ref, c1_ref, c2_ref, o_ref, *, dec_seq):
    nq = SUBLANES
    n_sub = HEAD_DIM // SUBLANES
    neg = -jnp.inf
    q_t = q_ref[...].T
    step = lax.broadcasted_iota(jnp.int32, (HEADS_PER_GROUP, nq), 1)
    step_wide = lax.broadcasted_iota(jnp.int32, (GROUP_WIDTH, nq), 1)
    real_step = step < dec_seq
    new_refs = (n0_ref, n1_ref, n2_ref)
    cache_refs = (c0_ref, c1_ref, c2_ref)

    def column(x, t, width):
        return jnp.broadcast_to(x[:, t:t + 1], (x.shape[0], width))

    def place(cols):
        rows = cols[0].shape[0]
        lane = lax.broadcasted_iota(jnp.int32, (rows, nq), 1)
        out = jnp.zeros((rows, nq), F32)
        for t, c in enumerate(cols):
            out = jnp.where(lane == t, jnp.broadcast_to(c, (rows, nq)), out)
        return out

    outs, lses = [], []
    for g, (win, dil) in enumerate(ATTN_GROUPS):
        c_ref = cache_refs[g]
        kv_new = new_refs[g][...]
        kn_t = kv_new[:, :GROUP_WIDTH].T
        vn_t = kv_new[:, GROUP_WIDTH:].T
        q_g = q_t[g * GROUP_WIDTH:(g + 1) * GROUP_WIDTH, :]
        pos = lax.broadcasted_iota(jnp.int32, (HEADS_PER_GROUP, win), 1)
        row8 = lax.broadcasted_iota(jnp.int32, (HEADS_PER_GROUP, win), 0)
        n_tiles = win // LANES

        def cache_scores(q_pat):
            s = jnp.zeros((HEADS_PER_GROUP, win), F32)
            for h in range(HEADS_PER_GROUP):
                part = None
                for j in range(n_sub):
                    r0 = h * HEAD_DIM + j * SUBLANES
                    qp = q_pat[r0:r0 + SUBLANES, :]
                    term = c_ref[0, r0:r0 + SUBLANES, :] * jnp.concatenate([qp] * n_tiles, axis=1)
                    part = term if part is None else part + term
                s = jnp.where(row8 == h, _sublane_total(part), s)
            return s

        def weighted_values(p):
            rows = []
            for h in range(HEADS_PER_GROUP):
                ph = jnp.broadcast_to(p[h:h + 1, :], (SUBLANES, win))
                for j in range(n_sub):
                    r0 = h * HEAD_DIM + j * SUBLANES
                    prod = ph * c_ref[1, r0:r0 + SUBLANES, :]
                    a = prod[:, :LANES]
                    for tile in range(1, n_tiles):
                        a = a + prod[:, tile * LANES:(tile + 1) * LANES]
                    rows.append(a)
            return jnp.concatenate(rows, axis=0)

        if dil == 1:
            spread = (lax.broadcasted_iota(jnp.int32, (nq, dec_seq * LANES), 0)
                      == lax.broadcasted_iota(jnp.int32, (nq, dec_seq * LANES), 1) // LANES).astype(BF16)
            q_cols = _split_dot(q_g, spread, terms=1)
            kn_cols = _split_dot(kn_t, spread, terms=3)
            vn_cols = _split_dot(vn_t, spread, terms=3)
            s_new = [_head_sum(q_g * kn_cols[:, u * LANES:u * LANES + nq]) for u in range(dec_seq)]
            ok_new = [(step >= u) & real_step for u in range(dec_seq)]
            s_cache = [cache_scores(q_cols[:, t * LANES:(t + 1) * LANES]) for t in range(dec_seq)]
            m = place([jnp.max(jnp.where(pos >= t, s_cache[t], neg), axis=1, keepdims=True) for t in range(dec_seq)])
            for u in range(dec_seq):
                m = jnp.maximum(m, jnp.where(ok_new[u], s_new[u], neg))
            e_cache = [jnp.where(pos >= t, jnp.exp(s_cache[t] - m[:, t:t + 1]), 0.0) for t in range(dec_seq)]
            e_new = [jnp.where(ok_new[u], jnp.exp(s_new[u] - m), 0.0) for u in range(dec_seq)]
            denom = place([jnp.sum(e, axis=1, keepdims=True) for e in e_cache])
            for u in range(dec_seq):
                denom = denom + e_new[u]
            denom = jnp.where(real_step, denom, 1.0)
            inv = 1.0 / denom
            cols = []
            for t in range(dec_seq):
                acc = weighted_values(e_cache[t] * inv[:, t:t + 1])
                cols.append(jnp.sum(acc, axis=1, keepdims=True))
            o_g = place(cols)
            for u in range(dec_seq):
                o_g = o_g + _head_expand(e_new[u] * inv) * vn_cols[:, u * LANES:u * LANES + nq]
        else:
            res = pos % dil
            t_i = lax.broadcasted_iota(jnp.int32, (nq, LANES), 0)
            l_i = lax.broadcasted_iota(jnp.int32, (nq, LANES), 1)
            residue_pat = ((l_i % dil == t_i) & (t_i < dec_seq)).astype(BF16)
            s_cache = cache_scores(_split_dot(q_g, residue_pat, terms=1))
            s_new = jnp.where(real_step, _head_sum(q_g * kn_t), 0.0)
            m_cache = place([jnp.max(jnp.where(res == t, s_cache, neg), axis=1, keepdims=True) for t in range(dec_seq)])
            m = jnp.maximum(m_cache, s_new)

            def by_position(stat):
                out = jnp.zeros((HEADS_PER_GROUP, win), F32)
                for t in range(dec_seq):
                    out = jnp.where(res == t, column(stat, t, win), out)
                return out

            e_cache = jnp.where(res < dec_seq, jnp.exp(s_cache - by_position(m)), 0.0)
            e_new = jnp.exp(s_new - m)
            denom = place([jnp.sum(jnp.where(res == t, e_cache, 0.0), axis=1, keepdims=True)
                           for t in range(dec_seq)]) + e_new
            inv = 1.0 / denom
            acc = weighted_values(e_cache * by_position(inv))
            l_i = lax.broadcasted_iota(jnp.int32, (LANES, LANES), 0)
            t_i = lax.broadcasted_iota(jnp.int32, (LANES, LANES), 1)
            sel = ((l_i % dil == t_i) & (t_i < dec_seq)).astype(BF16)
            o_g = _split_dot(acc, sel)[:, :nq] + _head_expand(e_new * inv) * vn_t
        outs.append(o_g)
        lses.append(m + jnp.log(denom))

    top = jnp.maximum(jnp.maximum(lses[0], lses[1]), lses[2])
    ws = [jnp.exp(l - top) for l in lses]
    total = ws[0] + ws[1] + ws[2]
    merged = jnp.zeros((GROUP_WIDTH, nq), F32)
    for g in range(N_ATTN_GROUPS):
        merged = merged + _head_expand(ws[g] / total) * outs[g]
    o_ref[...] = jnp.where(step_wide < dec_seq, merged, 0.0).T


def _decode_attention(q, caches, new_kv, dec_batch, dec_seq):
    nq = SUBLANES

    def pad_steps(a):
        a = a.astype(F32).reshape(dec_batch, dec_seq, a.shape[-1])
        return jnp.pad(a, ((0, 0), (0, nq - dec_seq), (0, 0)))

    args = [pad_steps(q)] + [pad_steps(n) for n in new_kv]
    in_specs = [pl.BlockSpec((None, nq, ATTN_WIDTH), lambda b: (b, 0, 0))]
    in_specs += [pl.BlockSpec((None, nq, 2 * GROUP_WIDTH), lambda b: (b, 0, 0)) for _ in new_kv]
    for (win, dil), cache in zip(ATTN_GROUPS, caches):
        args.append(jnp.transpose(cache, (0, 2, 3, 4, 1)).reshape(dec_batch, 2, GROUP_WIDTH, win))
        in_specs.append(pl.BlockSpec((None, 2, GROUP_WIDTH, win), lambda b: (b, 0, 0, 0)))
    o_t = pl.pallas_call(
        functools.partial(_decode_attn_kernel, dec_seq=dec_seq),
        grid=(dec_batch,),
        in_specs=in_specs,
        out_specs=pl.BlockSpec((None, nq, GROUP_WIDTH), lambda b: (b, 0, 0)),
        out_shape=jax.ShapeDtypeStruct((dec_batch, nq, GROUP_WIDTH), F32),
        compiler_params=_params("parallel"),
        name="decode_attention",
    )(*args)
    return o_t[:, :dec_seq, :].reshape(dec_batch * dec_seq, GROUP_WIDTH)


def _ssm_kernel(u_ref, h0r_ref, h0i_ref, ar_ref, ai_ref, bw_ref, cr_ref, ci_ref, d_ref,
                y_ref, hr_out, hi_out, bur, bui, hr_s, hi_s, *, nb, steps):
    chunk = pl.program_id(0)
    half_in = SSM_WIDTH // 2
    half_st = SSM_COLS // 2

    @pl.when(chunk == 0)
    def _():
        hr_s[...] = h0r_ref[...]
        hi_s[...] = h0i_ref[...]

    u = u_ref[...]
    ub = u.astype(BF16)
    for hf in range(2):
        r = jnp.dot(ub[:, hf * half_in:(hf + 1) * half_in], bw_ref[hf], preferred_element_type=F32)
        bur[:, hf * half_st:(hf + 1) * half_st] = r[:, :half_st]
        bui[:, hf * half_st:(hf + 1) * half_st] = r[:, half_st:]

    lane_chunk = 4 * LANES

    def sub_batch(s, carry):
        for lc in range(SSM_COLS // lane_chunk):
            ls = slice(lc * lane_chunk, (lc + 1) * lane_chunk)
            ar = ar_ref[:, ls]
            ai = ai_ref[:, ls]
            row0 = pl.multiple_of(s * SUBLANES, SUBLANES)

            def step(t, h):
                hr, hi = h
                row = pl.multiple_of(t * nb + s * SUBLANES, SUBLANES)
                nhr = ar * hr - ai * hi + bur[pl.ds(row, SUBLANES), ls]
                nhi = ar * hi + ai * hr + bui[pl.ds(row, SUBLANES), ls]
                bur[pl.ds(row, SUBLANES), ls] = nhr
                bui[pl.ds(row, SUBLANES), ls] = nhi
                return nhr, nhi

            hr, hi = lax.fori_loop(0, steps, step, (hr_s[pl.ds(row0, SUBLANES), ls], hi_s[pl.ds(row0, SUBLANES), ls]))
            hr_s[pl.ds(row0, SUBLANES), ls] = hr
            hi_s[pl.ds(row0, SUBLANES), ls] = hi
        return carry

    lax.fori_loop(0, nb // SUBLANES, sub_batch, 0)

    for hf in range(2):
        ss = slice(hf * half_st, (hf + 1) * half_st)
        y = jnp.dot(bur[:, ss].astype(BF16), cr_ref[hf], preferred_element_type=F32)
        y = y + jnp.dot(bui[:, ss].astype(BF16), ci_ref[hf], preferred_element_type=F32)
        cs = slice(hf * half_in, (hf + 1) * half_in)
        y_ref[:, cs] = y + d_ref[:, cs] * u[:, cs]

    @pl.when(chunk == pl.num_programs(0) - 1)
    def _():
        hr_out[...] = hr_s[...]
        hi_out[...] = hi_s[...]


def _ssm_weights(ssm_log_dt, a_re, a_im, b_re, b_im, c_re, c_im):
    dt = jnp.exp(ssm_log_dt.astype(F32))[:, None]
    lam = lax.complex(a_re.astype(F32), a_im.astype(F32))
    lam_bar = jnp.exp(lam * dt)
    b_bar = ((lam_bar - 1.0) / lam)[:, :, None] * lax.complex(b_re.astype(F32), b_im.astype(F32))
    gh = N_SSM_GROUPS // 2
    eye = jnp.eye(gh, dtype=F32)

    def block_diag(m):
        return jnp.einsum('gab,gh->gahb', m, eye).reshape(gh * m.shape[1], gh * m.shape[2])

    bw, cr, ci = [], [], []
    for hf in range(2):
        gs = slice(hf * gh, (hf + 1) * gh)
        b_t = jnp.transpose(b_bar[gs], (0, 2, 1))
        bw.append(jnp.concatenate([block_diag(jnp.real(b_t)), block_diag(jnp.imag(b_t))], axis=1))
        cr.append(block_diag(jnp.transpose(c_re[gs].astype(F32), (0, 2, 1))))
        ci.append(block_diag(jnp.transpose(-c_im[gs].astype(F32), (0, 2, 1))))
    ar = jnp.broadcast_to(jnp.real(lam_bar).reshape(1, SSM_COLS), (SUBLANES, SSM_COLS))
    ai = jnp.broadcast_to(jnp.imag(lam_bar).reshape(1, SSM_COLS), (SUBLANES, SSM_COLS))
    return ar, ai, jnp.stack(bw).astype(BF16), jnp.stack(cr).astype(BF16), jnp.stack(ci).astype(BF16)


def _ssm(u_tm, h0r, h0i, weights, d_flat, nb):
    ar, ai, bw, cr, ci = weights
    rows = u_tm.shape[0]
    steps = SSM_ROWS // nb
    const2 = lambda c: (0, 0)
    const3 = lambda c: (0, 0, 0)
    state_sds = jax.ShapeDtypeStruct((nb, SSM_COLS), F32)
    return pl.pallas_call(
        functools.partial(_ssm_kernel, nb=nb, steps=steps),
        grid=(rows // SSM_ROWS,),
        in_specs=[
            pl.BlockSpec((SSM_ROWS, SSM_WIDTH), lambda c: (c, 0)),
            pl.BlockSpec((nb, SSM_COLS), const2),
            pl.BlockSpec((nb, SSM_COLS), const2),
            pl.BlockSpec((SUBLANES, SSM_COLS), const2),
            pl.BlockSpec((SUBLANES, SSM_COLS), const2),
            pl.BlockSpec(bw.shape, const3),
            pl.BlockSpec(cr.shape, const3),
            pl.BlockSpec(ci.shape, const3),
            pl.BlockSpec((1, SSM_WIDTH), const2),
        ],
        out_specs=(
            pl.BlockSpec((SSM_ROWS, SSM_WIDTH), lambda c: (c, 0)),
            pl.BlockSpec((nb, SSM_COLS), const2),
            pl.BlockSpec((nb, SSM_COLS), const2),
        ),
        out_shape=(jax.ShapeDtypeStruct((rows, SSM_WIDTH), F32), state_sds, state_sds),
        scratch_shapes=[
            pltpu.VMEM((SSM_ROWS, SSM_COLS), F32),
            pltpu.VMEM((SSM_ROWS, SSM_COLS), F32),
            pltpu.VMEM((nb, SSM_COLS), F32),
            pltpu.VMEM((nb, SSM_COLS), F32),
        ],
        compiler_params=_params("arbitrary"),
        name="s5_scan",
    )(u_tm, h0r, h0i, ar, ai, bw, cr, ci, d_flat)


def _mix_kernel(*refs, merged_attn):
    if merged_attn:
        attn_ref = refs[0]
        rest = refs[1:13]
    else:
        group_refs = refs[0:2 * N_ATTN_GROUPS]
        rest = refs[2 * N_ATTN_GROUPS:2 * N_ATTN_GROUPS + 12]
        scratch = refs[2 * N_ATTN_GROUPS + 12:]
    (y_ref, gate_ref, x_ref, wa_ref, wglu_ref, wout_ref, gffn_ref, wr_ref, br_ref,
     h_ref, hn_ref, route_ref) = rest
    tm = x_ref.shape[0]
    if merged_attn:
        attn = attn_ref[...]
    else:
        natural = []
        for idx, ref in enumerate(group_refs):
            dil = ATTN_GROUPS[idx % N_ATTN_GROUPS][1]
            if dil == 1:
                natural.append(ref[...])
                continue
            scr = scratch[idx]
            n_chunks = GROUP_WIDTH // LANES
            for r in range(dil):
                for j in range(n_chunks):
                    col = r * GROUP_WIDTH + j * LANES
                    scr[j, pl.ds(r, tm // dil, stride=dil), :] = ref[:, col:col + LANES]
            natural.append(jnp.concatenate([scr[j] for j in range(n_chunks)], axis=1))
        os, ls = natural[:N_ATTN_GROUPS], natural[N_ATTN_GROUPS:]
        top = jnp.maximum(jnp.maximum(ls[0], ls[1]), ls[2])
        es = [jnp.exp(l - top) for l in ls]
        attn = (es[0] * os[0] + es[1] * os[1] + es[2] * os[2]) / (es[0] + es[1] + es[2])
    attn_out = jnp.dot(attn.astype(BF16), wa_ref[...], preferred_element_type=F32)
    glu = jnp.dot(jax.nn.gelu(y_ref[...]).astype(BF16), wglu_ref[...], preferred_element_type=F32)
    ssm_out = glu[:, :D_MODEL] * jax.nn.sigmoid(glu[:, D_MODEL:])
    merged = jax.nn.sigmoid(gate_ref[:, :D_MODEL]) * attn_out + jax.nn.sigmoid(gate_ref[:, D_MODEL:]) * ssm_out
    h = x_ref[...] + jnp.dot(merged.astype(BF16), wout_ref[...], preferred_element_type=F32)
    h_ref[...] = h
    var = jnp.mean(h * h, axis=-1, keepdims=True)
    hn = h * lax.rsqrt(var + NORM_EPS) * gffn_ref[...]
    _to_token_tiles(hn_ref, hn)
    hn_hi = hn.astype(BF16)
    hn_lo = (hn - hn_hi.astype(F32)).astype(BF16)
    w_hi = wr_ref[:, :LANES]
    w_lo = wr_ref[:, LANES:]
    logits = (jnp.dot(hn_hi, w_hi, preferred_element_type=F32) + jnp.dot(hn_hi, w_lo, preferred_element_type=F32)
              + jnp.dot(hn_lo, w_hi, preferred_element_type=F32)) + br_ref[...]
    lane = lax.broadcasted_iota(jnp.int32, logits.shape, 1).astype(F32)
    far = float(LANES)

    def first_argmax(vals):
        top_v = jnp.max(vals, axis=1, keepdims=True)
        return top_v, jnp.min(jnp.where(vals == top_v, lane, far), axis=1, keepdims=True)

    group_logits = jnp.where(lane < N_EXPERT_GROUPS, logits, -jnp.inf)
    g_top, g_idx = first_argmax(group_logits)
    p_group = 1.0 / jnp.sum(jnp.exp(group_logits - g_top), axis=1, keepdims=True)
    first_lane = N_EXPERT_GROUPS + g_idx * EXPERTS_PER_GROUP
    in_group = (lane >= first_lane) & (lane < first_lane + EXPERTS_PER_GROUP)
    expert_logits = jnp.where(in_group, logits, -jnp.inf)
    v1, i1 = first_argmax(expert_logits)
    v2, i2 = first_argmax(jnp.where(lane == i1, -jnp.inf, expert_logits))
    e2 = jnp.exp(v2 - v1)
    w1 = p_group / (1.0 + e2)
    w2 = p_group * e2 / (1.0 + e2)
    route = jnp.where(lane == 0, i1 - N_EXPERT_GROUPS,
                      jnp.where(lane == 1, i2 - N_EXPERT_GROUPS,
                                jnp.where(lane == 2, w1, jnp.where(lane == 3, w2, 0.0))))
    route_ref[...] = route


def _mix(attn_inputs, y, gates, x2d, wa, wglu, wout, gffn, wr, br):
    n = x2d.shape[0]
    tm = PROJ_TILE
    row = lambda i: (i, 0)
    const = lambda i: (0, 0)
    merged_attn = len(attn_inputs) == 1
    in_specs = [pl.BlockSpec((tm * GROUP_WIDTH // a.shape[1], a.shape[1]), row) for a in attn_inputs]
    scratch = [] if merged_attn else [pltpu.VMEM((GROUP_WIDTH // LANES, tm, LANES), F32) for _ in attn_inputs]
    in_specs += [
        pl.BlockSpec((tm, SSM_WIDTH), row),
        pl.BlockSpec((tm, 2 * D_MODEL), row),
        pl.BlockSpec((tm, D_MODEL), row),
        pl.BlockSpec(wa.shape, const),
        pl.BlockSpec(wglu.shape, const),
        pl.BlockSpec(wout.shape, const),
        pl.BlockSpec((1, D_MODEL), const),
        pl.BlockSpec(wr.shape, const),
        pl.BlockSpec((1, LANES), const),
    ]
    return pl.pallas_call(
        functools.partial(_mix_kernel, merged_attn=merged_attn),
        grid=(n // tm,),
        in_specs=in_specs,
        out_specs=(pl.BlockSpec((tm, D_MODEL), row), pl.BlockSpec((tm * TOKEN_TILE_ROWS, LANES), row),
                   pl.BlockSpec((tm, LANES), row)),
        out_shape=(jax.ShapeDtypeStruct((n, D_MODEL), F32), jax.ShapeDtypeStruct((n * TOKEN_TILE_ROWS, LANES), F32),
                   jax.ShapeDtypeStruct((n, LANES), F32)),
        scratch_shapes=scratch,
        compiler_params=_params("parallel"),
        name="branch_mix",
    )(*attn_inputs, y, gates, x2d, wa, wglu, wout, gffn, wr, br)


def _to_token_tiles(ref, val):
    rows = val.shape[0]
    for j in range(TOKEN_TILE_ROWS):
        ref[pl.ds(j, rows, stride=TOKEN_TILE_ROWS), :] = val[:, j * LANES:(j + 1) * LANES]


def _from_token_tiles(ref):
    rows = ref.shape[0] // TOKEN_TILE_ROWS
    return jnp.concatenate([ref[pl.ds(j, rows, stride=TOKEN_TILE_ROWS), :] for j in range(TOKEN_TILE_ROWS)], axis=1)


def _token_copy(src, src_tok, dst, dst_tok, sem):
    src_rows = pl.ds(pl.multiple_of(src_tok * TOKEN_TILE_ROWS, TOKEN_TILE_ROWS), TOKEN_TILE_ROWS)
    dst_rows = pl.ds(pl.multiple_of(dst_tok * TOKEN_TILE_ROWS, TOKEN_TILE_ROWS), TOKEN_TILE_ROWS)
    return pltpu.make_async_copy(src.at[src_rows], dst.at[dst_rows], sem)


def _dispatch_kernel(pad_end_ref, padded_ref, n_used_ref, slot_ref, hn_ref, xs_hbm, zero_buf, sem, zero_sem):
    block_rows = MOE_BLOCK * TOKEN_TILE_ROWS
    n_blocks = xs_hbm.shape[0] // block_rows

    @pl.when(pl.program_id(0) == 0)
    def _():
        zero_buf[...] = jnp.zeros_like(zero_buf)

        def block_copy(b):
            start = pl.multiple_of(b * block_rows, block_rows)
            return pltpu.make_async_copy(zero_buf, xs_hbm.at[pl.ds(start, block_rows)], zero_sem)

        def segment_tails(action):
            def body(e, carry):
                @pl.when(padded_ref[e] > 0)
                def _():
                    action(block_copy(pad_end_ref[e] // MOE_BLOCK - 1))
                return carry
            lax.fori_loop(0, N_EXPERTS, body, 0)

        def unused_blocks(action):
            def body(b, carry):
                @pl.when(b >= n_used_ref[0])
                def _():
                    action(block_copy(b))
                return carry
            lax.fori_loop(0, n_blocks, body, 0)

        segment_tails(lambda cp: cp.start())
        unused_blocks(lambda cp: cp.start())
        segment_tails(lambda cp: cp.wait())
        unused_blocks(lambda cp: cp.wait())

    def issue(j, carry):
        _token_copy(hn_ref, j, xs_hbm, slot_ref[2 * j], sem).start(priority=0)
        _token_copy(hn_ref, j, xs_hbm, slot_ref[2 * j + 1], sem).start(priority=1)
        return carry

    def drain(j, carry):
        _token_copy(hn_ref, 0, xs_hbm, 0, sem).wait()
        _token_copy(hn_ref, 0, xs_hbm, 0, sem).wait()
        return carry

    lax.fori_loop(0, ROW_TILE, issue, 0, unroll=DMA_UNROLL)
    lax.fori_loop(0, ROW_TILE, drain, 0, unroll=DMA_UNROLL)


def _dispatch(pad_end, padded, n_used, slots, hn, n_slots):
    n = hn.shape[0] // TOKEN_TILE_ROWS
    return pl.pallas_call(
        _dispatch_kernel,
        grid_spec=pltpu.PrefetchScalarGridSpec(
            num_scalar_prefetch=3,
            grid=(n // ROW_TILE,),
            in_specs=[
                pl.BlockSpec((2 * ROW_TILE,), lambda i, *_: (i,), memory_space=pltpu.SMEM),
                pl.BlockSpec((ROW_TILE * TOKEN_TILE_ROWS, LANES), lambda i, *_: (i, 0)),
            ],
            out_specs=pl.BlockSpec(memory_space=pl.ANY),
            scratch_shapes=[pltpu.VMEM((MOE_BLOCK * TOKEN_TILE_ROWS, LANES), F32), pltpu.SemaphoreType.DMA(()),
                            pltpu.SemaphoreType.DMA(())],
        ),
        out_shape=jax.ShapeDtypeStruct((n_slots * TOKEN_TILE_ROWS, LANES), F32),
        compiler_params=_params("arbitrary"),
        name="moe_dispatch",
    )(pad_end, padded, n_used, slots, hn)


def _expert_kernel(block_e_ref, n_used_ref, xs_ref, wg_ref, wu_ref, wd_ref, yb_ref):
    del block_e_ref
    i = pl.program_id(0)

    @pl.when(i < n_used_ref[0])
    def _():
        xb = _from_token_tiles(xs_ref).astype(BF16)
        gate = jnp.dot(xb, wg_ref[...].astype(BF16), preferred_element_type=F32)
        up = jnp.dot(xb, wu_ref[...].astype(BF16), preferred_element_type=F32)
        hmid = (jax.nn.silu(gate) * up).astype(BF16)
        _to_token_tiles(yb_ref, jnp.dot(hmid, wd_ref[...].astype(BF16), preferred_element_type=F32))

    @pl.when(i >= n_used_ref[0])
    def _():
        yb_ref[...] = jnp.zeros_like(yb_ref)


def _experts(block_e, n_used, xs, wg, wu, wd):
    block_rows = MOE_BLOCK * TOKEN_TILE_ROWS
    n_blocks = xs.shape[0] // block_rows
    grid_spec = pltpu.PrefetchScalarGridSpec(
        num_scalar_prefetch=2,
        grid=(n_blocks,),
        in_specs=[
            pl.BlockSpec((block_rows, LANES), lambda i, be, nu: (jnp.minimum(i, nu[0] - 1), 0)),
            pl.BlockSpec((None, D_MODEL, D_EXPERT), lambda i, be, nu: (be[i], 0, 0)),
            pl.BlockSpec((None, D_MODEL, D_EXPERT), lambda i, be, nu: (be[i], 0, 0)),
            pl.BlockSpec((None, D_EXPERT, D_MODEL), lambda i, be, nu: (be[i], 0, 0)),
        ],
        out_specs=pl.BlockSpec((block_rows, LANES), lambda i, be, nu: (i, 0)),
    )
    return pl.pallas_call(
        _expert_kernel,
        grid_spec=grid_spec,
        out_shape=jax.ShapeDtypeStruct(xs.shape, F32),
        compiler_params=_params("arbitrary"),
        name="moe_experts",
    )(block_e, n_used, xs, wg, wu, wd)


def _combine_kernel(slot_ref, h_ref, route_ref, g_ref, yb_hbm, out_ref, buf_a, buf_b, sem):
    def issue(j, carry):
        _token_copy(yb_hbm, slot_ref[2 * j], buf_a, j, sem).start(priority=0)
        _token_copy(yb_hbm, slot_ref[2 * j + 1], buf_b, j, sem).start(priority=1)
        return carry

    def drain(j, carry):
        _token_copy(yb_hbm, 0, buf_a, 0, sem).wait()
        _token_copy(yb_hbm, 0, buf_b, 0, sem).wait()
        return carry

    lax.fori_loop(0, ROW_TILE, issue, 0, unroll=DMA_UNROLL)
    lax.fori_loop(0, ROW_TILE, drain, 0, unroll=DMA_UNROLL)
    route = route_ref[...]
    h = h_ref[...] + (route[:, 2:3] * _from_token_tiles(buf_a) + route[:, 3:4] * _from_token_tiles(buf_b))
    var = jnp.mean(h * h, axis=-1, keepdims=True)
    out_ref[...] = h * lax.rsqrt(var + NORM_EPS) * g_ref[...]


def _combine(slots, h, route, g_final, yb):
    n = h.shape[0]
    row = lambda i: (i, 0)
    return pl.pallas_call(
        _combine_kernel,
        grid=(n // ROW_TILE,),
        in_specs=[
            pl.BlockSpec((2 * ROW_TILE,), lambda i: (i,), memory_space=pltpu.SMEM),
            pl.BlockSpec((ROW_TILE, D_MODEL), row),
            pl.BlockSpec((ROW_TILE, LANES), row),
            pl.BlockSpec((1, D_MODEL), lambda i: (0, 0)),
            pl.BlockSpec(memory_space=pl.ANY),
        ],
        out_specs=pl.BlockSpec((ROW_TILE, D_MODEL), row),
        out_shape=jax.ShapeDtypeStruct((n, D_MODEL), F32),
        scratch_shapes=[
            pltpu.VMEM((ROW_TILE * TOKEN_TILE_ROWS, LANES), F32),
            pltpu.VMEM((ROW_TILE * TOKEN_TILE_ROWS, LANES), F32),
            pltpu.SemaphoreType.DMA(()),
        ],
        compiler_params=_params("arbitrary"),
        name="moe_combine",
    )(slots, h, route, g_final.reshape(1, D_MODEL), yb)


def _slot_assignment(route, n_blocks):
    flat_e = route[:, 0:2].astype(jnp.int32).reshape(-1)
    onehot = (flat_e[:, None] == jnp.arange(N_EXPERTS, dtype=jnp.int32)[None, :]).astype(jnp.int32)
    running = jnp.cumsum(onehot, axis=0)
    rank = jnp.sum(onehot * running, axis=1) - 1
    counts = running[-1]
    padded = ((counts + MOE_BLOCK - 1) // MOE_BLOCK) * MOE_BLOCK
    pad_end = jnp.cumsum(padded)
    pad_start = pad_end - padded
    slots = jnp.sum(onehot * pad_start[None, :], axis=1) + rank
    block_start = jnp.arange(n_blocks, dtype=jnp.int32) * MOE_BLOCK
    block_e = jnp.minimum(jnp.sum((pad_end[None, :] <= block_start[:, None]).astype(jnp.int32), axis=1), N_EXPERTS - 1)
    n_used = (pad_end[-1:] // MOE_BLOCK).astype(jnp.int32)
    return slots.astype(jnp.int32), block_e, n_used, pad_end.astype(jnp.int32), padded.astype(jnp.int32)


def _moe_and_final_norm(h, hn, route, wg, wu, wd, g_final):
    n = h.shape[0]
    n_blocks = (2 * n) // MOE_BLOCK + N_EXPERTS
    slots, block_e, n_used, pad_end, padded = _slot_assignment(route, n_blocks)
    xs = _dispatch(pad_end, padded, n_used, slots, hn, n_blocks * MOE_BLOCK)
    yb = _experts(block_e, n_used, xs, wg, wu, wd)
    return _combine(slots, h, route, g_final, yb)


def kernel(x_prompt, x_sample, cache_kv_w128, cache_kv_w512, cache_kv_w2048, state_ssm, g_attn_norm, w_in, ssm_log_dt, ssm_a_re, ssm_a_im, ssm_b_re, ssm_b_im, ssm_c_re, ssm_c_im, ssm_d, w_glu, w_attn_branch, w_out, g_ffn_norm, w_router_group, b_router_group, w_router_expert, b_router_expert, w_exp_gate, w_exp_up, w_exp_down, g_final):
    batch, seq, _ = x_prompt.shape
    dec_batch, dec_seq, _ = x_sample.shape
    past_len = cache_kv_w2048.shape[2]
    layer = 0

    w_in_b = w_in[layer].astype(BF16)
    wa = w_attn_branch[layer].astype(BF16)
    wglu = w_glu[layer].astype(BF16)
    wout = w_out[layer].astype(BF16)
    wg, wu, wd = w_exp_gate[layer], w_exp_up[layer], w_exp_down[layer]
    gffn = g_ffn_norm[layer].reshape(1, D_MODEL)
    pad = LANES - N_EXPERT_GROUPS - N_EXPERTS
    wr_f32 = jnp.concatenate([w_router_group[layer], w_router_expert[layer], jnp.zeros((D_MODEL, pad), F32)], axis=1)
    wr_hi = wr_f32.astype(BF16)
    wr = jnp.concatenate([wr_hi, (wr_f32 - wr_hi.astype(F32)).astype(BF16)], axis=1)
    br = jnp.concatenate([b_router_group[layer], b_router_expert[layer], jnp.zeros((pad,), F32)]).reshape(1, LANES)
    ssm_w = _ssm_weights(ssm_log_dt[layer], ssm_a_re[layer], ssm_a_im[layer], ssm_b_re[layer], ssm_b_im[layer],
                         ssm_c_re[layer], ssm_c_im[layer])
    d_flat = ssm_d[layer].astype(F32).reshape(1, SSM_WIDTH)

    def time_major(a, nb, steps):
        return jnp.transpose(a.reshape(nb, steps, -1), (1, 0, 2)).reshape(nb * steps, -1)

    def batch_major(a, nb, steps):
        return jnp.transpose(a.reshape(steps, nb, -1), (1, 0, 2)).reshape(nb * steps, -1)

    xp = x_prompt.reshape(batch * seq, D_MODEL)
    q_p, kv_p, kv_t_p, u_p, gates_p = _in_projection_prompt(xp, g_attn_norm[layer], w_in_b, batch, seq)
    attn_in = []
    lse_in = []
    for g, (win, dil) in enumerate(ATTN_GROUPS):
        o, lse = _prompt_attention(q_p[g], kv_p[g], g, dil, batch, seq)
        attn_in.append(o)
        lse_in.append(lse)
    zeros_state = jnp.zeros((batch, SSM_COLS), F32)
    y_tm, hr_p, hi_p = _ssm(time_major(u_p, batch, seq), zeros_state, zeros_state, ssm_w, d_flat, batch)
    y_p = batch_major(y_tm, batch, seq)
    h_p, hn_p, route_p = _mix(attn_in + lse_in, y_p, gates_p, xp, wa, wglu, wout, gffn, wr, br)
    out_p = _moe_and_final_norm(h_p, hn_p, route_p, wg, wu, wd, g_final)

    xs = x_sample.reshape(dec_batch * dec_seq, D_MODEL)
    pos_s = past_len + (jnp.arange(dec_batch * dec_seq, dtype=jnp.int32) % dec_seq)
    q_s, kv0_s, kv1_s, kv2_s, u_s, gates_s = _in_projection_decode(xs, g_attn_norm[layer], w_in_b, pos_s)
    kv_s = (kv0_s, kv1_s, kv2_s)
    caches = (cache_kv_w128[layer], cache_kv_w512[layer], cache_kv_w2048[layer])
    attn_s = _decode_attention(q_s, caches, kv_s, dec_batch, dec_seq)
    st = state_ssm[layer].astype(F32).reshape(dec_batch, SSM_COLS, 2)
    ys_tm, hr_s, hi_s = _ssm(time_major(u_s, dec_batch, dec_seq), st[:, :, 0], st[:, :, 1], ssm_w, d_flat, dec_batch)
    y_s = batch_major(ys_tm, dec_batch, dec_seq)
    h_s, hn_s, route_s = _mix([attn_s], y_s, gates_s, xs, wa, wglu, wout, gffn, wr, br)
    out_s = _moe_and_final_norm(h_s, hn_s, route_s, wg, wu, wd, g_final)

    kv_tail = (2, HEADS_PER_GROUP, HEAD_DIM)
    outs = [out_p.reshape(batch, seq, D_MODEL), out_s.reshape(dec_batch, dec_seq, D_MODEL)]
    for g, (win, dil) in enumerate(ATTN_GROUPS):
        keep = min(win, seq)
        rows_p = jnp.transpose(kv_t_p[g].reshape(batch, 2, HEADS_PER_GROUP, HEAD_DIM, keep), (0, 4, 1, 2, 3))
        outs.append(rows_p[None])
        outs.append(kv_s[g].reshape((1, dec_batch, dec_seq) + kv_tail))
    outs.append(jnp.stack([hr_p, hi_p], axis=-1).reshape(1, batch, N_SSM_GROUPS, SSM_STATE, 2))
    outs.append(jnp.stack([hr_s, hi_s], axis=-1).reshape(1, dec_batch, N_SSM_GROUPS, SSM_STATE, 2))
    return tuple(outs)
```

```python
import functools
import math

import jax
import jax.numpy as jnp
from jax import lax
from jax.experimental import pallas as pl
from jax.experimental.pallas import tpu as pltpu

F32 = jnp.float32
BF16 = jnp.bfloat16

D_MODEL = 1024
HEAD_DIM = 64
HEADS_PER_GROUP = 8
GROUP_WIDTH = HEADS_PER_GROUP * HEAD_DIM
ATTN_GROUPS = ((128, 1), (512, 4), (2048, 16))
N_ATTN_GROUPS = len(ATTN_GROUPS)
ATTN_WIDTH = N_ATTN_GROUPS * GROUP_WIDTH
ROT_DIM = HEAD_DIM // 4
ROPE_THETA = 500000.0
WINDOW_KEYS = 128
SSM_GROUP_CH = 16
SSM_WIDTH = D_MODEL // 2
N_SSM_GROUPS = SSM_WIDTH // SSM_GROUP_CH
SSM_STATE = 64
SSM_COLS = N_SSM_GROUPS * SSM_STATE
IN_WIDTH = 3 * ATTN_WIDTH + SSM_WIDTH + 2 * D_MODEL
N_EXPERT_GROUPS = 4
EXPERTS_PER_GROUP = 8
N_EXPERTS = N_EXPERT_GROUPS * EXPERTS_PER_GROUP
D_EXPERT = D_MODEL // 4
NORM_EPS = 1e-6

LANES = 128
SUBLANES = 8
TOKEN_TILE_ROWS = D_MODEL // LANES
VMEM_LIMIT = 56 * 1024 * 1024

PROJ_TILE = 256
MATMUL_LOOKAHEAD = 1
ATTN_TILE = 128
ATTN_RESIDUES_PER_STEP = 4
ATTN_TILES_PER_STEP = 8
SSM_ROWS = 512
MOE_BLOCK = 256
ROW_TILE = 512
DMA_UNROLL = 8


def _params(*sem):
    return pltpu.CompilerParams(dimension_semantics=sem, vmem_limit_bytes=VMEM_LIMIT)


def _normed_input(x_ref, g_ref):
    x = x_ref[...]
    var = jnp.mean(x * x, axis=-1, keepdims=True)
    return (x * lax.rsqrt(var + NORM_EPS) * g_ref[...]).astype(BF16)


def _rope_fn(cos_ref, sin_ref):
    cos = cos_ref[...]
    sin = sin_ref[...]
    lane = lax.broadcasted_iota(jnp.int32, cos.shape, 1) % HEAD_DIM
    first_half = lane < ROT_DIM // 2
    rotated = lane < ROT_DIM

    def rope_chunk(c):
        partner = jnp.where(first_half, pltpu.roll(c, LANES - ROT_DIM // 2, 1), pltpu.roll(c, ROT_DIM // 2, 1))
        return jnp.where(rotated, c * cos + partner * sin, c)

    def rope(t):
        return jnp.concatenate([rope_chunk(t[:, j * LANES:(j + 1) * LANES]) for j in range(GROUP_WIDTH // LANES)], axis=1)

    return rope


def _projection_tiles(xn, w_ref):
    for c in range(IN_WIDTH // GROUP_WIDTH):
        yield c, jnp.dot(xn, w_ref[:, c * GROUP_WIDTH:(c + 1) * GROUP_WIDTH], preferred_element_type=F32)


def _inproj_decode_kernel(x_ref, g_ref, w_ref, cos_ref, sin_ref, q_ref, kv0_ref, kv1_ref, kv2_ref, u_ref, gate_ref):
    rope = _rope_fn(cos_ref, sin_ref)
    kv_refs = (kv0_ref, kv1_ref, kv2_ref)
    for c, acc in _projection_tiles(_normed_input(x_ref, g_ref), w_ref):
        if c < 3:
            q_ref[:, c * GROUP_WIDTH:(c + 1) * GROUP_WIDTH] = (rope(acc) * (HEAD_DIM ** -0.5)).astype(BF16)
        elif c < 6:
            kv_refs[c - 3][:, :GROUP_WIDTH] = rope(acc)
        elif c < 9:
            kv_refs[c - 6][:, GROUP_WIDTH:] = acc
        elif c == 9:
            u_ref[...] = acc
        else:
            gate_ref[:, (c - 10) * GROUP_WIDTH:(c - 9) * GROUP_WIDTH] = acc


def _transposed_projection(wt, xn, cos_t, sin_t):
    t = lax.dot_general(wt, xn, (((1,), (1,)), ((), ())), preferred_element_type=F32)
    if cos_t is None:
        return t
    half_rot = ROT_DIM // 2
    pieces = []
    for h in range(HEADS_PER_GROUP):
        x1 = t[h * HEAD_DIM:h * HEAD_DIM + half_rot]
        x2 = t[h * HEAD_DIM + half_rot:h * HEAD_DIM + ROT_DIM]
        pieces += [x1 * cos_t - x2 * sin_t, x2 * cos_t + x1 * sin_t, t[h * HEAD_DIM + ROT_DIM:(h + 1) * HEAD_DIM]]
    return jnp.concatenate(pieces, axis=0)


def _kv_tail_kernel(x_ref, g_ref, wk_ref, wv_ref, cos_t_ref, sin_t_ref, kt_ref):
    xn = _normed_input(x_ref, g_ref)
    kt_ref[0] = _transposed_projection(wk_ref[...], xn, cos_t_ref[...], sin_t_ref[...])
    kt_ref[1] = _transposed_projection(wv_ref[...], xn, None, None)


def _inproj_prompt_kernel(x_ref, g_ref, w_ref, cos_ref, sin_ref, wt_ref, cos_t_ref, sin_t_ref, *refs, full_groups):
    q_refs, kv_refs = refs[0:N_ATTN_GROUPS], refs[N_ATTN_GROUPS:2 * N_ATTN_GROUPS]
    kt_refs = refs[2 * N_ATTN_GROUPS:2 * N_ATTN_GROUPS + len(full_groups)]
    u_ref, gate_ref = refs[2 * N_ATTN_GROUPS + len(full_groups):2 * N_ATTN_GROUPS + len(full_groups) + 2]
    scratch = refs[2 * N_ATTN_GROUPS + len(full_groups) + 2:]
    rope = _rope_fn(cos_ref, sin_ref)
    tm = x_ref.shape[0]

    dilated = [g for g, (_, d) in enumerate(ATTN_GROUPS) if d > 1]

    def store_rows(dst_ref, val, kind, g, col0, col_stride):
        dil = ATTN_GROUPS[g][1]
        if dil == 1:
            dst_ref[:, col0:col0 + GROUP_WIDTH] = val.astype(BF16)
            return
        scr = scratch[kind * len(dilated) + dilated.index(g)]
        n_chunks = GROUP_WIDTH // LANES
        for j in range(n_chunks):
            scr[j] = val[:, j * LANES:(j + 1) * LANES]
        rows = tm // dil
        for r in range(dil):
            piece = jnp.concatenate([scr[j, pl.ds(r, rows, stride=dil), :] for j in range(n_chunks)], axis=1)
            dst_ref[:, col0 + r * col_stride:col0 + r * col_stride + GROUP_WIDTH] = piece.astype(BF16)

    xn = _normed_input(x_ref, g_ref)

    def emit(c, acc):
        g = c % N_ATTN_GROUPS
        if c < 3:
            store_rows(q_refs[g], rope(acc) * (HEAD_DIM ** -0.5), 0, g, 0, GROUP_WIDTH)
        elif c < 6:
            store_rows(kv_refs[g], rope(acc), 1, g, 0, 2 * GROUP_WIDTH)
            if g in full_groups:
                kt_refs[full_groups.index(g)][0] = _transposed_projection(
                    wt_ref[full_groups.index(g)], xn, cos_t_ref[...], sin_t_ref[...])
        elif c < 9:
            store_rows(kv_refs[g], acc, 2, g, GROUP_WIDTH, 2 * GROUP_WIDTH)
            if g in full_groups:
                kt_refs[full_groups.index(g)][1] = _transposed_projection(
                    wt_ref[len(full_groups) + full_groups.index(g)], xn, None, None)
        elif c == 9:
            u_ref[...] = acc
        else:
            gate_ref[:, (c - 10) * GROUP_WIDTH:(c - 9) * GROUP_WIDTH] = acc

    pending = []
    for item in _projection_tiles(xn, w_ref):
        pending.append(item)
        if len(pending) > MATMUL_LOOKAHEAD:
            emit(*pending.pop(0))
    for item in pending:
        emit(*item)


def _rope_tables(pos):
    half = ROT_DIM // 2
    inv_freq = ROPE_THETA ** (-(jnp.arange(half, dtype=F32) / half))
    ang = pos.astype(F32)[:, None] * inv_freq[None, :]
    cos, sin = jnp.cos(ang), jnp.sin(ang)
    n = pos.shape[0]
    rest = HEAD_DIM - ROT_DIM
    cos_h = jnp.concatenate([cos, cos, jnp.ones((n, rest), F32)], axis=1)
    sin_h = jnp.concatenate([-sin, sin, jnp.zeros((n, rest), F32)], axis=1)
    return jnp.tile(cos_h, (1, LANES // HEAD_DIM)), jnp.tile(sin_h, (1, LANES // HEAD_DIM))


def _inproj_in_specs(tm, n_pos_tiles):
    const = lambda i: (0, 0)
    tab = lambda i: (i % n_pos_tiles, 0)
    return [
        pl.BlockSpec((tm, D_MODEL), lambda i: (i, 0)),
        pl.BlockSpec((1, D_MODEL), const),
        pl.BlockSpec((D_MODEL, IN_WIDTH), const, pipeline_mode=pl.Buffered(1)),
        pl.BlockSpec((tm, LANES), tab),
        pl.BlockSpec((tm, LANES), tab),
    ]


def _in_projection_decode(x2d, g, w_bf16, pos):
    n = x2d.shape[0]
    tm = PROJ_TILE
    cos_t, sin_t = _rope_tables(pos)
    row = lambda i: (i, 0)
    widths = (ATTN_WIDTH, 2 * GROUP_WIDTH, 2 * GROUP_WIDTH, 2 * GROUP_WIDTH, SSM_WIDTH, 2 * D_MODEL)
    dtypes = (BF16, F32, F32, F32, F32, F32)
    return pl.pallas_call(
        _inproj_decode_kernel,
        grid=(n // tm,),
        in_specs=_inproj_in_specs(tm, pos.shape[0] // tm),
        out_specs=tuple(pl.BlockSpec((tm, w), row) for w in widths),
        out_shape=tuple(jax.ShapeDtypeStruct((n, w), d) for w, d in zip(widths, dtypes)),
        compiler_params=_params("parallel"),
        name="in_projection_decode",
    )(x2d, g.reshape(1, D_MODEL), w_bf16, cos_t, sin_t)


def _in_projection_prompt(x2d, g, w_bf16, batch, seq):
    n = batch * seq
    tm = PROJ_TILE
    tiles_per_seq = seq // tm
    pos = jnp.arange(seq, dtype=jnp.int32)
    cos_t, sin_t = _rope_tables(pos)
    half = ROT_DIM // 2
    ang_t = (ROPE_THETA ** (-(jnp.arange(half, dtype=F32) / half)))[:, None] * pos.astype(F32)[None, :]
    cos_tr, sin_tr = jnp.cos(ang_t), jnp.sin(ang_t)
    wk_t = jnp.transpose(w_bf16[:, ATTN_WIDTH:2 * ATTN_WIDTH]).reshape(N_ATTN_GROUPS, GROUP_WIDTH, D_MODEL)
    wv_t = jnp.transpose(w_bf16[:, 2 * ATTN_WIDTH:3 * ATTN_WIDTH]).reshape(N_ATTN_GROUPS, GROUP_WIDTH, D_MODEL)
    keeps = tuple(min(win, seq) for win, _ in ATTN_GROUPS)
    full_groups = tuple(i for i, keep in enumerate(keeps) if keep == seq)
    w_full_t = jnp.concatenate([wk_t[jnp.array(full_groups)], wv_t[jnp.array(full_groups)]], axis=0)
    g_row = g.reshape(1, D_MODEL)
    row = lambda i: (i, 0)
    in_specs = _inproj_in_specs(tm, tiles_per_seq) + [
        pl.BlockSpec(w_full_t.shape, lambda i: (0, 0, 0), pipeline_mode=pl.Buffered(1)),
        pl.BlockSpec((half, tm), lambda i: (0, i % tiles_per_seq)),
        pl.BlockSpec((half, tm), lambda i: (0, i % tiles_per_seq)),
    ]
    out_specs, out_shape = [], []
    for width in (GROUP_WIDTH, 2 * GROUP_WIDTH):
        for _, dil in ATTN_GROUPS:
            out_specs.append(pl.BlockSpec((tm // dil, dil * width), row))
            out_shape.append(jax.ShapeDtypeStruct((n // dil, dil * width), BF16))
    for _ in full_groups:
        out_specs.append(pl.BlockSpec((None, 2, GROUP_WIDTH, tm), lambda i: (i // tiles_per_seq, 0, 0, i % tiles_per_seq)))
        out_shape.append(jax.ShapeDtypeStruct((batch, 2, GROUP_WIDTH, seq), F32))
    out_specs += [pl.BlockSpec((tm, SSM_WIDTH), row), pl.BlockSpec((tm, 2 * D_MODEL), row)]
    out_shape += [jax.ShapeDtypeStruct((n, SSM_WIDTH), F32), jax.ShapeDtypeStruct((n, 2 * D_MODEL), F32)]
    outs = pl.pallas_call(
        functools.partial(_inproj_prompt_kernel, full_groups=full_groups),
        grid=(n // tm,),
        in_specs=in_specs,
        out_specs=tuple(out_specs),
        out_shape=tuple(out_shape),
        scratch_shapes=[pltpu.VMEM((GROUP_WIDTH // LANES, tm, LANES), F32)
                        for _ in range(3 * sum(1 for _, d in ATTN_GROUPS if d > 1))],
        compiler_params=_params("parallel"),
        name="in_projection_prompt",
    )(x2d, g_row, w_bf16, cos_t, sin_t, w_full_t, cos_tr, sin_tr)
    n_full = len(full_groups)
    kv_t = dict(zip(full_groups, outs[6:6 + n_full]))
    for grp, keep in enumerate(keeps):
        if grp in full_groups:
            continue
        blk = min(keep, tm)
        first = (seq - keep) // blk
        per_seq = seq // blk
        tail = lambda b, j, first=first: (0, first + j)
        kv_t[grp] = pl.pallas_call(
            _kv_tail_kernel,
            grid=(batch, keep // blk),
            in_specs=[
                pl.BlockSpec((blk, D_MODEL), lambda b, j, first=first, per_seq=per_seq: (b * per_seq + first + j, 0)),
                pl.BlockSpec((1, D_MODEL), lambda b, j: (0, 0)),
                pl.BlockSpec((None, GROUP_WIDTH, D_MODEL), lambda b, j, grp=grp: (grp, 0, 0)),
                pl.BlockSpec((None, GROUP_WIDTH, D_MODEL), lambda b, j, grp=grp: (grp, 0, 0)),
                pl.BlockSpec((half, blk), tail),
                pl.BlockSpec((half, blk), tail),
            ],
            out_specs=pl.BlockSpec((None, 2, GROUP_WIDTH, blk), lambda b, j: (b, 0, 0, j)),
            out_shape=jax.ShapeDtypeStruct((batch, 2, GROUP_WIDTH, keep), F32),
            compiler_params=_params("parallel", "parallel"),
            name=f"kv_tail_g{grp}",
        )(x2d, g_row, wk_t, wv_t, cos_tr, sin_tr)
    return outs[0:3], outs[3:6], tuple(kv_t[i] for i in range(N_ATTN_GROUPS)), outs[6 + n_full], outs[7 + n_full]


def _window_attn_kernel(q_ref, kvc_ref, *rest, windowed, residues, sub_tiles):
    if windowed:
        kvp_ref, o_ref, lse_ref = rest
    else:
        o_ref, lse_ref = rest
    n_keys = 2 * ATTN_TILE if windowed else ATTN_TILE
    rows = lax.broadcasted_iota(jnp.int32, (ATTN_TILE, n_keys), 0)
    cols = lax.broadcasted_iota(jnp.int32, (ATTN_TILE, n_keys), 1)
    if windowed:
        band = (cols >= rows) & (cols <= rows + WINDOW_KEYS)
        first_tile_band = band & ((cols >= ATTN_TILE) | (pl.program_id(2) > 0))
    else:
        band = cols <= rows
    low_head = lax.broadcasted_iota(jnp.int32, (ATTN_TILE, LANES), 1) < HEAD_DIM
    for r in range(residues):
        for s in range(sub_tiles):
            rs = slice(s * ATTN_TILE, (s + 1) * ATTN_TILE)
            k0 = r * 2 * GROUP_WIDTH
            q = q_ref[rs, r * GROUP_WIDTH:(r + 1) * GROUP_WIDTH]
            k = kvc_ref[rs, k0:k0 + GROUP_WIDTH]
            v = kvc_ref[rs, k0 + GROUP_WIDTH:k0 + 2 * GROUP_WIDTH]
            valid = band
            if windowed:
                if s == 0:
                    prev_ref, ps, valid = kvp_ref, slice(0, ATTN_TILE), first_tile_band
                else:
                    prev_ref, ps = kvc_ref, slice((s - 1) * ATTN_TILE, s * ATTN_TILE)
                k = jnp.concatenate([prev_ref[ps, k0:k0 + GROUP_WIDTH], k], axis=0)
                v = jnp.concatenate([prev_ref[ps, k0 + GROUP_WIDTH:k0 + 2 * GROUP_WIDTH], v], axis=0)
            valid2 = jnp.concatenate([valid, valid], axis=0)
            for j in range(GROUP_WIDTH // LANES):
                sl = slice(j * LANES, (j + 1) * LANES)
                osl = slice(r * GROUP_WIDTH + j * LANES, r * GROUP_WIDTH + (j + 1) * LANES)
                qj, kj, vj = q[:, sl], k[:, sl], v[:, sl]
                zero = jnp.zeros_like(qj)
                qm = jnp.concatenate([jnp.where(low_head, qj, zero), jnp.where(low_head, zero, qj)], axis=0)
                sc = lax.dot_general(qm, kj, (((1,), (1,)), ((), ())), preferred_element_type=F32)
                sc = jnp.where(valid2, sc, -jnp.inf)
                m = jnp.max(sc, axis=1, keepdims=True)
                p = jnp.exp(sc - m)
                l = jnp.sum(p, axis=1, keepdims=True)
                o2 = jnp.dot(p.astype(BF16), vj, preferred_element_type=F32) / l
                lse2 = jnp.broadcast_to(m + jnp.log(l), o2.shape)
                o_ref[rs, osl] = jnp.where(low_head, o2[:ATTN_TILE], o2[ATTN_TILE:])
                lse_ref[rs, osl] = jnp.where(low_head, lse2[:ATTN_TILE], lse2[ATTN_TILE:])


def _prompt_attention(q, kv, group, dilation, batch, seq):
    tg = seq // dilation
    n_tiles = tg // ATTN_TILE
    windowed = n_tiles > 1
    residues = min(dilation, ATTN_RESIDUES_PER_STEP)
    sub_tiles = min(n_tiles, ATTN_TILES_PER_STEP // residues)
    rows = sub_tiles * ATTN_TILE
    q3 = q.reshape(batch, tg, dilation * GROUP_WIDTH)
    kv3 = kv.reshape(batch, tg, dilation * 2 * GROUP_WIDTH)
    in_specs = [
        pl.BlockSpec((None, rows, residues * GROUP_WIDTH), lambda b, r, t: (b, t, r)),
        pl.BlockSpec((None, rows, residues * 2 * GROUP_WIDTH), lambda b, r, t: (b, t, r)),
    ]
    args = [q3, kv3]
    if windowed:
        in_specs.append(pl.BlockSpec((None, ATTN_TILE, residues * 2 * GROUP_WIDTH),
                                     lambda b, r, t: (b, jnp.maximum(t * sub_tiles - 1, 0), r)))
        args.append(kv3)
    out_spec = pl.BlockSpec((None, rows, residues * GROUP_WIDTH), lambda b, r, t: (b, t, r))
    out_sds = jax.ShapeDtypeStruct((batch, tg, dilation * GROUP_WIDTH), F32)
    o, lse = pl.pallas_call(
        functools.partial(_window_attn_kernel, windowed=windowed, residues=residues, sub_tiles=sub_tiles),
        grid=(batch, dilation // residues, n_tiles // sub_tiles),
        in_specs=in_specs,
        out_specs=(out_spec, out_spec),
        out_shape=(out_sds, out_sds),
        compiler_params=_params("parallel", "parallel", "arbitrary"),
        name=f"prompt_attention_g{group}",
    )(*args)
    flat = (batch * tg, dilation * GROUP_WIDTH)
    return o.reshape(flat), lse.reshape(flat)


def _sublane_total(x):
    x = x + pltpu.roll(x, 4, 0)
    x = x + pltpu.roll(x, 2, 0)
    return x + pltpu.roll(x, 1, 0)


def _head_sum(prod):
    width = prod.shape[1]
    row = lax.broadcasted_iota(jnp.int32, (HEADS_PER_GROUP, width), 0)
    out = jnp.zeros((HEADS_PER_GROUP, width), F32)
    for h in range(HEADS_PER_GROUP):
        part = prod[h * HEAD_DIM:h * HEAD_DIM + SUBLANES]
        for j in range(1, HEAD_DIM // SUBLANES):
            part = part + prod[h * HEAD_DIM + j * SUBLANES:h * HEAD_DIM + (j + 1) * SUBLANES]
        out = jnp.where(row == h, _sublane_total(part), out)
    return out


def _head_expand(x):
    width = x.shape[1]
    pieces = []
    for h in range(HEADS_PER_GROUP):
        pieces.extend([jnp.broadcast_to(x[h:h + 1, :], (SUBLANES, width))] * (HEAD_DIM // SUBLANES))
    return jnp.concatenate(pieces, axis=0)


def _split_dot(acc, sel, terms=2):
    out, rem = None, acc
    for _ in range(terms):
        hi = rem.astype(BF16)
        part = jnp.dot(hi, sel, preferred_element_type=F32)
        out = part if out is None else out + part
        rem = rem - hi.astype(F32)
    return out


def _decode_attn_kernel(q_ref, n0_ref, n1_ref, n2_ref, c0_ref, c1_ref, c2_ref, o_ref, *, dec_seq):
    nq = SUBLANES
    n_sub = HEAD_DIM // SUBLANES
    neg = -jnp.inf
    q_t = q_ref[...].T
    step = lax.broadcasted_iota(jnp.int32, (HEADS_PER_GROUP, nq), 1)
    step_wide = lax.broadcasted_iota(jnp.int32, (GROUP_WIDTH, nq), 1)
    real_step = step < dec_seq
    new_refs = (n0_ref, n1_ref, n2_ref)
    cache_refs = (c0_ref, c1_ref, c2_ref)

    def column(x, t, width):
        return jnp.broadcast_to(x[:, t:t + 1], (x.shape[0], width))

    def place(cols):
        rows = cols[0].shape[0]
        lane = lax.broadcasted_iota(jnp.int32, (rows, nq), 1)
        out = jnp.zeros((rows, nq), F32)
        for t, c in enumerate(cols):
            out = jnp.where(lane == t, jnp.broadcast_to(c, (rows, nq)), out)
        return out

    outs, lses = [], []
    for g, (win, dil) in enumerate(ATTN_GROUPS):
        c_ref = cache_refs[g]
        kv_new = new_refs[g][...]
        kn_t = kv_new[:, :GROUP_WIDTH].T
        vn_t = kv_new[:, GROUP_WIDTH:].T
        q_g = q_t[g * GROUP_WIDTH:(g + 1) * GROUP_WIDTH, :]
        pos = lax.broadcasted_iota(jnp.int32, (HEADS_PER_GROUP, win), 1)
        row8 = lax.broadcasted_iota(jnp.int32, (HEADS_PER_GROUP, win), 0)
        n_tiles = win // LANES

        def cache_scores(q_pat):
            s = jnp.zeros((HEADS_PER_GROUP, win), F32)
            for h in range(HEADS_PER_GROUP):
                part = None
                for j in range(n_sub):
                    r0 = h * HEAD_DIM + j * SUBLANES
                    qp = q_pat[r0:r0 + SUBLANES, :]
                    term = c_ref[0, r0:r0 + SUBLANES, :] * jnp.concatenate([qp] * n_tiles, axis=1)
                    part = term if part is None else part + term
                s = jnp.where(row8 == h, _sublane_total(part), s)
            return s

        def weighted_values(p):
            rows = []
            for h in range(HEADS_PER_GROUP):
                ph = jnp.broadcast_to(p[h:h + 1, :], (SUBLANES, win))
                for j in range(n_sub):
                    r0 = h * HEAD_DIM + j * SUBLANES
                    prod = ph * c_ref[1, r0:r0 + SUBLANES, :]
                    a = prod[:, :LANES]
                    for tile in range(1, n_tiles):
                        a = a + prod[:, tile * LANES:(tile + 1) * LANES]
                    rows.append(a)
            return jnp.concatenate(rows, axis=0)

        if dil == 1:
            spread = (lax.broadcasted_iota(jnp.int32, (nq, dec_seq * LANES), 0)
                      == lax.broadcasted_iota(jnp.int32, (nq, dec_seq * LANES), 1) // LANES).astype(BF16)
            q_cols = _split_dot(q_g, spread, terms=1)
            kn_cols = _split_dot(kn_t, spread, terms=3)
            vn_cols = _split_dot(vn_t, spread, terms=3)
            s_new = [_head_sum(q_g * kn_cols[:, u * LANES:u * LANES + nq]) for u in range(dec_seq)]
            ok_new = [(step >= u) & real_step for u in range(dec_seq)]
            s_cache = [cache_scores(q_cols[:, t * LANES:(t + 1) * LANES]) for t in range(dec_seq)]
            m = place([jnp.max(jnp.where(pos >= t, s_cache[t], neg), axis=1, keepdims=True) for t in range(dec_seq)])
            for u in range(dec_seq):
                m = jnp.maximum(m, jnp.where(ok_new[u], s_new[u], neg))
            e_cache = [jnp.where(pos >= t, jnp.exp(s_cache[t] - m[:, t:t + 1]), 0.0) for t in range(dec_seq)]
            e_new = [jnp.where(ok_new[u], jnp.exp(s_new[u] - m), 0.0) for u in range(dec_seq)]
            denom = place([jnp.sum(e, axis=1, keepdims=True) for e in e_cache])
            for u in range(dec_seq):
                denom = denom + e_new[u]
            denom = jnp.where(real_step, denom, 1.0)
            inv = 1.0 / denom
            cols = []
            for t in range(dec_seq):
                acc = weighted_values(e_cache[t] * inv[:, t:t + 1])
                cols.append(jnp.sum(acc, axis=1, keepdims=True))
            o_g = place(cols)
            for u in range(dec_seq):
                o_g = o_g + _head_expand(e_new[u] * inv) * vn_cols[:, u * LANES:u * LANES + nq]
        else:
            res = pos % dil
            t_i = lax.broadcasted_iota(jnp.int32, (nq, LANES), 0)
            l_i = lax.broadcasted_iota(jnp.int32, (nq, LANES), 1)
            residue_pat = ((l_i % dil == t_i) & (t_i < dec_seq)).astype(BF16)
            s_cache = cache_scores(_split_dot(q_g, residue_pat, terms=1))
            s_new = jnp.where(real_step, _head_sum(q_g * kn_t), 0.0)
            m_cache = place([jnp.max(jnp.where(res == t, s_cache, neg), axis=1, keepdims=True) for t in range(dec_seq)])
            m = jnp.maximum(m_cache, s_new)

            def by_position(stat):
                out = jnp.zeros((HEADS_PER_GROUP, win), F32)
                for t in range(dec_seq):
                    out = jnp.where(res == t, column(stat, t, win), out)
                return out

            e_cache = jnp.where(res < dec_seq, jnp.exp(s_cache - by_position(m)), 0.0)
            e_new = jnp.exp(s_new - m)
            denom = place([jnp.sum(jnp.where(res == t, e_cache, 0.0), axis=1, keepdims=True)
                           for t in range(dec_seq)]) + e_new
            inv = 1.0 / denom
            acc = weighted_values(e_cache * by_position(inv))
            l_i = lax.broadcasted_iota(jnp.int32, (LANES, LANES), 0)
            t_i = lax.broadcasted_iota(jnp.int32, (LANES, LANES), 1)
            sel = ((l_i % dil == t_i) & (t_i < dec_seq)).astype(BF16)
            o_g = _split_dot(acc, sel)[:, :nq] + _head_expand(e_new * inv) * vn_t
        outs.append(o_g)
        lses.append(m + jnp.log(denom))

    top = jnp.maximum(jnp.maximum(lses[0], lses[1]), lses[2])
    ws = [jnp.exp(l - top) for l in lses]
    total = ws[0] + ws[1] + ws[2]
    merged = jnp.zeros((GROUP_WIDTH, nq), F32)
    for g in range(N_ATTN_GROUPS):
        merged = merged + _head_expand(ws[g] / total) * outs[g]
    o_ref[...] = jnp.where(step_wide < dec_seq, merged, 0.0).T


def _decode_attention(q, caches, new_kv, dec_batch, dec_seq):
    nq = SUBLANES

    def pad_steps(a):
        a = a.astype(F32).reshape(dec_batch, dec_seq, a.shape[-1])
        return jnp.pad(a, ((0, 0), (0, nq - dec_seq), (0, 0)))

    args = [pad_steps(q)] + [pad_steps(n) for n in new_kv]
    in_specs = [pl.BlockSpec((None, nq, ATTN_WIDTH), lambda b: (b, 0, 0))]
    in_specs += [pl.BlockSpec((None, nq, 2 * GROUP_WIDTH), lambda b: (b, 0, 0)) for _ in new_kv]
    for (win, dil), cache in zip(ATTN_GROUPS, caches):
        args.append(jnp.transpose(cache, (0, 2, 3, 4, 1)).reshape(dec_batch, 2, GROUP_WIDTH, win))
        in_specs.append(pl.BlockSpec((None, 2, GROUP_WIDTH, win), lambda b: (b, 0, 0, 0)))
    o_t = pl.pallas_call(
        functools.partial(_decode_attn_kernel, dec_seq=dec_seq),
        grid=(dec_batch,),
        in_specs=in_specs,
        out_specs=pl.BlockSpec((None, nq, GROUP_WIDTH), lambda b: (b, 0, 0)),
        out_shape=jax.ShapeDtypeStruct((dec_batch, nq, GROUP_WIDTH), F32),
        compiler_params=_params("parallel"),
        name="decode_attention",
    )(*args)
    return o_t[:, :dec_seq, :].reshape(dec_batch * dec_seq, GROUP_WIDTH)


def _ssm_kernel(u_ref, h0r_ref, h0i_ref, ar_ref, ai_ref, bw_ref, cr_ref, ci_ref, d_ref,
                y_ref, hr_out, hi_out, bur, bui, hr_s, hi_s, *, nb, steps):
    chunk = pl.program_id(0)
    half_in = SSM_WIDTH // 2
    half_st = SSM_COLS // 2

    @pl.when(chunk == 0)
    def _():
        hr_s[...] = h0r_ref[...]
        hi_s[...] = h0i_ref[...]

    u = u_ref[...]
    ub = u.astype(BF16)
    for hf in range(2):
        r = jnp.dot(ub[:, hf * half_in:(hf + 1) * half_in], bw_ref[hf], preferred_element_type=F32)
        bur[:, hf * half_st:(hf + 1) * half_st] = r[:, :half_st]
        bui[:, hf * half_st:(hf + 1) * half_st] = r[:, half_st:]

    lane_chunk = 4 * LANES

    def sub_batch(s, carry):
        for lc in range(SSM_COLS // lane_chunk):
            ls = slice(lc * lane_chunk, (lc + 1) * lane_chunk)
            ar = ar_ref[:, ls]
            ai = ai_ref[:, ls]
            row0 = pl.multiple_of(s * SUBLANES, SUBLANES)

            def step(t, h):
                hr, hi = h
                row = pl.multiple_of(t * nb + s * SUBLANES, SUBLANES)
                nhr = ar * hr - ai * hi + bur[pl.ds(row, SUBLANES), ls]
                nhi = ar * hi + ai * hr + bui[pl.ds(row, SUBLANES), ls]
                bur[pl.ds(row, SUBLANES), ls] = nhr
                bui[pl.ds(row, SUBLANES), ls] = nhi
                return nhr, nhi

            hr, hi = lax.fori_loop(0, steps, step, (hr_s[pl.ds(row0, SUBLANES), ls], hi_s[pl.ds(row0, SUBLANES), ls]))
            hr_s[pl.ds(row0, SUBLANES), ls] = hr
            hi_s[pl.ds(row0, SUBLANES), ls] = hi
        return carry

    lax.fori_loop(0, nb // SUBLANES, sub_batch, 0)

    for hf in range(2):
        ss = slice(hf * half_st, (hf + 1) * half_st)
        y = jnp.dot(bur[:, ss].astype(BF16), cr_ref[hf], preferred_element_type=F32)
        y = y + jnp.dot(bui[:, ss].astype(BF16), ci_ref[hf], preferred_element_type=F32)
        cs = slice(hf * half_in, (hf + 1) * half_in)
        y_ref[:, cs] = y + d_ref[:, cs] * u[:, cs]

    @pl.when(chunk == pl.num_programs(0) - 1)
    def _():
        hr_out[...] = hr_s[...]
        hi_out[...] = hi_s[...]


def _ssm_weights(ssm_log_dt, a_re, a_im, b_re, b_im, c_re, c_im):
    dt = jnp.exp(ssm_log_dt.astype(F32))[:, None]
    lam = lax.complex(a_re.astype(F32), a_im.astype(F32))
    lam_bar = jnp.exp(lam * dt)
    b_bar = ((lam_bar - 1.0) / lam)[:, :, None] * lax.complex(b_re.astype(F32), b_im.astype(F32))
    gh = N_SSM_GROUPS // 2
    eye = jnp.eye(gh, dtype=F32)

    def block_diag(m):
        return jnp.einsum('gab,gh->gahb', m, eye).reshape(gh * m.shape[1], gh * m.shape[2])

    bw, cr, ci = [], [], []
    for hf in range(2):
        gs = slice(hf * gh, (hf + 1) * gh)
        b_t = jnp.transpose(b_bar[gs], (0, 2, 1))
        bw.append(jnp.concatenate([block_diag(jnp.real(b_t)), block_diag(jnp.imag(b_t))], axis=1))
        cr.append(block_diag(jnp.transpose(c_re[gs].astype(F32), (0, 2, 1))))
        ci.append(block_diag(jnp.transpose(-c_im[gs].astype(F32), (0, 2, 1))))
    ar = jnp.broadcast_to(jnp.real(lam_bar).reshape(1, SSM_COLS), (SUBLANES, SSM_COLS))
    ai = jnp.broadcast_to(jnp.imag(lam_bar).reshape(1, SSM_COLS), (SUBLANES, SSM_COLS))
    return ar, ai, jnp.stack(bw).astype(BF16), jnp.stack(cr).astype(BF16), jnp.stack(ci).astype(BF16)


def _ssm(u_tm, h0r, h0i, weights, d_flat, nb):
    ar, ai, bw, cr, ci = weights
    rows = u_tm.shape[0]
    steps = SSM_ROWS // nb
    const2 = lambda c: (0, 0)
    const3 = lambda c: (0, 0, 0)
    state_sds = jax.ShapeDtypeStruct((nb, SSM_COLS), F32)
    return pl.pallas_call(
        functools.partial(_ssm_kernel, nb=nb, steps=steps),
        grid=(rows // SSM_ROWS,),
        in_specs=[
            pl.BlockSpec((SSM_ROWS, SSM_WIDTH), lambda c: (c, 0)),
            pl.BlockSpec((nb, SSM_COLS), const2),
            pl.BlockSpec((nb, SSM_COLS), const2),
            pl.BlockSpec((SUBLANES, SSM_COLS), const2),
            pl.BlockSpec((SUBLANES, SSM_COLS), const2),
            pl.BlockSpec(bw.shape, const3),
            pl.BlockSpec(cr.shape, const3),
            pl.BlockSpec(ci.shape, const3),
            pl.BlockSpec((1, SSM_WIDTH), const2),
        ],
        out_specs=(
            pl.BlockSpec((SSM_ROWS, SSM_WIDTH), lambda c: (c, 0)),
            pl.BlockSpec((nb, SSM_COLS), const2),
            pl.BlockSpec((nb, SSM_COLS), const2),
        ),
        out_shape=(jax.ShapeDtypeStruct((rows, SSM_WIDTH), F32), state_sds, state_sds),
        scratch_shapes=[
            pltpu.VMEM((SSM_ROWS, SSM_COLS), F32),
            pltpu.VMEM((SSM_ROWS, SSM_COLS), F32),
            pltpu.VMEM((nb, SSM_COLS), F32),
            pltpu.VMEM((nb, SSM_COLS), F32),
        ],
        compiler_params=_params("arbitrary"),
        name="s5_scan",
    )(u_tm, h0r, h0i, ar, ai, bw, cr, ci, d_flat)


def _mix_kernel(*refs, merged_attn):
    if merged_attn:
        attn_ref = refs[0]
        rest = refs[1:13]
    else:
        group_refs = refs[0:2 * N_ATTN_GROUPS]
        rest = refs[2 * N_ATTN_GROUPS:2 * N_ATTN_GROUPS + 12]
        scratch = refs[2 * N_ATTN_GROUPS + 12:]
    (y_ref, gate_ref, x_ref, wa_ref, wglu_ref, wout_ref, gffn_ref, wr_ref, br_ref,
     h_ref, hn_ref, route_ref) = rest
    tm = x_ref.shape[0]
    if merged_attn:
        attn = attn_ref[...]
    else:
        natural = []
        for idx, ref in enumerate(group_refs):
            dil = ATTN_GROUPS[idx % N_ATTN_GROUPS][1]
            if dil == 1:
                natural.append(ref[...])
                continue
            scr = scratch[idx]
            n_chunks = GROUP_WIDTH // LANES
            for r in range(dil):
                for j in range(n_chunks):
                    col = r * GROUP_WIDTH + j * LANES
                    scr[j, pl.ds(r, tm // dil, stride=dil), :] = ref[:, col:col + LANES]
            natural.append(jnp.concatenate([scr[j] for j in range(n_chunks)], axis=1))
        os, ls = natural[:N_ATTN_GROUPS], natural[N_ATTN_GROUPS:]
        top = jnp.maximum(jnp.maximum(ls[0], ls[1]), ls[2])
        es = [jnp.exp(l - top) for l in ls]
        attn = (es[0] * os[0] + es[1] * os[1] + es[2] * os[2]) / (es[0] + es[1] + es[2])
    attn_out = jnp.dot(attn.astype(BF16), wa_ref[...], preferred_element_type=F32)
    glu = jnp.dot(jax.nn.gelu(y_ref[...]).astype(BF16), wglu_ref[...], preferred_element_type=F32)
    ssm_out = glu[:, :D_MODEL] * jax.nn.sigmoid(glu[:, D_MODEL:])
    merged = jax.nn.sigmoid(gate_ref[:, :D_MODEL]) * attn_out + jax.nn.sigmoid(gate_ref[:, D_MODEL:]) * ssm_out
    h = x_ref[...] + jnp.dot(merged.astype(BF16), wout_ref[...], preferred_element_type=F32)
    h_ref[...] = h
    var = jnp.mean(h * h, axis=-1, keepdims=True)
    hn = h * lax.rsqrt(var + NORM_EPS) * gffn_ref[...]
    _to_token_tiles(hn_ref, hn)
    hn_hi = hn.astype(BF16)
    hn_lo = (hn - hn_hi.astype(F32)).astype(BF16)
    w_hi = wr_ref[:, :LANES]
    w_lo = wr_ref[:, LANES:]
    logits = (jnp.dot(hn_hi, w_hi, preferred_element_type=F32) + jnp.dot(hn_hi, w_lo, preferred_element_type=F32)
              + jnp.dot(hn_lo, w_hi, preferred_element_type=F32)) + br_ref[...]
    lane = lax.broadcasted_iota(jnp.int32, logits.shape, 1).astype(F32)
    far = float(LANES)

    def first_argmax(vals):
        top_v = jnp.max(vals, axis=1, keepdims=True)
        return top_v, jnp.min(jnp.where(vals == top_v, lane, far), axis=1, keepdims=True)

    group_logits = jnp.where(lane < N_EXPERT_GROUPS, logits, -jnp.inf)
    g_top, g_idx = first_argmax(group_logits)
    p_group = 1.0 / jnp.sum(jnp.exp(group_logits - g_top), axis=1, keepdims=True)
    first_lane = N_EXPERT_GROUPS + g_idx * EXPERTS_PER_GROUP
    in_group = (lane >= first_lane) & (lane < first_lane + EXPERTS_PER_GROUP)
    expert_logits = jnp.where(in_group, logits, -jnp.inf)
    v1, i1 = first_argmax(expert_logits)
    v2, i2 = first_argmax(jnp.where(lane == i1, -jnp.inf, expert_logits))
    e2 = jnp.exp(v2 - v1)
    w1 = p_group / (1.0 + e2)
    w2 = p_group * e2 / (1.0 + e2)
    route = jnp.where(lane == 0, i1 - N_EXPERT_GROUPS,
                      jnp.where(lane == 1, i2 - N_EXPERT_GROUPS,
                                jnp.where(lane == 2, w1, jnp.where(lane == 3, w2, 0.0))))
    route_ref[...] = route


def _mix(attn_inputs, y, gates, x2d, wa, wglu, wout, gffn, wr, br):
    n = x2d.shape[0]
    tm = PROJ_TILE
    row = lambda i: (i, 0)
    const = lambda i: (0, 0)
    merged_attn = len(attn_inputs) == 1
    in_specs = [pl.BlockSpec((tm * GROUP_WIDTH // a.shape[1], a.shape[1]), row) for a in attn_inputs]
    scratch = [] if merged_attn else [pltpu.VMEM((GROUP_WIDTH // LANES, tm, LANES), F32) for _ in attn_inputs]
    in_specs += [
        pl.BlockSpec((tm, SSM_WIDTH), row),
        pl.BlockSpec((tm, 2 * D_MODEL), row),
        pl.BlockSpec((tm, D_MODEL), row),
        pl.BlockSpec(wa.shape, const),
        pl.BlockSpec(wglu.shape, const),
        pl.BlockSpec(wout.shape, const),
        pl.BlockSpec((1, D_MODEL), const),
        pl.BlockSpec(wr.shape, const),
        pl.BlockSpec((1, LANES), const),
    ]
    return pl.pallas_call(
        functools.partial(_mix_kernel, merged_attn=merged_attn),
        grid=(n // tm,),
        in_specs=in_specs,
        out_specs=(pl.BlockSpec((tm, D_MODEL), row), pl.BlockSpec((tm * TOKEN_TILE_ROWS, LANES), row),
                   pl.BlockSpec((tm, LANES), row)),
        out_shape=(jax.ShapeDtypeStruct((n, D_MODEL), F32), jax.ShapeDtypeStruct((n * TOKEN_TILE_ROWS, LANES), F32),
                   jax.ShapeDtypeStruct((n, LANES), F32)),
        scratch_shapes=scratch,
        compiler_params=_params("parallel"),
        name="branch_mix",
    )(*attn_inputs, y, gates, x2d, wa, wglu, wout, gffn, wr, br)


def _to_token_tiles(ref, val):
    rows = val.shape[0]
    for j in range(TOKEN_TILE_ROWS):
        ref[pl.ds(j, rows, stride=TOKEN_TILE_ROWS), :] = val[:, j * LANES:(j + 1) * LANES]


def _from_token_tiles(ref):
    rows = ref.shape[0] // TOKEN_TILE_ROWS
    return jnp.concatenate([ref[pl.ds(j, rows, stride=TOKEN_TILE_ROWS), :] for j in range(TOKEN_TILE_ROWS)], axis=1)


def _token_copy(src, src_tok, dst, dst_tok, sem):
    src_rows = pl.ds(pl.multiple_of(src_tok * TOKEN_TILE_ROWS, TOKEN_TILE_ROWS), TOKEN_TILE_ROWS)
    dst_rows = pl.ds(pl.multiple_of(dst_tok * TOKEN_TILE_ROWS, TOKEN_TILE_ROWS), TOKEN_TILE_ROWS)
    return pltpu.make_async_copy(src.at[src_rows], dst.at[dst_rows], sem)


def _dispatch_kernel(pad_end_ref, padded_ref, n_used_ref, slot_ref, *refs, set_steps):
    hn_refs = refs[:len(set_steps)]
    xs_hbm, zero_buf, sem, zero_sem = refs[len(set_steps):]
    block_rows = MOE_BLOCK * TOKEN_TILE_ROWS
    n_blocks = xs_hbm.shape[0] // block_rows

    @pl.when(pl.program_id(0) == 0)
    def _():
        zero_buf[...] = jnp.zeros_like(zero_buf)

        def block_copy(b):
            start = pl.multiple_of(b * block_rows, block_rows)
            return pltpu.make_async_copy(zero_buf, xs_hbm.at[pl.ds(start, block_rows)], zero_sem)

        def segment_tails(action):
            def body(e, carry):
                @pl.when(padded_ref[e] > 0)
                def _():
                    action(block_copy(pad_end_ref[e] // MOE_BLOCK - 1))
                return carry
            lax.fori_loop(0, N_EXPERTS, body, 0)

        def unused_blocks(action):
            def body(b, carry):
                @pl.when(b >= n_used_ref[0])
                def _():
                    action(block_copy(b))
                return carry
            lax.fori_loop(0, n_blocks, body, 0)

        segment_tails(lambda cp: cp.start())
        unused_blocks(lambda cp: cp.start())
        segment_tails(lambda cp: cp.wait())
        unused_blocks(lambda cp: cp.wait())

    first = 0
    for hn_ref, steps in zip(hn_refs, set_steps):
        @pl.when((pl.program_id(0) >= first) & (pl.program_id(0) < first + steps))
        def _(hn_ref=hn_ref):
            _scatter_rows(slot_ref, hn_ref, xs_hbm, sem)
        first += steps


def _scatter_rows(slot_ref, hn_ref, xs_hbm, sem):
    def issue(j, carry):
        _token_copy(hn_ref, j, xs_hbm, slot_ref[2 * j], sem).start(priority=0)
        _token_copy(hn_ref, j, xs_hbm, slot_ref[2 * j + 1], sem).start(priority=1)
        return carry

    def drain(j, carry):
        _token_copy(hn_ref, 0, xs_hbm, 0, sem).wait()
        _token_copy(hn_ref, 0, xs_hbm, 0, sem).wait()
        return carry

    lax.fori_loop(0, ROW_TILE, issue, 0, unroll=DMA_UNROLL)
    lax.fori_loop(0, ROW_TILE, drain, 0, unroll=DMA_UNROLL)


def _dispatch(pad_end, padded, n_used, slots, hns, n_slots):
    set_steps = tuple(hn.shape[0] // (TOKEN_TILE_ROWS * ROW_TILE) for hn in hns)
    hn_specs, first = [], 0
    for steps in set_steps:
        hn_specs.append(pl.BlockSpec(
            (ROW_TILE * TOKEN_TILE_ROWS, LANES),
            lambda i, *_, first=first, steps=steps: (jnp.clip(i - first, 0, steps - 1), 0)))
        first += steps
    return pl.pallas_call(
        functools.partial(_dispatch_kernel, set_steps=set_steps),
        grid_spec=pltpu.PrefetchScalarGridSpec(
            num_scalar_prefetch=3,
            grid=(sum(set_steps),),
            in_specs=[pl.BlockSpec((2 * ROW_TILE,), lambda i, *_: (i,), memory_space=pltpu.SMEM)] + hn_specs,
            out_specs=pl.BlockSpec(memory_space=pl.ANY),
            scratch_shapes=[pltpu.VMEM((MOE_BLOCK * TOKEN_TILE_ROWS, LANES), F32), pltpu.SemaphoreType.DMA(()),
                            pltpu.SemaphoreType.DMA(())],
        ),
        out_shape=jax.ShapeDtypeStruct((n_slots * TOKEN_TILE_ROWS, LANES), F32),
        compiler_params=_params("arbitrary"),
        name="moe_dispatch",
    )(pad_end, padded, n_used, slots, *hns)


def _expert_kernel(block_e_ref, n_used_ref, xs_ref, wg_ref, wu_ref, wd_ref, yb_ref):
    del block_e_ref
    i = pl.program_id(0)

    @pl.when(i < n_used_ref[0])
    def _():
        xb = _from_token_tiles(xs_ref).astype(BF16)
        gate = jnp.dot(xb, wg_ref[...].astype(BF16), preferred_element_type=F32)
        up = jnp.dot(xb, wu_ref[...].astype(BF16), preferred_element_type=F32)
        hmid = (jax.nn.silu(gate) * up).astype(BF16)
        _to_token_tiles(yb_ref, jnp.dot(hmid, wd_ref[...].astype(BF16), preferred_element_type=F32))

    @pl.when(i >= n_used_ref[0])
    def _():
        yb_ref[...] = jnp.zeros_like(yb_ref)


def _experts(block_e, n_used, xs, wg, wu, wd):
    block_rows = MOE_BLOCK * TOKEN_TILE_ROWS
    n_blocks = xs.shape[0] // block_rows
    grid_spec = pltpu.PrefetchScalarGridSpec(
        num_scalar_prefetch=2,
        grid=(n_blocks,),
        in_specs=[
            pl.BlockSpec((block_rows, LANES), lambda i, be, nu: (jnp.minimum(i, nu[0] - 1), 0)),
            pl.BlockSpec((None, D_MODEL, D_EXPERT), lambda i, be, nu: (be[i], 0, 0)),
            pl.BlockSpec((None, D_MODEL, D_EXPERT), lambda i, be, nu: (be[i], 0, 0)),
            pl.BlockSpec((None, D_EXPERT, D_MODEL), lambda i, be, nu: (be[i], 0, 0)),
        ],
        out_specs=pl.BlockSpec((block_rows, LANES), lambda i, be, nu: (i, 0)),
    )
    return pl.pallas_call(
        _expert_kernel,
        grid_spec=grid_spec,
        out_shape=jax.ShapeDtypeStruct(xs.shape, F32),
        compiler_params=_params("arbitrary"),
        name="moe_experts",
    )(block_e, n_used, xs, wg, wu, wd)


def _combine_kernel(slot_ref, next_slot_ref, h_ref, route_ref, g_ref, yb_hbm, out_ref, buf_a, buf_b, sem, *, steps):
    step = pl.program_id(0)
    cur = step % 2

    def gather(slots, buf_idx):
        def body(j, carry):
            _token_copy(yb_hbm, slots[2 * j], buf_a.at[buf_idx], j, sem.at[buf_idx]).start(priority=0)
            _token_copy(yb_hbm, slots[2 * j + 1], buf_b.at[buf_idx], j, sem.at[buf_idx]).start(priority=1)
            return carry
        lax.fori_loop(0, ROW_TILE, body, 0, unroll=DMA_UNROLL)

    @pl.when(step == 0)
    def _():
        gather(slot_ref, 0)

    if steps > 1:
        @pl.when(step + 1 < steps)
        def _():
            gather(next_slot_ref, 1 - cur)

    def drain(j, carry):
        _token_copy(yb_hbm, 0, buf_a.at[cur], 0, sem.at[cur]).wait()
        _token_copy(yb_hbm, 0, buf_b.at[cur], 0, sem.at[cur]).wait()
        return carry

    lax.fori_loop(0, ROW_TILE, drain, 0, unroll=DMA_UNROLL)
    route = route_ref[...]
    h = h_ref[...] + (route[:, 2:3] * _from_token_tiles(buf_a.at[cur]) + route[:, 3:4] * _from_token_tiles(buf_b.at[cur]))
    var = jnp.mean(h * h, axis=-1, keepdims=True)
    out_ref[...] = h * lax.rsqrt(var + NORM_EPS) * g_ref[...]


def _combine(slots, h, route, g_final, yb):
    n = h.shape[0]
    steps = n // ROW_TILE
    row = lambda i: (i, 0)
    return pl.pallas_call(
        functools.partial(_combine_kernel, steps=steps),
        grid=(steps,),
        in_specs=[
            pl.BlockSpec((2 * ROW_TILE,), lambda i: (i,), memory_space=pltpu.SMEM),
            pl.BlockSpec((2 * ROW_TILE,), lambda i: (jnp.minimum(i + 1, steps - 1),), memory_space=pltpu.SMEM),
            pl.BlockSpec((ROW_TILE, D_MODEL), row),
            pl.BlockSpec((ROW_TILE, LANES), row),
            pl.BlockSpec((1, D_MODEL), lambda i: (0, 0)),
            pl.BlockSpec(memory_space=pl.ANY),
        ],
        out_specs=pl.BlockSpec((ROW_TILE, D_MODEL), row),
        out_shape=jax.ShapeDtypeStruct((n, D_MODEL), F32),
        scratch_shapes=[
            pltpu.VMEM((2, ROW_TILE * TOKEN_TILE_ROWS, LANES), F32),
            pltpu.VMEM((2, ROW_TILE * TOKEN_TILE_ROWS, LANES), F32),
            pltpu.SemaphoreType.DMA((2,)),
        ],
        compiler_params=_params("arbitrary"),
        name="moe_combine",
    )(slots, slots, h, route, g_final.reshape(1, D_MODEL), yb)


def _slot_assignment(route, n_blocks):
    flat_e = route.astype(jnp.int32).reshape(-1)
    onehot = (flat_e[:, None] == jnp.arange(N_EXPERTS, dtype=jnp.int32)[None, :]).astype(jnp.int32)
    running = jnp.cumsum(onehot, axis=0)
    rank = jnp.sum(onehot * running, axis=1) - 1
    counts = running[-1]
    padded = ((counts + MOE_BLOCK - 1) // MOE_BLOCK) * MOE_BLOCK
    pad_end = jnp.cumsum(padded)
    pad_start = pad_end - padded
    slots = jnp.sum(onehot * pad_start[None, :], axis=1) + rank
    block_start = jnp.arange(n_blocks, dtype=jnp.int32) * MOE_BLOCK
    block_e = jnp.minimum(jnp.sum((pad_end[None, :] <= block_start[:, None]).astype(jnp.int32), axis=1), N_EXPERTS - 1)
    n_used = (pad_end[-1:] // MOE_BLOCK).astype(jnp.int32)
    return slots.astype(jnp.int32), block_e, n_used, pad_end.astype(jnp.int32), padded.astype(jnp.int32)


def _moe_and_final_norm(hs, hns, routes, wg, wu, wd, g_final):
    counts = [h.shape[0] for h in hs]
    n = sum(counts)
    n_blocks = (2 * n) // MOE_BLOCK + N_EXPERTS
    slots, block_e, n_used, pad_end, padded = _slot_assignment(jnp.concatenate([r[:, 0:2] for r in routes], axis=0), n_blocks)
    starts = [2 * sum(counts[:k]) for k in range(len(counts))]
    set_slots = [slots[s:s + 2 * c] for s, c in zip(starts, counts)]
    xs = _dispatch(pad_end, padded, n_used, slots, hns, n_blocks * MOE_BLOCK)
    yb = _experts(block_e, n_used, xs, wg, wu, wd)
    return [_combine(set_slots[k], hs[k], routes[k], g_final, yb) for k in range(len(hs))]


def kernel(x_prompt, x_sample, cache_kv_w128, cache_kv_w512, cache_kv_w2048, state_ssm, g_attn_norm, w_in, ssm_log_dt, ssm_a_re, ssm_a_im, ssm_b_re, ssm_b_im, ssm_c_re, ssm_c_im, ssm_d, w_glu, w_attn_branch, w_out, g_ffn_norm, w_router_group, b_router_group, w_router_expert, b_router_expert, w_exp_gate, w_exp_up, w_exp_down, g_final):
    batch, seq, _ = x_prompt.shape
    dec_batch, dec_seq, _ = x_sample.shape
    past_len = cache_kv_w2048.shape[2]
    layer = 0

    w_in_b = w_in[layer].astype(BF16)
    wa = w_attn_branch[layer].astype(BF16)
    wglu = w_glu[layer].astype(BF16)
    wout = w_out[layer].astype(BF16)
    wg, wu, wd = w_exp_gate[layer], w_exp_up[layer], w_exp_down[layer]
    gffn = g_ffn_norm[layer].reshape(1, D_MODEL)
    pad = LANES - N_EXPERT_GROUPS - N_EXPERTS
    wr_f32 = jnp.concatenate([w_router_group[layer], w_router_expert[layer], jnp.zeros((D_MODEL, pad), F32)], axis=1)
    wr_hi = wr_f32.astype(BF16)
    wr = jnp.concatenate([wr_hi, (wr_f32 - wr_hi.astype(F32)).astype(BF16)], axis=1)
    br = jnp.concatenate([b_router_group[layer], b_router_expert[layer], jnp.zeros((pad,), F32)]).reshape(1, LANES)
    ssm_w = _ssm_weights(ssm_log_dt[layer], ssm_a_re[layer], ssm_a_im[layer], ssm_b_re[layer], ssm_b_im[layer],
                         ssm_c_re[layer], ssm_c_im[layer])
    d_flat = ssm_d[layer].astype(F32).reshape(1, SSM_WIDTH)

    def time_major(a, nb, steps):
        return jnp.transpose(a.reshape(nb, steps, -1), (1, 0, 2)).reshape(nb * steps, -1)

    def batch_major(a, nb, steps):
        return jnp.transpose(a.reshape(steps, nb, -1), (1, 0, 2)).reshape(nb * steps, -1)

    xp = x_prompt.reshape(batch * seq, D_MODEL)
    q_p, kv_p, kv_t_p, u_p, gates_p = _in_projection_prompt(xp, g_attn_norm[layer], w_in_b, batch, seq)
    attn_in = []
    lse_in = []
    for g, (win, dil) in enumerate(ATTN_GROUPS):
        o, lse = _prompt_attention(q_p[g], kv_p[g], g, dil, batch, seq)
        attn_in.append(o)
        lse_in.append(lse)
    zeros_state = jnp.zeros((batch, SSM_COLS), F32)
    y_tm, hr_p, hi_p = _ssm(time_major(u_p, batch, seq), zeros_state, zeros_state, ssm_w, d_flat, batch)
    y_p = batch_major(y_tm, batch, seq)
    h_p, hn_p, route_p = _mix(attn_in + lse_in, y_p, gates_p, xp, wa, wglu, wout, gffn, wr, br)

    xs = x_sample.reshape(dec_batch * dec_seq, D_MODEL)
    pos_s = past_len + (jnp.arange(dec_batch * dec_seq, dtype=jnp.int32) % dec_seq)
    q_s, kv0_s, kv1_s, kv2_s, u_s, gates_s = _in_projection_decode(xs, g_attn_norm[layer], w_in_b, pos_s)
    kv_s = (kv0_s, kv1_s, kv2_s)
    caches = (cache_kv_w128[layer], cache_kv_w512[layer], cache_kv_w2048[layer])
    attn_s = _decode_attention(q_s, caches, kv_s, dec_batch, dec_seq)
    st = state_ssm[layer].astype(F32).reshape(dec_batch, SSM_COLS, 2)
    ys_tm, hr_s, hi_s = _ssm(time_major(u_s, dec_batch, dec_seq), st[:, :, 0], st[:, :, 1], ssm_w, d_flat, dec_batch)
    y_s = batch_major(ys_tm, dec_batch, dec_seq)
    h_s, hn_s, route_s = _mix([attn_s], y_s, gates_s, xs, wa, wglu, wout, gffn, wr, br)

    out_p, out_s = _moe_and_final_norm([h_p, h_s], [hn_p, hn_s], [route_p, route_s], wg, wu, wd, g_final)

    kv_tail = (2, HEADS_PER_GROUP, HEAD_DIM)
    outs = [out_p.reshape(batch, seq, D_MODEL), out_s.reshape(dec_batch, dec_seq, D_MODEL)]
    for g, (win, dil) in enumerate(ATTN_GROUPS):
        keep = min(win, seq)
        rows_p = jnp.transpose(kv_t_p[g].reshape(batch, 2, HEADS_PER_GROUP, HEAD_DIM, keep), (0, 4, 1, 2, 3))
        outs.append(rows_p[None])
        outs.append(kv_s[g].reshape((1, dec_batch, dec_seq) + kv_tail))
    outs.append(jnp.stack([hr_p, hi_p], axis=-1).reshape(1, batch, N_SSM_GROUPS, SSM_STATE, 2))
    outs.append(jnp.stack([hr_s, hi_s], axis=-1).reshape(1, dec_batch, N_SSM_GROUPS, SSM_STATE, 2))
    return tuple(outs)
```

```python
import functools
import math

import jax
import jax.numpy as jnp
from jax import lax
from jax.experimental import pallas as pl
from jax.experimental.pallas import tpu as pltpu

F32 = jnp.float32
BF16 = jnp.bfloat16

D_MODEL = 1024
HEAD_DIM = 64
HEADS_PER_GROUP = 8
GROUP_WIDTH = HEADS_PER_GROUP * HEAD_DIM
ATTN_GROUPS = ((128, 1), (512, 4), (2048, 16))
N_ATTN_GROUPS = len(ATTN_GROUPS)
ATTN_WIDTH = N_ATTN_GROUPS * GROUP_WIDTH
ROT_DIM = HEAD_DIM // 4
ROPE_THETA = 500000.0
WINDOW_KEYS = 128
SSM_GROUP_CH = 16
SSM_WIDTH = D_MODEL // 2
N_SSM_GROUPS = SSM_WIDTH // SSM_GROUP_CH
SSM_STATE = 64
SSM_COLS = N_SSM_GROUPS * SSM_STATE
IN_WIDTH = 3 * ATTN_WIDTH + SSM_WIDTH + 2 * D_MODEL
N_EXPERT_GROUPS = 4
EXPERTS_PER_GROUP = 8
N_EXPERTS = N_EXPERT_GROUPS * EXPERTS_PER_GROUP
D_EXPERT = D_MODEL // 4
NORM_EPS = 1e-6

LANES = 128
SUBLANES = 8
TOKEN_TILE_ROWS = D_MODEL // LANES
VMEM_LIMIT = 56 * 1024 * 1024

PROJ_TILE = 256
MATMUL_LOOKAHEAD = 1
ATTN_TILE = 128
ATTN_RESIDUES_PER_STEP = 4
ATTN_TILES_PER_STEP = 8
SSM_ROWS = 512
MOE_BLOCK = 256
ROW_TILE = 512
DMA_UNROLL = 8


def _params(*sem):
    return pltpu.CompilerParams(dimension_semantics=sem, vmem_limit_bytes=VMEM_LIMIT)


def _normed_input(x_ref, g_ref):
    x = x_ref[...]
    var = jnp.mean(x * x, axis=-1, keepdims=True)
    return (x * lax.rsqrt(var + NORM_EPS) * g_ref[...]).astype(BF16)


def _rope_fn(cos_ref, sin_ref):
    cos = cos_ref[...]
    sin = sin_ref[...]
    lane = lax.broadcasted_iota(jnp.int32, cos.shape, 1) % HEAD_DIM
    first_half = lane < ROT_DIM // 2
    rotated = lane < ROT_DIM

    def rope_chunk(c):
        partner = jnp.where(first_half, pltpu.roll(c, LANES - ROT_DIM // 2, 1), pltpu.roll(c, ROT_DIM // 2, 1))
        return jnp.where(rotated, c * cos + partner * sin, c)

    def rope(t):
        return jnp.concatenate([rope_chunk(t[:, j * LANES:(j + 1) * LANES]) for j in range(GROUP_WIDTH // LANES)], axis=1)

    return rope


def _projection_tiles(xn, w_ref):
    for c in range(IN_WIDTH // GROUP_WIDTH):
        yield c, jnp.dot(xn, w_ref[:, c * GROUP_WIDTH:(c + 1) * GROUP_WIDTH], preferred_element_type=F32)


def _inproj_decode_kernel(x_ref, g_ref, w_ref, cos_ref, sin_ref, q_ref, kv0_ref, kv1_ref, kv2_ref, u_ref, gate_ref):
    rope = _rope_fn(cos_ref, sin_ref)
    kv_refs = (kv0_ref, kv1_ref, kv2_ref)
    for c, acc in _projection_tiles(_normed_input(x_ref, g_ref), w_ref):
        if c < 3:
            q_ref[:, c * GROUP_WIDTH:(c + 1) * GROUP_WIDTH] = (rope(acc) * (HEAD_DIM ** -0.5)).astype(BF16)
        elif c < 6:
            kv_refs[c - 3][:, :GROUP_WIDTH] = rope(acc)
        elif c < 9:
            kv_refs[c - 6][:, GROUP_WIDTH:] = acc
        elif c == 9:
            u_ref[...] = acc
        else:
            gate_ref[:, (c - 10) * GROUP_WIDTH:(c - 9) * GROUP_WIDTH] = acc


def _transposed_projection(wt, xn, cos_t, sin_t):
    t = lax.dot_general(wt, xn, (((1,), (1,)), ((), ())), preferred_element_type=F32)
    if cos_t is None:
        return t
    half_rot = ROT_DIM // 2
    pieces = []
    for h in range(HEADS_PER_GROUP):
        x1 = t[h * HEAD_DIM:h * HEAD_DIM + half_rot]
        x2 = t[h * HEAD_DIM + half_rot:h * HEAD_DIM + ROT_DIM]
        pieces += [x1 * cos_t - x2 * sin_t, x2 * cos_t + x1 * sin_t, t[h * HEAD_DIM + ROT_DIM:(h + 1) * HEAD_DIM]]
    return jnp.concatenate(pieces, axis=0)


def _kv_tail_kernel(x_ref, g_ref, wk_ref, wv_ref, cos_t_ref, sin_t_ref, kt_ref):
    xn = _normed_input(x_ref, g_ref)
    kt_ref[0] = _transposed_projection(wk_ref[...], xn, cos_t_ref[...], sin_t_ref[...])
    kt_ref[1] = _transposed_projection(wv_ref[...], xn, None, None)


def _inproj_prompt_kernel(x_ref, g_ref, w_ref, cos_ref, sin_ref, wt_ref, cos_t_ref, sin_t_ref, *refs, full_groups):
    q_refs, kv_refs = refs[0:N_ATTN_GROUPS], refs[N_ATTN_GROUPS:2 * N_ATTN_GROUPS]
    kt_refs = refs[2 * N_ATTN_GROUPS:2 * N_ATTN_GROUPS + len(full_groups)]
    u_ref, gate_ref = refs[2 * N_ATTN_GROUPS + len(full_groups):2 * N_ATTN_GROUPS + len(full_groups) + 2]
    scratch = refs[2 * N_ATTN_GROUPS + len(full_groups) + 2:]
    rope = _rope_fn(cos_ref, sin_ref)
    tm = x_ref.shape[0]

    dilated = [g for g, (_, d) in enumerate(ATTN_GROUPS) if d > 1]

    def store_rows(dst_ref, val, kind, g, col0, col_stride):
        dil = ATTN_GROUPS[g][1]
        if dil == 1:
            dst_ref[:, col0:col0 + GROUP_WIDTH] = val.astype(BF16)
            return
        scr = scratch[kind * len(dilated) + dilated.index(g)]
        n_chunks = GROUP_WIDTH // LANES
        for j in range(n_chunks):
            scr[j] = val[:, j * LANES:(j + 1) * LANES]
        rows = tm // dil
        for r in range(dil):
            piece = jnp.concatenate([scr[j, pl.ds(r, rows, stride=dil), :] for j in range(n_chunks)], axis=1)
            dst_ref[:, col0 + r * col_stride:col0 + r * col_stride + GROUP_WIDTH] = piece.astype(BF16)

    xn = _normed_input(x_ref, g_ref)

    def emit(c, acc):
        g = c % N_ATTN_GROUPS
        if c < 3:
            store_rows(q_refs[g], rope(acc) * (HEAD_DIM ** -0.5), 0, g, 0, GROUP_WIDTH)
        elif c < 6:
            store_rows(kv_refs[g], rope(acc), 1, g, 0, 2 * GROUP_WIDTH)
            if g in full_groups:
                kt_refs[full_groups.index(g)][0] = _transposed_projection(
                    wt_ref[full_groups.index(g)], xn, cos_t_ref[...], sin_t_ref[...])
        elif c < 9:
            store_rows(kv_refs[g], acc, 2, g, GROUP_WIDTH, 2 * GROUP_WIDTH)
            if g in full_groups:
                kt_refs[full_groups.index(g)][1] = _transposed_projection(
                    wt_ref[len(full_groups) + full_groups.index(g)], xn, None, None)
        elif c == 9:
            u_ref[...] = acc
        else:
            gate_ref[:, (c - 10) * GROUP_WIDTH:(c - 9) * GROUP_WIDTH] = acc

    pending = []
    for item in _projection_tiles(xn, w_ref):
        pending.append(item)
        if len(pending) > MATMUL_LOOKAHEAD:
            emit(*pending.pop(0))
    for item in pending:
        emit(*item)


def _rope_tables(pos):
    half = ROT_DIM // 2
    inv_freq = ROPE_THETA ** (-(jnp.arange(half, dtype=F32) / half))
    ang = pos.astype(F32)[:, None] * inv_freq[None, :]
    cos, sin = jnp.cos(ang), jnp.sin(ang)
    n = pos.shape[0]
    rest = HEAD_DIM - ROT_DIM
    cos_h = jnp.concatenate([cos, cos, jnp.ones((n, rest), F32)], axis=1)
    sin_h = jnp.concatenate([-sin, sin, jnp.zeros((n, rest), F32)], axis=1)
    return jnp.tile(cos_h, (1, LANES // HEAD_DIM)), jnp.tile(sin_h, (1, LANES // HEAD_DIM))


def _inproj_in_specs(tm, n_pos_tiles):
    const = lambda i: (0, 0)
    tab = lambda i: (i % n_pos_tiles, 0)
    return [
        pl.BlockSpec((tm, D_MODEL), lambda i: (i, 0)),
        pl.BlockSpec((1, D_MODEL), const),
        pl.BlockSpec((D_MODEL, IN_WIDTH), const, pipeline_mode=pl.Buffered(1)),
        pl.BlockSpec((tm, LANES), tab),
        pl.BlockSpec((tm, LANES), tab),
    ]


def _in_projection_decode(x2d, g, w_bf16, pos):
    n = x2d.shape[0]
    tm = PROJ_TILE
    cos_t, sin_t = _rope_tables(pos)
    row = lambda i: (i, 0)
    widths = (ATTN_WIDTH, 2 * GROUP_WIDTH, 2 * GROUP_WIDTH, 2 * GROUP_WIDTH, SSM_WIDTH, 2 * D_MODEL)
    dtypes = (BF16, F32, F32, F32, F32, F32)
    return pl.pallas_call(
        _inproj_decode_kernel,
        grid=(n // tm,),
        in_specs=_inproj_in_specs(tm, pos.shape[0] // tm),
        out_specs=tuple(pl.BlockSpec((tm, w), row) for w in widths),
        out_shape=tuple(jax.ShapeDtypeStruct((n, w), d) for w, d in zip(widths, dtypes)),
        compiler_params=_params("parallel"),
        name="in_projection_decode",
    )(x2d, g.reshape(1, D_MODEL), w_bf16, cos_t, sin_t)


def _in_projection_prompt(x2d, g, w_bf16, batch, seq):
    n = batch * seq
    tm = PROJ_TILE
    tiles_per_seq = seq // tm
    pos = jnp.arange(seq, dtype=jnp.int32)
    cos_t, sin_t = _rope_tables(pos)
    half = ROT_DIM // 2
    ang_t = (ROPE_THETA ** (-(jnp.arange(half, dtype=F32) / half)))[:, None] * pos.astype(F32)[None, :]
    cos_tr, sin_tr = jnp.cos(ang_t), jnp.sin(ang_t)
    wk_t = jnp.transpose(w_bf16[:, ATTN_WIDTH:2 * ATTN_WIDTH]).reshape(N_ATTN_GROUPS, GROUP_WIDTH, D_MODEL)
    wv_t = jnp.transpose(w_bf16[:, 2 * ATTN_WIDTH:3 * ATTN_WIDTH]).reshape(N_ATTN_GROUPS, GROUP_WIDTH, D_MODEL)
    keeps = tuple(min(win, seq) for win, _ in ATTN_GROUPS)
    full_groups = tuple(i for i, keep in enumerate(keeps) if keep == seq)
    w_full_t = jnp.concatenate([wk_t[jnp.array(full_groups)], wv_t[jnp.array(full_groups)]], axis=0)
    g_row = g.reshape(1, D_MODEL)
    row = lambda i: (i, 0)
    in_specs = _inproj_in_specs(tm, tiles_per_seq) + [
        pl.BlockSpec(w_full_t.shape, lambda i: (0, 0, 0), pipeline_mode=pl.Buffered(1)),
        pl.BlockSpec((half, tm), lambda i: (0, i % tiles_per_seq)),
        pl.BlockSpec((half, tm), lambda i: (0, i % tiles_per_seq)),
    ]
    out_specs, out_shape = [], []
    for width in (GROUP_WIDTH, 2 * GROUP_WIDTH):
        for _, dil in ATTN_GROUPS:
            out_specs.append(pl.BlockSpec((tm // dil, dil * width), row))
            out_shape.append(jax.ShapeDtypeStruct((n // dil, dil * width), BF16))
    for _ in full_groups:
        out_specs.append(pl.BlockSpec((None, 2, GROUP_WIDTH, tm), lambda i: (i // tiles_per_seq, 0, 0, i % tiles_per_seq)))
        out_shape.append(jax.ShapeDtypeStruct((batch, 2, GROUP_WIDTH, seq), F32))
    out_specs += [pl.BlockSpec((tm, SSM_WIDTH), row), pl.BlockSpec((tm, 2 * D_MODEL), row)]
    out_shape += [jax.ShapeDtypeStruct((n, SSM_WIDTH), F32), jax.ShapeDtypeStruct((n, 2 * D_MODEL), F32)]
    outs = pl.pallas_call(
        functools.partial(_inproj_prompt_kernel, full_groups=full_groups),
        grid=(n // tm,),
        in_specs=in_specs,
        out_specs=tuple(out_specs),
        out_shape=tuple(out_shape),
        scratch_shapes=[pltpu.VMEM((GROUP_WIDTH // LANES, tm, LANES), F32)
                        for _ in range(3 * sum(1 for _, d in ATTN_GROUPS if d > 1))],
        compiler_params=_params("parallel"),
        name="in_projection_prompt",
    )(x2d, g_row, w_bf16, cos_t, sin_t, w_full_t, cos_tr, sin_tr)
    n_full = len(full_groups)
    kv_t = dict(zip(full_groups, outs[6:6 + n_full]))
    for grp, keep in enumerate(keeps):
        if grp in full_groups:
            continue
        blk = min(keep, tm)
        first = (seq - keep) // blk
        per_seq = seq // blk
        tail = lambda b, j, first=first: (0, first + j)
        kv_t[grp] = pl.pallas_call(
            _kv_tail_kernel,
            grid=(batch, keep // blk),
            in_specs=[
                pl.BlockSpec((blk, D_MODEL), lambda b, j, first=first, per_seq=per_seq: (b * per_seq + first + j, 0)),
                pl.BlockSpec((1, D_MODEL), lambda b, j: (0, 0)),
                pl.BlockSpec((None, GROUP_WIDTH, D_MODEL), lambda b, j, grp=grp: (grp, 0, 0)),
                pl.BlockSpec((None, GROUP_WIDTH, D_MODEL), lambda b, j, grp=grp: (grp, 0, 0)),
                pl.BlockSpec((half, blk), tail),
                pl.BlockSpec((half, blk), tail),
            ],
            out_specs=pl.BlockSpec((None, 2, GROUP_WIDTH, blk), lambda b, j: (b, 0, 0, j)),
            out_shape=jax.ShapeDtypeStruct((batch, 2, GROUP_WIDTH, keep), F32),
            compiler_params=_params("parallel", "parallel"),
            name=f"kv_tail_g{grp}",
        )(x2d, g_row, wk_t, wv_t, cos_tr, sin_tr)
    return outs[0:3], outs[3:6], tuple(kv_t[i] for i in range(N_ATTN_GROUPS)), outs[6 + n_full], outs[7 + n_full]


def _window_attn_kernel(q_ref, kvc_ref, *rest, windowed, residues, sub_tiles):
    if windowed:
        kvp_ref, o_ref, lse_ref = rest
    else:
        o_ref, lse_ref = rest
    n_keys = 2 * ATTN_TILE if windowed else ATTN_TILE
    rows = lax.broadcasted_iota(jnp.int32, (ATTN_TILE, n_keys), 0)
    cols = lax.broadcasted_iota(jnp.int32, (ATTN_TILE, n_keys), 1)
    if windowed:
        band = (cols >= rows) & (cols <= rows + WINDOW_KEYS)
        first_tile_band = band & ((cols >= ATTN_TILE) | (pl.program_id(2) > 0))
    else:
        band = cols <= rows
    low_head = lax.broadcasted_iota(jnp.int32, (ATTN_TILE, LANES), 1) < HEAD_DIM
    for r in range(residues):
        for s in range(sub_tiles):
            rs = slice(s * ATTN_TILE, (s + 1) * ATTN_TILE)
            k0 = r * 2 * GROUP_WIDTH
            q = q_ref[rs, r * GROUP_WIDTH:(r + 1) * GROUP_WIDTH]
            k = kvc_ref[rs, k0:k0 + GROUP_WIDTH]
            v = kvc_ref[rs, k0 + GROUP_WIDTH:k0 + 2 * GROUP_WIDTH]
            valid = band
            if windowed:
                if s == 0:
                    prev_ref, ps, valid = kvp_ref, slice(0, ATTN_TILE), first_tile_band
                else:
                    prev_ref, ps = kvc_ref, slice((s - 1) * ATTN_TILE, s * ATTN_TILE)
                k = jnp.concatenate([prev_ref[ps, k0:k0 + GROUP_WIDTH], k], axis=0)
                v = jnp.concatenate([prev_ref[ps, k0 + GROUP_WIDTH:k0 + 2 * GROUP_WIDTH], v], axis=0)
            valid2 = jnp.concatenate([valid, valid], axis=0)
            for j in range(GROUP_WIDTH // LANES):
                sl = slice(j * LANES, (j + 1) * LANES)
                osl = slice(r * GROUP_WIDTH + j * LANES, r * GROUP_WIDTH + (j + 1) * LANES)
                qj, kj, vj = q[:, sl], k[:, sl], v[:, sl]
                zero = jnp.zeros_like(qj)
                qm = jnp.concatenate([jnp.where(low_head, qj, zero), jnp.where(low_head, zero, qj)], axis=0)
                sc = lax.dot_general(qm, kj, (((1,), (1,)), ((), ())), preferred_element_type=F32)
                sc = jnp.where(valid2, sc, -jnp.inf)
                m = jnp.max(sc, axis=1, keepdims=True)
                p = jnp.exp(sc - m)
                l = jnp.sum(p, axis=1, keepdims=True)
                o2 = jnp.dot(p.astype(BF16), vj, preferred_element_type=F32) / l
                lse2 = jnp.broadcast_to(m + jnp.log(l), o2.shape)
                o_ref[rs, osl] = jnp.where(low_head, o2[:ATTN_TILE], o2[ATTN_TILE:])
                lse_ref[rs, osl] = jnp.where(low_head, lse2[:ATTN_TILE], lse2[ATTN_TILE:])


def _prompt_attention(q, kv, group, dilation, batch, seq):
    tg = seq // dilation
    n_tiles = tg // ATTN_TILE
    windowed = n_tiles > 1
    residues = min(dilation, ATTN_RESIDUES_PER_STEP)
    sub_tiles = min(n_tiles, ATTN_TILES_PER_STEP // residues)
    rows = sub_tiles * ATTN_TILE
    q3 = q.reshape(batch, tg, dilation * GROUP_WIDTH)
    kv3 = kv.reshape(batch, tg, dilation * 2 * GROUP_WIDTH)
    in_specs = [
        pl.BlockSpec((None, rows, residues * GROUP_WIDTH), lambda b, r, t: (b, t, r)),
        pl.BlockSpec((None, rows, residues * 2 * GROUP_WIDTH), lambda b, r, t: (b, t, r)),
    ]
    args = [q3, kv3]
    if windowed:
        in_specs.append(pl.BlockSpec((None, ATTN_TILE, residues * 2 * GROUP_WIDTH),
                                     lambda b, r, t: (b, jnp.maximum(t * sub_tiles - 1, 0), r)))
        args.append(kv3)
    out_spec = pl.BlockSpec((None, rows, residues * GROUP_WIDTH), lambda b, r, t: (b, t, r))
    out_sds = jax.ShapeDtypeStruct((batch, tg, dilation * GROUP_WIDTH), F32)
    o, lse = pl.pallas_call(
        functools.partial(_window_attn_kernel, windowed=windowed, residues=residues, sub_tiles=sub_tiles),
        grid=(batch, dilation // residues, n_tiles // sub_tiles),
        in_specs=in_specs,
        out_specs=(out_spec, out_spec),
        out_shape=(out_sds, out_sds),
        compiler_params=_params("parallel", "parallel", "arbitrary"),
        name=f"prompt_attention_g{group}",
    )(*args)
    flat = (batch * tg, dilation * GROUP_WIDTH)
    return o.reshape(flat), lse.reshape(flat)


def _sublane_total(x):
    x = x + pltpu.roll(x, 4, 0)
    x = x + pltpu.roll(x, 2, 0)
    return x + pltpu.roll(x, 1, 0)


def _head_sum(prod):
    width = prod.shape[1]
    row = lax.broadcasted_iota(jnp.int32, (HEADS_PER_GROUP, width), 0)
    out = jnp.zeros((HEADS_PER_GROUP, width), F32)
    for h in range(HEADS_PER_GROUP):
        part = prod[h * HEAD_DIM:h * HEAD_DIM + SUBLANES]
        for j in range(1, HEAD_DIM // SUBLANES):
            part = part + prod[h * HEAD_DIM + j * SUBLANES:h * HEAD_DIM + (j + 1) * SUBLANES]
        out = jnp.where(row == h, _sublane_total(part), out)
    return out


def _head_expand(x):
    width = x.shape[1]
    pieces = []
    for h in range(HEADS_PER_GROUP):
        pieces.extend([jnp.broadcast_to(x[h:h + 1, :], (SUBLANES, width))] * (HEAD_DIM // SUBLANES))
    return jnp.concatenate(pieces, axis=0)


def _split_dot(acc, sel, terms=2):
    out, rem = None, acc
    for _ in range(terms):
        hi = rem.astype(BF16)
        part = jnp.dot(hi, sel, preferred_element_type=F32)
        out = part if out is None else out + part
        rem = rem - hi.astype(F32)
    return out


def _decode_attn_kernel(q_ref, n0_ref, n1_ref, n2_ref, c0_ref, c1_ref, c2_ref, o_ref, *, dec_seq):
    nq = SUBLANES
    n_sub = HEAD_DIM // SUBLANES
    neg = -jnp.inf
    q_t = q_ref[...].T
    step = lax.broadcasted_iota(jnp.int32, (HEADS_PER_GROUP, nq), 1)
    step_wide = lax.broadcasted_iota(jnp.int32, (GROUP_WIDTH, nq), 1)
    real_step = step < dec_seq
    new_refs = (n0_ref, n1_ref, n2_ref)
    cache_refs = (c0_ref, c1_ref, c2_ref)

    def column(x, t, width):
        return jnp.broadcast_to(x[:, t:t + 1], (x.shape[0], width))

    def place(cols):
        rows = cols[0].shape[0]
        lane = lax.broadcasted_iota(jnp.int32, (rows, nq), 1)
        out = jnp.zeros((rows, nq), F32)
        for t, c in enumerate(cols):
            out = jnp.where(lane == t, jnp.broadcast_to(c, (rows, nq)), out)
        return out

    outs, lses = [], []
    for g, (win, dil) in enumerate(ATTN_GROUPS):
        c_ref = cache_refs[g]
        kv_new = new_refs[g][...]
        kn_t = kv_new[:, :GROUP_WIDTH].T
        vn_t = kv_new[:, GROUP_WIDTH:].T
        q_g = q_t[g * GROUP_WIDTH:(g + 1) * GROUP_WIDTH, :]
        pos = lax.broadcasted_iota(jnp.int32, (HEADS_PER_GROUP, win), 1)
        row8 = lax.broadcasted_iota(jnp.int32, (HEADS_PER_GROUP, win), 0)
        n_tiles = win // LANES

        def cache_scores(q_pat):
            s = jnp.zeros((HEADS_PER_GROUP, win), F32)
            for h in range(HEADS_PER_GROUP):
                part = None
                for j in range(n_sub):
                    r0 = h * HEAD_DIM + j * SUBLANES
                    qp = q_pat[r0:r0 + SUBLANES, :]
                    term = c_ref[0, r0:r0 + SUBLANES, :] * jnp.concatenate([qp] * n_tiles, axis=1)
                    part = term if part is None else part + term
                s = jnp.where(row8 == h, _sublane_total(part), s)
            return s

        def weighted_values(p):
            rows = []
            for h in range(HEADS_PER_GROUP):
                ph = jnp.broadcast_to(p[h:h + 1, :], (SUBLANES, win))
                for j in range(n_sub):
                    r0 = h * HEAD_DIM + j * SUBLANES
                    prod = ph * c_ref[1, r0:r0 + SUBLANES, :]
                    a = prod[:, :LANES]
                    for tile in range(1, n_tiles):
                        a = a + prod[:, tile * LANES:(tile + 1) * LANES]
                    rows.append(a)
            return jnp.concatenate(rows, axis=0)

        if dil == 1:
            spread = (lax.broadcasted_iota(jnp.int32, (nq, dec_seq * LANES), 0)
                      == lax.broadcasted_iota(jnp.int32, (nq, dec_seq * LANES), 1) // LANES).astype(BF16)
            q_cols = _split_dot(q_g, spread, terms=1)
            kn_cols = _split_dot(kn_t, spread, terms=3)
            vn_cols = _split_dot(vn_t, spread, terms=3)
            s_new = [_head_sum(q_g * kn_cols[:, u * LANES:u * LANES + nq]) for u in range(dec_seq)]
            ok_new = [(step >= u) & real_step for u in range(dec_seq)]
            s_cache = [cache_scores(q_cols[:, t * LANES:(t + 1) * LANES]) for t in range(dec_seq)]
            m = place([jnp.max(jnp.where(pos >= t, s_cache[t], neg), axis=1, keepdims=True) for t in range(dec_seq)])
            for u in range(dec_seq):
                m = jnp.maximum(m, jnp.where(ok_new[u], s_new[u], neg))
            e_cache = [jnp.where(pos >= t, jnp.exp(s_cache[t] - m[:, t:t + 1]), 0.0) for t in range(dec_seq)]
            e_new = [jnp.where(ok_new[u], jnp.exp(s_new[u] - m), 0.0) for u in range(dec_seq)]
            denom = place([jnp.sum(e, axis=1, keepdims=True) for e in e_cache])
            for u in range(dec_seq):
                denom = denom + e_new[u]
            denom = jnp.where(real_step, denom, 1.0)
            inv = 1.0 / denom
            cols = []
            for t in range(dec_seq):
                acc = weighted_values(e_cache[t] * inv[:, t:t + 1])
                cols.append(jnp.sum(acc, axis=1, keepdims=True))
            o_g = place(cols)
            for u in range(dec_seq):
                o_g = o_g + _head_expand(e_new[u] * inv) * vn_cols[:, u * LANES:u * LANES + nq]
        else:
            res = pos % dil
            t_i = lax.broadcasted_iota(jnp.int32, (nq, LANES), 0)
            l_i = lax.broadcasted_iota(jnp.int32, (nq, LANES), 1)
            residue_pat = ((l_i % dil == t_i) & (t_i < dec_seq)).astype(BF16)
            s_cache = cache_scores(_split_dot(q_g, residue_pat, terms=1))
            s_new = jnp.where(real_step, _head_sum(q_g * kn_t), 0.0)
            m_cache = place([jnp.max(jnp.where(res == t, s_cache, neg), axis=1, keepdims=True) for t in range(dec_seq)])
            m = jnp.maximum(m_cache, s_new)

            def by_position(stat):
                out = jnp.zeros((HEADS_PER_GROUP, win), F32)
                for t in range(dec_seq):
                    out = jnp.where(res == t, column(stat, t, win), out)
                return out

            e_cache = jnp.where(res < dec_seq, jnp.exp(s_cache - by_position(m)), 0.0)
            e_new = jnp.exp(s_new - m)
            denom = place([jnp.sum(jnp.where(res == t, e_cache, 0.0), axis=1, keepdims=True)
                           for t in range(dec_seq)]) + e_new
            inv = 1.0 / denom
            acc = weighted_values(e_cache * by_position(inv))
            l_i = lax.broadcasted_iota(jnp.int32, (LANES, LANES), 0)
            t_i = lax.broadcasted_iota(jnp.int32, (LANES, LANES), 1)
            sel = ((l_i % dil == t_i) & (t_i < dec_seq)).astype(BF16)
            o_g = _split_dot(acc, sel)[:, :nq] + _head_expand(e_new * inv) * vn_t
        outs.append(o_g)
        lses.append(m + jnp.log(denom))

    top = jnp.maximum(jnp.maximum(lses[0], lses[1]), lses[2])
    ws = [jnp.exp(l - top) for l in lses]
    total = ws[0] + ws[1] + ws[2]
    merged = jnp.zeros((GROUP_WIDTH, nq), F32)
    for g in range(N_ATTN_GROUPS):
        merged = merged + _head_expand(ws[g] / total) * outs[g]
    o_ref[...] = jnp.where(step_wide < dec_seq, merged, 0.0).T


def _decode_attention(q, caches, new_kv, dec_batch, dec_seq):
    nq = SUBLANES

    def pad_steps(a):
        a = a.astype(F32).reshape(dec_batch, dec_seq, a.shape[-1])
        return jnp.pad(a, ((0, 0), (0, nq - dec_seq), (0, 0)))

    args = [pad_steps(q)] + [pad_steps(n) for n in new_kv]
    in_specs = [pl.BlockSpec((None, nq, ATTN_WIDTH), lambda b: (b, 0, 0))]
    in_specs += [pl.BlockSpec((None, nq, 2 * GROUP_WIDTH), lambda b: (b, 0, 0)) for _ in new_kv]
    for (win, dil), cache in zip(ATTN_GROUPS, caches):
        args.append(jnp.transpose(cache, (0, 2, 3, 4, 1)).reshape(dec_batch, 2, GROUP_WIDTH, win))
        in_specs.append(pl.BlockSpec((None, 2, GROUP_WIDTH, win), lambda b: (b, 0, 0, 0)))
    o_t = pl.pallas_call(
        functools.partial(_decode_attn_kernel, dec_seq=dec_seq),
        grid=(dec_batch,),
        in_specs=in_specs,
        out_specs=pl.BlockSpec((None, nq, GROUP_WIDTH), lambda b: (b, 0, 0)),
        out_shape=jax.ShapeDtypeStruct((dec_batch, nq, GROUP_WIDTH), F32),
        compiler_params=_params("parallel"),
        name="decode_attention",
    )(*args)
    return o_t[:, :dec_seq, :].reshape(dec_batch * dec_seq, GROUP_WIDTH)


def _ssm_kernel(u_ref, h0r_ref, h0i_ref, ar_ref, ai_ref, bw_ref, cr_ref, ci_ref, d_ref,
                y_ref, hr_out, hi_out, bur0, bui0, bur1, bui1, hr_s, hi_s, *, nb, steps):
    chunk = pl.program_id(0)
    half_in = SSM_WIDTH // 2
    half_st = SSM_COLS // 2
    lane_chunk = 4 * LANES
    bu_refs = ((bur0, bui0), (bur1, bui1))

    @pl.when(chunk == 0)
    def _():
        hr_s[...] = h0r_ref[...]
        hi_s[...] = h0i_ref[...]

    u = u_ref[...]
    ub = u.astype(BF16)

    def project(hf):
        r = jnp.dot(ub[:, hf * half_in:(hf + 1) * half_in], bw_ref[hf], preferred_element_type=F32)
        bu_refs[hf][0][...] = r[:, :half_st]
        bu_refs[hf][1][...] = r[:, half_st:]

    def recur(hf):
        br, bi = bu_refs[hf]
        for s in range(nb // SUBLANES):
            rs = slice(s * SUBLANES, (s + 1) * SUBLANES)
            for lc in range(half_st // lane_chunk):
                ls = slice(lc * lane_chunk, (lc + 1) * lane_chunk)
                gs = slice(hf * half_st + lc * lane_chunk, hf * half_st + (lc + 1) * lane_chunk)
                ar = ar_ref[:, gs]
                ai = ai_ref[:, gs]
                hr, hi = hr_s[rs, gs], hi_s[rs, gs]
                for t in range(steps):
                    row = slice(t * nb + s * SUBLANES, t * nb + (s + 1) * SUBLANES)
                    hr, hi = ar * hr - ai * hi + br[row, ls], ar * hi + ai * hr + bi[row, ls]
                    br[row, ls] = hr
                    bi[row, ls] = hi
                hr_s[rs, gs] = hr
                hi_s[rs, gs] = hi

    def readout(hf):
        br, bi = bu_refs[hf]
        y = jnp.dot(br[...].astype(BF16), cr_ref[hf], preferred_element_type=F32)
        y = y + jnp.dot(bi[...].astype(BF16), ci_ref[hf], preferred_element_type=F32)
        cs = slice(hf * half_in, (hf + 1) * half_in)
        y_ref[:, cs] = y + d_ref[:, cs] * u[:, cs]

    project(0)
    project(1)
    recur(0)
    readout(0)
    recur(1)
    readout(1)

    @pl.when(chunk == pl.num_programs(0) - 1)
    def _():
        hr_out[...] = hr_s[...]
        hi_out[...] = hi_s[...]


def _ssm_weights(ssm_log_dt, a_re, a_im, b_re, b_im, c_re, c_im):
    dt = jnp.exp(ssm_log_dt.astype(F32))[:, None]
    lam = lax.complex(a_re.astype(F32), a_im.astype(F32))
    lam_bar = jnp.exp(lam * dt)
    b_bar = ((lam_bar - 1.0) / lam)[:, :, None] * lax.complex(b_re.astype(F32), b_im.astype(F32))
    gh = N_SSM_GROUPS // 2
    eye = jnp.eye(gh, dtype=F32)

    def block_diag(m):
        return jnp.einsum('gab,gh->gahb', m, eye).reshape(gh * m.shape[1], gh * m.shape[2])

    bw, cr, ci = [], [], []
    for hf in range(2):
        gs = slice(hf * gh, (hf + 1) * gh)
        b_t = jnp.transpose(b_bar[gs], (0, 2, 1))
        bw.append(jnp.concatenate([block_diag(jnp.real(b_t)), block_diag(jnp.imag(b_t))], axis=1))
        cr.append(block_diag(jnp.transpose(c_re[gs].astype(F32), (0, 2, 1))))
        ci.append(block_diag(jnp.transpose(-c_im[gs].astype(F32), (0, 2, 1))))
    ar = jnp.broadcast_to(jnp.real(lam_bar).reshape(1, SSM_COLS), (SUBLANES, SSM_COLS))
    ai = jnp.broadcast_to(jnp.imag(lam_bar).reshape(1, SSM_COLS), (SUBLANES, SSM_COLS))
    return ar, ai, jnp.stack(bw).astype(BF16), jnp.stack(cr).astype(BF16), jnp.stack(ci).astype(BF16)


def _ssm(u_tm, h0r, h0i, weights, d_flat, nb):
    ar, ai, bw, cr, ci = weights
    rows = u_tm.shape[0]
    steps = SSM_ROWS // nb
    const2 = lambda c: (0, 0)
    const3 = lambda c: (0, 0, 0)
    state_sds = jax.ShapeDtypeStruct((nb, SSM_COLS), F32)
    return pl.pallas_call(
        functools.partial(_ssm_kernel, nb=nb, steps=steps),
        grid=(rows // SSM_ROWS,),
        in_specs=[
            pl.BlockSpec((SSM_ROWS, SSM_WIDTH), lambda c: (c, 0)),
            pl.BlockSpec((nb, SSM_COLS), const2),
            pl.BlockSpec((nb, SSM_COLS), const2),
            pl.BlockSpec((SUBLANES, SSM_COLS), const2),
            pl.BlockSpec((SUBLANES, SSM_COLS), const2),
            pl.BlockSpec(bw.shape, const3),
            pl.BlockSpec(cr.shape, const3),
            pl.BlockSpec(ci.shape, const3),
            pl.BlockSpec((1, SSM_WIDTH), const2),
        ],
        out_specs=(
            pl.BlockSpec((SSM_ROWS, SSM_WIDTH), lambda c: (c, 0)),
            pl.BlockSpec((nb, SSM_COLS), const2),
            pl.BlockSpec((nb, SSM_COLS), const2),
        ),
        out_shape=(jax.ShapeDtypeStruct((rows, SSM_WIDTH), F32), state_sds, state_sds),
        scratch_shapes=[
            pltpu.VMEM((SSM_ROWS, SSM_COLS // 2), F32),
            pltpu.VMEM((SSM_ROWS, SSM_COLS // 2), F32),
            pltpu.VMEM((SSM_ROWS, SSM_COLS // 2), F32),
            pltpu.VMEM((SSM_ROWS, SSM_COLS // 2), F32),
            pltpu.VMEM((nb, SSM_COLS), F32),
            pltpu.VMEM((nb, SSM_COLS), F32),
        ],
        compiler_params=_params("arbitrary"),
        name="s5_scan",
    )(u_tm, h0r, h0i, ar, ai, bw, cr, ci, d_flat)


def _mix_kernel(*refs, merged_attn):
    if merged_attn:
        attn_ref = refs[0]
        rest = refs[1:13]
    else:
        group_refs = refs[0:2 * N_ATTN_GROUPS]
        rest = refs[2 * N_ATTN_GROUPS:2 * N_ATTN_GROUPS + 12]
        scratch = refs[2 * N_ATTN_GROUPS + 12:]
    (y_ref, gate_ref, x_ref, wa_ref, wglu_ref, wout_ref, gffn_ref, wr_ref, br_ref,
     h_ref, hn_ref, route_ref) = rest
    tm = x_ref.shape[0]
    if merged_attn:
        attn = attn_ref[...]
    else:
        natural = []
        for idx, ref in enumerate(group_refs):
            dil = ATTN_GROUPS[idx % N_ATTN_GROUPS][1]
            if dil == 1:
                natural.append(ref[...])
                continue
            scr = scratch[idx]
            n_chunks = GROUP_WIDTH // LANES
            for r in range(dil):
                for j in range(n_chunks):
                    col = r * GROUP_WIDTH + j * LANES
                    scr[j, pl.ds(r, tm // dil, stride=dil), :] = ref[:, col:col + LANES]
            natural.append(jnp.concatenate([scr[j] for j in range(n_chunks)], axis=1))
        os, ls = natural[:N_ATTN_GROUPS], natural[N_ATTN_GROUPS:]
        top = jnp.maximum(jnp.maximum(ls[0], ls[1]), ls[2])
        es = [jnp.exp(l - top) for l in ls]
        attn = (es[0] * os[0] + es[1] * os[1] + es[2] * os[2]) / (es[0] + es[1] + es[2])
    attn_out = jnp.dot(attn.astype(BF16), wa_ref[...], preferred_element_type=F32)
    glu = jnp.dot(jax.nn.gelu(y_ref[...]).astype(BF16), wglu_ref[...], preferred_element_type=F32)
    ssm_out = glu[:, :D_MODEL] * jax.nn.sigmoid(glu[:, D_MODEL:])
    merged = jax.nn.sigmoid(gate_ref[:, :D_MODEL]) * attn_out + jax.nn.sigmoid(gate_ref[:, D_MODEL:]) * ssm_out
    h = x_ref[...] + jnp.dot(merged.astype(BF16), wout_ref[...], preferred_element_type=F32)
    h_ref[...] = h
    var = jnp.mean(h * h, axis=-1, keepdims=True)
    hn = h * lax.rsqrt(var + NORM_EPS) * gffn_ref[...]
    _to_token_tiles(hn_ref, hn)
    hn_hi = hn.astype(BF16)
    hn_lo = (hn - hn_hi.astype(F32)).astype(BF16)
    w_hi = wr_ref[:, :LANES]
    w_lo = wr_ref[:, LANES:]
    logits = (jnp.dot(hn_hi, w_hi, preferred_element_type=F32) + jnp.dot(hn_hi, w_lo, preferred_element_type=F32)
              + jnp.dot(hn_lo, w_hi, preferred_element_type=F32)) + br_ref[...]
    lane = lax.broadcasted_iota(jnp.int32, logits.shape, 1).astype(F32)
    far = float(LANES)

    def first_argmax(vals):
        top_v = jnp.max(vals, axis=1, keepdims=True)
        return top_v, jnp.min(jnp.where(vals == top_v, lane, far), axis=1, keepdims=True)

    group_logits = jnp.where(lane < N_EXPERT_GROUPS, logits, -jnp.inf)
    g_top, g_idx = first_argmax(group_logits)
    p_group = 1.0 / jnp.sum(jnp.exp(group_logits - g_top), axis=1, keepdims=True)
    first_lane = N_EXPERT_GROUPS + g_idx * EXPERTS_PER_GROUP
    in_group = (lane >= first_lane) & (lane < first_lane + EXPERTS_PER_GROUP)
    expert_logits = jnp.where(in_group, logits, -jnp.inf)
    v1, i1 = first_argmax(expert_logits)
    v2, i2 = first_argmax(jnp.where(lane == i1, -jnp.inf, expert_logits))
    e2 = jnp.exp(v2 - v1)
    w1 = p_group / (1.0 + e2)
    w2 = p_group * e2 / (1.0 + e2)
    route = jnp.where(lane == 0, i1 - N_EXPERT_GROUPS,
                      jnp.where(lane == 1, i2 - N_EXPERT_GROUPS,
                                jnp.where(lane == 2, w1, jnp.where(lane == 3, w2, 0.0))))
    route_ref[...] = route


def _mix(attn_inputs, y, gates, x2d, wa, wglu, wout, gffn, wr, br):
    n = x2d.shape[0]
    tm = PROJ_TILE
    row = lambda i: (i, 0)
    const = lambda i: (0, 0)
    merged_attn = len(attn_inputs) == 1
    in_specs = [pl.BlockSpec((tm * GROUP_WIDTH // a.shape[1], a.shape[1]), row) for a in attn_inputs]
    scratch = [] if merged_attn else [pltpu.VMEM((GROUP_WIDTH // LANES, tm, LANES), F32) for _ in attn_inputs]
    in_specs += [
        pl.BlockSpec((tm, SSM_WIDTH), row),
        pl.BlockSpec((tm, 2 * D_MODEL), row),
        pl.BlockSpec((tm, D_MODEL), row),
        pl.BlockSpec(wa.shape, const),
        pl.BlockSpec(wglu.shape, const),
        pl.BlockSpec(wout.shape, const),
        pl.BlockSpec((1, D_MODEL), const),
        pl.BlockSpec(wr.shape, const),
        pl.BlockSpec((1, LANES), const),
    ]
    return pl.pallas_call(
        functools.partial(_mix_kernel, merged_attn=merged_attn),
        grid=(n // tm,),
        in_specs=in_specs,
        out_specs=(pl.BlockSpec((tm, D_MODEL), row), pl.BlockSpec((tm * TOKEN_TILE_ROWS, LANES), row),
                   pl.BlockSpec((tm, LANES), row)),
        out_shape=(jax.ShapeDtypeStruct((n, D_MODEL), F32), jax.ShapeDtypeStruct((n * TOKEN_TILE_ROWS, LANES), F32),
                   jax.ShapeDtypeStruct((n, LANES), F32)),
        scratch_shapes=scratch,
        compiler_params=_params("parallel"),
        name="branch_mix",
    )(*attn_inputs, y, gates, x2d, wa, wglu, wout, gffn, wr, br)


def _to_token_tiles(ref, val):
    rows = val.shape[0]
    for j in range(TOKEN_TILE_ROWS):
        ref[pl.ds(j, rows, stride=TOKEN_TILE_ROWS), :] = val[:, j * LANES:(j + 1) * LANES]


def _from_token_tiles(ref):
    rows = ref.shape[0] // TOKEN_TILE_ROWS
    return jnp.concatenate([ref[pl.ds(j, rows, stride=TOKEN_TILE_ROWS), :] for j in range(TOKEN_TILE_ROWS)], axis=1)


def _token_copy(src, src_tok, dst, dst_tok, sem):
    src_rows = pl.ds(pl.multiple_of(src_tok * TOKEN_TILE_ROWS, TOKEN_TILE_ROWS), TOKEN_TILE_ROWS)
    dst_rows = pl.ds(pl.multiple_of(dst_tok * TOKEN_TILE_ROWS, TOKEN_TILE_ROWS), TOKEN_TILE_ROWS)
    return pltpu.make_async_copy(src.at[src_rows], dst.at[dst_rows], sem)


def _dispatch_kernel(pad_end_ref, padded_ref, n_used_ref, slot_ref, *refs, set_steps):
    hn_refs = refs[:len(set_steps)]
    xs_hbm, zero_buf, sem, zero_sem = refs[len(set_steps):]
    block_rows = MOE_BLOCK * TOKEN_TILE_ROWS
    n_blocks = xs_hbm.shape[0] // block_rows

    @pl.when(pl.program_id(0) == 0)
    def _():
        zero_buf[...] = jnp.zeros_like(zero_buf)

        def block_copy(b):
            start = pl.multiple_of(b * block_rows, block_rows)
            return pltpu.make_async_copy(zero_buf, xs_hbm.at[pl.ds(start, block_rows)], zero_sem)

        def segment_tails(action):
            def body(e, carry):
                @pl.when(padded_ref[e] > 0)
                def _():
                    action(block_copy(pad_end_ref[e] // MOE_BLOCK - 1))
                return carry
            lax.fori_loop(0, N_EXPERTS, body, 0)

        def unused_blocks(action):
            def body(b, carry):
                @pl.when(b >= n_used_ref[0])
                def _():
                    action(block_copy(b))
                return carry
            lax.fori_loop(0, n_blocks, body, 0)

        segment_tails(lambda cp: cp.start())
        unused_blocks(lambda cp: cp.start())
        segment_tails(lambda cp: cp.wait())
        unused_blocks(lambda cp: cp.wait())

    first = 0
    for hn_ref, steps in zip(hn_refs, set_steps):
        @pl.when((pl.program_id(0) >= first) & (pl.program_id(0) < first + steps))
        def _(hn_ref=hn_ref):
            _scatter_rows(slot_ref, hn_ref, xs_hbm, sem)
        first += steps


def _scatter_rows(slot_ref, hn_ref, xs_hbm, sem):
    def issue(j, carry):
        _token_copy(hn_ref, j, xs_hbm, slot_ref[2 * j], sem).start(priority=0)
        _token_copy(hn_ref, j, xs_hbm, slot_ref[2 * j + 1], sem).start(priority=1)
        return carry

    def drain(j, carry):
        _token_copy(hn_ref, 0, xs_hbm, 0, sem).wait()
        _token_copy(hn_ref, 0, xs_hbm, 0, sem).wait()
        return carry

    lax.fori_loop(0, ROW_TILE, issue, 0, unroll=DMA_UNROLL)
    lax.fori_loop(0, ROW_TILE, drain, 0, unroll=DMA_UNROLL)


def _dispatch(pad_end, padded, n_used, slots, hns, n_slots):
    set_steps = tuple(hn.shape[0] // (TOKEN_TILE_ROWS * ROW_TILE) for hn in hns)
    hn_specs, first = [], 0
    for steps in set_steps:
        hn_specs.append(pl.BlockSpec(
            (ROW_TILE * TOKEN_TILE_ROWS, LANES),
            lambda i, *_, first=first, steps=steps: (jnp.clip(i - first, 0, steps - 1), 0)))
        first += steps
    return pl.pallas_call(
        functools.partial(_dispatch_kernel, set_steps=set_steps),
        grid_spec=pltpu.PrefetchScalarGridSpec(
            num_scalar_prefetch=3,
            grid=(sum(set_steps),),
            in_specs=[pl.BlockSpec((2 * ROW_TILE,), lambda i, *_: (i,), memory_space=pltpu.SMEM)] + hn_specs,
            out_specs=pl.BlockSpec(memory_space=pl.ANY),
            scratch_shapes=[pltpu.VMEM((MOE_BLOCK * TOKEN_TILE_ROWS, LANES), F32), pltpu.SemaphoreType.DMA(()),
                            pltpu.SemaphoreType.DMA(())],
        ),
        out_shape=jax.ShapeDtypeStruct((n_slots * TOKEN_TILE_ROWS, LANES), F32),
        compiler_params=_params("arbitrary"),
        name="moe_dispatch",
    )(pad_end, padded, n_used, slots, *hns)


def _expert_kernel(block_e_ref, n_used_ref, xs_ref, wg_ref, wu_ref, wd_ref, yb_ref):
    del block_e_ref
    i = pl.program_id(0)

    @pl.when(i < n_used_ref[0])
    def _():
        xb = _from_token_tiles(xs_ref).astype(BF16)
        gate = jnp.dot(xb, wg_ref[...].astype(BF16), preferred_element_type=F32)
        up = jnp.dot(xb, wu_ref[...].astype(BF16), preferred_element_type=F32)
        hmid = (jax.nn.silu(gate) * up).astype(BF16)
        _to_token_tiles(yb_ref, jnp.dot(hmid, wd_ref[...].astype(BF16), preferred_element_type=F32))

    @pl.when(i >= n_used_ref[0])
    def _():
        yb_ref[...] = jnp.zeros_like(yb_ref)


def _experts(block_e, n_used, xs, wg, wu, wd):
    block_rows = MOE_BLOCK * TOKEN_TILE_ROWS
    n_blocks = xs.shape[0] // block_rows
    grid_spec = pltpu.PrefetchScalarGridSpec(
        num_scalar_prefetch=2,
        grid=(n_blocks,),
        in_specs=[
            pl.BlockSpec((block_rows, LANES), lambda i, be, nu: (jnp.minimum(i, nu[0] - 1), 0)),
            pl.BlockSpec((None, D_MODEL, D_EXPERT), lambda i, be, nu: (be[i], 0, 0)),
            pl.BlockSpec((None, D_MODEL, D_EXPERT), lambda i, be, nu: (be[i], 0, 0)),
            pl.BlockSpec((None, D_EXPERT, D_MODEL), lambda i, be, nu: (be[i], 0, 0)),
        ],
        out_specs=pl.BlockSpec((block_rows, LANES), lambda i, be, nu: (i, 0)),
    )
    return pl.pallas_call(
        _expert_kernel,
        grid_spec=grid_spec,
        out_shape=jax.ShapeDtypeStruct(xs.shape, F32),
        compiler_params=_params("arbitrary"),
        name="moe_experts",
    )(block_e, n_used, xs, wg, wu, wd)


def _combine_kernel(slot_ref, next_slot_ref, h_ref, route_ref, g_ref, yb_hbm, out_ref, buf_a, buf_b, sem, *, steps):
    step = pl.program_id(0)
    cur = step % 2

    def gather(slots, buf_idx):
        def body(j, carry):
            _token_copy(yb_hbm, slots[2 * j], buf_a.at[buf_idx], j, sem.at[buf_idx]).start(priority=0)
            _token_copy(yb_hbm, slots[2 * j + 1], buf_b.at[buf_idx], j, sem.at[buf_idx]).start(priority=1)
            return carry
        lax.fori_loop(0, ROW_TILE, body, 0, unroll=DMA_UNROLL)

    @pl.when(step == 0)
    def _():
        gather(slot_ref, 0)

    if steps > 1:
        @pl.when(step + 1 < steps)
        def _():
            gather(next_slot_ref, 1 - cur)

    def drain(j, carry):
        _token_copy(yb_hbm, 0, buf_a.at[cur], 0, sem.at[cur]).wait()
        _token_copy(yb_hbm, 0, buf_b.at[cur], 0, sem.at[cur]).wait()
        return carry

    lax.fori_loop(0, ROW_TILE, drain, 0, unroll=DMA_UNROLL)
    route = route_ref[...]
    h = h_ref[...] + (route[:, 2:3] * _from_token_tiles(buf_a.at[cur]) + route[:, 3:4] * _from_token_tiles(buf_b.at[cur]))
    var = jnp.mean(h * h, axis=-1, keepdims=True)
    out_ref[...] = h * lax.rsqrt(var + NORM_EPS) * g_ref[...]


def _combine(slots, h, route, g_final, yb):
    n = h.shape[0]
    steps = n // ROW_TILE
    row = lambda i: (i, 0)
    return pl.pallas_call(
        functools.partial(_combine_kernel, steps=steps),
        grid=(steps,),
        in_specs=[
            pl.BlockSpec((2 * ROW_TILE,), lambda i: (i,), memory_space=pltpu.SMEM),
            pl.BlockSpec((2 * ROW_TILE,), lambda i: (jnp.minimum(i + 1, steps - 1),), memory_space=pltpu.SMEM),
            pl.BlockSpec((ROW_TILE, D_MODEL), row),
            pl.BlockSpec((ROW_TILE, LANES), row),
            pl.BlockSpec((1, D_MODEL), lambda i: (0, 0)),
            pl.BlockSpec(memory_space=pl.ANY),
        ],
        out_specs=pl.BlockSpec((ROW_TILE, D_MODEL), row),
        out_shape=jax.ShapeDtypeStruct((n, D_MODEL), F32),
        scratch_shapes=[
            pltpu.VMEM((2, ROW_TILE * TOKEN_TILE_ROWS, LANES), F32),
            pltpu.VMEM((2, ROW_TILE * TOKEN_TILE_ROWS, LANES), F32),
            pltpu.SemaphoreType.DMA((2,)),
        ],
        compiler_params=_params("arbitrary"),
        name="moe_combine",
    )(slots, slots, h, route, g_final.reshape(1, D_MODEL), yb)


def _slot_assignment(route, n_blocks):
    flat_e = route.astype(jnp.int32).reshape(-1)
    onehot = (flat_e[:, None] == jnp.arange(N_EXPERTS, dtype=jnp.int32)[None, :]).astype(jnp.int32)
    running = jnp.cumsum(onehot, axis=0)
    rank = jnp.sum(onehot * running, axis=1) - 1
    counts = running[-1]
    padded = ((counts + MOE_BLOCK - 1) // MOE_BLOCK) * MOE_BLOCK
    pad_end = jnp.cumsum(padded)
    pad_start = pad_end - padded
    slots = jnp.sum(onehot * pad_start[None, :], axis=1) + rank
    block_start = jnp.arange(n_blocks, dtype=jnp.int32) * MOE_BLOCK
    block_e = jnp.minimum(jnp.sum((pad_end[None, :] <= block_start[:, None]).astype(jnp.int32), axis=1), N_EXPERTS - 1)
    n_used = (pad_end[-1:] // MOE_BLOCK).astype(jnp.int32)
    return slots.astype(jnp.int32), block_e, n_used, pad_end.astype(jnp.int32), padded.astype(jnp.int32)


def _moe_and_final_norm(hs, hns, routes, wg, wu, wd, g_final):
    counts = [h.shape[0] for h in hs]
    n = sum(counts)
    n_blocks = (2 * n) // MOE_BLOCK + N_EXPERTS
    slots, block_e, n_used, pad_end, padded = _slot_assignment(jnp.concatenate([r[:, 0:2] for r in routes], axis=0), n_blocks)
    starts = [2 * sum(counts[:k]) for k in range(len(counts))]
    set_slots = [slots[s:s + 2 * c] for s, c in zip(starts, counts)]
    xs = _dispatch(pad_end, padded, n_used, slots, hns, n_blocks * MOE_BLOCK)
    yb = _experts(block_e, n_used, xs, wg, wu, wd)
    return [_combine(set_slots[k], hs[k], routes[k], g_final, yb) for k in range(len(hs))]


def kernel(x_prompt, x_sample, cache_kv_w128, cache_kv_w512, cache_kv_w2048, state_ssm, g_attn_norm, w_in, ssm_log_dt, ssm_a_re, ssm_a_im, ssm_b_re, ssm_b_im, ssm_c_re, ssm_c_im, ssm_d, w_glu, w_attn_branch, w_out, g_ffn_norm, w_router_group, b_router_group, w_router_expert, b_router_expert, w_exp_gate, w_exp_up, w_exp_down, g_final):
    batch, seq, _ = x_prompt.shape
    dec_batch, dec_seq, _ = x_sample.shape
    past_len = cache_kv_w2048.shape[2]
    layer = 0

    w_in_b = w_in[layer].astype(BF16)
    wa = w_attn_branch[layer].astype(BF16)
    wglu = w_glu[layer].astype(BF16)
    wout = w_out[layer].astype(BF16)
    wg, wu, wd = w_exp_gate[layer], w_exp_up[layer], w_exp_down[layer]
    gffn = g_ffn_norm[layer].reshape(1, D_MODEL)
    pad = LANES - N_EXPERT_GROUPS - N_EXPERTS
    wr_f32 = jnp.concatenate([w_router_group[layer], w_router_expert[layer], jnp.zeros((D_MODEL, pad), F32)], axis=1)
    wr_hi = wr_f32.astype(BF16)
    wr = jnp.concatenate([wr_hi, (wr_f32 - wr_hi.astype(F32)).astype(BF16)], axis=1)
    br = jnp.concatenate([b_router_group[layer], b_router_expert[layer], jnp.zeros((pad,), F32)]).reshape(1, LANES)
    ssm_w = _ssm_weights(ssm_log_dt[layer], ssm_a_re[layer], ssm_a_im[layer], ssm_b_re[layer], ssm_b_im[layer],
                         ssm_c_re[layer], ssm_c_im[layer])
    d_flat = ssm_d[layer].astype(F32).reshape(1, SSM_WIDTH)

    def time_major(a, nb, steps):
        return jnp.transpose(a.reshape(nb, steps, -1), (1, 0, 2)).reshape(nb * steps, -1)

    def batch_major(a, nb, steps):
        return jnp.transpose(a.reshape(steps, nb, -1), (1, 0, 2)).reshape(nb * steps, -1)

    xp = x_prompt.reshape(batch * seq, D_MODEL)
    q_p, kv_p, kv_t_p, u_p, gates_p = _in_projection_prompt(xp, g_attn_norm[layer], w_in_b, batch, seq)
    attn_in = []
    lse_in = []
    for g, (win, dil) in enumerate(ATTN_GROUPS):
        o, lse = _prompt_attention(q_p[g], kv_p[g], g, dil, batch, seq)
        attn_in.append(o)
        lse_in.append(lse)
    zeros_state = jnp.zeros((batch, SSM_COLS), F32)
    y_tm, hr_p, hi_p = _ssm(time_major(u_p, batch, seq), zeros_state, zeros_state, ssm_w, d_flat, batch)
    y_p = batch_major(y_tm, batch, seq)
    h_p, hn_p, route_p = _mix(attn_in + lse_in, y_p, gates_p, xp, wa, wglu, wout, gffn, wr, br)

    xs = x_sample.reshape(dec_batch * dec_seq, D_MODEL)
    pos_s = past_len + (jnp.arange(dec_batch * dec_seq, dtype=jnp.int32) % dec_seq)
    q_s, kv0_s, kv1_s, kv2_s, u_s, gates_s = _in_projection_decode(xs, g_attn_norm[layer], w_in_b, pos_s)
    kv_s = (kv0_s, kv1_s, kv2_s)
    caches = (cache_kv_w128[layer], cache_kv_w512[layer], cache_kv_w2048[layer])
    attn_s = _decode_attention(q_s, caches, kv_s, dec_batch, dec_seq)
    st = state_ssm[layer].astype(F32).reshape(dec_batch, SSM_COLS, 2)
    ys_tm, hr_s, hi_s = _ssm(time_major(u_s, dec_batch, dec_seq), st[:, :, 0], st[:, :, 1], ssm_w, d_flat, dec_batch)
    y_s = batch_major(ys_tm, dec_batch, dec_seq)
    h_s, hn_s, route_s = _mix([attn_s], y_s, gates_s, xs, wa, wglu, wout, gffn, wr, br)

    out_p, out_s = _moe_and_final_norm([h_p, h_s], [hn_p, hn_s], [route_p, route_s], wg, wu, wd, g_final)

    kv_tail = (2, HEADS_PER_GROUP, HEAD_DIM)
    outs = [out_p.reshape(batch, seq, D_MODEL), out_s.reshape(dec_batch, dec_seq, D_MODEL)]
    for g, (win, dil) in enumerate(ATTN_GROUPS):
        keep = min(win, seq)
        rows_p = jnp.transpose(kv_t_p[g].reshape(batch, 2, HEADS_PER_GROUP, HEAD_DIM, keep), (0, 4, 1, 2, 3))
        outs.append(rows_p[None])
        outs.append(kv_s[g].reshape((1, dec_batch, dec_seq) + kv_tail))
    outs.append(jnp.stack([hr_p, hi_p], axis=-1).reshape(1, batch, N_SSM_GROUPS, SSM_STATE, 2))
    outs.append(jnp.stack([hr_s, hi_s], axis=-1).reshape(1, dec_batch, N_SSM_GROUPS, SSM_STATE, 2))
    return tuple(outs)
```

```python
import functools
import math

import jax
import jax.numpy as jnp
from jax import lax
from jax.experimental import pallas as pl
from jax.experimental.pallas import tpu as pltpu

F32 = jnp.float32
BF16 = jnp.bfloat16

D_MODEL = 1024
HEAD_DIM = 64
HEADS_PER_GROUP = 8
GROUP_WIDTH = HEADS_PER_GROUP * HEAD_DIM
ATTN_GROUPS = ((128, 1), (512, 4), (2048, 16))
N_ATTN_GROUPS = len(ATTN_GROUPS)
ATTN_WIDTH = N_ATTN_GROUPS * GROUP_WIDTH
ROT_DIM = HEAD_DIM // 4
ROPE_THETA = 500000.0
WINDOW_KEYS = 128
SSM_GROUP_CH = 16
SSM_WIDTH = D_MODEL // 2
N_SSM_GROUPS = SSM_WIDTH // SSM_GROUP_CH
SSM_STATE = 64
SSM_COLS = N_SSM_GROUPS * SSM_STATE
IN_WIDTH = 3 * ATTN_WIDTH + SSM_WIDTH + 2 * D_MODEL
N_EXPERT_GROUPS = 4
EXPERTS_PER_GROUP = 8
N_EXPERTS = N_EXPERT_GROUPS * EXPERTS_PER_GROUP
D_EXPERT = D_MODEL // 4
NORM_EPS = 1e-6

LANES = 128
SUBLANES = 8
TOKEN_TILE_ROWS = D_MODEL // LANES
VMEM_LIMIT = 56 * 1024 * 1024

PROJ_TILE = 256
MATMUL_LOOKAHEAD = 1
ATTN_TILE = 128
ATTN_RESIDUES_PER_STEP = 4
ATTN_TILES_PER_STEP = 8
SSM_ROWS = 512
MOE_BLOCK = 256
ROW_TILE = 512
DMA_UNROLL = 8
DECODE_BATCHES_PER_STEP = 2


def _params(*sem):
    return pltpu.CompilerParams(dimension_semantics=sem, vmem_limit_bytes=VMEM_LIMIT)


def _normed_input(x_ref, g_ref):
    x = x_ref[...]
    var = jnp.mean(x * x, axis=-1, keepdims=True)
    return (x * lax.rsqrt(var + NORM_EPS) * g_ref[...]).astype(BF16)


def _rope_fn(cos_ref, sin_ref):
    cos = cos_ref[...]
    sin = sin_ref[...]
    lane = lax.broadcasted_iota(jnp.int32, cos.shape, 1) % HEAD_DIM
    first_half = lane < ROT_DIM // 2
    rotated = lane < ROT_DIM

    def rope_chunk(c):
        partner = jnp.where(first_half, pltpu.roll(c, LANES - ROT_DIM // 2, 1), pltpu.roll(c, ROT_DIM // 2, 1))
        return jnp.where(rotated, c * cos + partner * sin, c)

    def rope(t):
        return jnp.concatenate([rope_chunk(t[:, j * LANES:(j + 1) * LANES]) for j in range(GROUP_WIDTH // LANES)], axis=1)

    return rope


def _projection_tiles(xn, w_ref):
    for c in range(IN_WIDTH // GROUP_WIDTH):
        yield c, jnp.dot(xn, w_ref[:, c * GROUP_WIDTH:(c + 1) * GROUP_WIDTH], preferred_element_type=F32)


def _inproj_decode_kernel(x_ref, g_ref, w_ref, cos_ref, sin_ref, q_ref, kv0_ref, kv1_ref, kv2_ref, u_ref, gate_ref):
    rope = _rope_fn(cos_ref, sin_ref)
    kv_refs = (kv0_ref, kv1_ref, kv2_ref)
    for c, acc in _projection_tiles(_normed_input(x_ref, g_ref), w_ref):
        if c < 3:
            q_ref[:, c * GROUP_WIDTH:(c + 1) * GROUP_WIDTH] = (rope(acc) * (HEAD_DIM ** -0.5)).astype(BF16)
        elif c < 6:
            kv_refs[c - 3][:, :GROUP_WIDTH] = rope(acc)
        elif c < 9:
            kv_refs[c - 6][:, GROUP_WIDTH:] = acc
        elif c == 9:
            u_ref[...] = acc
        else:
            gate_ref[:, (c - 10) * GROUP_WIDTH:(c - 9) * GROUP_WIDTH] = acc


def _transposed_projection(wt, xn, cos_t, sin_t):
    t = lax.dot_general(wt, xn, (((1,), (1,)), ((), ())), preferred_element_type=F32)
    if cos_t is None:
        return t
    half_rot = ROT_DIM // 2
    pieces = []
    for h in range(HEADS_PER_GROUP):
        x1 = t[h * HEAD_DIM:h * HEAD_DIM + half_rot]
        x2 = t[h * HEAD_DIM + half_rot:h * HEAD_DIM + ROT_DIM]
        pieces += [x1 * cos_t - x2 * sin_t, x2 * cos_t + x1 * sin_t, t[h * HEAD_DIM + ROT_DIM:(h + 1) * HEAD_DIM]]
    return jnp.concatenate(pieces, axis=0)


def _kv_tail_kernel(x_ref, g_ref, wk_ref, wv_ref, cos_t_ref, sin_t_ref, kt_ref):
    xn = _normed_input(x_ref, g_ref)
    kt_ref[0] = _transposed_projection(wk_ref[...], xn, cos_t_ref[...], sin_t_ref[...])
    kt_ref[1] = _transposed_projection(wv_ref[...], xn, None, None)


def _inproj_prompt_kernel(x_ref, g_ref, w_ref, cos_ref, sin_ref, wt_ref, cos_t_ref, sin_t_ref, *refs, full_groups):
    q_refs, kv_refs = refs[0:N_ATTN_GROUPS], refs[N_ATTN_GROUPS:2 * N_ATTN_GROUPS]
    kt_refs = refs[2 * N_ATTN_GROUPS:2 * N_ATTN_GROUPS + len(full_groups)]
    u_ref, gate_ref = refs[2 * N_ATTN_GROUPS + len(full_groups):2 * N_ATTN_GROUPS + len(full_groups) + 2]
    scratch = refs[2 * N_ATTN_GROUPS + len(full_groups) + 2:]
    rope = _rope_fn(cos_ref, sin_ref)
    tm = x_ref.shape[0]

    dilated = [g for g, (_, d) in enumerate(ATTN_GROUPS) if d > 1]

    def store_rows(dst_ref, val, kind, g, col0, col_stride):
        dil = ATTN_GROUPS[g][1]
        if dil == 1:
            dst_ref[:, col0:col0 + GROUP_WIDTH] = val.astype(BF16)
            return
        scr = scratch[kind * len(dilated) + dilated.index(g)]
        n_chunks = GROUP_WIDTH // LANES
        for j in range(n_chunks):
            scr[j] = val[:, j * LANES:(j + 1) * LANES]
        rows = tm // dil
        for r in range(dil):
            piece = jnp.concatenate([scr[j, pl.ds(r, rows, stride=dil), :] for j in range(n_chunks)], axis=1)
            dst_ref[:, col0 + r * col_stride:col0 + r * col_stride + GROUP_WIDTH] = piece.astype(BF16)

    xn = _normed_input(x_ref, g_ref)

    def emit(c, acc):
        g = c % N_ATTN_GROUPS
        if c < 3:
            store_rows(q_refs[g], rope(acc) * (HEAD_DIM ** -0.5), 0, g, 0, GROUP_WIDTH)
        elif c < 6:
            store_rows(kv_refs[g], rope(acc), 1, g, 0, 2 * GROUP_WIDTH)
            if g in full_groups:
                kt_refs[full_groups.index(g)][0] = _transposed_projection(
                    wt_ref[full_groups.index(g)], xn, cos_t_ref[...], sin_t_ref[...])
        elif c < 9:
            store_rows(kv_refs[g], acc, 2, g, GROUP_WIDTH, 2 * GROUP_WIDTH)
            if g in full_groups:
                kt_refs[full_groups.index(g)][1] = _transposed_projection(
                    wt_ref[len(full_groups) + full_groups.index(g)], xn, None, None)
        elif c == 9:
            u_ref[...] = acc
        else:
            gate_ref[:, (c - 10) * GROUP_WIDTH:(c - 9) * GROUP_WIDTH] = acc

    pending = []
    for item in _projection_tiles(xn, w_ref):
        pending.append(item)
        if len(pending) > MATMUL_LOOKAHEAD:
            emit(*pending.pop(0))
    for item in pending:
        emit(*item)


def _rope_tables(pos):
    half = ROT_DIM // 2
    inv_freq = ROPE_THETA ** (-(jnp.arange(half, dtype=F32) / half))
    ang = pos.astype(F32)[:, None] * inv_freq[None, :]
    cos, sin = jnp.cos(ang), jnp.sin(ang)
    n = pos.shape[0]
    rest = HEAD_DIM - ROT_DIM
    cos_h = jnp.concatenate([cos, cos, jnp.ones((n, rest), F32)], axis=1)
    sin_h = jnp.concatenate([-sin, sin, jnp.zeros((n, rest), F32)], axis=1)
    return jnp.tile(cos_h, (1, LANES // HEAD_DIM)), jnp.tile(sin_h, (1, LANES // HEAD_DIM))


def _inproj_in_specs(tm, n_pos_tiles):
    const = lambda i: (0, 0)
    tab = lambda i: (i % n_pos_tiles, 0)
    return [
        pl.BlockSpec((tm, D_MODEL), lambda i: (i, 0)),
        pl.BlockSpec((1, D_MODEL), const),
        pl.BlockSpec((D_MODEL, IN_WIDTH), const, pipeline_mode=pl.Buffered(1)),
        pl.BlockSpec((tm, LANES), tab),
        pl.BlockSpec((tm, LANES), tab),
    ]


def _in_projection_decode(x2d, g, w_bf16, pos):
    n = x2d.shape[0]
    tm = PROJ_TILE
    cos_t, sin_t = _rope_tables(pos)
    row = lambda i: (i, 0)
    widths = (ATTN_WIDTH, 2 * GROUP_WIDTH, 2 * GROUP_WIDTH, 2 * GROUP_WIDTH, SSM_WIDTH, 2 * D_MODEL)
    dtypes = (BF16, F32, F32, F32, F32, F32)
    return pl.pallas_call(
        _inproj_decode_kernel,
        grid=(n // tm,),
        in_specs=_inproj_in_specs(tm, pos.shape[0] // tm),
        out_specs=tuple(pl.BlockSpec((tm, w), row) for w in widths),
        out_shape=tuple(jax.ShapeDtypeStruct((n, w), d) for w, d in zip(widths, dtypes)),
        compiler_params=_params("parallel"),
        name="in_projection_decode",
    )(x2d, g.reshape(1, D_MODEL), w_bf16, cos_t, sin_t)


def _in_projection_prompt(x2d, g, w_bf16, batch, seq):
    n = batch * seq
    tm = PROJ_TILE
    tiles_per_seq = seq // tm
    pos = jnp.arange(seq, dtype=jnp.int32)
    cos_t, sin_t = _rope_tables(pos)
    half = ROT_DIM // 2
    ang_t = (ROPE_THETA ** (-(jnp.arange(half, dtype=F32) / half)))[:, None] * pos.astype(F32)[None, :]
    cos_tr, sin_tr = jnp.cos(ang_t), jnp.sin(ang_t)
    wk_t = jnp.transpose(w_bf16[:, ATTN_WIDTH:2 * ATTN_WIDTH]).reshape(N_ATTN_GROUPS, GROUP_WIDTH, D_MODEL)
    wv_t = jnp.transpose(w_bf16[:, 2 * ATTN_WIDTH:3 * ATTN_WIDTH]).reshape(N_ATTN_GROUPS, GROUP_WIDTH, D_MODEL)
    keeps = tuple(min(win, seq) for win, _ in ATTN_GROUPS)
    full_groups = tuple(i for i, keep in enumerate(keeps) if keep == seq)
    w_full_t = jnp.concatenate([wk_t[jnp.array(full_groups)], wv_t[jnp.array(full_groups)]], axis=0)
    g_row = g.reshape(1, D_MODEL)
    row = lambda i: (i, 0)
    in_specs = _inproj_in_specs(tm, tiles_per_seq) + [
        pl.BlockSpec(w_full_t.shape, lambda i: (0, 0, 0), pipeline_mode=pl.Buffered(1)),
        pl.BlockSpec((half, tm), lambda i: (0, i % tiles_per_seq)),
        pl.BlockSpec((half, tm), lambda i: (0, i % tiles_per_seq)),
    ]
    out_specs, out_shape = [], []
    for width in (GROUP_WIDTH, 2 * GROUP_WIDTH):
        for _, dil in ATTN_GROUPS:
            out_specs.append(pl.BlockSpec((tm // dil, dil * width), row))
            out_shape.append(jax.ShapeDtypeStruct((n // dil, dil * width), BF16))
    for _ in full_groups:
        out_specs.append(pl.BlockSpec((None, 2, GROUP_WIDTH, tm), lambda i: (i // tiles_per_seq, 0, 0, i % tiles_per_seq)))
        out_shape.append(jax.ShapeDtypeStruct((batch, 2, GROUP_WIDTH, seq), F32))
    out_specs += [pl.BlockSpec((tm, SSM_WIDTH), row), pl.BlockSpec((tm, 2 * D_MODEL), row)]
    out_shape += [jax.ShapeDtypeStruct((n, SSM_WIDTH), F32), jax.ShapeDtypeStruct((n, 2 * D_MODEL), F32)]
    outs = pl.pallas_call(
        functools.partial(_inproj_prompt_kernel, full_groups=full_groups),
        grid=(n // tm,),
        in_specs=in_specs,
        out_specs=tuple(out_specs),
        out_shape=tuple(out_shape),
        scratch_shapes=[pltpu.VMEM((GROUP_WIDTH // LANES, tm, LANES), F32)
                        for _ in range(3 * sum(1 for _, d in ATTN_GROUPS if d > 1))],
        compiler_params=_params("parallel"),
        name="in_projection_prompt",
    )(x2d, g_row, w_bf16, cos_t, sin_t, w_full_t, cos_tr, sin_tr)
    n_full = len(full_groups)
    kv_t = dict(zip(full_groups, outs[6:6 + n_full]))
    for grp, keep in enumerate(keeps):
        if grp in full_groups:
            continue
        blk = min(keep, tm)
        first = (seq - keep) // blk
        per_seq = seq // blk
        tail = lambda b, j, first=first: (0, first + j)
        kv_t[grp] = pl.pallas_call(
            _kv_tail_kernel,
            grid=(batch, keep // blk),
            in_specs=[
                pl.BlockSpec((blk, D_MODEL), lambda b, j, first=first, per_seq=per_seq: (b * per_seq + first + j, 0)),
                pl.BlockSpec((1, D_MODEL), lambda b, j: (0, 0)),
                pl.BlockSpec((None, GROUP_WIDTH, D_MODEL), lambda b, j, grp=grp: (grp, 0, 0)),
                pl.BlockSpec((None, GROUP_WIDTH, D_MODEL), lambda b, j, grp=grp: (grp, 0, 0)),
                pl.BlockSpec((half, blk), tail),
                pl.BlockSpec((half, blk), tail),
            ],
            out_specs=pl.BlockSpec((None, 2, GROUP_WIDTH, blk), lambda b, j: (b, 0, 0, j)),
            out_shape=jax.ShapeDtypeStruct((batch, 2, GROUP_WIDTH, keep), F32),
            compiler_params=_params("parallel", "parallel"),
            name=f"kv_tail_g{grp}",
        )(x2d, g_row, wk_t, wv_t, cos_tr, sin_tr)
    return outs[0:3], outs[3:6], tuple(kv_t[i] for i in range(N_ATTN_GROUPS)), outs[6 + n_full], outs[7 + n_full]


def _window_attn_kernel(q_ref, kvc_ref, *rest, windowed, residues, sub_tiles):
    if windowed:
        kvp_ref, o_ref, lse_ref = rest
    else:
        o_ref, lse_ref = rest
    n_keys = 2 * ATTN_TILE if windowed else ATTN_TILE
    rows = lax.broadcasted_iota(jnp.int32, (ATTN_TILE, n_keys), 0)
    cols = lax.broadcasted_iota(jnp.int32, (ATTN_TILE, n_keys), 1)
    if windowed:
        band = (cols >= rows) & (cols <= rows + WINDOW_KEYS)
        first_tile_band = band & ((cols >= ATTN_TILE) | (pl.program_id(2) > 0))
    else:
        band = cols <= rows
    low_head = lax.broadcasted_iota(jnp.int32, (ATTN_TILE, LANES), 1) < HEAD_DIM
    for r in range(residues):
        for s in range(sub_tiles):
            rs = slice(s * ATTN_TILE, (s + 1) * ATTN_TILE)
            k0 = r * 2 * GROUP_WIDTH
            q = q_ref[rs, r * GROUP_WIDTH:(r + 1) * GROUP_WIDTH]
            k = kvc_ref[rs, k0:k0 + GROUP_WIDTH]
            v = kvc_ref[rs, k0 + GROUP_WIDTH:k0 + 2 * GROUP_WIDTH]
            valid = band
            if windowed:
                if s == 0:
                    prev_ref, ps, valid = kvp_ref, slice(0, ATTN_TILE), first_tile_band
                else:
                    prev_ref, ps = kvc_ref, slice((s - 1) * ATTN_TILE, s * ATTN_TILE)
                k = jnp.concatenate([prev_ref[ps, k0:k0 + GROUP_WIDTH], k], axis=0)
                v = jnp.concatenate([prev_ref[ps, k0 + GROUP_WIDTH:k0 + 2 * GROUP_WIDTH], v], axis=0)
            valid2 = jnp.concatenate([valid, valid], axis=0)
            for j in range(GROUP_WIDTH // LANES):
                sl = slice(j * LANES, (j + 1) * LANES)
                osl = slice(r * GROUP_WIDTH + j * LANES, r * GROUP_WIDTH + (j + 1) * LANES)
                qj, kj, vj = q[:, sl], k[:, sl], v[:, sl]
                zero = jnp.zeros_like(qj)
                qm = jnp.concatenate([jnp.where(low_head, qj, zero), jnp.where(low_head, zero, qj)], axis=0)
                sc = lax.dot_general(qm, kj, (((1,), (1,)), ((), ())), preferred_element_type=F32)
                sc = jnp.where(valid2, sc, -jnp.inf)
                m = jnp.max(sc, axis=1, keepdims=True)
                p = jnp.exp(sc - m)
                l = jnp.sum(p, axis=1, keepdims=True)
                o2 = jnp.dot(p.astype(BF16), vj, preferred_element_type=F32) / l
                lse2 = jnp.broadcast_to(m + jnp.log(l), o2.shape)
                o_ref[rs, osl] = jnp.where(low_head, o2[:ATTN_TILE], o2[ATTN_TILE:])
                lse_ref[rs, osl] = jnp.where(low_head, lse2[:ATTN_TILE], lse2[ATTN_TILE:])


def _prompt_attention(q, kv, group, dilation, batch, seq):
    tg = seq // dilation
    n_tiles = tg // ATTN_TILE
    windowed = n_tiles > 1
    residues = min(dilation, ATTN_RESIDUES_PER_STEP)
    sub_tiles = min(n_tiles, ATTN_TILES_PER_STEP // residues)
    rows = sub_tiles * ATTN_TILE
    q3 = q.reshape(batch, tg, dilation * GROUP_WIDTH)
    kv3 = kv.reshape(batch, tg, dilation * 2 * GROUP_WIDTH)
    in_specs = [
        pl.BlockSpec((None, rows, residues * GROUP_WIDTH), lambda b, r, t: (b, t, r)),
        pl.BlockSpec((None, rows, residues * 2 * GROUP_WIDTH), lambda b, r, t: (b, t, r)),
    ]
    args = [q3, kv3]
    if windowed:
        in_specs.append(pl.BlockSpec((None, ATTN_TILE, residues * 2 * GROUP_WIDTH),
                                     lambda b, r, t: (b, jnp.maximum(t * sub_tiles - 1, 0), r)))
        args.append(kv3)
    out_spec = pl.BlockSpec((None, rows, residues * GROUP_WIDTH), lambda b, r, t: (b, t, r))
    out_sds = jax.ShapeDtypeStruct((batch, tg, dilation * GROUP_WIDTH), F32)
    o, lse = pl.pallas_call(
        functools.partial(_window_attn_kernel, windowed=windowed, residues=residues, sub_tiles=sub_tiles),
        grid=(batch, dilation // residues, n_tiles // sub_tiles),
        in_specs=in_specs,
        out_specs=(out_spec, out_spec),
        out_shape=(out_sds, out_sds),
        compiler_params=_params("parallel", "parallel", "arbitrary"),
        name=f"prompt_attention_g{group}",
    )(*args)
    flat = (batch * tg, dilation * GROUP_WIDTH)
    return o.reshape(flat), lse.reshape(flat)


def _sublane_total(x):
    x = x + pltpu.roll(x, 4, 0)
    x = x + pltpu.roll(x, 2, 0)
    return x + pltpu.roll(x, 1, 0)


def _head_sum(prod):
    width = prod.shape[1]
    row = lax.broadcasted_iota(jnp.int32, (HEADS_PER_GROUP, width), 0)
    out = jnp.zeros((HEADS_PER_GROUP, width), F32)
    for h in range(HEADS_PER_GROUP):
        part = prod[h * HEAD_DIM:h * HEAD_DIM + SUBLANES]
        for j in range(1, HEAD_DIM // SUBLANES):
            part = part + prod[h * HEAD_DIM + j * SUBLANES:h * HEAD_DIM + (j + 1) * SUBLANES]
        out = jnp.where(row == h, _sublane_total(part), out)
    return out


def _head_expand(x):
    width = x.shape[1]
    pieces = []
    for h in range(HEADS_PER_GROUP):
        pieces.extend([jnp.broadcast_to(x[h:h + 1, :], (SUBLANES, width))] * (HEAD_DIM // SUBLANES))
    return jnp.concatenate(pieces, axis=0)


def _split_dot(acc, sel, terms=2):
    out, rem = None, acc
    for _ in range(terms):
        hi = rem.astype(BF16)
        part = jnp.dot(hi, sel, preferred_element_type=F32)
        out = part if out is None else out + part
        rem = rem - hi.astype(F32)
    return out


def _decode_attn_kernel(*refs, dec_seq):
    for bl in range(refs[0].shape[0]):
        _decode_attn_one_batch(*[r.at[bl] for r in refs], dec_seq=dec_seq)


def _decode_attn_one_batch(q_ref, n0_ref, n1_ref, n2_ref, c0_ref, c1_ref, c2_ref, o_ref, *, dec_seq):
    nq = SUBLANES
    n_sub = HEAD_DIM // SUBLANES
    neg = -jnp.inf
    q_t = q_ref[...].T
    step = lax.broadcasted_iota(jnp.int32, (HEADS_PER_GROUP, nq), 1)
    step_wide = lax.broadcasted_iota(jnp.int32, (GROUP_WIDTH, nq), 1)
    real_step = step < dec_seq
    new_refs = (n0_ref, n1_ref, n2_ref)
    cache_refs = (c0_ref, c1_ref, c2_ref)

    def column(x, t, width):
        return jnp.broadcast_to(x[:, t:t + 1], (x.shape[0], width))

    def place(cols):
        rows = cols[0].shape[0]
        lane = lax.broadcasted_iota(jnp.int32, (rows, nq), 1)
        out = jnp.zeros((rows, nq), F32)
        for t, c in enumerate(cols):
            out = jnp.where(lane == t, jnp.broadcast_to(c, (rows, nq)), out)
        return out

    outs, lses = [], []
    for g, (win, dil) in enumerate(ATTN_GROUPS):
        c_ref = cache_refs[g]
        kv_new = new_refs[g][...]
        kn_t = kv_new[:, :GROUP_WIDTH].T
        vn_t = kv_new[:, GROUP_WIDTH:].T
        q_g = q_t[g * GROUP_WIDTH:(g + 1) * GROUP_WIDTH, :]
        pos = lax.broadcasted_iota(jnp.int32, (HEADS_PER_GROUP, win), 1)
        row8 = lax.broadcasted_iota(jnp.int32, (HEADS_PER_GROUP, win), 0)
        n_tiles = win // LANES

        def cache_scores(q_pat):
            s = jnp.zeros((HEADS_PER_GROUP, win), F32)
            for h in range(HEADS_PER_GROUP):
                part = None
                for j in range(n_sub):
                    r0 = h * HEAD_DIM + j * SUBLANES
                    qp = q_pat[r0:r0 + SUBLANES, :]
                    term = c_ref[0, r0:r0 + SUBLANES, :] * jnp.concatenate([qp] * n_tiles, axis=1)
                    part = term if part is None else part + term
                s = jnp.where(row8 == h, _sublane_total(part), s)
            return s

        def weighted_values(p):
            rows = []
            for h in range(HEADS_PER_GROUP):
                ph = jnp.broadcast_to(p[h:h + 1, :], (SUBLANES, win))
                for j in range(n_sub):
                    r0 = h * HEAD_DIM + j * SUBLANES
                    prod = ph * c_ref[1, r0:r0 + SUBLANES, :]
                    a = prod[:, :LANES]
                    for tile in range(1, n_tiles):
                        a = a + prod[:, tile * LANES:(tile + 1) * LANES]
                    rows.append(a)
            return jnp.concatenate(rows, axis=0)

        if dil == 1:
            spread = (lax.broadcasted_iota(jnp.int32, (nq, dec_seq * LANES), 0)
                      == lax.broadcasted_iota(jnp.int32, (nq, dec_seq * LANES), 1) // LANES).astype(BF16)
            q_cols = _split_dot(q_g, spread, terms=1)
            kn_cols = _split_dot(kn_t, spread, terms=3)
            vn_cols = _split_dot(vn_t, spread, terms=3)
            s_new = [_head_sum(q_g * kn_cols[:, u * LANES:u * LANES + nq]) for u in range(dec_seq)]
            ok_new = [(step >= u) & real_step for u in range(dec_seq)]
            s_cache = [cache_scores(q_cols[:, t * LANES:(t + 1) * LANES]) for t in range(dec_seq)]
            m = place([jnp.max(jnp.where(pos >= t, s_cache[t], neg), axis=1, keepdims=True) for t in range(dec_seq)])
            for u in range(dec_seq):
                m = jnp.maximum(m, jnp.where(ok_new[u], s_new[u], neg))
            e_cache = [jnp.where(pos >= t, jnp.exp(s_cache[t] - m[:, t:t + 1]), 0.0) for t in range(dec_seq)]
            e_new = [jnp.where(ok_new[u], jnp.exp(s_new[u] - m), 0.0) for u in range(dec_seq)]
            denom = place([jnp.sum(e, axis=1, keepdims=True) for e in e_cache])
            for u in range(dec_seq):
                denom = denom + e_new[u]
            denom = jnp.where(real_step, denom, 1.0)
            inv = 1.0 / denom
            cols = []
            for t in range(dec_seq):
                acc = weighted_values(e_cache[t] * inv[:, t:t + 1])
                cols.append(jnp.sum(acc, axis=1, keepdims=True))
            o_g = place(cols)
            for u in range(dec_seq):
                o_g = o_g + _head_expand(e_new[u] * inv) * vn_cols[:, u * LANES:u * LANES + nq]
        else:
            res = pos % dil
            t_i = lax.broadcasted_iota(jnp.int32, (nq, LANES), 0)
            l_i = lax.broadcasted_iota(jnp.int32, (nq, LANES), 1)
            residue_pat = ((l_i % dil == t_i) & (t_i < dec_seq)).astype(BF16)
            s_cache = cache_scores(_split_dot(q_g, residue_pat, terms=1))
            s_new = jnp.where(real_step, _head_sum(q_g * kn_t), 0.0)
            m_cache = place([jnp.max(jnp.where(res == t, s_cache, neg), axis=1, keepdims=True) for t in range(dec_seq)])
            m = jnp.maximum(m_cache, s_new)

            def by_position(stat):
                out = jnp.zeros((HEADS_PER_GROUP, win), F32)
                for t in range(dec_seq):
                    out = jnp.where(res == t, column(stat, t, win), out)
                return out

            e_cache = jnp.where(res < dec_seq, jnp.exp(s_cache - by_position(m)), 0.0)
            e_new = jnp.exp(s_new - m)
            denom = place([jnp.sum(jnp.where(res == t, e_cache, 0.0), axis=1, keepdims=True)
                           for t in range(dec_seq)]) + e_new
            inv = 1.0 / denom
            acc = weighted_values(e_cache * by_position(inv))
            l_i = lax.broadcasted_iota(jnp.int32, (LANES, LANES), 0)
            t_i = lax.broadcasted_iota(jnp.int32, (LANES, LANES), 1)
            sel = ((l_i % dil == t_i) & (t_i < dec_seq)).astype(BF16)
            o_g = _split_dot(acc, sel)[:, :nq] + _head_expand(e_new * inv) * vn_t
        outs.append(o_g)
        lses.append(m + jnp.log(denom))

    top = jnp.maximum(jnp.maximum(lses[0], lses[1]), lses[2])
    ws = [jnp.exp(l - top) for l in lses]
    total = ws[0] + ws[1] + ws[2]
    merged = jnp.zeros((GROUP_WIDTH, nq), F32)
    for g in range(N_ATTN_GROUPS):
        merged = merged + _head_expand(ws[g] / total) * outs[g]
    o_ref[...] = jnp.where(step_wide < dec_seq, merged, 0.0).T


def _decode_attention(q, caches, new_kv, dec_batch, dec_seq):
    nq = SUBLANES

    def pad_steps(a):
        a = a.astype(F32).reshape(dec_batch, dec_seq, a.shape[-1])
        return jnp.pad(a, ((0, 0), (0, nq - dec_seq), (0, 0)))

    args = [pad_steps(q)] + [pad_steps(n) for n in new_kv]
    bt = DECODE_BATCHES_PER_STEP
    in_specs = [pl.BlockSpec((bt, nq, ATTN_WIDTH), lambda b: (b, 0, 0))]
    in_specs += [pl.BlockSpec((bt, nq, 2 * GROUP_WIDTH), lambda b: (b, 0, 0)) for _ in new_kv]
    for (win, dil), cache in zip(ATTN_GROUPS, caches):
        args.append(jnp.transpose(cache, (0, 2, 3, 4, 1)).reshape(dec_batch, 2, GROUP_WIDTH, win))
        in_specs.append(pl.BlockSpec((bt, 2, GROUP_WIDTH, win), lambda b: (b, 0, 0, 0)))
    o_t = pl.pallas_call(
        functools.partial(_decode_attn_kernel, dec_seq=dec_seq),
        grid=(dec_batch // bt,),
        in_specs=in_specs,
        out_specs=pl.BlockSpec((bt, nq, GROUP_WIDTH), lambda b: (b, 0, 0)),
        out_shape=jax.ShapeDtypeStruct((dec_batch, nq, GROUP_WIDTH), F32),
        compiler_params=_params("parallel"),
        name="decode_attention",
    )(*args)
    return o_t[:, :dec_seq, :].reshape(dec_batch * dec_seq, GROUP_WIDTH)


def _ssm_kernel(u_ref, h0r_ref, h0i_ref, ar_ref, ai_ref, bw_ref, cr_ref, ci_ref, d_ref,
                y_ref, hr_out, hi_out, bur0, bui0, bur1, bui1, hr_s, hi_s, *, nb, steps):
    chunk = pl.program_id(0)
    half_in = SSM_WIDTH // 2
    half_st = SSM_COLS // 2
    lane_chunk = 4 * LANES
    bu_refs = ((bur0, bui0), (bur1, bui1))

    @pl.when(chunk == 0)
    def _():
        hr_s[...] = h0r_ref[...]
        hi_s[...] = h0i_ref[...]

    u = u_ref[...]
    ub = u.astype(BF16)

    def project(hf):
        r = jnp.dot(ub[:, hf * half_in:(hf + 1) * half_in], bw_ref[hf], preferred_element_type=F32)
        bu_refs[hf][0][...] = r[:, :half_st]
        bu_refs[hf][1][...] = r[:, half_st:]

    def recur(hf):
        br, bi = bu_refs[hf]
        for s in range(nb // SUBLANES):
            rs = slice(s * SUBLANES, (s + 1) * SUBLANES)
            for lc in range(half_st // lane_chunk):
                ls = slice(lc * lane_chunk, (lc + 1) * lane_chunk)
                gs = slice(hf * half_st + lc * lane_chunk, hf * half_st + (lc + 1) * lane_chunk)
                ar = ar_ref[:, gs]
                ai = ai_ref[:, gs]
                hr, hi = hr_s[rs, gs], hi_s[rs, gs]
                for t in range(steps):
                    row = slice(t * nb + s * SUBLANES, t * nb + (s + 1) * SUBLANES)
                    hr, hi = ar * hr - ai * hi + br[row, ls], ar * hi + ai * hr + bi[row, ls]
                    br[row, ls] = hr
                    bi[row, ls] = hi
                hr_s[rs, gs] = hr
                hi_s[rs, gs] = hi

    def readout(hf):
        br, bi = bu_refs[hf]
        y = jnp.dot(br[...].astype(BF16), cr_ref[hf], preferred_element_type=F32)
        y = y + jnp.dot(bi[...].astype(BF16), ci_ref[hf], preferred_element_type=F32)
        cs = slice(hf * half_in, (hf + 1) * half_in)
        y_ref[:, cs] = y + d_ref[:, cs] * u[:, cs]

    project(0)
    project(1)
    recur(0)
    readout(0)
    recur(1)
    readout(1)

    @pl.when(chunk == pl.num_programs(0) - 1)
    def _():
        hr_out[...] = hr_s[...]
        hi_out[...] = hi_s[...]


def _ssm_weights(ssm_log_dt, a_re, a_im, b_re, b_im, c_re, c_im):
    dt = jnp.exp(ssm_log_dt.astype(F32))[:, None]
    lam = lax.complex(a_re.astype(F32), a_im.astype(F32))
    lam_bar = jnp.exp(lam * dt)
    b_bar = ((lam_bar - 1.0) / lam)[:, :, None] * lax.complex(b_re.astype(F32), b_im.astype(F32))
    gh = N_SSM_GROUPS // 2
    eye = jnp.eye(gh, dtype=F32)

    def block_diag(m):
        return jnp.einsum('gab,gh->gahb', m, eye).reshape(gh * m.shape[1], gh * m.shape[2])

    bw, cr, ci = [], [], []
    for hf in range(2):
        gs = slice(hf * gh, (hf + 1) * gh)
        b_t = jnp.transpose(b_bar[gs], (0, 2, 1))
        bw.append(jnp.concatenate([block_diag(jnp.real(b_t)), block_diag(jnp.imag(b_t))], axis=1))
        cr.append(block_diag(jnp.transpose(c_re[gs].astype(F32), (0, 2, 1))))
        ci.append(block_diag(jnp.transpose(-c_im[gs].astype(F32), (0, 2, 1))))
    ar = jnp.broadcast_to(jnp.real(lam_bar).reshape(1, SSM_COLS), (SUBLANES, SSM_COLS))
    ai = jnp.broadcast_to(jnp.imag(lam_bar).reshape(1, SSM_COLS), (SUBLANES, SSM_COLS))
    return ar, ai, jnp.stack(bw).astype(BF16), jnp.stack(cr).astype(BF16), jnp.stack(ci).astype(BF16)


def _ssm(u_tm, h0r, h0i, weights, d_flat, nb):
    ar, ai, bw, cr, ci = weights
    rows = u_tm.shape[0]
    steps = SSM_ROWS // nb
    const2 = lambda c: (0, 0)
    const3 = lambda c: (0, 0, 0)
    state_sds = jax.ShapeDtypeStruct((nb, SSM_COLS), F32)
    return pl.pallas_call(
        functools.partial(_ssm_kernel, nb=nb, steps=steps),
        grid=(rows // SSM_ROWS,),
        in_specs=[
            pl.BlockSpec((SSM_ROWS, SSM_WIDTH), lambda c: (c, 0)),
            pl.BlockSpec((nb, SSM_COLS), const2),
            pl.BlockSpec((nb, SSM_COLS), const2),
            pl.BlockSpec((SUBLANES, SSM_COLS), const2),
            pl.BlockSpec((SUBLANES, SSM_COLS), const2),
            pl.BlockSpec(bw.shape, const3),
            pl.BlockSpec(cr.shape, const3),
            pl.BlockSpec(ci.shape, const3),
            pl.BlockSpec((1, SSM_WIDTH), const2),
        ],
        out_specs=(
            pl.BlockSpec((SSM_ROWS, SSM_WIDTH), lambda c: (c, 0)),
            pl.BlockSpec((nb, SSM_COLS), const2),
            pl.BlockSpec((nb, SSM_COLS), const2),
        ),
        out_shape=(jax.ShapeDtypeStruct((rows, SSM_WIDTH), F32), state_sds, state_sds),
        scratch_shapes=[
            pltpu.VMEM((SSM_ROWS, SSM_COLS // 2), F32),
            pltpu.VMEM((SSM_ROWS, SSM_COLS // 2), F32),
            pltpu.VMEM((SSM_ROWS, SSM_COLS // 2), F32),
            pltpu.VMEM((SSM_ROWS, SSM_COLS // 2), F32),
            pltpu.VMEM((nb, SSM_COLS), F32),
            pltpu.VMEM((nb, SSM_COLS), F32),
        ],
        compiler_params=_params("arbitrary"),
        name="s5_scan",
    )(u_tm, h0r, h0i, ar, ai, bw, cr, ci, d_flat)


def _mix_kernel(*refs, merged_attn):
    if merged_attn:
        attn_ref = refs[0]
        rest = refs[1:13]
    else:
        group_refs = refs[0:2 * N_ATTN_GROUPS]
        rest = refs[2 * N_ATTN_GROUPS:2 * N_ATTN_GROUPS + 12]
        scratch = refs[2 * N_ATTN_GROUPS + 12:]
    (y_ref, gate_ref, x_ref, wa_ref, wglu_ref, wout_ref, gffn_ref, wr_ref, br_ref,
     h_ref, hn_ref, route_ref) = rest
    tm = x_ref.shape[0]
    if merged_attn:
        attn = attn_ref[...]
    else:
        natural = []
        for idx, ref in enumerate(group_refs):
            dil = ATTN_GROUPS[idx % N_ATTN_GROUPS][1]
            if dil == 1:
                natural.append(ref[...])
                continue
            scr = scratch[idx]
            n_chunks = GROUP_WIDTH // LANES
            for r in range(dil):
                for j in range(n_chunks):
                    col = r * GROUP_WIDTH + j * LANES
                    scr[j, pl.ds(r, tm // dil, stride=dil), :] = ref[:, col:col + LANES]
            natural.append(jnp.concatenate([scr[j] for j in range(n_chunks)], axis=1))
        os, ls = natural[:N_ATTN_GROUPS], natural[N_ATTN_GROUPS:]
        top = jnp.maximum(jnp.maximum(ls[0], ls[1]), ls[2])
        es = [jnp.exp(l - top) for l in ls]
        attn = (es[0] * os[0] + es[1] * os[1] + es[2] * os[2]) / (es[0] + es[1] + es[2])
    attn_out = jnp.dot(attn.astype(BF16), wa_ref[...], preferred_element_type=F32)
    glu = jnp.dot(jax.nn.gelu(y_ref[...]).astype(BF16), wglu_ref[...], preferred_element_type=F32)
    ssm_out = glu[:, :D_MODEL] * jax.nn.sigmoid(glu[:, D_MODEL:])
    merged = jax.nn.sigmoid(gate_ref[:, :D_MODEL]) * attn_out + jax.nn.sigmoid(gate_ref[:, D_MODEL:]) * ssm_out
    h = x_ref[...] + jnp.dot(merged.astype(BF16), wout_ref[...], preferred_element_type=F32)
    h_ref[...] = h
    var = jnp.mean(h * h, axis=-1, keepdims=True)
    hn = h * lax.rsqrt(var + NORM_EPS) * gffn_ref[...]
    _to_token_tiles(hn_ref, hn)
    hn_hi = hn.astype(BF16)
    hn_lo = (hn - hn_hi.astype(F32)).astype(BF16)
    w_hi = wr_ref[:, :LANES]
    w_lo = wr_ref[:, LANES:]
    logits = (jnp.dot(hn_hi, w_hi, preferred_element_type=F32) + jnp.dot(hn_hi, w_lo, preferred_element_type=F32)
              + jnp.dot(hn_lo, w_hi, preferred_element_type=F32)) + br_ref[...]
    lane = lax.broadcasted_iota(jnp.int32, logits.shape, 1).astype(F32)
    far = float(LANES)

    def first_argmax(vals):
        top_v = jnp.max(vals, axis=1, keepdims=True)
        return top_v, jnp.min(jnp.where(vals == top_v, lane, far), axis=1, keepdims=True)

    group_logits = jnp.where(lane < N_EXPERT_GROUPS, logits, -jnp.inf)
    g_top, g_idx = first_argmax(group_logits)
    p_group = 1.0 / jnp.sum(jnp.exp(group_logits - g_top), axis=1, keepdims=True)
    first_lane = N_EXPERT_GROUPS + g_idx * EXPERTS_PER_GROUP
    in_group = (lane >= first_lane) & (lane < first_lane + EXPERTS_PER_GROUP)
    expert_logits = jnp.where(in_group, logits, -jnp.inf)
    v1, i1 = first_argmax(expert_logits)
    v2, i2 = first_argmax(jnp.where(lane == i1, -jnp.inf, expert_logits))
    e2 = jnp.exp(v2 - v1)
    w1 = p_group / (1.0 + e2)
    w2 = p_group * e2 / (1.0 + e2)
    route = jnp.where(lane == 0, i1 - N_EXPERT_GROUPS,
                      jnp.where(lane == 1, i2 - N_EXPERT_GROUPS,
                                jnp.where(lane == 2, w1, jnp.where(lane == 3, w2, 0.0))))
    route_ref[...] = route


def _mix(attn_inputs, y, gates, x2d, wa, wglu, wout, gffn, wr, br):
    n = x2d.shape[0]
    tm = PROJ_TILE
    row = lambda i: (i, 0)
    const = lambda i: (0, 0)
    merged_attn = len(attn_inputs) == 1
    in_specs = [pl.BlockSpec((tm * GROUP_WIDTH // a.shape[1], a.shape[1]), row) for a in attn_inputs]
    scratch = [] if merged_attn else [pltpu.VMEM((GROUP_WIDTH // LANES, tm, LANES), F32) for _ in attn_inputs]
    in_specs += [
        pl.BlockSpec((tm, SSM_WIDTH), row),
        pl.BlockSpec((tm, 2 * D_MODEL), row),
        pl.BlockSpec((tm, D_MODEL), row),
        pl.BlockSpec(wa.shape, const),
        pl.BlockSpec(wglu.shape, const),
        pl.BlockSpec(wout.shape, const),
        pl.BlockSpec((1, D_MODEL), const),
        pl.BlockSpec(wr.shape, const),
        pl.BlockSpec((1, LANES), const),
    ]
    return pl.pallas_call(
        functools.partial(_mix_kernel, merged_attn=merged_attn),
        grid=(n // tm,),
        in_specs=in_specs,
        out_specs=(pl.BlockSpec((tm, D_MODEL), row), pl.BlockSpec((tm * TOKEN_TILE_ROWS, LANES), row),
                   pl.BlockSpec((tm, LANES), row)),
        out_shape=(jax.ShapeDtypeStruct((n, D_MODEL), F32), jax.ShapeDtypeStruct((n * TOKEN_TILE_ROWS, LANES), F32),
                   jax.ShapeDtypeStruct((n, LANES), F32)),
        scratch_shapes=scratch,
        compiler_params=_params("parallel"),
        name="branch_mix",
    )(*attn_inputs, y, gates, x2d, wa, wglu, wout, gffn, wr, br)


def _to_token_tiles(ref, val):
    rows = val.shape[0]
    for j in range(TOKEN_TILE_ROWS):
        ref[pl.ds(j, rows, stride=TOKEN_TILE_ROWS), :] = val[:, j * LANES:(j + 1) * LANES]


def _from_token_tiles(ref):
    rows = ref.shape[0] // TOKEN_TILE_ROWS
    return jnp.concatenate([ref[pl.ds(j, rows, stride=TOKEN_TILE_ROWS), :] for j in range(TOKEN_TILE_ROWS)], axis=1)


def _token_copy(src, src_tok, dst, dst_tok, sem):
    src_rows = pl.ds(pl.multiple_of(src_tok * TOKEN_TILE_ROWS, TOKEN_TILE_ROWS), TOKEN_TILE_ROWS)
    dst_rows = pl.ds(pl.multiple_of(dst_tok * TOKEN_TILE_ROWS, TOKEN_TILE_ROWS), TOKEN_TILE_ROWS)
    return pltpu.make_async_copy(src.at[src_rows], dst.at[dst_rows], sem)


def _dispatch_kernel(pad_end_ref, padded_ref, n_used_ref, slot_ref, *refs, set_steps):
    hn_refs = refs[:len(set_steps)]
    xs_hbm, zero_buf, sem, zero_sem = refs[len(set_steps):]
    block_rows = MOE_BLOCK * TOKEN_TILE_ROWS
    n_blocks = xs_hbm.shape[0] // block_rows

    @pl.when(pl.program_id(0) == 0)
    def _():
        zero_buf[...] = jnp.zeros_like(zero_buf)

        def block_copy(b):
            start = pl.multiple_of(b * block_rows, block_rows)
            return pltpu.make_async_copy(zero_buf, xs_hbm.at[pl.ds(start, block_rows)], zero_sem)

        def segment_tails(action):
            def body(e, carry):
                @pl.when(padded_ref[e] > 0)
                def _():
                    action(block_copy(pad_end_ref[e] // MOE_BLOCK - 1))
                return carry
            lax.fori_loop(0, N_EXPERTS, body, 0)

        def unused_blocks(action):
            def body(b, carry):
                @pl.when(b >= n_used_ref[0])
                def _():
                    action(block_copy(b))
                return carry
            lax.fori_loop(0, n_blocks, body, 0)

        segment_tails(lambda cp: cp.start())
        unused_blocks(lambda cp: cp.start())
        segment_tails(lambda cp: cp.wait())
        unused_blocks(lambda cp: cp.wait())

    first = 0
    for hn_ref, steps in zip(hn_refs, set_steps):
        @pl.when((pl.program_id(0) >= first) & (pl.program_id(0) < first + steps))
        def _(hn_ref=hn_ref):
            _scatter_rows(slot_ref, hn_ref, xs_hbm, sem)
        first += steps


def _scatter_rows(slot_ref, hn_ref, xs_hbm, sem):
    def issue(j, carry):
        _token_copy(hn_ref, j, xs_hbm, slot_ref[2 * j], sem).start(priority=0)
        _token_copy(hn_ref, j, xs_hbm, slot_ref[2 * j + 1], sem).start(priority=1)
        return carry

    def drain(j, carry):
        _token_copy(hn_ref, 0, xs_hbm, 0, sem).wait()
        _token_copy(hn_ref, 0, xs_hbm, 0, sem).wait()
        return carry

    lax.fori_loop(0, ROW_TILE, issue, 0, unroll=DMA_UNROLL)
    lax.fori_loop(0, ROW_TILE, drain, 0, unroll=DMA_UNROLL)


def _dispatch(pad_end, padded, n_used, slots, hns, n_slots):
    set_steps = tuple(hn.shape[0] // (TOKEN_TILE_ROWS * ROW_TILE) for hn in hns)
    hn_specs, first = [], 0
    for steps in set_steps:
        hn_specs.append(pl.BlockSpec(
            (ROW_TILE * TOKEN_TILE_ROWS, LANES),
            lambda i, *_, first=first, steps=steps: (jnp.clip(i - first, 0, steps - 1), 0)))
        first += steps
    return pl.pallas_call(
        functools.partial(_dispatch_kernel, set_steps=set_steps),
        grid_spec=pltpu.PrefetchScalarGridSpec(
            num_scalar_prefetch=3,
            grid=(sum(set_steps),),
            in_specs=[pl.BlockSpec((2 * ROW_TILE,), lambda i, *_: (i,), memory_space=pltpu.SMEM)] + hn_specs,
            out_specs=pl.BlockSpec(memory_space=pl.ANY),
            scratch_shapes=[pltpu.VMEM((MOE_BLOCK * TOKEN_TILE_ROWS, LANES), F32), pltpu.SemaphoreType.DMA(()),
                            pltpu.SemaphoreType.DMA(())],
        ),
        out_shape=jax.ShapeDtypeStruct((n_slots * TOKEN_TILE_ROWS, LANES), F32),
        compiler_params=_params("arbitrary"),
        name="moe_dispatch",
    )(pad_end, padded, n_used, slots, *hns)


def _expert_kernel(block_e_ref, n_used_ref, xs_ref, wg_ref, wu_ref, wd_ref, yb_ref):
    del block_e_ref
    i = pl.program_id(0)

    @pl.when(i < n_used_ref[0])
    def _():
        xb = _from_token_tiles(xs_ref).astype(BF16)
        gate = jnp.dot(xb, wg_ref[...].astype(BF16), preferred_element_type=F32)
        up = jnp.dot(xb, wu_ref[...].astype(BF16), preferred_element_type=F32)
        hmid = (jax.nn.silu(gate) * up).astype(BF16)
        _to_token_tiles(yb_ref, jnp.dot(hmid, wd_ref[...].astype(BF16), preferred_element_type=F32))

    @pl.when(i >= n_used_ref[0])
    def _():
        yb_ref[...] = jnp.zeros_like(yb_ref)


def _experts(block_e, n_used, xs, wg, wu, wd):
    block_rows = MOE_BLOCK * TOKEN_TILE_ROWS
    n_blocks = xs.shape[0] // block_rows
    grid_spec = pltpu.PrefetchScalarGridSpec(
        num_scalar_prefetch=2,
        grid=(n_blocks,),
        in_specs=[
            pl.BlockSpec((block_rows, LANES), lambda i, be, nu: (jnp.minimum(i, nu[0] - 1), 0)),
            pl.BlockSpec((None, D_MODEL, D_EXPERT), lambda i, be, nu: (be[i], 0, 0)),
            pl.BlockSpec((None, D_MODEL, D_EXPERT), lambda i, be, nu: (be[i], 0, 0)),
            pl.BlockSpec((None, D_EXPERT, D_MODEL), lambda i, be, nu: (be[i], 0, 0)),
        ],
        out_specs=pl.BlockSpec((block_rows, LANES), lambda i, be, nu: (i, 0)),
    )
    return pl.pallas_call(
        _expert_kernel,
        grid_spec=grid_spec,
        out_shape=jax.ShapeDtypeStruct(xs.shape, F32),
        compiler_params=_params("arbitrary"),
        name="moe_experts",
    )(block_e, n_used, xs, wg, wu, wd)


def _combine_kernel(slot_ref, next_slot_ref, h_ref, route_ref, g_ref, yb_hbm, out_ref, buf_a, buf_b, sem, *, steps):
    step = pl.program_id(0)
    cur = step % 2

    def gather(slots, buf_idx):
        def body(j, carry):
            _token_copy(yb_hbm, slots[2 * j], buf_a.at[buf_idx], j, sem.at[buf_idx]).start(priority=0)
            _token_copy(yb_hbm, slots[2 * j + 1], buf_b.at[buf_idx], j, sem.at[buf_idx]).start(priority=1)
            return carry
        lax.fori_loop(0, ROW_TILE, body, 0, unroll=DMA_UNROLL)

    @pl.when(step == 0)
    def _():
        gather(slot_ref, 0)

    if steps > 1:
        @pl.when(step + 1 < steps)
        def _():
            gather(next_slot_ref, 1 - cur)

    def drain(j, carry):
        _token_copy(yb_hbm, 0, buf_a.at[cur], 0, sem.at[cur]).wait()
        _token_copy(yb_hbm, 0, buf_b.at[cur], 0, sem.at[cur]).wait()
        return carry

    lax.fori_loop(0, ROW_TILE, drain, 0, unroll=DMA_UNROLL)
    route = route_ref[...]
    h = h_ref[...] + (route[:, 2:3] * _from_token_tiles(buf_a.at[cur]) + route[:, 3:4] * _from_token_tiles(buf_b.at[cur]))
    var = jnp.mean(h * h, axis=-1, keepdims=True)
    out_ref[...] = h * lax.rsqrt(var + NORM_EPS) * g_ref[...]


def _combine(slots, h, route, g_final, yb):
    n = h.shape[0]
    steps = n // ROW_TILE
    row = lambda i: (i, 0)
    return pl.pallas_call(
        functools.partial(_combine_kernel, steps=steps),
        grid=(steps,),
        in_specs=[
            pl.BlockSpec((2 * ROW_TILE,), lambda i: (i,), memory_space=pltpu.SMEM),
            pl.BlockSpec((2 * ROW_TILE,), lambda i: (jnp.minimum(i + 1, steps - 1),), memory_space=pltpu.SMEM),
            pl.BlockSpec((ROW_TILE, D_MODEL), row),
            pl.BlockSpec((ROW_TILE, LANES), row),
            pl.BlockSpec((1, D_MODEL), lambda i: (0, 0)),
            pl.BlockSpec(memory_space=pl.ANY),
        ],
        out_specs=pl.BlockSpec((ROW_TILE, D_MODEL), row),
        out_shape=jax.ShapeDtypeStruct((n, D_MODEL), F32),
        scratch_shapes=[
            pltpu.VMEM((2, ROW_TILE * TOKEN_TILE_ROWS, LANES), F32),
            pltpu.VMEM((2, ROW_TILE * TOKEN_TILE_ROWS, LANES), F32),
            pltpu.SemaphoreType.DMA((2,)),
        ],
        compiler_params=_params("arbitrary"),
        name="moe_combine",
    )(slots, slots, h, route, g_final.reshape(1, D_MODEL), yb)


def _slot_assignment(route, n_blocks):
    flat_e = route.astype(jnp.int32).reshape(-1)
    onehot = (flat_e[:, None] == jnp.arange(N_EXPERTS, dtype=jnp.int32)[None, :]).astype(jnp.int32)
    running = jnp.cumsum(onehot, axis=0)
    rank = jnp.sum(onehot * running, axis=1) - 1
    counts = running[-1]
    padded = ((counts + MOE_BLOCK - 1) // MOE_BLOCK) * MOE_BLOCK
    pad_end = jnp.cumsum(padded)
    pad_start = pad_end - padded
    slots = jnp.sum(onehot * pad_start[None, :], axis=1) + rank
    block_start = jnp.arange(n_blocks, dtype=jnp.int32) * MOE_BLOCK
    block_e = jnp.minimum(jnp.sum((pad_end[None, :] <= block_start[:, None]).astype(jnp.int32), axis=1), N_EXPERTS - 1)
    n_used = (pad_end[-1:] // MOE_BLOCK).astype(jnp.int32)
    return slots.astype(jnp.int32), block_e, n_used, pad_end.astype(jnp.int32), padded.astype(jnp.int32)


def _moe_and_final_norm(hs, hns, routes, wg, wu, wd, g_final):
    counts = [h.shape[0] for h in hs]
    n = sum(counts)
    n_blocks = (2 * n) // MOE_BLOCK + N_EXPERTS
    slots, block_e, n_used, pad_end, padded = _slot_assignment(jnp.concatenate([r[:, 0:2] for r in routes], axis=0), n_blocks)
    starts = [2 * sum(counts[:k]) for k in range(len(counts))]
    set_slots = [slots[s:s + 2 * c] for s, c in zip(starts, counts)]
    xs = _dispatch(pad_end, padded, n_used, slots, hns, n_blocks * MOE_BLOCK)
    yb = _experts(block_e, n_used, xs, wg, wu, wd)
    return [_combine(set_slots[k], hs[k], routes[k], g_final, yb) for k in range(len(hs))]


def kernel(x_prompt, x_sample, cache_kv_w128, cache_kv_w512, cache_kv_w2048, state_ssm, g_attn_norm, w_in, ssm_log_dt, ssm_a_re, ssm_a_im, ssm_b_re, ssm_b_im, ssm_c_re, ssm_c_im, ssm_d, w_glu, w_attn_branch, w_out, g_ffn_norm, w_router_group, b_router_group, w_router_expert, b_router_expert, w_exp_gate, w_exp_up, w_exp_down, g_final):
    batch, seq, _ = x_prompt.shape
    dec_batch, dec_seq, _ = x_sample.shape
    past_len = cache_kv_w2048.shape[2]
    layer = 0

    w_in_b = w_in[layer].astype(BF16)
    wa = w_attn_branch[layer].astype(BF16)
    wglu = w_glu[layer].astype(BF16)
    wout = w_out[layer].astype(BF16)
    wg, wu, wd = w_exp_gate[layer], w_exp_up[layer], w_exp_down[layer]
    gffn = g_ffn_norm[layer].reshape(1, D_MODEL)
    pad = LANES - N_EXPERT_GROUPS - N_EXPERTS
    wr_f32 = jnp.concatenate([w_router_group[layer], w_router_expert[layer], jnp.zeros((D_MODEL, pad), F32)], axis=1)
    wr_hi = wr_f32.astype(BF16)
    wr = jnp.concatenate([wr_hi, (wr_f32 - wr_hi.astype(F32)).astype(BF16)], axis=1)
    br = jnp.concatenate([b_router_group[layer], b_router_expert[layer], jnp.zeros((pad,), F32)]).reshape(1, LANES)
    ssm_w = _ssm_weights(ssm_log_dt[layer], ssm_a_re[layer], ssm_a_im[layer], ssm_b_re[layer], ssm_b_im[layer],
                         ssm_c_re[layer], ssm_c_im[layer])
    d_flat = ssm_d[layer].astype(F32).reshape(1, SSM_WIDTH)

    def time_major(a, nb, steps):
        return jnp.transpose(a.reshape(nb, steps, -1), (1, 0, 2)).reshape(nb * steps, -1)

    def batch_major(a, nb, steps):
        return jnp.transpose(a.reshape(steps, nb, -1), (1, 0, 2)).reshape(nb * steps, -1)

    xp = x_prompt.reshape(batch * seq, D_MODEL)
    q_p, kv_p, kv_t_p, u_p, gates_p = _in_projection_prompt(xp, g_attn_norm[layer], w_in_b, batch, seq)
    attn_in = []
    lse_in = []
    for g, (win, dil) in enumerate(ATTN_GROUPS):
        o, lse = _prompt_attention(q_p[g], kv_p[g], g, dil, batch, seq)
        attn_in.append(o)
        lse_in.append(lse)
    zeros_state = jnp.zeros((batch, SSM_COLS), F32)
    y_tm, hr_p, hi_p = _ssm(time_major(u_p, batch, seq), zeros_state, zeros_state, ssm_w, d_flat, batch)
    y_p = batch_major(y_tm, batch, seq)
    h_p, hn_p, route_p = _mix(attn_in + lse_in, y_p, gates_p, xp, wa, wglu, wout, gffn, wr, br)

    xs = x_sample.reshape(dec_batch * dec_seq, D_MODEL)
    pos_s = past_len + (jnp.arange(dec_batch * dec_seq, dtype=jnp.int32) % dec_seq)
    q_s, kv0_s, kv1_s, kv2_s, u_s, gates_s = _in_projection_decode(xs, g_attn_norm[layer], w_in_b, pos_s)
    kv_s = (kv0_s, kv1_s, kv2_s)
    caches = (cache_kv_w128[layer], cache_kv_w512[layer], cache_kv_w2048[layer])
    attn_s = _decode_attention(q_s, caches, kv_s, dec_batch, dec_seq)
    st = state_ssm[layer].astype(F32).reshape(dec_batch, SSM_COLS, 2)
    ys_tm, hr_s, hi_s = _ssm(time_major(u_s, dec_batch, dec_seq), st[:, :, 0], st[:, :, 1], ssm_w, d_flat, dec_batch)
    y_s = batch_major(ys_tm, dec_batch, dec_seq)
    h_s, hn_s, route_s = _mix([attn_s], y_s, gates_s, xs, wa, wglu, wout, gffn, wr, br)

    out_p, out_s = _moe_and_final_norm([h_p, h_s], [hn_p, hn_s], [route_p, route_s], wg, wu, wd, g_final)

    kv_tail = (2, HEADS_PER_GROUP, HEAD_DIM)
    outs = [out_p.reshape(batch, seq, D_MODEL), out_s.reshape(dec_batch, dec_seq, D_MODEL)]
    for g, (win, dil) in enumerate(ATTN_GROUPS):
        keep = min(win, seq)
        rows_p = jnp.transpose(kv_t_p[g].reshape(batch, 2, HEADS_PER_GROUP, HEAD_DIM, keep), (0, 4, 1, 2, 3))
        outs.append(rows_p[None])
        outs.append(kv_s[g].reshape((1, dec_batch, dec_seq) + kv_tail))
    outs.append(jnp.stack([hr_p, hi_p], axis=-1).reshape(1, batch, N_SSM_GROUPS, SSM_STATE, 2))
    outs.append(jnp.stack([hr_s, hi_s], axis=-1).reshape(1, dec_batch, N_SSM_GROUPS, SSM_STATE, 2))
    return tuple(outs)
```

```python
import functools
import math

import jax
import jax.numpy as jnp
from jax import lax
from jax.experimental import pallas as pl
from jax.experimental.pallas import tpu as pltpu

F32 = jnp.float32
BF16 = jnp.bfloat16

D_MODEL = 1024
HEAD_DIM = 64
HEADS_PER_GROUP = 8
GROUP_WIDTH = HEADS_PER_GROUP * HEAD_DIM
ATTN_GROUPS = ((128, 1), (512, 4), (2048, 16))
N_ATTN_GROUPS = len(ATTN_GROUPS)
ATTN_WIDTH = N_ATTN_GROUPS * GROUP_WIDTH
ROT_DIM = HEAD_DIM // 4
ROPE_THETA = 500000.0
WINDOW_KEYS = 128
SSM_GROUP_CH = 16
SSM_WIDTH = D_MODEL // 2
N_SSM_GROUPS = SSM_WIDTH // SSM_GROUP_CH
SSM_STATE = 64
SSM_COLS = N_SSM_GROUPS * SSM_STATE
IN_WIDTH = 3 * ATTN_WIDTH + SSM_WIDTH + 2 * D_MODEL
N_EXPERT_GROUPS = 4
EXPERTS_PER_GROUP = 8
N_EXPERTS = N_EXPERT_GROUPS * EXPERTS_PER_GROUP
D_EXPERT = D_MODEL // 4
NORM_EPS = 1e-6

LANES = 128
SUBLANES = 8
TOKEN_TILE_ROWS = D_MODEL // LANES
VMEM_LIMIT = 56 * 1024 * 1024

PROJ_TILE = 256
MATMUL_LOOKAHEAD = 1
ATTN_TILE = 128
ATTN_RESIDUES_PER_STEP = 4
ATTN_TILES_PER_STEP = 8
SSM_ROWS = 512
MOE_BLOCK = 512
ROW_TILE = 512
DMA_UNROLL = 8
DECODE_BATCHES_PER_STEP = 2


def _params(*sem):
    return pltpu.CompilerParams(dimension_semantics=sem, vmem_limit_bytes=VMEM_LIMIT)


def _normed_input(x_ref, g_ref):
    x = x_ref[...]
    var = jnp.mean(x * x, axis=-1, keepdims=True)
    return (x * lax.rsqrt(var + NORM_EPS) * g_ref[...]).astype(BF16)


def _rope_fn(cos_ref, sin_ref):
    cos = cos_ref[...]
    sin = sin_ref[...]
    lane = lax.broadcasted_iota(jnp.int32, cos.shape, 1) % HEAD_DIM
    first_half = lane < ROT_DIM // 2
    rotated = lane < ROT_DIM

    def rope_chunk(c):
        partner = jnp.where(first_half, pltpu.roll(c, LANES - ROT_DIM // 2, 1), pltpu.roll(c, ROT_DIM // 2, 1))
        return jnp.where(rotated, c * cos + partner * sin, c)

    def rope(t):
        return jnp.concatenate([rope_chunk(t[:, j * LANES:(j + 1) * LANES]) for j in range(GROUP_WIDTH // LANES)], axis=1)

    return rope


def _projection_tiles(xn, w_ref):
    for c in range(IN_WIDTH // GROUP_WIDTH):
        yield c, jnp.dot(xn, w_ref[:, c * GROUP_WIDTH:(c + 1) * GROUP_WIDTH], preferred_element_type=F32)


def _inproj_decode_kernel(x_ref, g_ref, w_ref, cos_ref, sin_ref, q_ref, kv0_ref, kv1_ref, kv2_ref, u_ref, gate_ref):
    rope = _rope_fn(cos_ref, sin_ref)
    kv_refs = (kv0_ref, kv1_ref, kv2_ref)
    for c, acc in _projection_tiles(_normed_input(x_ref, g_ref), w_ref):
        if c < 3:
            q_ref[:, c * GROUP_WIDTH:(c + 1) * GROUP_WIDTH] = (rope(acc) * (HEAD_DIM ** -0.5)).astype(BF16)
        elif c < 6:
            kv_refs[c - 3][:, :GROUP_WIDTH] = rope(acc)
        elif c < 9:
            kv_refs[c - 6][:, GROUP_WIDTH:] = acc
        elif c == 9:
            u_ref[...] = acc
        else:
            gate_ref[:, (c - 10) * GROUP_WIDTH:(c - 9) * GROUP_WIDTH] = acc


def _transposed_projection(wt, xn, cos_t, sin_t):
    t = lax.dot_general(wt, xn, (((1,), (1,)), ((), ())), preferred_element_type=F32)
    if cos_t is None:
        return t
    half_rot = ROT_DIM // 2
    pieces = []
    for h in range(HEADS_PER_GROUP):
        x1 = t[h * HEAD_DIM:h * HEAD_DIM + half_rot]
        x2 = t[h * HEAD_DIM + half_rot:h * HEAD_DIM + ROT_DIM]
        pieces += [x1 * cos_t - x2 * sin_t, x2 * cos_t + x1 * sin_t, t[h * HEAD_DIM + ROT_DIM:(h + 1) * HEAD_DIM]]
    return jnp.concatenate(pieces, axis=0)


def _kv_tail_kernel(x_ref, g_ref, wk_ref, wv_ref, cos_t_ref, sin_t_ref, kt_ref):
    xn = _normed_input(x_ref, g_ref)
    kt_ref[0] = _transposed_projection(wk_ref[...], xn, cos_t_ref[...], sin_t_ref[...])
    kt_ref[1] = _transposed_projection(wv_ref[...], xn, None, None)


def _inproj_prompt_kernel(x_ref, g_ref, w_ref, cos_ref, sin_ref, wt_ref, cos_t_ref, sin_t_ref, *refs, full_groups):
    q_refs, kv_refs = refs[0:N_ATTN_GROUPS], refs[N_ATTN_GROUPS:2 * N_ATTN_GROUPS]
    kt_refs = refs[2 * N_ATTN_GROUPS:2 * N_ATTN_GROUPS + len(full_groups)]
    u_ref, gate_ref = refs[2 * N_ATTN_GROUPS + len(full_groups):2 * N_ATTN_GROUPS + len(full_groups) + 2]
    scratch = refs[2 * N_ATTN_GROUPS + len(full_groups) + 2:]
    rope = _rope_fn(cos_ref, sin_ref)
    tm = x_ref.shape[0]

    dilated = [g for g, (_, d) in enumerate(ATTN_GROUPS) if d > 1]

    def store_rows(dst_ref, val, kind, g, col0, col_stride):
        dil = ATTN_GROUPS[g][1]
        if dil == 1:
            dst_ref[:, col0:col0 + GROUP_WIDTH] = val.astype(BF16)
            return
        scr = scratch[kind * len(dilated) + dilated.index(g)]
        n_chunks = GROUP_WIDTH // LANES
        for j in range(n_chunks):
            scr[j] = val[:, j * LANES:(j + 1) * LANES]
        rows = tm // dil
        for r in range(dil):
            piece = jnp.concatenate([scr[j, pl.ds(r, rows, stride=dil), :] for j in range(n_chunks)], axis=1)
            dst_ref[:, col0 + r * col_stride:col0 + r * col_stride + GROUP_WIDTH] = piece.astype(BF16)

    xn = _normed_input(x_ref, g_ref)

    def emit(c, acc):
        g = c % N_ATTN_GROUPS
        if c < 3:
            store_rows(q_refs[g], rope(acc) * (HEAD_DIM ** -0.5), 0, g, 0, GROUP_WIDTH)
        elif c < 6:
            store_rows(kv_refs[g], rope(acc), 1, g, 0, 2 * GROUP_WIDTH)
            if g in full_groups:
                kt_refs[full_groups.index(g)][0] = _transposed_projection(
                    wt_ref[full_groups.index(g)], xn, cos_t_ref[...], sin_t_ref[...])
        elif c < 9:
            store_rows(kv_refs[g], acc, 2, g, GROUP_WIDTH, 2 * GROUP_WIDTH)
            if g in full_groups:
                kt_refs[full_groups.index(g)][1] = _transposed_projection(
                    wt_ref[len(full_groups) + full_groups.index(g)], xn, None, None)
        elif c == 9:
            u_ref[...] = acc
        else:
            gate_ref[:, (c - 10) * GROUP_WIDTH:(c - 9) * GROUP_WIDTH] = acc

    pending = []
    for item in _projection_tiles(xn, w_ref):
        pending.append(item)
        if len(pending) > MATMUL_LOOKAHEAD:
            emit(*pending.pop(0))
    for item in pending:
        emit(*item)


def _rope_tables(pos):
    half = ROT_DIM // 2
    inv_freq = ROPE_THETA ** (-(jnp.arange(half, dtype=F32) / half))
    ang = pos.astype(F32)[:, None] * inv_freq[None, :]
    cos, sin = jnp.cos(ang), jnp.sin(ang)
    n = pos.shape[0]
    rest = HEAD_DIM - ROT_DIM
    cos_h = jnp.concatenate([cos, cos, jnp.ones((n, rest), F32)], axis=1)
    sin_h = jnp.concatenate([-sin, sin, jnp.zeros((n, rest), F32)], axis=1)
    return jnp.tile(cos_h, (1, LANES // HEAD_DIM)), jnp.tile(sin_h, (1, LANES // HEAD_DIM))


def _inproj_in_specs(tm, n_pos_tiles):
    const = lambda i: (0, 0)
    tab = lambda i: (i % n_pos_tiles, 0)
    return [
        pl.BlockSpec((tm, D_MODEL), lambda i: (i, 0)),
        pl.BlockSpec((1, D_MODEL), const),
        pl.BlockSpec((D_MODEL, IN_WIDTH), const, pipeline_mode=pl.Buffered(1)),
        pl.BlockSpec((tm, LANES), tab),
        pl.BlockSpec((tm, LANES), tab),
    ]


def _in_projection_decode(x2d, g, w_bf16, pos):
    n = x2d.shape[0]
    tm = PROJ_TILE
    cos_t, sin_t = _rope_tables(pos)
    row = lambda i: (i, 0)
    widths = (ATTN_WIDTH, 2 * GROUP_WIDTH, 2 * GROUP_WIDTH, 2 * GROUP_WIDTH, SSM_WIDTH, 2 * D_MODEL)
    dtypes = (BF16, F32, F32, F32, F32, F32)
    return pl.pallas_call(
        _inproj_decode_kernel,
        grid=(n // tm,),
        in_specs=_inproj_in_specs(tm, pos.shape[0] // tm),
        out_specs=tuple(pl.BlockSpec((tm, w), row) for w in widths),
        out_shape=tuple(jax.ShapeDtypeStruct((n, w), d) for w, d in zip(widths, dtypes)),
        compiler_params=_params("parallel"),
        name="in_projection_decode",
    )(x2d, g.reshape(1, D_MODEL), w_bf16, cos_t, sin_t)


def _in_projection_prompt(x2d, g, w_bf16, batch, seq):
    n = batch * seq
    tm = PROJ_TILE
    tiles_per_seq = seq // tm
    pos = jnp.arange(seq, dtype=jnp.int32)
    cos_t, sin_t = _rope_tables(pos)
    half = ROT_DIM // 2
    ang_t = (ROPE_THETA ** (-(jnp.arange(half, dtype=F32) / half)))[:, None] * pos.astype(F32)[None, :]
    cos_tr, sin_tr = jnp.cos(ang_t), jnp.sin(ang_t)
    wk_t = jnp.transpose(w_bf16[:, ATTN_WIDTH:2 * ATTN_WIDTH]).reshape(N_ATTN_GROUPS, GROUP_WIDTH, D_MODEL)
    wv_t = jnp.transpose(w_bf16[:, 2 * ATTN_WIDTH:3 * ATTN_WIDTH]).reshape(N_ATTN_GROUPS, GROUP_WIDTH, D_MODEL)
    keeps = tuple(min(win, seq) for win, _ in ATTN_GROUPS)
    full_groups = tuple(i for i, keep in enumerate(keeps) if keep == seq)
    w_full_t = jnp.concatenate([wk_t[jnp.array(full_groups)], wv_t[jnp.array(full_groups)]], axis=0)
    g_row = g.reshape(1, D_MODEL)
    row = lambda i: (i, 0)
    in_specs = _inproj_in_specs(tm, tiles_per_seq) + [
        pl.BlockSpec(w_full_t.shape, lambda i: (0, 0, 0), pipeline_mode=pl.Buffered(1)),
        pl.BlockSpec((half, tm), lambda i: (0, i % tiles_per_seq)),
        pl.BlockSpec((half, tm), lambda i: (0, i % tiles_per_seq)),
    ]
    out_specs, out_shape = [], []
    for width in (GROUP_WIDTH, 2 * GROUP_WIDTH):
        for _, dil in ATTN_GROUPS:
            out_specs.append(pl.BlockSpec((tm // dil, dil * width), row))
            out_shape.append(jax.ShapeDtypeStruct((n // dil, dil * width), BF16))
    for _ in full_groups:
        out_specs.append(pl.BlockSpec((None, 2, GROUP_WIDTH, tm), lambda i: (i // tiles_per_seq, 0, 0, i % tiles_per_seq)))
        out_shape.append(jax.ShapeDtypeStruct((batch, 2, GROUP_WIDTH, seq), F32))
    out_specs += [pl.BlockSpec((tm, SSM_WIDTH), row), pl.BlockSpec((tm, 2 * D_MODEL), row)]
    out_shape += [jax.ShapeDtypeStruct((n, SSM_WIDTH), F32), jax.ShapeDtypeStruct((n, 2 * D_MODEL), F32)]
    outs = pl.pallas_call(
        functools.partial(_inproj_prompt_kernel, full_groups=full_groups),
        grid=(n // tm,),
        in_specs=in_specs,
        out_specs=tuple(out_specs),
        out_shape=tuple(out_shape),
        scratch_shapes=[pltpu.VMEM((GROUP_WIDTH // LANES, tm, LANES), F32)
                        for _ in range(3 * sum(1 for _, d in ATTN_GROUPS if d > 1))],
        compiler_params=_params("parallel"),
        name="in_projection_prompt",
    )(x2d, g_row, w_bf16, cos_t, sin_t, w_full_t, cos_tr, sin_tr)
    n_full = len(full_groups)
    kv_t = dict(zip(full_groups, outs[6:6 + n_full]))
    for grp, keep in enumerate(keeps):
        if grp in full_groups:
            continue
        blk = min(keep, tm)
        first = (seq - keep) // blk
        per_seq = seq // blk
        tail = lambda b, j, first=first: (0, first + j)
        kv_t[grp] = pl.pallas_call(
            _kv_tail_kernel,
            grid=(batch, keep // blk),
            in_specs=[
                pl.BlockSpec((blk, D_MODEL), lambda b, j, first=first, per_seq=per_seq: (b * per_seq + first + j, 0)),
                pl.BlockSpec((1, D_MODEL), lambda b, j: (0, 0)),
                pl.BlockSpec((None, GROUP_WIDTH, D_MODEL), lambda b, j, grp=grp: (grp, 0, 0)),
                pl.BlockSpec((None, GROUP_WIDTH, D_MODEL), lambda b, j, grp=grp: (grp, 0, 0)),
                pl.BlockSpec((half, blk), tail),
                pl.BlockSpec((half, blk), tail),
            ],
            out_specs=pl.BlockSpec((None, 2, GROUP_WIDTH, blk), lambda b, j: (b, 0, 0, j)),
            out_shape=jax.ShapeDtypeStruct((batch, 2, GROUP_WIDTH, keep), F32),
            compiler_params=_params("parallel", "parallel"),
            name=f"kv_tail_g{grp}",
        )(x2d, g_row, wk_t, wv_t, cos_tr, sin_tr)
    return outs[0:3], outs[3:6], tuple(kv_t[i] for i in range(N_ATTN_GROUPS)), outs[6 + n_full], outs[7 + n_full]


def _window_attn_kernel(q_ref, kvc_ref, *rest, windowed, residues, sub_tiles):
    if windowed:
        kvp_ref, o_ref, lse_ref = rest
    else:
        o_ref, lse_ref = rest
    n_keys = 2 * ATTN_TILE if windowed else ATTN_TILE
    rows = lax.broadcasted_iota(jnp.int32, (ATTN_TILE, n_keys), 0)
    cols = lax.broadcasted_iota(jnp.int32, (ATTN_TILE, n_keys), 1)
    if windowed:
        band = (cols >= rows) & (cols <= rows + WINDOW_KEYS)
        first_tile_band = band & ((cols >= ATTN_TILE) | (pl.program_id(2) > 0))
    else:
        band = cols <= rows
    low_head = lax.broadcasted_iota(jnp.int32, (ATTN_TILE, LANES), 1) < HEAD_DIM
    for r in range(residues):
        for s in range(sub_tiles):
            rs = slice(s * ATTN_TILE, (s + 1) * ATTN_TILE)
            k0 = r * 2 * GROUP_WIDTH
            q = q_ref[rs, r * GROUP_WIDTH:(r + 1) * GROUP_WIDTH]
            k = kvc_ref[rs, k0:k0 + GROUP_WIDTH]
            v = kvc_ref[rs, k0 + GROUP_WIDTH:k0 + 2 * GROUP_WIDTH]
            valid = band
            if windowed:
                if s == 0:
                    prev_ref, ps, valid = kvp_ref, slice(0, ATTN_TILE), first_tile_band
                else:
                    prev_ref, ps = kvc_ref, slice((s - 1) * ATTN_TILE, s * ATTN_TILE)
                k = jnp.concatenate([prev_ref[ps, k0:k0 + GROUP_WIDTH], k], axis=0)
                v = jnp.concatenate([prev_ref[ps, k0 + GROUP_WIDTH:k0 + 2 * GROUP_WIDTH], v], axis=0)
            valid2 = jnp.concatenate([valid, valid], axis=0)
            for j in range(GROUP_WIDTH // LANES):
                sl = slice(j * LANES, (j + 1) * LANES)
                osl = slice(r * GROUP_WIDTH + j * LANES, r * GROUP_WIDTH + (j + 1) * LANES)
                qj, kj, vj = q[:, sl], k[:, sl], v[:, sl]
                zero = jnp.zeros_like(qj)
                qm = jnp.concatenate([jnp.where(low_head, qj, zero), jnp.where(low_head, zero, qj)], axis=0)
                sc = lax.dot_general(qm, kj, (((1,), (1,)), ((), ())), preferred_element_type=F32)
                sc = jnp.where(valid2, sc, -jnp.inf)
                m = jnp.max(sc, axis=1, keepdims=True)
                p = jnp.exp(sc - m)
                l = jnp.sum(p, axis=1, keepdims=True)
                o2 = jnp.dot(p.astype(BF16), vj, preferred_element_type=F32) / l
                lse2 = jnp.broadcast_to(m + jnp.log(l), o2.shape)
                o_ref[rs, osl] = jnp.where(low_head, o2[:ATTN_TILE], o2[ATTN_TILE:])
                lse_ref[rs, osl] = jnp.where(low_head, lse2[:ATTN_TILE], lse2[ATTN_TILE:])


def _prompt_attention(q, kv, group, dilation, batch, seq):
    tg = seq // dilation
    n_tiles = tg // ATTN_TILE
    windowed = n_tiles > 1
    residues = min(dilation, ATTN_RESIDUES_PER_STEP)
    sub_tiles = min(n_tiles, ATTN_TILES_PER_STEP // residues)
    rows = sub_tiles * ATTN_TILE
    q3 = q.reshape(batch, tg, dilation * GROUP_WIDTH)
    kv3 = kv.reshape(batch, tg, dilation * 2 * GROUP_WIDTH)
    in_specs = [
        pl.BlockSpec((None, rows, residues * GROUP_WIDTH), lambda b, r, t: (b, t, r)),
        pl.BlockSpec((None, rows, residues * 2 * GROUP_WIDTH), lambda b, r, t: (b, t, r)),
    ]
    args = [q3, kv3]
    if windowed:
        in_specs.append(pl.BlockSpec((None, ATTN_TILE, residues * 2 * GROUP_WIDTH),
                                     lambda b, r, t: (b, jnp.maximum(t * sub_tiles - 1, 0), r)))
        args.append(kv3)
    out_spec = pl.BlockSpec((None, rows, residues * GROUP_WIDTH), lambda b, r, t: (b, t, r))
    out_sds = jax.ShapeDtypeStruct((batch, tg, dilation * GROUP_WIDTH), F32)
    o, lse = pl.pallas_call(
        functools.partial(_window_attn_kernel, windowed=windowed, residues=residues, sub_tiles=sub_tiles),
        grid=(batch, dilation // residues, n_tiles // sub_tiles),
        in_specs=in_specs,
        out_specs=(out_spec, out_spec),
        out_shape=(out_sds, out_sds),
        compiler_params=_params("parallel", "parallel", "arbitrary"),
        name=f"prompt_attention_g{group}",
    )(*args)
    flat = (batch * tg, dilation * GROUP_WIDTH)
    return o.reshape(flat), lse.reshape(flat)


def _sublane_total(x):
    x = x + pltpu.roll(x, 4, 0)
    x = x + pltpu.roll(x, 2, 0)
    return x + pltpu.roll(x, 1, 0)


def _head_sum(prod):
    width = prod.shape[1]
    row = lax.broadcasted_iota(jnp.int32, (HEADS_PER_GROUP, width), 0)
    out = jnp.zeros((HEADS_PER_GROUP, width), F32)
    for h in range(HEADS_PER_GROUP):
        part = prod[h * HEAD_DIM:h * HEAD_DIM + SUBLANES]
        for j in range(1, HEAD_DIM // SUBLANES):
            part = part + prod[h * HEAD_DIM + j * SUBLANES:h * HEAD_DIM + (j + 1) * SUBLANES]
        out = jnp.where(row == h, _sublane_total(part), out)
    return out


def _head_expand(x):
    width = x.shape[1]
    pieces = []
    for h in range(HEADS_PER_GROUP):
        pieces.extend([jnp.broadcast_to(x[h:h + 1, :], (SUBLANES, width))] * (HEAD_DIM // SUBLANES))
    return jnp.concatenate(pieces, axis=0)


def _split_dot(acc, sel, terms=2):
    out, rem = None, acc
    for _ in range(terms):
        hi = rem.astype(BF16)
        part = jnp.dot(hi, sel, preferred_element_type=F32)
        out = part if out is None else out + part
        rem = rem - hi.astype(F32)
    return out


def _decode_attn_kernel(*refs, dec_seq):
    for bl in range(refs[0].shape[0]):
        _decode_attn_one_batch(*[r.at[bl] for r in refs], dec_seq=dec_seq)


def _decode_attn_one_batch(q_ref, n0_ref, n1_ref, n2_ref, c0_ref, c1_ref, c2_ref, o_ref, *, dec_seq):
    nq = SUBLANES
    n_sub = HEAD_DIM // SUBLANES
    neg = -jnp.inf
    q_t = q_ref[...].T
    step = lax.broadcasted_iota(jnp.int32, (HEADS_PER_GROUP, nq), 1)
    step_wide = lax.broadcasted_iota(jnp.int32, (GROUP_WIDTH, nq), 1)
    real_step = step < dec_seq
    new_refs = (n0_ref, n1_ref, n2_ref)
    cache_refs = (c0_ref, c1_ref, c2_ref)

    def column(x, t, width):
        return jnp.broadcast_to(x[:, t:t + 1], (x.shape[0], width))

    def place(cols):
        rows = cols[0].shape[0]
        lane = lax.broadcasted_iota(jnp.int32, (rows, nq), 1)
        out = jnp.zeros((rows, nq), F32)
        for t, c in enumerate(cols):
            out = jnp.where(lane == t, jnp.broadcast_to(c, (rows, nq)), out)
        return out

    outs, lses = [], []
    for g, (win, dil) in enumerate(ATTN_GROUPS):
        c_ref = cache_refs[g]
        kv_new = new_refs[g][...]
        kn_t = kv_new[:, :GROUP_WIDTH].T
        vn_t = kv_new[:, GROUP_WIDTH:].T
        q_g = q_t[g * GROUP_WIDTH:(g + 1) * GROUP_WIDTH, :]
        pos = lax.broadcasted_iota(jnp.int32, (HEADS_PER_GROUP, win), 1)
        row8 = lax.broadcasted_iota(jnp.int32, (HEADS_PER_GROUP, win), 0)
        n_tiles = win // LANES

        def cache_scores(q_pat):
            s = jnp.zeros((HEADS_PER_GROUP, win), F32)
            for h in range(HEADS_PER_GROUP):
                part = None
                for j in range(n_sub):
                    r0 = h * HEAD_DIM + j * SUBLANES
                    qp = q_pat[r0:r0 + SUBLANES, :]
                    term = c_ref[0, r0:r0 + SUBLANES, :] * jnp.concatenate([qp] * n_tiles, axis=1)
                    part = term if part is None else part + term
                s = jnp.where(row8 == h, _sublane_total(part), s)
            return s

        def weighted_values(p):
            rows = []
            for h in range(HEADS_PER_GROUP):
                ph = jnp.broadcast_to(p[h:h + 1, :], (SUBLANES, win))
                for j in range(n_sub):
                    r0 = h * HEAD_DIM + j * SUBLANES
                    prod = ph * c_ref[1, r0:r0 + SUBLANES, :]
                    a = prod[:, :LANES]
                    for tile in range(1, n_tiles):
                        a = a + prod[:, tile * LANES:(tile + 1) * LANES]
                    rows.append(a)
            return jnp.concatenate(rows, axis=0)

        if dil == 1:
            spread = (lax.broadcasted_iota(jnp.int32, (nq, dec_seq * LANES), 0)
                      == lax.broadcasted_iota(jnp.int32, (nq, dec_seq * LANES), 1) // LANES).astype(BF16)
            q_cols = _split_dot(q_g, spread, terms=1)
            kn_cols = _split_dot(kn_t, spread, terms=3)
            vn_cols = _split_dot(vn_t, spread, terms=3)
            s_new = [_head_sum(q_g * kn_cols[:, u * LANES:u * LANES + nq]) for u in range(dec_seq)]
            ok_new = [(step >= u) & real_step for u in range(dec_seq)]
            s_cache = [cache_scores(q_cols[:, t * LANES:(t + 1) * LANES]) for t in range(dec_seq)]
            m = place([jnp.max(jnp.where(pos >= t, s_cache[t], neg), axis=1, keepdims=True) for t in range(dec_seq)])
            for u in range(dec_seq):
                m = jnp.maximum(m, jnp.where(ok_new[u], s_new[u], neg))
            e_cache = [jnp.where(pos >= t, jnp.exp(s_cache[t] - m[:, t:t + 1]), 0.0) for t in range(dec_seq)]
            e_new = [jnp.where(ok_new[u], jnp.exp(s_new[u] - m), 0.0) for u in range(dec_seq)]
            denom = place([jnp.sum(e, axis=1, keepdims=True) for e in e_cache])
            for u in range(dec_seq):
                denom = denom + e_new[u]
            denom = jnp.where(real_step, denom, 1.0)
            inv = 1.0 / denom
            cols = []
            for t in range(dec_seq):
                acc = weighted_values(e_cache[t] * inv[:, t:t + 1])
                cols.append(jnp.sum(acc, axis=1, keepdims=True))
            o_g = place(cols)
            for u in range(dec_seq):
                o_g = o_g + _head_expand(e_new[u] * inv) * vn_cols[:, u * LANES:u * LANES + nq]
        else:
            res = pos % dil
            t_i = lax.broadcasted_iota(jnp.int32, (nq, LANES), 0)
            l_i = lax.broadcasted_iota(jnp.int32, (nq, LANES), 1)
            residue_pat = ((l_i % dil == t_i) & (t_i < dec_seq)).astype(BF16)
            s_cache = cache_scores(_split_dot(q_g, residue_pat, terms=1))
            s_new = jnp.where(real_step, _head_sum(q_g * kn_t), 0.0)
            m_cache = place([jnp.max(jnp.where(res == t, s_cache, neg), axis=1, keepdims=True) for t in range(dec_seq)])
            m = jnp.maximum(m_cache, s_new)

            def by_position(stat):
                out = jnp.zeros((HEADS_PER_GROUP, win), F32)
                for t in range(dec_seq):
                    out = jnp.where(res == t, column(stat, t, win), out)
                return out

            e_cache = jnp.where(res < dec_seq, jnp.exp(s_cache - by_position(m)), 0.0)
            e_new = jnp.exp(s_new - m)
            denom = place([jnp.sum(jnp.where(res == t, e_cache, 0.0), axis=1, keepdims=True)
                           for t in range(dec_seq)]) + e_new
            inv = 1.0 / denom
            acc = weighted_values(e_cache * by_position(inv))
            l_i = lax.broadcasted_iota(jnp.int32, (LANES, LANES), 0)
            t_i = lax.broadcasted_iota(jnp.int32, (LANES, LANES), 1)
            sel = ((l_i % dil == t_i) & (t_i < dec_seq)).astype(BF16)
            o_g = _split_dot(acc, sel)[:, :nq] + _head_expand(e_new * inv) * vn_t
        outs.append(o_g)
        lses.append(m + jnp.log(denom))

    top = jnp.maximum(jnp.maximum(lses[0], lses[1]), lses[2])
    ws = [jnp.exp(l - top) for l in lses]
    total = ws[0] + ws[1] + ws[2]
    merged = jnp.zeros((GROUP_WIDTH, nq), F32)
    for g in range(N_ATTN_GROUPS):
        merged = merged + _head_expand(ws[g] / total) * outs[g]
    o_ref[...] = jnp.where(step_wide < dec_seq, merged, 0.0).T


def _decode_attention(q, caches, new_kv, dec_batch, dec_seq):
    nq = SUBLANES

    def pad_steps(a):
        a = a.astype(F32).reshape(dec_batch, dec_seq, a.shape[-1])
        return jnp.pad(a, ((0, 0), (0, nq - dec_seq), (0, 0)))

    args = [pad_steps(q)] + [pad_steps(n) for n in new_kv]
    bt = DECODE_BATCHES_PER_STEP
    in_specs = [pl.BlockSpec((bt, nq, ATTN_WIDTH), lambda b: (b, 0, 0))]
    in_specs += [pl.BlockSpec((bt, nq, 2 * GROUP_WIDTH), lambda b: (b, 0, 0)) for _ in new_kv]
    for (win, dil), cache in zip(ATTN_GROUPS, caches):
        args.append(jnp.transpose(cache, (0, 2, 3, 4, 1)).reshape(dec_batch, 2, GROUP_WIDTH, win))
        in_specs.append(pl.BlockSpec((bt, 2, GROUP_WIDTH, win), lambda b: (b, 0, 0, 0)))
    o_t = pl.pallas_call(
        functools.partial(_decode_attn_kernel, dec_seq=dec_seq),
        grid=(dec_batch // bt,),
        in_specs=in_specs,
        out_specs=pl.BlockSpec((bt, nq, GROUP_WIDTH), lambda b: (b, 0, 0)),
        out_shape=jax.ShapeDtypeStruct((dec_batch, nq, GROUP_WIDTH), F32),
        compiler_params=_params("parallel"),
        name="decode_attention",
    )(*args)
    return o_t[:, :dec_seq, :].reshape(dec_batch * dec_seq, GROUP_WIDTH)


def _ssm_kernel(u_ref, h0r_ref, h0i_ref, ar_ref, ai_ref, bw_ref, cr_ref, ci_ref, d_ref,
                y_ref, hr_out, hi_out, bur0, bui0, bur1, bui1, hr_s, hi_s, *, nb, steps):
    chunk = pl.program_id(0)
    half_in = SSM_WIDTH // 2
    half_st = SSM_COLS // 2
    lane_chunk = 4 * LANES
    bu_refs = ((bur0, bui0), (bur1, bui1))

    @pl.when(chunk == 0)
    def _():
        hr_s[...] = h0r_ref[...]
        hi_s[...] = h0i_ref[...]

    u = u_ref[...]
    ub = u.astype(BF16)

    def project(hf):
        r = jnp.dot(ub[:, hf * half_in:(hf + 1) * half_in], bw_ref[hf], preferred_element_type=F32)
        bu_refs[hf][0][...] = r[:, :half_st]
        bu_refs[hf][1][...] = r[:, half_st:]

    def recur(hf):
        br, bi = bu_refs[hf]
        for s in range(nb // SUBLANES):
            rs = slice(s * SUBLANES, (s + 1) * SUBLANES)
            for lc in range(half_st // lane_chunk):
                ls = slice(lc * lane_chunk, (lc + 1) * lane_chunk)
                gs = slice(hf * half_st + lc * lane_chunk, hf * half_st + (lc + 1) * lane_chunk)
                ar = ar_ref[:, gs]
                ai = ai_ref[:, gs]
                hr, hi = hr_s[rs, gs], hi_s[rs, gs]
                for t in range(steps):
                    row = slice(t * nb + s * SUBLANES, t * nb + (s + 1) * SUBLANES)
                    hr, hi = ar * hr - ai * hi + br[row, ls], ar * hi + ai * hr + bi[row, ls]
                    br[row, ls] = hr
                    bi[row, ls] = hi
                hr_s[rs, gs] = hr
                hi_s[rs, gs] = hi

    def readout(hf):
        br, bi = bu_refs[hf]
        y = jnp.dot(br[...].astype(BF16), cr_ref[hf], preferred_element_type=F32)
        y = y + jnp.dot(bi[...].astype(BF16), ci_ref[hf], preferred_element_type=F32)
        cs = slice(hf * half_in, (hf + 1) * half_in)
        y_ref[:, cs] = y + d_ref[:, cs] * u[:, cs]

    project(0)
    project(1)
    recur(0)
    readout(0)
    recur(1)
    readout(1)

    @pl.when(chunk == pl.num_programs(0) - 1)
    def _():
        hr_out[...] = hr_s[...]
        hi_out[...] = hi_s[...]


def _ssm_weights(ssm_log_dt, a_re, a_im, b_re, b_im, c_re, c_im):
    dt = jnp.exp(ssm_log_dt.astype(F32))[:, None]
    lam = lax.complex(a_re.astype(F32), a_im.astype(F32))
    lam_bar = jnp.exp(lam * dt)
    b_bar = ((lam_bar - 1.0) / lam)[:, :, None] * lax.complex(b_re.astype(F32), b_im.astype(F32))
    gh = N_SSM_GROUPS // 2
    eye = jnp.eye(gh, dtype=F32)

    def block_diag(m):
        return jnp.einsum('gab,gh->gahb', m, eye).reshape(gh * m.shape[1], gh * m.shape[2])

    bw, cr, ci = [], [], []
    for hf in range(2):
        gs = slice(hf * gh, (hf + 1) * gh)
        b_t = jnp.transpose(b_bar[gs], (0, 2, 1))
        bw.append(jnp.concatenate([block_diag(jnp.real(b_t)), block_diag(jnp.imag(b_t))], axis=1))
        cr.append(block_diag(jnp.transpose(c_re[gs].astype(F32), (0, 2, 1))))
        ci.append(block_diag(jnp.transpose(-c_im[gs].astype(F32), (0, 2, 1))))
    ar = jnp.broadcast_to(jnp.real(lam_bar).reshape(1, SSM_COLS), (SUBLANES, SSM_COLS))
    ai = jnp.broadcast_to(jnp.imag(lam_bar).reshape(1, SSM_COLS), (SUBLANES, SSM_COLS))
    return ar, ai, jnp.stack(bw).astype(BF16), jnp.stack(cr).astype(BF16), jnp.stack(ci).astype(BF16)


def _ssm(u_tm, h0r, h0i, weights, d_flat, nb):
    ar, ai, bw, cr, ci = weights
    rows = u_tm.shape[0]
    steps = SSM_ROWS // nb
    const2 = lambda c: (0, 0)
    const3 = lambda c: (0, 0, 0)
    state_sds = jax.ShapeDtypeStruct((nb, SSM_COLS), F32)
    return pl.pallas_call(
        functools.partial(_ssm_kernel, nb=nb, steps=steps),
        grid=(rows // SSM_ROWS,),
        in_specs=[
            pl.BlockSpec((SSM_ROWS, SSM_WIDTH), lambda c: (c, 0)),
            pl.BlockSpec((nb, SSM_COLS), const2),
            pl.BlockSpec((nb, SSM_COLS), const2),
            pl.BlockSpec((SUBLANES, SSM_COLS), const2),
            pl.BlockSpec((SUBLANES, SSM_COLS), const2),
            pl.BlockSpec(bw.shape, const3),
            pl.BlockSpec(cr.shape, const3),
            pl.BlockSpec(ci.shape, const3),
            pl.BlockSpec((1, SSM_WIDTH), const2),
        ],
        out_specs=(
            pl.BlockSpec((SSM_ROWS, SSM_WIDTH), lambda c: (c, 0)),
            pl.BlockSpec((nb, SSM_COLS), const2),
            pl.BlockSpec((nb, SSM_COLS), const2),
        ),
        out_shape=(jax.ShapeDtypeStruct((rows, SSM_WIDTH), F32), state_sds, state_sds),
        scratch_shapes=[
            pltpu.VMEM((SSM_ROWS, SSM_COLS // 2), F32),
            pltpu.VMEM((SSM_ROWS, SSM_COLS // 2), F32),
            pltpu.VMEM((SSM_ROWS, SSM_COLS // 2), F32),
            pltpu.VMEM((SSM_ROWS, SSM_COLS // 2), F32),
            pltpu.VMEM((nb, SSM_COLS), F32),
            pltpu.VMEM((nb, SSM_COLS), F32),
        ],
        compiler_params=_params("arbitrary"),
        name="s5_scan",
    )(u_tm, h0r, h0i, ar, ai, bw, cr, ci, d_flat)


def _mix_kernel(*refs, merged_attn):
    if merged_attn:
        attn_ref = refs[0]
        rest = refs[1:13]
    else:
        group_refs = refs[0:2 * N_ATTN_GROUPS]
        rest = refs[2 * N_ATTN_GROUPS:2 * N_ATTN_GROUPS + 12]
        scratch = refs[2 * N_ATTN_GROUPS + 12:]
    (y_ref, gate_ref, x_ref, wa_ref, wglu_ref, wout_ref, gffn_ref, wr_ref, br_ref,
     h_ref, hn_ref, route_ref) = rest
    tm = x_ref.shape[0]
    if merged_attn:
        attn = attn_ref[...]
    else:
        natural = []
        for idx, ref in enumerate(group_refs):
            dil = ATTN_GROUPS[idx % N_ATTN_GROUPS][1]
            if dil == 1:
                natural.append(ref[...])
                continue
            scr = scratch[idx]
            n_chunks = GROUP_WIDTH // LANES
            for r in range(dil):
                for j in range(n_chunks):
                    col = r * GROUP_WIDTH + j * LANES
                    scr[j, pl.ds(r, tm // dil, stride=dil), :] = ref[:, col:col + LANES]
            natural.append(jnp.concatenate([scr[j] for j in range(n_chunks)], axis=1))
        os, ls = natural[:N_ATTN_GROUPS], natural[N_ATTN_GROUPS:]
        top = jnp.maximum(jnp.maximum(ls[0], ls[1]), ls[2])
        es = [jnp.exp(l - top) for l in ls]
        attn = (es[0] * os[0] + es[1] * os[1] + es[2] * os[2]) / (es[0] + es[1] + es[2])
    attn_out = jnp.dot(attn.astype(BF16), wa_ref[...], preferred_element_type=F32)
    glu = jnp.dot(jax.nn.gelu(y_ref[...]).astype(BF16), wglu_ref[...], preferred_element_type=F32)
    ssm_out = glu[:, :D_MODEL] * jax.nn.sigmoid(glu[:, D_MODEL:])
    merged = jax.nn.sigmoid(gate_ref[:, :D_MODEL]) * attn_out + jax.nn.sigmoid(gate_ref[:, D_MODEL:]) * ssm_out
    h = x_ref[...] + jnp.dot(merged.astype(BF16), wout_ref[...], preferred_element_type=F32)
    h_ref[...] = h
    var = jnp.mean(h * h, axis=-1, keepdims=True)
    hn = h * lax.rsqrt(var + NORM_EPS) * gffn_ref[...]
    _to_token_tiles(hn_ref, hn)
    hn_hi = hn.astype(BF16)
    hn_lo = (hn - hn_hi.astype(F32)).astype(BF16)
    w_hi = wr_ref[:, :LANES]
    w_lo = wr_ref[:, LANES:]
    logits = (jnp.dot(hn_hi, w_hi, preferred_element_type=F32) + jnp.dot(hn_hi, w_lo, preferred_element_type=F32)
              + jnp.dot(hn_lo, w_hi, preferred_element_type=F32)) + br_ref[...]
    lane = lax.broadcasted_iota(jnp.int32, logits.shape, 1).astype(F32)
    far = float(LANES)

    def first_argmax(vals):
        top_v = jnp.max(vals, axis=1, keepdims=True)
        return top_v, jnp.min(jnp.where(vals == top_v, lane, far), axis=1, keepdims=True)

    group_logits = jnp.where(lane < N_EXPERT_GROUPS, logits, -jnp.inf)
    g_top, g_idx = first_argmax(group_logits)
    p_group = 1.0 / jnp.sum(jnp.exp(group_logits - g_top), axis=1, keepdims=True)
    first_lane = N_EXPERT_GROUPS + g_idx * EXPERTS_PER_GROUP
    in_group = (lane >= first_lane) & (lane < first_lane + EXPERTS_PER_GROUP)
    expert_logits = jnp.where(in_group, logits, -jnp.inf)
    v1, i1 = first_argmax(expert_logits)
    v2, i2 = first_argmax(jnp.where(lane == i1, -jnp.inf, expert_logits))
    e2 = jnp.exp(v2 - v1)
    w1 = p_group / (1.0 + e2)
    w2 = p_group * e2 / (1.0 + e2)
    route = jnp.where(lane == 0, i1 - N_EXPERT_GROUPS,
                      jnp.where(lane == 1, i2 - N_EXPERT_GROUPS,
                                jnp.where(lane == 2, w1, jnp.where(lane == 3, w2, 0.0))))
    route_ref[...] = route


def _mix(attn_inputs, y, gates, x2d, wa, wglu, wout, gffn, wr, br):
    n = x2d.shape[0]
    tm = PROJ_TILE
    row = lambda i: (i, 0)
    const = lambda i: (0, 0)
    merged_attn = len(attn_inputs) == 1
    in_specs = [pl.BlockSpec((tm * GROUP_WIDTH // a.shape[1], a.shape[1]), row) for a in attn_inputs]
    scratch = [] if merged_attn else [pltpu.VMEM((GROUP_WIDTH // LANES, tm, LANES), F32) for _ in attn_inputs]
    in_specs += [
        pl.BlockSpec((tm, SSM_WIDTH), row),
        pl.BlockSpec((tm, 2 * D_MODEL), row),
        pl.BlockSpec((tm, D_MODEL), row),
        pl.BlockSpec(wa.shape, const),
        pl.BlockSpec(wglu.shape, const),
        pl.BlockSpec(wout.shape, const),
        pl.BlockSpec((1, D_MODEL), const),
        pl.BlockSpec(wr.shape, const),
        pl.BlockSpec((1, LANES), const),
    ]
    return pl.pallas_call(
        functools.partial(_mix_kernel, merged_attn=merged_attn),
        grid=(n // tm,),
        in_specs=in_specs,
        out_specs=(pl.BlockSpec((tm, D_MODEL), row), pl.BlockSpec((tm * TOKEN_TILE_ROWS, LANES), row),
                   pl.BlockSpec((tm, LANES), row)),
        out_shape=(jax.ShapeDtypeStruct((n, D_MODEL), F32), jax.ShapeDtypeStruct((n * TOKEN_TILE_ROWS, LANES), F32),
                   jax.ShapeDtypeStruct((n, LANES), F32)),
        scratch_shapes=scratch,
        compiler_params=_params("parallel"),
        name="branch_mix",
    )(*attn_inputs, y, gates, x2d, wa, wglu, wout, gffn, wr, br)


def _to_token_tiles(ref, val):
    rows = val.shape[0]
    for j in range(TOKEN_TILE_ROWS):
        ref[pl.ds(j, rows, stride=TOKEN_TILE_ROWS), :] = val[:, j * LANES:(j + 1) * LANES]


def _from_token_tiles(ref):
    rows = ref.shape[0] // TOKEN_TILE_ROWS
    return jnp.concatenate([ref[pl.ds(j, rows, stride=TOKEN_TILE_ROWS), :] for j in range(TOKEN_TILE_ROWS)], axis=1)


def _token_copy(src, src_tok, dst, dst_tok, sem):
    src_rows = pl.ds(pl.multiple_of(src_tok * TOKEN_TILE_ROWS, TOKEN_TILE_ROWS), TOKEN_TILE_ROWS)
    dst_rows = pl.ds(pl.multiple_of(dst_tok * TOKEN_TILE_ROWS, TOKEN_TILE_ROWS), TOKEN_TILE_ROWS)
    return pltpu.make_async_copy(src.at[src_rows], dst.at[dst_rows], sem)


def _dispatch_kernel(pad_end_ref, padded_ref, n_used_ref, slot_ref, *refs, set_steps):
    hn_refs = refs[:len(set_steps)]
    xs_hbm, zero_buf, sem, zero_sem = refs[len(set_steps):]
    block_rows = MOE_BLOCK * TOKEN_TILE_ROWS
    n_blocks = xs_hbm.shape[0] // block_rows

    @pl.when(pl.program_id(0) == 0)
    def _():
        zero_buf[...] = jnp.zeros_like(zero_buf)

        def block_copy(b):
            start = pl.multiple_of(b * block_rows, block_rows)
            return pltpu.make_async_copy(zero_buf, xs_hbm.at[pl.ds(start, block_rows)], zero_sem)

        def segment_tails(action):
            def body(e, carry):
                @pl.when(padded_ref[e] > 0)
                def _():
                    action(block_copy(pad_end_ref[e] // MOE_BLOCK - 1))
                return carry
            lax.fori_loop(0, N_EXPERTS, body, 0)

        def unused_blocks(action):
            def body(b, carry):
                @pl.when(b >= n_used_ref[0])
                def _():
                    action(block_copy(b))
                return carry
            lax.fori_loop(0, n_blocks, body, 0)

        segment_tails(lambda cp: cp.start())
        unused_blocks(lambda cp: cp.start())
        segment_tails(lambda cp: cp.wait())
        unused_blocks(lambda cp: cp.wait())

    first = 0
    for hn_ref, steps in zip(hn_refs, set_steps):
        @pl.when((pl.program_id(0) >= first) & (pl.program_id(0) < first + steps))
        def _(hn_ref=hn_ref):
            _scatter_rows(slot_ref, hn_ref, xs_hbm, sem)
        first += steps


def _scatter_rows(slot_ref, hn_ref, xs_hbm, sem):
    def issue(j, carry):
        _token_copy(hn_ref, j, xs_hbm, slot_ref[2 * j], sem).start(priority=0)
        _token_copy(hn_ref, j, xs_hbm, slot_ref[2 * j + 1], sem).start(priority=1)
        return carry

    def drain(j, carry):
        _token_copy(hn_ref, 0, xs_hbm, 0, sem).wait()
        _token_copy(hn_ref, 0, xs_hbm, 0, sem).wait()
        return carry

    lax.fori_loop(0, ROW_TILE, issue, 0, unroll=DMA_UNROLL)
    lax.fori_loop(0, ROW_TILE, drain, 0, unroll=DMA_UNROLL)


def _dispatch(pad_end, padded, n_used, slots, hns, n_slots):
    set_steps = tuple(hn.shape[0] // (TOKEN_TILE_ROWS * ROW_TILE) for hn in hns)
    hn_specs, first = [], 0
    for steps in set_steps:
        hn_specs.append(pl.BlockSpec(
            (ROW_TILE * TOKEN_TILE_ROWS, LANES),
            lambda i, *_, first=first, steps=steps: (jnp.clip(i - first, 0, steps - 1), 0)))
        first += steps
    return pl.pallas_call(
        functools.partial(_dispatch_kernel, set_steps=set_steps),
        grid_spec=pltpu.PrefetchScalarGridSpec(
            num_scalar_prefetch=3,
            grid=(sum(set_steps),),
            in_specs=[pl.BlockSpec((2 * ROW_TILE,), lambda i, *_: (i,), memory_space=pltpu.SMEM)] + hn_specs,
            out_specs=pl.BlockSpec(memory_space=pl.ANY),
            scratch_shapes=[pltpu.VMEM((MOE_BLOCK * TOKEN_TILE_ROWS, LANES), F32), pltpu.SemaphoreType.DMA(()),
                            pltpu.SemaphoreType.DMA(())],
        ),
        out_shape=jax.ShapeDtypeStruct((n_slots * TOKEN_TILE_ROWS, LANES), F32),
        compiler_params=_params("arbitrary"),
        name="moe_dispatch",
    )(pad_end, padded, n_used, slots, *hns)


def _expert_kernel(block_e_ref, n_used_ref, xs_ref, wg_ref, wu_ref, wd_ref, yb_ref):
    del block_e_ref
    i = pl.program_id(0)

    @pl.when(i < n_used_ref[0])
    def _():
        xb = _from_token_tiles(xs_ref).astype(BF16)
        gate = jnp.dot(xb, wg_ref[...].astype(BF16), preferred_element_type=F32)
        up = jnp.dot(xb, wu_ref[...].astype(BF16), preferred_element_type=F32)
        hmid = (jax.nn.silu(gate) * up).astype(BF16)
        _to_token_tiles(yb_ref, jnp.dot(hmid, wd_ref[...].astype(BF16), preferred_element_type=F32))

    @pl.when(i >= n_used_ref[0])
    def _():
        yb_ref[...] = jnp.zeros_like(yb_ref)


def _experts(block_e, n_used, xs, wg, wu, wd):
    block_rows = MOE_BLOCK * TOKEN_TILE_ROWS
    n_blocks = xs.shape[0] // block_rows
    grid_spec = pltpu.PrefetchScalarGridSpec(
        num_scalar_prefetch=2,
        grid=(n_blocks,),
        in_specs=[
            pl.BlockSpec((block_rows, LANES), lambda i, be, nu: (jnp.minimum(i, nu[0] - 1), 0)),
            pl.BlockSpec((None, D_MODEL, D_EXPERT), lambda i, be, nu: (be[i], 0, 0)),
            pl.BlockSpec((None, D_MODEL, D_EXPERT), lambda i, be, nu: (be[i], 0, 0)),
            pl.BlockSpec((None, D_EXPERT, D_MODEL), lambda i, be, nu: (be[i], 0, 0)),
        ],
        out_specs=pl.BlockSpec((block_rows, LANES), lambda i, be, nu: (i, 0)),
    )
    return pl.pallas_call(
        _expert_kernel,
        grid_spec=grid_spec,
        out_shape=jax.ShapeDtypeStruct(xs.shape, F32),
        compiler_params=_params("arbitrary"),
        name="moe_experts",
    )(block_e, n_used, xs, wg, wu, wd)


def _combine_kernel(slot_ref, next_slot_ref, h_ref, route_ref, g_ref, yb_hbm, out_ref, buf_a, buf_b, sem, *, steps):
    step = pl.program_id(0)
    cur = step % 2

    def gather(slots, buf_idx):
        def body(j, carry):
            _token_copy(yb_hbm, slots[2 * j], buf_a.at[buf_idx], j, sem.at[buf_idx]).start(priority=0)
            _token_copy(yb_hbm, slots[2 * j + 1], buf_b.at[buf_idx], j, sem.at[buf_idx]).start(priority=1)
            return carry
        lax.fori_loop(0, ROW_TILE, body, 0, unroll=DMA_UNROLL)

    @pl.when(step == 0)
    def _():
        gather(slot_ref, 0)

    if steps > 1:
        @pl.when(step + 1 < steps)
        def _():
            gather(next_slot_ref, 1 - cur)

    def drain(j, carry):
        _token_copy(yb_hbm, 0, buf_a.at[cur], 0, sem.at[cur]).wait()
        _token_copy(yb_hbm, 0, buf_b.at[cur], 0, sem.at[cur]).wait()
        return carry

    lax.fori_loop(0, ROW_TILE, drain, 0, unroll=DMA_UNROLL)
    route = route_ref[...]
    h = h_ref[...] + (route[:, 2:3] * _from_token_tiles(buf_a.at[cur]) + route[:, 3:4] * _from_token_tiles(buf_b.at[cur]))
    var = jnp.mean(h * h, axis=-1, keepdims=True)
    out_ref[...] = h * lax.rsqrt(var + NORM_EPS) * g_ref[...]


def _combine(slots, h, route, g_final, yb):
    n = h.shape[0]
    steps = n // ROW_TILE
    row = lambda i: (i, 0)
    return pl.pallas_call(
        functools.partial(_combine_kernel, steps=steps),
        grid=(steps,),
        in_specs=[
            pl.BlockSpec((2 * ROW_TILE,), lambda i: (i,), memory_space=pltpu.SMEM),
            pl.BlockSpec((2 * ROW_TILE,), lambda i: (jnp.minimum(i + 1, steps - 1),), memory_space=pltpu.SMEM),
            pl.BlockSpec((ROW_TILE, D_MODEL), row),
            pl.BlockSpec((ROW_TILE, LANES), row),
            pl.BlockSpec((1, D_MODEL), lambda i: (0, 0)),
            pl.BlockSpec(memory_space=pl.ANY),
        ],
        out_specs=pl.BlockSpec((ROW_TILE, D_MODEL), row),
        out_shape=jax.ShapeDtypeStruct((n, D_MODEL), F32),
        scratch_shapes=[
            pltpu.VMEM((2, ROW_TILE * TOKEN_TILE_ROWS, LANES), F32),
            pltpu.VMEM((2, ROW_TILE * TOKEN_TILE_ROWS, LANES), F32),
            pltpu.SemaphoreType.DMA((2,)),
        ],
        compiler_params=_params("arbitrary"),
        name="moe_combine",
    )(slots, slots, h, route, g_final.reshape(1, D_MODEL), yb)


def _slot_assignment(route, n_blocks):
    flat_e = route.astype(jnp.int32).reshape(-1)
    onehot = (flat_e[:, None] == jnp.arange(N_EXPERTS, dtype=jnp.int32)[None, :]).astype(jnp.int32)
    running = jnp.cumsum(onehot, axis=0)
    rank = jnp.sum(onehot * running, axis=1) - 1
    counts = running[-1]
    padded = ((counts + MOE_BLOCK - 1) // MOE_BLOCK) * MOE_BLOCK
    pad_end = jnp.cumsum(padded)
    pad_start = pad_end - padded
    slots = jnp.sum(onehot * pad_start[None, :], axis=1) + rank
    block_start = jnp.arange(n_blocks, dtype=jnp.int32) * MOE_BLOCK
    block_e = jnp.minimum(jnp.sum((pad_end[None, :] <= block_start[:, None]).astype(jnp.int32), axis=1), N_EXPERTS - 1)
    n_used = (pad_end[-1:] // MOE_BLOCK).astype(jnp.int32)
    return slots.astype(jnp.int32), block_e, n_used, pad_end.astype(jnp.int32), padded.astype(jnp.int32)


def _moe_and_final_norm(hs, hns, routes, wg, wu, wd, g_final):
    counts = [h.shape[0] for h in hs]
    n = sum(counts)
    n_blocks = (2 * n) // MOE_BLOCK + N_EXPERTS
    slots, block_e, n_used, pad_end, padded = _slot_assignment(jnp.concatenate([r[:, 0:2] for r in routes], axis=0), n_blocks)
    starts = [2 * sum(counts[:k]) for k in range(len(counts))]
    set_slots = [slots[s:s + 2 * c] for s, c in zip(starts, counts)]
    xs = _dispatch(pad_end, padded, n_used, slots, hns, n_blocks * MOE_BLOCK)
    yb = _experts(block_e, n_used, xs, wg, wu, wd)
    return [_combine(set_slots[k], hs[k], routes[k], g_final, yb) for k in range(len(hs))]


def kernel(x_prompt, x_sample, cache_kv_w128, cache_kv_w512, cache_kv_w2048, state_ssm, g_attn_norm, w_in, ssm_log_dt, ssm_a_re, ssm_a_im, ssm_b_re, ssm_b_im, ssm_c_re, ssm_c_im, ssm_d, w_glu, w_attn_branch, w_out, g_ffn_norm, w_router_group, b_router_group, w_router_expert, b_router_expert, w_exp_gate, w_exp_up, w_exp_down, g_final):
    batch, seq, _ = x_prompt.shape
    dec_batch, dec_seq, _ = x_sample.shape
    past_len = cache_kv_w2048.shape[2]
    layer = 0

    w_in_b = w_in[layer].astype(BF16)
    wa = w_attn_branch[layer].astype(BF16)
    wglu = w_glu[layer].astype(BF16)
    wout = w_out[layer].astype(BF16)
    wg, wu, wd = w_exp_gate[layer], w_exp_up[layer], w_exp_down[layer]
    gffn = g_ffn_norm[layer].reshape(1, D_MODEL)
    pad = LANES - N_EXPERT_GROUPS - N_EXPERTS
    wr_f32 = jnp.concatenate([w_router_group[layer], w_router_expert[layer], jnp.zeros((D_MODEL, pad), F32)], axis=1)
    wr_hi = wr_f32.astype(BF16)
    wr = jnp.concatenate([wr_hi, (wr_f32 - wr_hi.astype(F32)).astype(BF16)], axis=1)
    br = jnp.concatenate([b_router_group[layer], b_router_expert[layer], jnp.zeros((pad,), F32)]).reshape(1, LANES)
    ssm_w = _ssm_weights(ssm_log_dt[layer], ssm_a_re[layer], ssm_a_im[layer], ssm_b_re[layer], ssm_b_im[layer],
                         ssm_c_re[layer], ssm_c_im[layer])
    d_flat = ssm_d[layer].astype(F32).reshape(1, SSM_WIDTH)

    def time_major(a, nb, steps):
        return jnp.transpose(a.reshape(nb, steps, -1), (1, 0, 2)).reshape(nb * steps, -1)

    def batch_major(a, nb, steps):
        return jnp.transpose(a.reshape(steps, nb, -1), (1, 0, 2)).reshape(nb * steps, -1)

    xp = x_prompt.reshape(batch * seq, D_MODEL)
    q_p, kv_p, kv_t_p, u_p, gates_p = _in_projection_prompt(xp, g_attn_norm[layer], w_in_b, batch, seq)
    attn_in = []
    lse_in = []
    for g, (win, dil) in enumerate(ATTN_GROUPS):
        o, lse = _prompt_attention(q_p[g], kv_p[g], g, dil, batch, seq)
        attn_in.append(o)
        lse_in.append(lse)
    zeros_state = jnp.zeros((batch, SSM_COLS), F32)
    y_tm, hr_p, hi_p = _ssm(time_major(u_p, batch, seq), zeros_state, zeros_state, ssm_w, d_flat, batch)
    y_p = batch_major(y_tm, batch, seq)
    h_p, hn_p, route_p = _mix(attn_in + lse_in, y_p, gates_p, xp, wa, wglu, wout, gffn, wr, br)

    xs = x_sample.reshape(dec_batch * dec_seq, D_MODEL)
    pos_s = past_len + (jnp.arange(dec_batch * dec_seq, dtype=jnp.int32) % dec_seq)
    q_s, kv0_s, kv1_s, kv2_s, u_s, gates_s = _in_projection_decode(xs, g_attn_norm[layer], w_in_b, pos_s)
    kv_s = (kv0_s, kv1_s, kv2_s)
    caches = (cache_kv_w128[layer], cache_kv_w512[layer], cache_kv_w2048[layer])
    attn_s = _decode_attention(q_s, caches, kv_s, dec_batch, dec_seq)
    st = state_ssm[layer].astype(F32).reshape(dec_batch, SSM_COLS, 2)
    ys_tm, hr_s, hi_s = _ssm(time_major(u_s, dec_batch, dec_seq), st[:, :, 0], st[:, :, 1], ssm_w, d_flat, dec_batch)
    y_s = batch_major(ys_tm, dec_batch, dec_seq)
    h_s, hn_s, route_s = _mix([attn_s], y_s, gates_s, xs, wa, wglu, wout, gffn, wr, br)

    out_p, out_s = _moe_and_final_norm([h_p, h_s], [hn_p, hn_s], [route_p, route_s], wg, wu, wd, g_final)

    kv_tail = (2, HEADS_PER_GROUP, HEAD_DIM)
    outs = [out_p.reshape(batch, seq, D_MODEL), out_s.reshape(dec_batch, dec_seq, D_MODEL)]
    for g, (win, dil) in enumerate(ATTN_GROUPS):
        keep = min(win, seq)
        rows_p = jnp.transpose(kv_t_p[g].reshape(batch, 2, HEADS_PER_GROUP, HEAD_DIM, keep), (0, 4, 1, 2, 3))
        outs.append(rows_p[None])
        outs.append(kv_s[g].reshape((1, dec_batch, dec_seq) + kv_tail))
    outs.append(jnp.stack([hr_p, hi_p], axis=-1).reshape(1, batch, N_SSM_GROUPS, SSM_STATE, 2))
    outs.append(jnp.stack([hr_s, hi_s], axis=-1).reshape(1, dec_batch, N_SSM_GROUPS, SSM_STATE, 2))
    return tuple(outs)
```
